```python
import math
import jax, jax.numpy as jnp
from jax import lax
import numpy as np

D_MODEL = 1024
BATCH = 32
SEQ = 2048
DEPTH = 2

MEM_LEN = 256
NORM_EPS = 1e-6
RWKV_HEADS = 8
RWKV_HEAD_DIM = 64
RWKV_WIDTH = RWKV_HEADS * RWKV_HEAD_DIM
DECAY_LORA = 64
ICLR_LORA = 64
GATE_LORA = 128
RWKV_IN = 3 * RWKV_WIDTH + 2 * DECAY_LORA + 2 * ICLR_LORA + GATE_LORA
RWKV_SPLITS = [RWKV_WIDTH, 2 * RWKV_WIDTH, 3 * RWKV_WIDTH,
               3 * RWKV_WIDTH + 2 * DECAY_LORA,
               3 * RWKV_WIDTH + 2 * DECAY_LORA + 2 * ICLR_LORA]
SGU_WIDTH = D_MODEL // 2
SGU_GROUPS = 4
SGU_GROUP_DIM = SGU_WIDTH // SGU_GROUPS
SGU_CHUNK = 128
AB_IN = RWKV_IN + 2 * SGU_WIDTH
AB_OUT = RWKV_WIDTH + SGU_WIDTH
DIFF_HEADS = 8
DIFF_HEAD_DIM = 64
DIFF_V_DIM = 2 * DIFF_HEAD_DIM
DIFF_QK_WIDTH = DIFF_HEADS * 2 * DIFF_HEAD_DIM
DIFF_V_WIDTH = DIFF_HEADS * DIFF_V_DIM
DIFF_IN = 2 * DIFF_QK_WIDTH + DIFF_V_WIDTH
Q_BLOCK = 128
CROSS_HEADS = 4
CROSS_HEAD_DIM = D_MODEL // CROSS_HEADS
D_FF = 2816
CONV_WIDTH = 3
N_EVEN = (DEPTH + 1) // 2
N_ODD = DEPTH // 2

kernel_name = 'hybrid_rwkv7_sgu_diffattn_encoder'


def rms_norm(x, gain):
    xf = x.astype(jnp.float32)
    normed = xf * lax.rsqrt(jnp.mean(xf * xf, axis=-1, keepdims=True) + NORM_EPS)
    return normed.astype(x.dtype) * gain


def alibi_slopes(n_heads):
    return 2.0 ** (-8.0 * jnp.arange(1, n_heads + 1, dtype=jnp.float32) / n_heads)


def token_shift_centred(z, mu_prev, mu_next):
    z_prev = jnp.pad(z[:, :-1], ((0, 0), (1, 0), (0, 0)))
    z_next = jnp.pad(z[:, 1:], ((0, 0), (0, 1), (0, 0)))
    return z + mu_prev * (z_prev - z) + mu_next * (z_next - z)


def rwkv7_step(state, inp):
    r, w, k, v, kk, a = inp
    sa = jnp.einsum('...vk,...k->...v', state, -kk)
    state = (state * w[..., None, :]
             + sa[..., :, None] * (kk * a)[..., None, :]
             + v[..., :, None] * k[..., None, :])
    y = jnp.einsum('...vk,...k->...v', state, r)
    return state, y


def rwkv7_bidirectional(z, mu_prev, mu_next, w0, w2, a0, a2, g2, k_k, k_a, r_k, out_gain):
    bsz, seq, _ = z.shape
    H, N = RWKV_HEADS, RWKV_HEAD_DIM
    z = token_shift_centred(z, mu_prev, mu_next)
    r, k, v, wl, al, gl = jnp.split(z, RWKV_SPLITS, axis=-1)
    wl = jnp.tanh(wl.reshape(bsz, seq, 2, DECAY_LORA))
    w_log = -jax.nn.softplus(-(w0 + jnp.einsum('btdr,drc->btdc', wl, w2))) - 0.5
    decay = jnp.exp(-jnp.exp(w_log.astype(jnp.float32)))
    a = jax.nn.sigmoid(a0 + jnp.einsum('btdr,drc->btdc', al.reshape(bsz, seq, 2, ICLR_LORA), a2))
    g = jax.nn.sigmoid(gl) @ g2
    kk = (k * k_k).reshape(bsz, seq, H, N).astype(jnp.float32)
    kk = kk * lax.rsqrt(jnp.sum(kk * kk, axis=-1, keepdims=True) + 1e-12)
    k_dir = k[:, :, None, :] * (1.0 + (a - 1.0) * k_a)

    rh = r.reshape(bsz, seq, H, N)
    vh = v.reshape(bsz, seq, H, N)
    dh = decay.reshape(bsz, seq, 2, H, N)
    ah = a.reshape(bsz, seq, 2, H, N)
    kh = k_dir.reshape(bsz, seq, 2, H, N)

    def dir_stack(fwd, bwd):
        s = jnp.stack([fwd, jnp.flip(bwd, axis=1)], axis=0).astype(jnp.float32)
        return jnp.transpose(s, (2, 0, 1, 3, 4))

    xs = (dir_stack(rh, rh), dir_stack(dh[:, :, 0], dh[:, :, 1]),
          dir_stack(kh[:, :, 0], kh[:, :, 1]), dir_stack(vh, vh),
          dir_stack(kk, kk), dir_stack(ah[:, :, 0], ah[:, :, 1]))
    state0 = jnp.zeros((2, bsz, H, N, N), jnp.float32)
    _, ys = lax.scan(rwkv7_step, state0, xs)
    y = ys[:, 0] + jnp.flip(ys[:, 1], axis=0)
    y = rms_norm(jnp.transpose(y, (1, 0, 2, 3)), out_gain).astype(z.dtype)
    bonus = jnp.sum(rh[:, :, None] * kh * r_k, axis=(2, 4))
    out = (y + bonus[..., None] * vh).reshape(bsz, seq, RWKV_WIDTH)
    return out * g


def chunked_spatial_gating(u, v, v_gain, w_s, b):
    bsz, seq, _ = v.shape
    vg = rms_norm(v, v_gain).reshape(bsz, seq // SGU_CHUNK, SGU_CHUNK, SGU_GROUPS, SGU_GROUP_DIM)
    s = jnp.einsum('gpq,bnqgc->bnpgc', w_s, vg) + jnp.transpose(b)[:, :, None]
    return u * s.reshape(bsz, seq, SGU_WIDTH)


def differential_attention(z, q_gain, k_gain, lq1, lk1, lq2, lk2, out_gain, lambda_init):
    bsz, seq, _ = z.shape
    H, Dh = DIFF_HEADS, DIFF_HEAD_DIM
    n_blocks = seq // Q_BLOCK
    q = rms_norm(z[..., :DIFF_QK_WIDTH].reshape(bsz, seq, H, 2, Dh), q_gain)
    k = rms_norm(z[..., DIFF_QK_WIDTH:2 * DIFF_QK_WIDTH].reshape(bsz, seq, H, 2, Dh), k_gain)
    v = z[..., 2 * DIFF_QK_WIDTH:].reshape(bsz, seq, H, DIFF_V_DIM)
    lam = (jnp.exp(jnp.sum(lq1.astype(jnp.float32) * lk1.astype(jnp.float32)))
           - jnp.exp(jnp.sum(lq2.astype(jnp.float32) * lk2.astype(jnp.float32))) + lambda_init)
    slopes = alibi_slopes(H)
    pos = jnp.arange(seq)
    scale = 1.0 / math.sqrt(Dh)
    qb = jnp.transpose(q.reshape(bsz, n_blocks, Q_BLOCK, H, 2, Dh), (1, 0, 2, 3, 4, 5))
    starts = jnp.arange(n_blocks) * Q_BLOCK

    def block(args):
        q_blk, start = args
        q_pos = start + jnp.arange(Q_BLOCK)
        dist = jnp.abs(q_pos[:, None] - pos[None, :]).astype(jnp.float32)
        bias = -slopes[:, None, None] * dist
        logits = jnp.einsum('bqhcd,bkhcd->bhcqk', q_blk, k).astype(jnp.float32) * scale
        p = jax.nn.softmax(logits + bias[None, :, None], axis=-1)
        attn = p[:, :, 0] - lam * p[:, :, 1]
        return jnp.einsum('bhqk,bkhe->bqhe', attn.astype(v.dtype), v)

    o = lax.map(block, (qb, starts))
    o = jnp.transpose(o, (1, 0, 2, 3, 4)).reshape(bsz, seq, H, DIFF_V_DIM)
    o = rms_norm(o, out_gain) * (1.0 - lambda_init)
    return o.reshape(bsz, seq, DIFF_V_WIDTH)


def memory_cross_attention(h, mem_n, wq, wkv, q_gain, k_gain, wo):
    bsz, seq, _ = h.shape
    m = mem_n.shape[1]
    q = rms_norm((h @ wq).reshape(bsz, seq, CROSS_HEADS, CROSS_HEAD_DIM), q_gain)
    kv = mem_n @ wkv
    k = rms_norm(kv[..., :D_MODEL].reshape(bsz, m, CROSS_HEADS, CROSS_HEAD_DIM), k_gain)
    v = kv[..., D_MODEL:].reshape(bsz, m, CROSS_HEADS, CROSS_HEAD_DIM)
    logits = jnp.einsum('bqhd,bkhd->bhqk', q, k).astype(jnp.float32) / math.sqrt(CROSS_HEAD_DIM)
    p = jax.nn.softmax(logits, axis=-1)
    o = jnp.einsum('bhqk,bkhd->bqhd', p.astype(v.dtype), v).reshape(bsz, seq, D_MODEL)
    return o @ wo


def conv_gated_ffn(h, w_up, conv_w, conv_b, w_down):
    seq = h.shape[1]
    u = h @ w_up
    pad = CONV_WIDTH // 2
    u_pad = jnp.pad(u, ((0, 0), (pad, pad), (0, 0)))
    u = conv_b + sum(u_pad[:, j:j + seq] * conv_w[j] for j in range(CONV_WIDTH))
    gate, val = jnp.split(u, 2, axis=-1)
    return (jax.nn.silu(gate) * val) @ w_down


def setup_inputs(seed: int = 0) -> dict:
    key = jax.random.key(seed)
    ks = iter(jax.random.split(key, 64))
    D, L, E, O = D_MODEL, DEPTH, N_EVEN, N_ODD

    def nrm(shape, scale):
        return jax.random.normal(next(ks), shape, jnp.float32) * scale

    def gain(shape):
        return 1.0 + nrm(shape, 0.02)

    def unif(shape, lo, hi):
        return jax.random.uniform(next(ks), shape, jnp.float32, lo, hi)

    return {
        'x': nrm((BATCH, SEQ, D), 1.0),
        'mem': nrm((BATCH, MEM_LEN, D), 1.0),
        'mem_norm': gain((D,)),
        'mix_norm': gain((L, D)),
        'cross_norm': gain((L, D)),
        'ffn_norm': gain((L, D)),
        'ab_w_in': nrm((E, D, AB_IN), D ** -0.5),
        'ab_shift_prev': unif((E, RWKV_IN), 0.0, 0.5),
        'ab_shift_next': unif((E, RWKV_IN), 0.0, 0.5),
        'rwkv_w0': -3.0 + nrm((E, 2, RWKV_WIDTH), 1.0),
        'rwkv_w2': nrm((E, 2, DECAY_LORA, RWKV_WIDTH), 0.5 * DECAY_LORA ** -0.5),
        'rwkv_a0': nrm((E, 2, RWKV_WIDTH), 0.5),
        'rwkv_a2': nrm((E, 2, ICLR_LORA, RWKV_WIDTH), ICLR_LORA ** -0.5),
        'rwkv_g2': nrm((E, GATE_LORA, RWKV_WIDTH), GATE_LORA ** -0.5),
        'rwkv_k_k': 0.85 + nrm((E, RWKV_WIDTH), 0.05),
        'rwkv_k_a': 1.0 + nrm((E, RWKV_WIDTH), 0.05),
        'rwkv_r_k': nrm((E, RWKV_HEADS, RWKV_HEAD_DIM), 0.1),
        'rwkv_out_gain': gain((E, RWKV_HEADS, RWKV_HEAD_DIM)),
        'sgu_v_gain': gain((E, SGU_WIDTH)),
        'sgu_w_s': nrm((E, SGU_GROUPS, SGU_CHUNK, SGU_CHUNK), SGU_CHUNK ** -0.5),
        'sgu_b': 1.0 + nrm((E, SGU_GROUPS, SGU_CHUNK), 0.01),
        'ab_w_out': nrm((E, AB_OUT, D), AB_OUT ** -0.5),
        'diff_w_in': nrm((O, D, DIFF_IN), D ** -0.5),
        'diff_q_gain': gain((O, DIFF_HEAD_DIM)),
        'diff_k_gain': gain((O, DIFF_HEAD_DIM)),
        'diff_lambda_q1': nrm((O, DIFF_HEAD_DIM), 0.1),
        'diff_lambda_k1': nrm((O, DIFF_HEAD_DIM), 0.1),
        'diff_lambda_q2': nrm((O, DIFF_HEAD_DIM), 0.1),
        'diff_lambda_k2': nrm((O, DIFF_HEAD_DIM), 0.1),
        'diff_out_gain': gain((O, DIFF_V_DIM)),
        'diff_w_out': nrm((O, DIFF_V_WIDTH, D), DIFF_V_WIDTH ** -0.5),
        'cross_wq': nrm((L, D, D), D ** -0.5),
        'cross_wkv': nrm((L, D, 2 * D), D ** -0.5),
        'cross_q_gain': gain((L, CROSS_HEAD_DIM)),
        'cross_k_gain': gain((L, CROSS_HEAD_DIM)),
        'cross_wo': nrm((L, D, D), D ** -0.5),
        'ffn_w_up': nrm((L, D, 2 * D_FF), D ** -0.5),
        'ffn_conv_w': nrm((L, CONV_WIDTH, 2 * D_FF), CONV_WIDTH ** -0.5),
        'ffn_conv_b': nrm((L, 2 * D_FF), 0.01),
        'ffn_w_down': nrm((L, D_FF, D), D_FF ** -0.5),
    }


def reference(x, mem, mem_norm, mix_norm, cross_norm, ffn_norm,
              ab_w_in, ab_shift_prev, ab_shift_next, rwkv_w0, rwkv_w2, rwkv_a0, rwkv_a2,
              rwkv_g2, rwkv_k_k, rwkv_k_a, rwkv_r_k, rwkv_out_gain,
              sgu_v_gain, sgu_w_s, sgu_b, ab_w_out,
              diff_w_in, diff_q_gain, diff_k_gain, diff_lambda_q1, diff_lambda_k1,
              diff_lambda_q2, diff_lambda_k2, diff_out_gain, diff_w_out,
              cross_wq, cross_wkv, cross_q_gain, cross_k_gain, cross_wo,
              ffn_w_up, ffn_conv_w, ffn_conv_b, ffn_w_down):
    mem_n = rms_norm(mem, mem_norm)
    for l in range(DEPTH):
        h = rms_norm(x, mix_norm[l])
        if l % 2 == 0:
            e = l // 2
            z = h @ ab_w_in[e]
            a_out = rwkv7_bidirectional(z[..., :RWKV_IN], ab_shift_prev[e], ab_shift_next[e],
                                        rwkv_w0[e], rwkv_w2[e], rwkv_a0[e], rwkv_a2[e],
                                        rwkv_g2[e], rwkv_k_k[e], rwkv_k_a[e], rwkv_r_k[e],
                                        rwkv_out_gain[e])
            u, v = jnp.split(jax.nn.gelu(z[..., RWKV_IN:]), 2, axis=-1)
            b_out = chunked_spatial_gating(u, v, sgu_v_gain[e], sgu_w_s[e], sgu_b[e])
            x = x + jnp.concatenate([a_out, b_out], axis=-1) @ ab_w_out[e]
        else:
            o = l // 2
            lambda_init = 0.8 - 0.6 * math.exp(-0.3 * l)
            c_out = differential_attention(h @ diff_w_in[o], diff_q_gain[o], diff_k_gain[o],
                                           diff_lambda_q1[o], diff_lambda_k1[o],
                                           diff_lambda_q2[o], diff_lambda_k2[o],
                                           diff_out_gain[o], lambda_init)
            x = x + c_out @ diff_w_out[o]
        x = x + memory_cross_attention(rms_norm(x, cross_norm[l]), mem_n, cross_wq[l],
                                       cross_wkv[l], cross_q_gain[l], cross_k_gain[l],
                                       cross_wo[l])
        x = x + conv_gated_ffn(rms_norm(x, ffn_norm[l]), ffn_w_up[l], ffn_conv_w[l],
                               ffn_conv_b[l], ffn_w_down[l])
    return x
```

```python
import functools
import math

import jax
import jax.numpy as jnp
from jax import lax
from jax.experimental import pallas as pl
from jax.experimental.pallas import tpu as pltpu

NORM_EPS = 1e-6
KK_EPS = 1e-12
RWKV_HEADS = 8
HEAD_DIM = 64
RWKV_WIDTH = RWKV_HEADS * HEAD_DIM
LORA_W = 128
RWKV_IN = 3 * RWKV_WIDTH + 3 * LORA_W
SGU_WIDTH = 512
SGU_GROUPS = 4
SGU_CHUNK = 128
DIFF_HEADS = 8
DIFF_V_DIM = 2 * HEAD_DIM
CROSS_HEADS = 4
CONV_WIDTH = 3
RWKV_CHUNK = 64
LANES = 128
HALO = 8
FF_CHUNK = 256
VMEM_LIMIT = 56 * 1024 * 1024

F32 = jnp.float32
BF16 = jnp.bfloat16


def _bf(x):
    return x.astype(BF16)


def _dot(a, b):
    return jnp.dot(_bf(a), _bf(b), preferred_element_type=F32)


def _dot_nt(a, b):
    return lax.dot_general(_bf(a), _bf(b), (((1,), (1,)), ((), ())), preferred_element_type=F32)


def _rms(x, gain):
    return x * lax.rsqrt(jnp.mean(x * x, axis=-1, keepdims=True) + NORM_EPS) * gain


def _sigmoid(x):
    return 1.0 / (1.0 + jnp.exp(-x))


def _iota(shape, dim):
    return lax.broadcasted_iota(jnp.int32, shape, dim)


def _shift_prev(u, halo_row):
    rolled = pltpu.roll(u, 1, 0)
    return jnp.where(_iota(u.shape, 0) == 0, halo_row, rolled)


def _shift_next(u, halo_row):
    n = u.shape[0]
    rolled = pltpu.roll(u, n - 1, 0)
    return jnp.where(_iota(u.shape, 0) == n - 1, halo_row, rolled)


def _halo_rows(xp_ref, xn_ref, gain, i, nt):
    xh = jnp.concatenate([xp_ref[0], xn_ref[0]], axis=0)
    hh = _rms(xh, gain)
    row = _iota(hh.shape, 0)
    has_prev = jnp.where(i > 0, 1.0, 0.0)
    has_next = jnp.where(i < nt - 1, 1.0, 0.0)
    return _bf(hh * jnp.where(row < HALO, has_prev, has_next))


def _const_spec(shape):
    zeros = (0,) * len(shape)
    return pl.BlockSpec(shape, lambda *_: zeros, pipeline_mode=pl.Buffered(1))


def _params(n_axes):
    return pltpu.CompilerParams(dimension_semantics=("arbitrary",) * n_axes,
                                vmem_limit_bytes=VMEM_LIMIT)


def _row_specs(tm, d, t):
    nb = tm // HALO
    last = t // HALO - 1
    main = pl.BlockSpec((1, tm, d), lambda b, i: (b, i, 0))
    prev = pl.BlockSpec((1, HALO, d), lambda b, i: (b, jnp.maximum(i * nb - 1, 0), 0))
    nxt = pl.BlockSpec((1, HALO, d), lambda b, i: (b, jnp.minimum((i + 1) * nb, last), 0))
    return main, prev, nxt


def _mem_kv_kernel(mem_ref, mnorm_ref, wkv_ref, kg_ref, k_ref, v_ref):
    d = mem_ref.shape[-1]
    hd = d // CROSS_HEADS
    mn = _rms(mem_ref[0], mnorm_ref[...])
    kv = jnp.dot(_bf(mn), wkv_ref[0], preferred_element_type=F32)
    for h in range(CROSS_HEADS):
        kh = _rms(kv[:, h * hd:(h + 1) * hd], kg_ref[0]) * (1.0 / math.sqrt(hd))
        k_ref[0, 0, :, h * hd:(h + 1) * hd] = _bf(kh)
    v_ref[0, 0] = _bf(kv[:, d:])


def _mem_kv(mem, mem_norm, wkv, k_gain):
    b, m, d = mem.shape
    nl = wkv.shape[0]
    out = jax.ShapeDtypeStruct((nl, b, m, d), BF16)
    return pl.pallas_call(
        _mem_kv_kernel,
        grid=(nl, b),
        in_specs=[pl.BlockSpec((1, m, d), lambda l, i: (i, 0, 0)),
                  pl.BlockSpec((1, d), lambda l, i: (0, 0)),
                  pl.BlockSpec((1, d, 2 * d), lambda l, i: (l, 0, 0)),
                  pl.BlockSpec((1, 1, d // CROSS_HEADS), lambda l, i: (l, 0, 0))],
        out_specs=[pl.BlockSpec((1, 1, m, d), lambda l, i: (l, i, 0, 0)),
                   pl.BlockSpec((1, 1, m, d), lambda l, i: (l, i, 0, 0))],
        out_shape=[out, out],
        compiler_params=_params(2),
        name="mem_kv",
    )(mem, mem_norm.reshape(1, d), _bf(wkv), k_gain.reshape(nl, 1, -1))


def _cross_kernel(x_ref, g_ref, wq_ref, qg_ref, k_ref, v_ref, wo_ref, o_ref):
    x = x_ref[0]
    d = x.shape[-1]
    hd = d // CROSS_HEADS
    hn = _bf(_rms(x, g_ref[...]))
    q = jnp.dot(hn, wq_ref[...], preferred_element_type=F32)
    outs = []
    for h in range(CROSS_HEADS):
        sl = slice(h * hd, (h + 1) * hd)
        qh = _rms(q[:, sl], qg_ref[...])
        logits = _dot_nt(qh, k_ref[0, 0, :, sl])
        p = jnp.exp(logits - jnp.max(logits, axis=-1, keepdims=True))
        s = jnp.sum(p, axis=-1, keepdims=True)
        outs.append(jnp.dot(_bf(p), v_ref[0, 0, :, sl], preferred_element_type=F32) / s)
    o = jnp.concatenate(outs, axis=1)
    o_ref[0] = x + jnp.dot(_bf(o), wo_ref[...], preferred_element_type=F32)


def _cross_attention(x, gain, wq, q_gain, k_all, v_all, wo, layer, tq):
    b, t, d = x.shape
    m = k_all.shape[2]
    hd = d // CROSS_HEADS
    return pl.pallas_call(
        _cross_kernel,
        grid=(b, t // tq),
        in_specs=[pl.BlockSpec((1, tq, d), lambda i, j: (i, j, 0)),
                  _const_spec((1, d)),
                  _const_spec((d, d)),
                  _const_spec((1, hd)),
                  pl.BlockSpec((1, 1, m, d), lambda i, j: (layer, i, 0, 0)),
                  pl.BlockSpec((1, 1, m, d), lambda i, j: (layer, i, 0, 0)),
                  _const_spec((d, d))],
        out_specs=pl.BlockSpec((1, tq, d), lambda i, j: (i, j, 0)),
        out_shape=jax.ShapeDtypeStruct((b, t, d), F32),
        compiler_params=_params(2),
        name=f"cross_attn_{layer}",
    )(x, gain.reshape(1, d), _bf(wq), q_gain.reshape(1, hd), k_all, v_all, _bf(wo))


def _ffn_kernel(x_ref, xp_ref, xn_ref, g_ref, wup_ref, cw_ref, cb_ref, wdn_ref, o_ref,
                hn_ref, hh_ref, acc_ref):
    i = pl.program_id(1)
    nt = pl.num_programs(1)
    n_chunks = wdn_ref.shape[0]
    x = x_ref[0]
    gain = g_ref[...]
    hn_ref[...] = _bf(_rms(x, gain))
    hh_ref[...] = _halo_rows(xp_ref, xn_ref, gain, i, nt)
    acc_ref[...] = jnp.zeros_like(acc_ref)

    def conv_half(idx):
        w = wup_ref[idx]
        u = jnp.dot(hn_ref[...], w, preferred_element_type=F32)
        uh = jnp.dot(hh_ref[...], w, preferred_element_type=F32)
        cw = cw_ref[idx]
        up = _shift_prev(u, uh[HALO - 1:HALO])
        un = _shift_next(u, uh[HALO:HALO + 1])
        return cb_ref[idx] + up * cw[0:1] + u * cw[1:2] + un * cw[2:3]

    def body(c, carry):
        gate = conv_half(c)
        val = conv_half(c + n_chunks)
        act = gate * _sigmoid(gate) * val
        acc_ref[...] += jnp.dot(_bf(act), wdn_ref[c], preferred_element_type=F32)
        return carry

    lax.fori_loop(0, n_chunks, body, 0)
    o_ref[0] = x + acc_ref[...]


def _conv_ffn(x, gain, w_up, conv_w, conv_b, w_down, tm):
    b, t, d = x.shape
    ff = w_down.shape[0]
    nc = ff // FF_CHUNK
    wup = _bf(w_up).reshape(d, 2 * nc, FF_CHUNK).transpose(1, 0, 2)
    cw = conv_w.reshape(CONV_WIDTH, 2 * nc, FF_CHUNK).transpose(1, 0, 2)
    cb = conv_b.reshape(2 * nc, 1, FF_CHUNK)
    wdn = _bf(w_down).reshape(nc, FF_CHUNK, d)
    main, prev, nxt = _row_specs(tm, d, t)
    return pl.pallas_call(
        _ffn_kernel,
        grid=(b, t // tm),
        in_specs=[main, prev, nxt,
                  _const_spec((1, d)),
                  _const_spec((2 * nc, d, FF_CHUNK)),
                  _const_spec((2 * nc, CONV_WIDTH, FF_CHUNK)),
                  _const_spec((2 * nc, 1, FF_CHUNK)),
                  _const_spec((nc, FF_CHUNK, d))],
        out_specs=pl.BlockSpec((1, tm, d), lambda i, j: (i, j, 0)),
        out_shape=jax.ShapeDtypeStruct((b, t, d), F32),
        scratch_shapes=[pltpu.VMEM((tm, d), BF16),
                        pltpu.VMEM((2 * HALO, d), BF16),
                        pltpu.VMEM((tm, d), F32)],
        compiler_params=_params(2),
        name="conv_ffn",
    )(x, x, x, gain.reshape(1, d), wup, cw, cb, wdn)


def _ab_in_kernel(x_ref, xp_ref, xn_ref, g_ref, win_ref, mup_ref, mun_ref,
                  w0_ref, w2_ref, a0_ref, a2_ref, g2_ref, kk_ref, ka_ref, rk_ref, bd_ref,
                  vg_ref, ws_ref, sb_ref,
                  r_o, v_o, kk_o, lw0_o, lw1_o, be0_o, be1_o, k0_o, k1_o, g_o, bv_o, b_o):
    i = pl.program_id(1)
    nt = pl.num_programs(1)
    w = RWKV_WIDTH
    x = x_ref[0]
    tm = x.shape[0]
    gain = g_ref[...]
    hn = _bf(_rms(x, gain))
    z = jnp.dot(hn, win_ref[...], preferred_element_type=F32)
    hh = _halo_rows(xp_ref, xn_ref, gain, i, nt)
    zh = jnp.dot(hh, win_ref[:, :RWKV_IN], preferred_element_type=F32)

    za = z[:, :RWKV_IN]
    zp = _shift_prev(za, zh[HALO - 1:HALO])
    zn = _shift_next(za, zh[HALO:HALO + 1])
    zs = za + mup_ref[...] * (zp - za) + mun_ref[...] * (zn - za)
    r = zs[:, 0:w]
    k = zs[:, w:2 * w]
    v = zs[:, 2 * w:3 * w]
    wl = jnp.tanh(zs[:, 3 * w:3 * w + LORA_W])
    al = zs[:, 3 * w + LORA_W:3 * w + 2 * LORA_W]
    gl = zs[:, 3 * w + 2 * LORA_W:]

    wpre = w0_ref[...] + _dot(wl, w2_ref[...])
    t_ = -wpre
    softplus = jnp.maximum(t_, 0.0) + jnp.log1p(jnp.exp(-jnp.abs(t_)))
    lw = -jnp.exp(-softplus - 0.5)
    a = _sigmoid(a0_ref[...] + _dot(al, a2_ref[...]))
    g = _dot(_sigmoid(gl), g2_ref[...])

    bd = bd_ref[...]
    kk0 = k * kk_ref[...]
    kk = kk0 * lax.rsqrt(_dot(kk0 * kk0, bd) + KK_EPS)
    ka = ka_ref[...]
    k0 = k * (1.0 + (a[:, :w] - 1.0) * ka)
    k1 = k * (1.0 + (a[:, w:] - 1.0) * ka)
    bonus = _dot(r * rk_ref[...] * (k0 + k1), bd)

    r_o[0] = r
    v_o[0] = v
    kk_o[0] = kk
    lw0_o[0] = lw[:, :w]
    lw1_o[0] = lw[:, w:]
    be0_o[0] = kk * a[:, :w]
    be1_o[0] = kk * a[:, w:]
    k0_o[0] = k0
    k1_o[0] = k1
    g_o[0] = g
    bv_o[0] = bonus * v

    zg = z[:, RWKV_IN:]
    zg = zg * (0.5 * (1.0 + jnp.tanh(math.sqrt(2.0 / math.pi) * (zg + 0.044715 * (zg * zg * zg)))))
    u = zg[:, :SGU_WIDTH]
    vn = _bf(_rms(zg[:, SGU_WIDTH:], vg_ref[...]))
    gd = SGU_WIDTH // SGU_GROUPS
    for n in range(tm // SGU_CHUNK):
        rows = slice(n * SGU_CHUNK, (n + 1) * SGU_CHUNK)
        for gi in range(SGU_GROUPS):
            cols = slice(gi * gd, (gi + 1) * gd)
            s = jnp.dot(ws_ref[gi], vn[rows, cols], preferred_element_type=F32) + sb_ref[gi]
            b_o[0, rows, cols] = u[rows, cols] * s


def _ab_in(x, gain, w_in, mu_prev, mu_next, w0, w2, a0, a2, g2, k_k, k_a, r_k, v_gain, w_s, s_b, tm):
    b, t, d = x.shape
    w = RWKV_WIDTH
    n_in = w_in.shape[1]
    half = LORA_W // 2

    def lora_cat(m):
        z = jnp.zeros((half, w), m.dtype)
        return _bf(jnp.concatenate([jnp.concatenate([m[0], z], axis=1),
                                    jnp.concatenate([z, m[1]], axis=1)], axis=0))

    head = jnp.arange(w) // HEAD_DIM
    bd = _bf(head[:, None] == head[None, :])
    sb = jnp.broadcast_to(s_b[:, :, None], (SGU_GROUPS, SGU_CHUNK, SGU_WIDTH // SGU_GROUPS))
    main, prev, nxt = _row_specs(tm, d, t)
    out = jax.ShapeDtypeStruct((b, t, w), F32)
    ospec = pl.BlockSpec((1, tm, w), lambda i, j: (i, j, 0))
    return pl.pallas_call(
        _ab_in_kernel,
        grid=(b, t // tm),
        in_specs=[main, prev, nxt,
                  _const_spec((1, d)),
                  _const_spec((d, n_in)),
                  _const_spec((1, RWKV_IN)), _const_spec((1, RWKV_IN)),
                  _const_spec((1, 2 * w)), _const_spec((LORA_W, 2 * w)),
                  _const_spec((1, 2 * w)), _const_spec((LORA_W, 2 * w)),
                  _const_spec((LORA_W, w)),
                  _const_spec((1, w)), _const_spec((1, w)), _const_spec((1, w)),
                  _const_spec((w, w)),
                  _const_spec((1, SGU_WIDTH)),
                  _const_spec((SGU_GROUPS, SGU_CHUNK, SGU_CHUNK)),
                  _const_spec((SGU_GROUPS, SGU_CHUNK, SGU_WIDTH // SGU_GROUPS))],
        out_specs=[ospec] * 12,
        out_shape=[out] * 12,
        compiler_params=_params(2),
        name="ab_in",
    )(x, x, x, gain.reshape(1, d), _bf(w_in), mu_prev.reshape(1, -1), mu_next.reshape(1, -1),
      w0.reshape(1, 2 * w), lora_cat(w2), a0.reshape(1, 2 * w), lora_cat(a2), _bf(g2),
      k_k.reshape(1, w), k_a.reshape(1, w), r_k.reshape(1, w), bd,
      v_gain.reshape(1, -1), _bf(w_s), sb)


def _stack_heads(x):
    even = (_iota(x.shape, 1) & (LANES - 1)) < HEAD_DIM
    return jnp.concatenate([jnp.where(even, x, 0.0), jnp.where(even, 0.0, x)], axis=0)


def _rwkv_chunk(r, v, kk, lw, be, kd, z_ref, reverse):
    c = r.shape[0]
    rr = _iota((c, LANES), 0)
    cc = _iota((c, LANES), 1) & (HEAD_DIM - 1)
    if reverse:
        strict, incl = cc > rr, cc >= rr
    else:
        strict, incl = cc < rr, cc <= rr
    tri = _bf(incl[:, :c])

    l1 = _bf(lw)
    l2 = _bf(lw - l1.astype(F32))
    l3 = _bf(lw - l1.astype(F32) - l2.astype(F32))
    cum = (jnp.dot(tri, l1, preferred_element_type=F32) + jnp.dot(tri, l2, preferred_element_type=F32)
           + jnp.dot(tri, l3, preferred_element_type=F32))
    tot = cum[0:1] if reverse else cum[c - 1:c]
    e_neg = jnp.exp(-cum)
    e_end = jnp.exp(tot - cum)
    ag = -kk * jnp.exp(cum - lw)
    rg = r * jnp.exp(cum)
    bi, ki = be * e_neg, kd * e_neg
    bee, kee = be * e_end, kd * e_end
    gam = jnp.exp(tot)

    r128 = _iota((LANES, LANES), 0)
    c128 = _iota((LANES, LANES), 1)
    same_head = (r128 < HEAD_DIM) == (c128 < HEAD_DIM)
    diag = r128 == c128

    ys = []
    for p in range(r.shape[1] // LANES):
        sl = slice(p * LANES, (p + 1) * LANES)
        agp, rgp, vp = ag[:, sl], rg[:, sl], v[:, sl]
        aa = _dot_nt(jnp.concatenate([agp, rgp], axis=0),
                     jnp.concatenate([_stack_heads(bi[:, sl]), _stack_heads(ki[:, sl])], axis=0))
        a_ab = jnp.where(strict, aa[:c, :LANES], 0.0)
        a_rb = jnp.where(incl, aa[c:, :LANES], 0.0)
        a_ak = jnp.where(strict, aa[:c, LANES:], 0.0)
        a_rk = jnp.where(incl, aa[c:, LANES:], 0.0)
        xv = _dot(jnp.concatenate([a_ak, a_rk], axis=0), _stack_heads(vp))
        x = jnp.concatenate([agp, xv[:c]], axis=1)
        pw = a_ab
        n_round = int(math.log2(c))
        for j in range(n_round):
            x = x + _dot(pw, _stack_heads(x))
            if j < n_round - 1:
                pw = _dot(pw, jnp.where(same_head, jnp.concatenate([pw, pw], axis=0), 0.0))
        ry = jnp.concatenate([rgp, xv[c:]], axis=1) + _dot(a_rb, _stack_heads(x))
        z0 = z_ref[p]
        ys.append(_dot(ry[:, :LANES], z0) + ry[:, LANES:])

        lt = jnp.concatenate([bee[:, sl], kee[:, sl]], axis=0).T
        rhs = jnp.concatenate([x, jnp.concatenate([jnp.zeros_like(vp), vp], axis=1)], axis=0)
        mg = _dot(lt, rhs)
        gam_col = jnp.sum(jnp.where(diag, gam[:, sl], 0.0), axis=1, keepdims=True)
        z_ref[p] = (gam_col * z0 + _dot(jnp.where(same_head, mg[:, :LANES], 0.0), z0)
                    + jnp.where(same_head, mg[:, LANES:], 0.0))
    return jnp.concatenate(ys, axis=1)


def _rwkv_kernel(rf, vf, kkf, lwf, bef, kf, rb, vb, kkb, lwb, beb, kb, yf_o, yb_o, zf_ref, zb_ref):
    @pl.when(pl.program_id(1) == 0)
    def _():
        zf_ref[...] = jnp.zeros_like(zf_ref)
        zb_ref[...] = jnp.zeros_like(zb_ref)

    yf_o[0] = _rwkv_chunk(rf[0], vf[0], kkf[0], lwf[0], bef[0], kf[0], zf_ref, False)
    yb_o[0] = _rwkv_chunk(rb[0], vb[0], kkb[0], lwb[0], beb[0], kb[0], zb_ref, True)


def _rwkv_scan(r, v, kk, lw0, lw1, be0, be1, k0, k1):
    b, t, w = r.shape
    c = RWKV_CHUNK
    nc = t // c
    fwd = pl.BlockSpec((1, c, w), lambda i, j: (i, j, 0))
    bwd = pl.BlockSpec((1, c, w), lambda i, j: (i, nc - 1 - j, 0))
    out = jax.ShapeDtypeStruct((b, t, w), F32)
    zshape = pltpu.VMEM((w // LANES, LANES, LANES), F32)
    return pl.pallas_call(
        _rwkv_kernel,
        grid=(b, nc),
        in_specs=[fwd] * 6 + [bwd] * 6,
        out_specs=[fwd, bwd],
        out_shape=[out, out],
        scratch_shapes=[zshape, zshape],
        compiler_params=_params(2),
        name="rwkv_scan",
    )(r, v, kk, lw0, be0, k0, r, v, kk, lw1, be1, k1)


def _ab_out_kernel(x_ref, yf_ref, yb_ref, bv_ref, g_ref, b_ref, og_ref, bd_ref, wo_ref, o_ref):
    y = yf_ref[0] + yb_ref[0]
    ms = _dot(y * y, bd_ref[...]) * (1.0 / HEAD_DIM)
    yn = y * lax.rsqrt(ms + NORM_EPS) * og_ref[...]
    a_out = (yn + bv_ref[0]) * g_ref[0]
    cat = jnp.concatenate([a_out, b_ref[0]], axis=1)
    o_ref[0] = x_ref[0] + jnp.dot(_bf(cat), wo_ref[...], preferred_element_type=F32)


def _ab_out(x, yf, yb, bv, g, b_out, out_gain, w_out, tm):
    b, t, d = x.shape
    w = RWKV_WIDTH
    head = jnp.arange(w) // HEAD_DIM
    bd = _bf(head[:, None] == head[None, :])
    xs = pl.BlockSpec((1, tm, d), lambda i, j: (i, j, 0))
    ws = pl.BlockSpec((1, tm, w), lambda i, j: (i, j, 0))
    return pl.pallas_call(
        _ab_out_kernel,
        grid=(b, t // tm),
        in_specs=[xs, ws, ws, ws, ws, ws, _const_spec((1, w)), _const_spec((w, w)),
                  _const_spec((w_out.shape[0], d))],
        out_specs=xs,
        out_shape=jax.ShapeDtypeStruct((b, t, d), F32),
        compiler_params=_params(2),
        name="ab_out",
    )(x, yf, yb, bv, g, b_out, out_gain.reshape(1, w), bd, _bf(w_out))


def _diff_in_kernel(x_ref, g_ref, w_ref, qg_ref, kg_ref, bd_ref, q_o, k_o, v_o):
    x = x_ref[0]
    d = x.shape[-1]
    hn = _bf(_rms(x, g_ref[...]))
    z = jnp.dot(hn, w_ref[...], preferred_element_type=F32)
    bd = bd_ref[...]
    half = bd.shape[0]

    def qk_norm(y, gain, scale):
        parts = []
        for j in range(d // half):
            yj = y[:, j * half:(j + 1) * half]
            ms = _dot(yj * yj, bd) * (1.0 / HEAD_DIM)
            parts.append(yj * lax.rsqrt(ms + NORM_EPS) * gain * scale)
        return _bf(jnp.concatenate(parts, axis=1))

    q_o[0] = qk_norm(z[:, :d], qg_ref[...], 1.0 / math.sqrt(HEAD_DIM))
    k_o[0] = qk_norm(z[:, d:2 * d], kg_ref[...], 1.0)
    v_o[0] = _bf(z[:, 2 * d:])


def _diff_in(x, gain, w_in, q_gain, k_gain, tm):
    b, t, d = x.shape
    half = 512
    head = jnp.arange(half) // HEAD_DIM
    bd = _bf(head[:, None] == head[None, :])
    xs = pl.BlockSpec((1, tm, d), lambda i, j: (i, j, 0))
    out = jax.ShapeDtypeStruct((b, t, d), BF16)
    return pl.pallas_call(
        _diff_in_kernel,
        grid=(b, t // tm),
        in_specs=[xs, _const_spec((1, d)), _const_spec((d, 3 * d)),
                  _const_spec((1, half)), _const_spec((1, half)), _const_spec((half, half))],
        out_specs=[xs, xs, xs],
        out_shape=[out, out, out],
        compiler_params=_params(2),
        name="diff_in",
    )(x, gain.reshape(1, d), _bf(w_in), jnp.tile(q_gain, half // HEAD_DIM).reshape(1, half),
      jnp.tile(k_gain, half // HEAD_DIM).reshape(1, half), bd)


def _diff_attn_kernel(q_ref, k_ref, v_ref, slope_ref, lam_ref, og_ref, o_ref, *, lambda_init):
    qt = pl.program_id(2)
    q = q_ref[0]
    k = k_ref[0]
    tq, t = q.shape[0], k.shape[0]
    first = _iota(q.shape, 1) < HEAD_DIM
    zero = jnp.zeros_like(q)
    dist = jnp.abs((qt * tq + _iota((tq, t), 0) - _iota((tq, t), 1)).astype(F32))
    bias = dist * (-slope_ref[0][:, 0:1])

    def probs(qc):
        logits = lax.dot_general(qc, k, (((1,), (1,)), ((), ())), preferred_element_type=F32) + bias
        p = jnp.exp(logits - jnp.max(logits, axis=-1, keepdims=True))
        return p, jnp.sum(p, axis=-1, keepdims=True)

    p0, s0 = probs(jnp.where(first, q, zero))
    p1, s1 = probs(jnp.where(first, zero, q))
    lv = lam_ref[...]
    lam = (jnp.exp(jnp.sum(lv[0:1] * lv[1:2], axis=-1, keepdims=True))
           - jnp.exp(jnp.sum(lv[2:3] * lv[3:4], axis=-1, keepdims=True)) + lambda_init)
    attn = p0 * (1.0 / s0) - p1 * (lam / s1)
    o = jnp.dot(_bf(attn), v_ref[0], preferred_element_type=F32)
    o_ref[0] = _rms(o, og_ref[...]) * (1.0 - lambda_init)


def _diff_attention(q, k, v, lam_vecs, out_gain, lambda_init, tq):
    b, t, d = q.shape
    nh = d // DIFF_V_DIM
    slopes = 2.0 ** (-8.0 * jnp.arange(1, nh + 1, dtype=F32) / nh)
    slopes = jnp.broadcast_to(slopes[:, None, None], (nh, 1, LANES))
    return pl.pallas_call(
        functools.partial(_diff_attn_kernel, lambda_init=lambda_init),
        grid=(b, nh, t // tq),
        in_specs=[pl.BlockSpec((1, tq, DIFF_V_DIM), lambda i, h, j: (i, j, h)),
                  pl.BlockSpec((1, t, DIFF_V_DIM), lambda i, h, j: (i, 0, h)),
                  pl.BlockSpec((1, t, DIFF_V_DIM), lambda i, h, j: (i, 0, h)),
                  pl.BlockSpec((1, 1, LANES), lambda i, h, j: (h, 0, 0)),
                  _const_spec((4, HEAD_DIM)),
                  _const_spec((1, DIFF_V_DIM))],
        out_specs=pl.BlockSpec((1, tq, DIFF_V_DIM), lambda i, h, j: (i, j, h)),
        out_shape=jax.ShapeDtypeStruct((b, t, d), F32),
        compiler_params=_params(3),
        name="diff_attn",
    )(q, k, v, slopes, lam_vecs, out_gain.reshape(1, DIFF_V_DIM))


def _proj_residual_kernel(x_ref, a_ref, w_ref, o_ref):
    o_ref[0] = x_ref[0] + jnp.dot(_bf(a_ref[0]), w_ref[...], preferred_element_type=F32)


def _proj_residual(x, a, w, tm):
    b, t, d = x.shape
    n = a.shape[-1]
    return pl.pallas_call(
        _proj_residual_kernel,
        grid=(b, t // tm),
        in_specs=[pl.BlockSpec((1, tm, d), lambda i, j: (i, j, 0)),
                  pl.BlockSpec((1, tm, n), lambda i, j: (i, j, 0)),
                  _const_spec((n, d))],
        out_specs=pl.BlockSpec((1, tm, d), lambda i, j: (i, j, 0)),
        out_shape=jax.ShapeDtypeStruct((b, t, d), F32),
        compiler_params=_params(2),
        name="proj_residual",
    )(x, a, _bf(w))


def kernel(x, mem, mem_norm, mix_norm, cross_norm, ffn_norm, ab_w_in, ab_shift_prev, ab_shift_next, rwkv_w0, rwkv_w2, rwkv_a0, rwkv_a2, rwkv_g2, rwkv_k_k, rwkv_k_a, rwkv_r_k, rwkv_out_gain, sgu_v_gain, sgu_w_s, sgu_b, ab_w_out, diff_w_in, diff_q_gain, diff_k_gain, diff_lambda_q1, diff_lambda_k1, diff_lambda_q2, diff_lambda_k2, diff_out_gain, diff_w_out, cross_wq, cross_wkv, cross_q_gain, cross_k_gain, cross_wo, ffn_w_up, ffn_conv_w, ffn_conv_b, ffn_w_down):
    b, t, d = x.shape
    depth = mix_norm.shape[0]
    tm = min(512, t)
    tq = min(256, t)
    assert t % tm == 0 and t % RWKV_CHUNK == 0 and tm % SGU_CHUNK == 0

    k_all, v_all = _mem_kv(mem, mem_norm, cross_wkv, cross_k_gain)
    for l in range(depth):
        if l % 2 == 0:
            e = l // 2
            (r, v, kk, lw0, lw1, be0, be1, k0, k1, g, bv, b_out) = _ab_in(
                x, mix_norm[l], ab_w_in[e], ab_shift_prev[e], ab_shift_next[e],
                rwkv_w0[e], rwkv_w2[e], rwkv_a0[e], rwkv_a2[e], rwkv_g2[e],
                rwkv_k_k[e], rwkv_k_a[e], rwkv_r_k[e], sgu_v_gain[e], sgu_w_s[e], sgu_b[e], tm)
            yf, yb = _rwkv_scan(r, v, kk, lw0, lw1, be0, be1, k0, k1)
            x = _ab_out(x, yf, yb, bv, g, b_out, rwkv_out_gain[e], ab_w_out[e], tm)
        else:
            o = l // 2
            lambda_init = 0.8 - 0.6 * math.exp(-0.3 * l)
            q, k, v = _diff_in(x, mix_norm[l], diff_w_in[o], diff_q_gain[o], diff_k_gain[o], tm)
            lam_vecs = jnp.stack([diff_lambda_q1[o], diff_lambda_k1[o], diff_lambda_q2[o], diff_lambda_k2[o]])
            c_out = _diff_attention(q, k, v, lam_vecs, diff_out_gain[o], lambda_init, tq)
            x = _proj_residual(x, c_out, diff_w_out[o], tm)
        x = _cross_attention(x, cross_norm[l], cross_wq[l], cross_q_gain[l], k_all, v_all, cross_wo[l], l, tm)
        x = _conv_ffn(x, ffn_norm[l], ffn_w_up[l], ffn_conv_w[l], ffn_conv_b[l], ffn_w_down[l], tm)
    return x
```

```python
import functools
import math

import jax
import jax.numpy as jnp
from jax import lax
from jax.experimental import pallas as pl
from jax.experimental.pallas import tpu as pltpu

NORM_EPS = 1e-6
KK_EPS = 1e-12
RWKV_HEADS = 8
HEAD_DIM = 64
RWKV_WIDTH = RWKV_HEADS * HEAD_DIM
LORA_W = 128
RWKV_IN = 3 * RWKV_WIDTH + 3 * LORA_W
SGU_WIDTH = 512
SGU_GROUPS = 4
SGU_CHUNK = 128
DIFF_HEADS = 8
DIFF_V_DIM = 2 * HEAD_DIM
CROSS_HEADS = 4
CONV_WIDTH = 3
RWKV_CHUNK = 64
RWKV_STEP_CHUNKS = 2
LANES = 128
HALO = 8
FF_CHUNK = 256
VMEM_LIMIT = 56 * 1024 * 1024

F32 = jnp.float32
BF16 = jnp.bfloat16


def _bf(x):
    return x.astype(BF16)


def _dot(a, b):
    return jnp.dot(_bf(a), _bf(b), preferred_element_type=F32)


def _dot_nt(a, b):
    return lax.dot_general(_bf(a), _bf(b), (((1,), (1,)), ((), ())), preferred_element_type=F32)


def _rms(x, gain):
    return x * lax.rsqrt(jnp.mean(x * x, axis=-1, keepdims=True) + NORM_EPS) * gain


def _sigmoid(x):
    return 1.0 / (1.0 + jnp.exp(-x))


def _iota(shape, dim):
    return lax.broadcasted_iota(jnp.int32, shape, dim)


def _shift_prev(u, halo_row):
    rolled = pltpu.roll(u, 1, 0)
    return jnp.where(_iota(u.shape, 0) == 0, halo_row, rolled)


def _shift_next(u, halo_row):
    n = u.shape[0]
    rolled = pltpu.roll(u, n - 1, 0)
    return jnp.where(_iota(u.shape, 0) == n - 1, halo_row, rolled)


def _halo_rows(xp_ref, xn_ref, gain, i, nt):
    xh = jnp.concatenate([xp_ref[0], xn_ref[0]], axis=0)
    hh = _rms(xh, gain)
    row = _iota(hh.shape, 0)
    has_prev = jnp.where(i > 0, 1.0, 0.0)
    has_next = jnp.where(i < nt - 1, 1.0, 0.0)
    return _bf(hh * jnp.where(row < HALO, has_prev, has_next))


def _const_spec(shape):
    zeros = (0,) * len(shape)
    return pl.BlockSpec(shape, lambda *_: zeros, pipeline_mode=pl.Buffered(1))


def _params(n_axes):
    return pltpu.CompilerParams(dimension_semantics=("arbitrary",) * n_axes,
                                vmem_limit_bytes=VMEM_LIMIT)


def _row_specs(tm, d, t):
    nb = tm // HALO
    last = t // HALO - 1
    main = pl.BlockSpec((1, tm, d), lambda b, i: (b, i, 0))
    prev = pl.BlockSpec((1, HALO, d), lambda b, i: (b, jnp.maximum(i * nb - 1, 0), 0))
    nxt = pl.BlockSpec((1, HALO, d), lambda b, i: (b, jnp.minimum((i + 1) * nb, last), 0))
    return main, prev, nxt


def _mem_kv_kernel(mem_ref, mnorm_ref, wkv_ref, kg_ref, k_ref, v_ref):
    d = mem_ref.shape[-1]
    hd = d // CROSS_HEADS
    mn = _rms(mem_ref[0], mnorm_ref[...])
    kv = jnp.dot(_bf(mn), wkv_ref[0], preferred_element_type=F32)
    for h in range(CROSS_HEADS):
        kh = _rms(kv[:, h * hd:(h + 1) * hd], kg_ref[0]) * (1.0 / math.sqrt(hd))
        k_ref[0, 0, :, h * hd:(h + 1) * hd] = _bf(kh)
    v_ref[0, 0] = _bf(kv[:, d:])


def _mem_kv(mem, mem_norm, wkv, k_gain):
    b, m, d = mem.shape
    nl = wkv.shape[0]
    out = jax.ShapeDtypeStruct((nl, b, m, d), BF16)
    return pl.pallas_call(
        _mem_kv_kernel,
        grid=(nl, b),
        in_specs=[pl.BlockSpec((1, m, d), lambda l, i: (i, 0, 0)),
                  pl.BlockSpec((1, d), lambda l, i: (0, 0)),
                  pl.BlockSpec((1, d, 2 * d), lambda l, i: (l, 0, 0)),
                  pl.BlockSpec((1, 1, d // CROSS_HEADS), lambda l, i: (l, 0, 0))],
        out_specs=[pl.BlockSpec((1, 1, m, d), lambda l, i: (l, i, 0, 0)),
                   pl.BlockSpec((1, 1, m, d), lambda l, i: (l, i, 0, 0))],
        out_shape=[out, out],
        compiler_params=_params(2),
        name="mem_kv",
    )(mem, mem_norm.reshape(1, d), _bf(wkv), k_gain.reshape(nl, 1, -1))


def _cross_kernel(x_ref, g_ref, wq_ref, qg_ref, k_ref, v_ref, wo_ref, o_ref):
    x = x_ref[0]
    d = x.shape[-1]
    hd = d // CROSS_HEADS
    hn = _bf(_rms(x, g_ref[...]))
    q = jnp.dot(hn, wq_ref[...], preferred_element_type=F32)
    outs = []
    for h in range(CROSS_HEADS):
        sl = slice(h * hd, (h + 1) * hd)
        qh = _rms(q[:, sl], qg_ref[...])
        logits = _dot_nt(qh, k_ref[0, 0, :, sl])
        p = jnp.exp(logits - jnp.max(logits, axis=-1, keepdims=True))
        s = jnp.sum(p, axis=-1, keepdims=True)
        outs.append(jnp.dot(_bf(p), v_ref[0, 0, :, sl], preferred_element_type=F32) / s)
    o = jnp.concatenate(outs, axis=1)
    o_ref[0] = x + jnp.dot(_bf(o), wo_ref[...], preferred_element_type=F32)


def _cross_attention(x, gain, wq, q_gain, k_all, v_all, wo, layer, tq):
    b, t, d = x.shape
    m = k_all.shape[2]
    hd = d // CROSS_HEADS
    return pl.pallas_call(
        _cross_kernel,
        grid=(b, t // tq),
        in_specs=[pl.BlockSpec((1, tq, d), lambda i, j: (i, j, 0)),
                  _const_spec((1, d)),
                  _const_spec((d, d)),
                  _const_spec((1, hd)),
                  pl.BlockSpec((1, 1, m, d), lambda i, j: (layer, i, 0, 0)),
                  pl.BlockSpec((1, 1, m, d), lambda i, j: (layer, i, 0, 0)),
                  _const_spec((d, d))],
        out_specs=pl.BlockSpec((1, tq, d), lambda i, j: (i, j, 0)),
        out_shape=jax.ShapeDtypeStruct((b, t, d), F32),
        compiler_params=_params(2),
        name=f"cross_attn_{layer}",
    )(x, gain.reshape(1, d), _bf(wq), q_gain.reshape(1, hd), k_all, v_all, _bf(wo))


def _ffn_kernel(x_ref, xp_ref, xn_ref, g_ref, wup_ref, cw_ref, cb_ref, wdn_ref, o_ref,
                hn_ref, hh_ref, acc_ref):
    i = pl.program_id(1)
    nt = pl.num_programs(1)
    n_chunks = wdn_ref.shape[0]
    x = x_ref[0]
    gain = g_ref[...]
    hn_ref[...] = _bf(_rms(x, gain))
    hh_ref[...] = _halo_rows(xp_ref, xn_ref, gain, i, nt)
    acc_ref[...] = jnp.zeros_like(acc_ref)

    def conv_half(idx):
        w = wup_ref[idx]
        u = jnp.dot(hn_ref[...], w, preferred_element_type=F32)
        uh = jnp.dot(hh_ref[...], w, preferred_element_type=F32)
        cw = cw_ref[idx]
        up = _shift_prev(u, uh[HALO - 1:HALO])
        un = _shift_next(u, uh[HALO:HALO + 1])
        return cb_ref[idx] + up * cw[0:1] + u * cw[1:2] + un * cw[2:3]

    def body(c, carry):
        gate = conv_half(c)
        val = conv_half(c + n_chunks)
        act = gate * _sigmoid(gate) * val
        acc_ref[...] += jnp.dot(_bf(act), wdn_ref[c], preferred_element_type=F32)
        return carry

    lax.fori_loop(0, n_chunks, body, 0)
    o_ref[0] = x + acc_ref[...]


def _conv_ffn(x, gain, w_up, conv_w, conv_b, w_down, tm):
    b, t, d = x.shape
    ff = w_down.shape[0]
    nc = ff // FF_CHUNK
    wup = _bf(w_up).reshape(d, 2 * nc, FF_CHUNK).transpose(1, 0, 2)
    cw = conv_w.reshape(CONV_WIDTH, 2 * nc, FF_CHUNK).transpose(1, 0, 2)
    cb = conv_b.reshape(2 * nc, 1, FF_CHUNK)
    wdn = _bf(w_down).reshape(nc, FF_CHUNK, d)
    main, prev, nxt = _row_specs(tm, d, t)
    return pl.pallas_call(
        _ffn_kernel,
        grid=(b, t // tm),
        in_specs=[main, prev, nxt,
                  _const_spec((1, d)),
                  _const_spec((2 * nc, d, FF_CHUNK)),
                  _const_spec((2 * nc, CONV_WIDTH, FF_CHUNK)),
                  _const_spec((2 * nc, 1, FF_CHUNK)),
                  _const_spec((nc, FF_CHUNK, d))],
        out_specs=pl.BlockSpec((1, tm, d), lambda i, j: (i, j, 0)),
        out_shape=jax.ShapeDtypeStruct((b, t, d), F32),
        scratch_shapes=[pltpu.VMEM((tm, d), BF16),
                        pltpu.VMEM((2 * HALO, d), BF16),
                        pltpu.VMEM((tm, d), F32)],
        compiler_params=_params(2),
        name="conv_ffn",
    )(x, x, x, gain.reshape(1, d), wup, cw, cb, wdn)


def _ab_in_kernel(x_ref, xp_ref, xn_ref, g_ref, win_ref, mup_ref, mun_ref,
                  w0_ref, w2_ref, a0_ref, a2_ref, g2_ref, kk_ref, ka_ref, rk_ref, bd_ref,
                  vg_ref, ws_ref, sb_ref,
                  r_o, v_o, kk_o, lw0_o, lw1_o, be0_o, be1_o, k0_o, k1_o, g_o, bv_o, b_o):
    i = pl.program_id(1)
    nt = pl.num_programs(1)
    w = RWKV_WIDTH
    x = x_ref[0]
    tm = x.shape[0]
    gain = g_ref[...]
    hn = _bf(_rms(x, gain))
    z = jnp.dot(hn, win_ref[...], preferred_element_type=F32)
    hh = _halo_rows(xp_ref, xn_ref, gain, i, nt)
    zh = jnp.dot(hh, win_ref[:, :RWKV_IN], preferred_element_type=F32)

    za = z[:, :RWKV_IN]
    zp = _shift_prev(za, zh[HALO - 1:HALO])
    zn = _shift_next(za, zh[HALO:HALO + 1])
    zs = za + mup_ref[...] * (zp - za) + mun_ref[...] * (zn - za)
    r = zs[:, 0:w]
    k = zs[:, w:2 * w]
    v = zs[:, 2 * w:3 * w]
    wl = jnp.tanh(zs[:, 3 * w:3 * w + LORA_W])
    al = zs[:, 3 * w + LORA_W:3 * w + 2 * LORA_W]
    gl = zs[:, 3 * w + 2 * LORA_W:]

    wpre = w0_ref[...] + _dot(wl, w2_ref[...])
    t_ = -wpre
    softplus = jnp.maximum(t_, 0.0) + jnp.log1p(jnp.exp(-jnp.abs(t_)))
    lw = -jnp.exp(-softplus - 0.5)
    a = _sigmoid(a0_ref[...] + _dot(al, a2_ref[...]))
    g = _dot(_sigmoid(gl), g2_ref[...])

    bd = bd_ref[...]
    kk0 = k * kk_ref[...]
    kk = kk0 * lax.rsqrt(_dot(kk0 * kk0, bd) + KK_EPS)
    ka = ka_ref[...]
    k0 = k * (1.0 + (a[:, :w] - 1.0) * ka)
    k1 = k * (1.0 + (a[:, w:] - 1.0) * ka)
    bonus = _dot(r * rk_ref[...] * (k0 + k1), bd)

    r_o[0] = r
    v_o[0] = v
    kk_o[0] = kk
    lw0_o[0] = lw[:, :w]
    lw1_o[0] = lw[:, w:]
    be0_o[0] = kk * a[:, :w]
    be1_o[0] = kk * a[:, w:]
    k0_o[0] = k0
    k1_o[0] = k1
    g_o[0] = g
    bv_o[0] = bonus * v

    zg = z[:, RWKV_IN:]
    zg = zg * (0.5 * (1.0 + jnp.tanh(math.sqrt(2.0 / math.pi) * (zg + 0.044715 * (zg * zg * zg)))))
    u = zg[:, :SGU_WIDTH]
    vn = _bf(_rms(zg[:, SGU_WIDTH:], vg_ref[...]))
    gd = SGU_WIDTH // SGU_GROUPS
    for n in range(tm // SGU_CHUNK):
        rows = slice(n * SGU_CHUNK, (n + 1) * SGU_CHUNK)
        for gi in range(SGU_GROUPS):
            cols = slice(gi * gd, (gi + 1) * gd)
            s = jnp.dot(ws_ref[gi], vn[rows, cols], preferred_element_type=F32) + sb_ref[gi]
            b_o[0, rows, cols] = u[rows, cols] * s


def _ab_in(x, gain, w_in, mu_prev, mu_next, w0, w2, a0, a2, g2, k_k, k_a, r_k, v_gain, w_s, s_b, tm):
    b, t, d = x.shape
    w = RWKV_WIDTH
    n_in = w_in.shape[1]
    half = LORA_W // 2

    def lora_cat(m):
        z = jnp.zeros((half, w), m.dtype)
        return _bf(jnp.concatenate([jnp.concatenate([m[0], z], axis=1),
                                    jnp.concatenate([z, m[1]], axis=1)], axis=0))

    head = jnp.arange(w) // HEAD_DIM
    bd = _bf(head[:, None] == head[None, :])
    sb = jnp.broadcast_to(s_b[:, :, None], (SGU_GROUPS, SGU_CHUNK, SGU_WIDTH // SGU_GROUPS))
    main, prev, nxt = _row_specs(tm, d, t)
    out = jax.ShapeDtypeStruct((b, t, w), F32)
    ospec = pl.BlockSpec((1, tm, w), lambda i, j: (i, j, 0))
    return pl.pallas_call(
        _ab_in_kernel,
        grid=(b, t // tm),
        in_specs=[main, prev, nxt,
                  _const_spec((1, d)),
                  _const_spec((d, n_in)),
                  _const_spec((1, RWKV_IN)), _const_spec((1, RWKV_IN)),
                  _const_spec((1, 2 * w)), _const_spec((LORA_W, 2 * w)),
                  _const_spec((1, 2 * w)), _const_spec((LORA_W, 2 * w)),
                  _const_spec((LORA_W, w)),
                  _const_spec((1, w)), _const_spec((1, w)), _const_spec((1, w)),
                  _const_spec((w, w)),
                  _const_spec((1, SGU_WIDTH)),
                  _const_spec((SGU_GROUPS, SGU_CHUNK, SGU_CHUNK)),
                  _const_spec((SGU_GROUPS, SGU_CHUNK, SGU_WIDTH // SGU_GROUPS))],
        out_specs=[ospec] * 12,
        out_shape=[out] * 12,
        compiler_params=_params(2),
        name="ab_in",
    )(x, x, x, gain.reshape(1, d), _bf(w_in), mu_prev.reshape(1, -1), mu_next.reshape(1, -1),
      w0.reshape(1, 2 * w), lora_cat(w2), a0.reshape(1, 2 * w), lora_cat(a2), _bf(g2),
      k_k.reshape(1, w), k_a.reshape(1, w), r_k.reshape(1, w), bd,
      v_gain.reshape(1, -1), _bf(w_s), sb)


def _stack_heads(x):
    even = (_iota(x.shape, 1) & (LANES - 1)) < HEAD_DIM
    return jnp.concatenate([jnp.where(even, x, 0.0), jnp.where(even, 0.0, x)], axis=0)


def _rwkv_prep(r, v, kk, lw, be, kd, reverse):
    c = r.shape[0]
    rr = _iota((c, LANES), 0)
    cc = _iota((c, LANES), 1) & (HEAD_DIM - 1)
    if reverse:
        strict, incl = cc > rr, cc >= rr
    else:
        strict, incl = cc < rr, cc <= rr
    tri = _bf(incl[:, :c])

    l1 = _bf(lw)
    l2 = _bf(lw - l1.astype(F32))
    l3 = _bf(lw - l1.astype(F32) - l2.astype(F32))
    cum = (jnp.dot(tri, l1, preferred_element_type=F32) + jnp.dot(tri, l2, preferred_element_type=F32)
           + jnp.dot(tri, l3, preferred_element_type=F32))
    tot = cum[0:1] if reverse else cum[c - 1:c]
    e_neg = jnp.exp(-cum)
    e_end = jnp.exp(tot - cum)
    return dict(ag=-kk * jnp.exp(cum - lw), rg=r * jnp.exp(cum), bi=be * e_neg, ki=kd * e_neg,
                bee=be * e_end, kee=kd * e_end, gam=jnp.exp(tot), v=v, strict=strict, incl=incl)


def _rwkv_local(chunks):
    c = chunks[0]["v"].shape[0]
    n_tiles = chunks[0]["v"].shape[1] // LANES
    r128 = _iota((LANES, LANES), 0)
    c128 = _iota((LANES, LANES), 1)
    same_head = (r128 < HEAD_DIM) == (c128 < HEAD_DIM)
    diag = r128 == c128
    ctx = [(ch, slice(p * LANES, (p + 1) * LANES)) for ch in chunks for p in range(n_tiles)]

    aa = [_dot_nt(jnp.concatenate([ch["ag"][:, sl], ch["rg"][:, sl]], axis=0),
                  jnp.concatenate([_stack_heads(ch["bi"][:, sl]), _stack_heads(ch["ki"][:, sl])], axis=0))
          for ch, sl in ctx]
    a_ab = [jnp.where(ch["strict"], m[:c, :LANES], 0.0) for (ch, _), m in zip(ctx, aa)]
    a_rb = [jnp.where(ch["incl"], m[c:, :LANES], 0.0) for (ch, _), m in zip(ctx, aa)]
    a_k = [jnp.concatenate([jnp.where(ch["strict"], m[:c, LANES:], 0.0),
                            jnp.where(ch["incl"], m[c:, LANES:], 0.0)], axis=0) for (ch, _), m in zip(ctx, aa)]
    xv = [_dot(m, _stack_heads(ch["v"][:, sl])) for (ch, sl), m in zip(ctx, a_k)]
    x = [jnp.concatenate([ch["ag"][:, sl], m[:c]], axis=1) for (ch, sl), m in zip(ctx, xv)]
    pw = a_ab
    n_round = int(math.log2(c))
    for j in range(n_round):
        x = [xi + _dot(pi, _stack_heads(xi)) for xi, pi in zip(x, pw)]
        if j < n_round - 1:
            pw = [_dot(pi, jnp.where(same_head, jnp.concatenate([pi, pi], axis=0), 0.0)) for pi in pw]
    ry = [jnp.concatenate([ch["rg"][:, sl], m[c:]], axis=1) + _dot(ai, _stack_heads(xi))
          for (ch, sl), m, ai, xi in zip(ctx, xv, a_rb, x)]
    mg = [_dot(jnp.concatenate([ch["bee"][:, sl], ch["kee"][:, sl]], axis=0).T,
               jnp.concatenate([xi, jnp.concatenate([jnp.zeros_like(ch["v"][:, sl]), ch["v"][:, sl]], axis=1)],
                               axis=0))
          for (ch, sl), xi in zip(ctx, x)]
    out = []
    for (ch, sl), ryi, mgi in zip(ctx, ry, mg):
        gam_col = jnp.sum(jnp.where(diag, ch["gam"][:, sl], 0.0), axis=1, keepdims=True)
        out.append((ryi[:, :LANES], ryi[:, LANES:], jnp.where(same_head, mgi[:, :LANES], 0.0),
                    jnp.where(same_head, mgi[:, LANES:], 0.0), gam_col))
    return [out[i * n_tiles:(i + 1) * n_tiles] for i in range(len(chunks))]


def _rwkv_kernel(rf, vf, kkf, lwf, bef, kf, rb, vb, kkb, lwb, beb, kb, yf_o, yb_o, zf_ref, zb_ref):
    @pl.when(pl.program_id(1) == 0)
    def _():
        zf_ref[...] = jnp.zeros_like(zf_ref)
        zb_ref[...] = jnp.zeros_like(zb_ref)

    c = RWKV_CHUNK
    n = rf.shape[1] // c
    n_tiles = rf.shape[2] // LANES
    chunks = []
    for refs, reverse in (((rf, vf, kkf, lwf, bef, kf), False), ((rb, vb, kkb, lwb, beb, kb), True)):
        for ci in range(n):
            chunks.append(_rwkv_prep(*[ref[0, ci * c:(ci + 1) * c, :] for ref in refs], reverse))
    local = _rwkv_local(chunks)

    for d, (z_ref, y_o) in enumerate(((zf_ref, yf_o), (zb_ref, yb_o))):
        order = range(n) if d == 0 else range(n - 1, -1, -1)
        zs = [z_ref[p] for p in range(n_tiles)]
        for ci in order:
            ys = []
            for p in range(n_tiles):
                rt, yl, mt, gt, gam_col = local[d * n + ci][p]
                ys.append(_dot(rt, zs[p]) + yl)
                zs[p] = gam_col * zs[p] + _dot(mt, zs[p]) + gt
            y_o[0, ci * c:(ci + 1) * c, :] = jnp.concatenate(ys, axis=1)
        for p in range(n_tiles):
            z_ref[p] = zs[p]


def _rwkv_scan(r, v, kk, lw0, lw1, be0, be1, k0, k1):
    b, t, w = r.shape
    c = RWKV_CHUNK * RWKV_STEP_CHUNKS
    nc = t // c
    fwd = pl.BlockSpec((1, c, w), lambda i, j: (i, j, 0))
    bwd = pl.BlockSpec((1, c, w), lambda i, j: (i, nc - 1 - j, 0))
    out = jax.ShapeDtypeStruct((b, t, w), F32)
    zshape = pltpu.VMEM((w // LANES, LANES, LANES), F32)
    return pl.pallas_call(
        _rwkv_kernel,
        grid=(b, nc),
        in_specs=[fwd] * 6 + [bwd] * 6,
        out_specs=[fwd, bwd],
        out_shape=[out, out],
        scratch_shapes=[zshape, zshape],
        compiler_params=_params(2),
        name="rwkv_scan",
    )(r, v, kk, lw0, be0, k0, r, v, kk, lw1, be1, k1)


def _ab_out_kernel(x_ref, yf_ref, yb_ref, bv_ref, g_ref, b_ref, og_ref, bd_ref, wo_ref, o_ref):
    y = yf_ref[0] + yb_ref[0]
    ms = _dot(y * y, bd_ref[...]) * (1.0 / HEAD_DIM)
    yn = y * lax.rsqrt(ms + NORM_EPS) * og_ref[...]
    a_out = (yn + bv_ref[0]) * g_ref[0]
    cat = jnp.concatenate([a_out, b_ref[0]], axis=1)
    o_ref[0] = x_ref[0] + jnp.dot(_bf(cat), wo_ref[...], preferred_element_type=F32)


def _ab_out(x, yf, yb, bv, g, b_out, out_gain, w_out, tm):
    b, t, d = x.shape
    w = RWKV_WIDTH
    head = jnp.arange(w) // HEAD_DIM
    bd = _bf(head[:, None] == head[None, :])
    xs = pl.BlockSpec((1, tm, d), lambda i, j: (i, j, 0))
    ws = pl.BlockSpec((1, tm, w), lambda i, j: (i, j, 0))
    return pl.pallas_call(
        _ab_out_kernel,
        grid=(b, t // tm),
        in_specs=[xs, ws, ws, ws, ws, ws, _const_spec((1, w)), _const_spec((w, w)),
                  _const_spec((w_out.shape[0], d))],
        out_specs=xs,
        out_shape=jax.ShapeDtypeStruct((b, t, d), F32),
        compiler_params=_params(2),
        name="ab_out",
    )(x, yf, yb, bv, g, b_out, out_gain.reshape(1, w), bd, _bf(w_out))


def _diff_in_kernel(x_ref, g_ref, w_ref, qg_ref, kg_ref, bd_ref, q_o, k_o, v_o):
    x = x_ref[0]
    d = x.shape[-1]
    hn = _bf(_rms(x, g_ref[...]))
    z = jnp.dot(hn, w_ref[...], preferred_element_type=F32)
    bd = bd_ref[...]
    half = bd.shape[0]

    def qk_norm(y, gain, scale):
        parts = []
        for j in range(d // half):
            yj = y[:, j * half:(j + 1) * half]
            ms = _dot(yj * yj, bd) * (1.0 / HEAD_DIM)
            parts.append(yj * lax.rsqrt(ms + NORM_EPS) * gain * scale)
        return _bf(jnp.concatenate(parts, axis=1))

    q_o[0] = qk_norm(z[:, :d], qg_ref[...], 1.0 / math.sqrt(HEAD_DIM))
    k_o[0] = qk_norm(z[:, d:2 * d], kg_ref[...], 1.0)
    v_o[0] = _bf(z[:, 2 * d:])


def _diff_in(x, gain, w_in, q_gain, k_gain, tm):
    b, t, d = x.shape
    half = 512
    head = jnp.arange(half) // HEAD_DIM
    bd = _bf(head[:, None] == head[None, :])
    xs = pl.BlockSpec((1, tm, d), lambda i, j: (i, j, 0))
    out = jax.ShapeDtypeStruct((b, t, d), BF16)
    return pl.pallas_call(
        _diff_in_kernel,
        grid=(b, t // tm),
        in_specs=[xs, _const_spec((1, d)), _const_spec((d, 3 * d)),
                  _const_spec((1, half)), _const_spec((1, half)), _const_spec((half, half))],
        out_specs=[xs, xs, xs],
        out_shape=[out, out, out],
        compiler_params=_params(2),
        name="diff_in",
    )(x, gain.reshape(1, d), _bf(w_in), jnp.tile(q_gain, half // HEAD_DIM).reshape(1, half),
      jnp.tile(k_gain, half // HEAD_DIM).reshape(1, half), bd)


def _diff_attn_kernel(q_ref, k_ref, v_ref, slope_ref, lam_ref, og_ref, o_ref, *, lambda_init):
    qt = pl.program_id(2)
    q = q_ref[0]
    k = k_ref[0]
    tq, t = q.shape[0], k.shape[0]
    first = _iota(q.shape, 1) < HEAD_DIM
    zero = jnp.zeros_like(q)
    dist = jnp.abs((qt * tq + _iota((tq, t), 0) - _iota((tq, t), 1)).astype(F32))
    bias = dist * (-slope_ref[0][:, 0:1])

    def probs(qc):
        logits = lax.dot_general(qc, k, (((1,), (1,)), ((), ())), preferred_element_type=F32) + bias
        p = jnp.exp(logits - jnp.max(logits, axis=-1, keepdims=True))
        return p, jnp.sum(p, axis=-1, keepdims=True)

    p0, s0 = probs(jnp.where(first, q, zero))
    p1, s1 = probs(jnp.where(first, zero, q))
    lv = lam_ref[...]
    lam = (jnp.exp(jnp.sum(lv[0:1] * lv[1:2], axis=-1, keepdims=True))
           - jnp.exp(jnp.sum(lv[2:3] * lv[3:4], axis=-1, keepdims=True)) + lambda_init)
    attn = p0 * (1.0 / s0) - p1 * (lam / s1)
    o = jnp.dot(_bf(attn), v_ref[0], preferred_element_type=F32)
    o_ref[0] = _rms(o, og_ref[...]) * (1.0 - lambda_init)


def _diff_attention(q, k, v, lam_vecs, out_gain, lambda_init, tq):
    b, t, d = q.shape
    nh = d // DIFF_V_DIM
    slopes = 2.0 ** (-8.0 * jnp.arange(1, nh + 1, dtype=F32) / nh)
    slopes = jnp.broadcast_to(slopes[:, None, None], (nh, 1, LANES))
    return pl.pallas_call(
        functools.partial(_diff_attn_kernel, lambda_init=lambda_init),
        grid=(b, nh, t // tq),
        in_specs=[pl.BlockSpec((1, tq, DIFF_V_DIM), lambda i, h, j: (i, j, h)),
                  pl.BlockSpec((1, t, DIFF_V_DIM), lambda i, h, j: (i, 0, h)),
                  pl.BlockSpec((1, t, DIFF_V_DIM), lambda i, h, j: (i, 0, h)),
                  pl.BlockSpec((1, 1, LANES), lambda i, h, j: (h, 0, 0)),
                  _const_spec((4, HEAD_DIM)),
                  _const_spec((1, DIFF_V_DIM))],
        out_specs=pl.BlockSpec((1, tq, DIFF_V_DIM), lambda i, h, j: (i, j, h)),
        out_shape=jax.ShapeDtypeStruct((b, t, d), F32),
        compiler_params=_params(3),
        name="diff_attn",
    )(q, k, v, slopes, lam_vecs, out_gain.reshape(1, DIFF_V_DIM))


def _proj_residual_kernel(x_ref, a_ref, w_ref, o_ref):
    o_ref[0] = x_ref[0] + jnp.dot(_bf(a_ref[0]), w_ref[...], preferred_element_type=F32)


def _proj_residual(x, a, w, tm):
    b, t, d = x.shape
    n = a.shape[-1]
    return pl.pallas_call(
        _proj_residual_kernel,
        grid=(b, t // tm),
        in_specs=[pl.BlockSpec((1, tm, d), lambda i, j: (i, j, 0)),
                  pl.BlockSpec((1, tm, n), lambda i, j: (i, j, 0)),
                  _const_spec((n, d))],
        out_specs=pl.BlockSpec((1, tm, d), lambda i, j: (i, j, 0)),
        out_shape=jax.ShapeDtypeStruct((b, t, d), F32),
        compiler_params=_params(2),
        name="proj_residual",
    )(x, a, _bf(w))


def kernel(x, mem, mem_norm, mix_norm, cross_norm, ffn_norm, ab_w_in, ab_shift_prev, ab_shift_next, rwkv_w0, rwkv_w2, rwkv_a0, rwkv_a2, rwkv_g2, rwkv_k_k, rwkv_k_a, rwkv_r_k, rwkv_out_gain, sgu_v_gain, sgu_w_s, sgu_b, ab_w_out, diff_w_in, diff_q_gain, diff_k_gain, diff_lambda_q1, diff_lambda_k1, diff_lambda_q2, diff_lambda_k2, diff_out_gain, diff_w_out, cross_wq, cross_wkv, cross_q_gain, cross_k_gain, cross_wo, ffn_w_up, ffn_conv_w, ffn_conv_b, ffn_w_down):
    b, t, d = x.shape
    depth = mix_norm.shape[0]
    tm = min(512, t)
    tq = min(256, t)
    assert t % tm == 0 and t % (RWKV_CHUNK * RWKV_STEP_CHUNKS) == 0 and tm % SGU_CHUNK == 0

    k_all, v_all = _mem_kv(mem, mem_norm, cross_wkv, cross_k_gain)
    for l in range(depth):
        if l % 2 == 0:
            e = l // 2
            (r, v, kk, lw0, lw1, be0, be1, k0, k1, g, bv, b_out) = _ab_in(
                x, mix_norm[l], ab_w_in[e], ab_shift_prev[e], ab_shift_next[e],
                rwkv_w0[e], rwkv_w2[e], rwkv_a0[e], rwkv_a2[e], rwkv_g2[e],
                rwkv_k_k[e], rwkv_k_a[e], rwkv_r_k[e], sgu_v_gain[e], sgu_w_s[e], sgu_b[e], tm)
            yf, yb = _rwkv_scan(r, v, kk, lw0, lw1, be0, be1, k0, k1)
            x = _ab_out(x, yf, yb, bv, g, b_out, rwkv_out_gain[e], ab_w_out[e], tm)
        else:
            o = l // 2
            lambda_init = 0.8 - 0.6 * math.exp(-0.3 * l)
            q, k, v = _diff_in(x, mix_norm[l], diff_w_in[o], diff_q_gain[o], diff_k_gain[o], tm)
            lam_vecs = jnp.stack([diff_lambda_q1[o], diff_lambda_k1[o], diff_lambda_q2[o], diff_lambda_k2[o]])
            c_out = _diff_attention(q, k, v, lam_vecs, diff_out_gain[o], lambda_init, tq)
            x = _proj_residual(x, c_out, diff_w_out[o], tm)
        x = _cross_attention(x, cross_norm[l], cross_wq[l], cross_q_gain[l], k_all, v_all, cross_wo[l], l, tm)
        x = _conv_ffn(x, ffn_norm[l], ffn_w_up[l], ffn_conv_w[l], ffn_conv_b[l], ffn_w_down[l], tm)
    return x
```

```python
import functools
import math

import jax
import jax.numpy as jnp
from jax import lax
from jax.experimental import pallas as pl
from jax.experimental.pallas import tpu as pltpu

NORM_EPS = 1e-6
KK_EPS = 1e-12
RWKV_HEADS = 8
HEAD_DIM = 64
RWKV_WIDTH = RWKV_HEADS * HEAD_DIM
LORA_W = 128
RWKV_IN = 3 * RWKV_WIDTH + 3 * LORA_W
SGU_WIDTH = 512
SGU_GROUPS = 4
SGU_CHUNK = 128
DIFF_HEADS = 8
DIFF_V_DIM = 2 * HEAD_DIM
CROSS_HEADS = 4
CONV_WIDTH = 3
RWKV_CHUNK = 64
RWKV_STEP_CHUNKS = 2
LANES = 128
HALO = 8
FF_CHUNK = 256
KEY_BLOCK = 128
QUERY_BLOCK = 256
ONES_ROWS = 16
VMEM_LIMIT = 56 * 1024 * 1024
LOG2E = math.log2(math.e)

F32 = jnp.float32
BF16 = jnp.bfloat16


def _bf(x):
    return x.astype(BF16)


def _dot(a, b):
    return jnp.dot(_bf(a), _bf(b), preferred_element_type=F32)


def _dot_nt(a, b):
    return lax.dot_general(_bf(a), _bf(b), (((1,), (1,)), ((), ())), preferred_element_type=F32)


def _rms(x, gain):
    return x * lax.rsqrt(jnp.mean(x * x, axis=-1, keepdims=True) + NORM_EPS) * gain


def _sigmoid(x):
    return 1.0 / (1.0 + jnp.exp(-x))


def _iota(shape, dim):
    return lax.broadcasted_iota(jnp.int32, shape, dim)


def _shift_prev(u, halo_row):
    rolled = pltpu.roll(u, 1, 0)
    return jnp.where(_iota(u.shape, 0) == 0, halo_row, rolled)


def _shift_next(u, halo_row):
    n = u.shape[0]
    rolled = pltpu.roll(u, n - 1, 0)
    return jnp.where(_iota(u.shape, 0) == n - 1, halo_row, rolled)


def _halo_rows(xp_ref, xn_ref, gain, i, nt):
    xh = jnp.concatenate([xp_ref[0], xn_ref[0]], axis=0)
    hh = _rms(xh, gain)
    row = _iota(hh.shape, 0)
    has_prev = jnp.where(i > 0, 1.0, 0.0)
    has_next = jnp.where(i < nt - 1, 1.0, 0.0)
    return _bf(hh * jnp.where(row < HALO, has_prev, has_next))


def _const_spec(shape):
    zeros = (0,) * len(shape)
    return pl.BlockSpec(shape, lambda *_: zeros, pipeline_mode=pl.Buffered(1))


def _params(n_axes):
    return pltpu.CompilerParams(dimension_semantics=("arbitrary",) * n_axes,
                                vmem_limit_bytes=VMEM_LIMIT)


def _row_specs(tm, d, t):
    nb = tm // HALO
    last = t // HALO - 1
    main = pl.BlockSpec((1, tm, d), lambda b, i: (b, i, 0))
    prev = pl.BlockSpec((1, HALO, d), lambda b, i: (b, jnp.maximum(i * nb - 1, 0), 0))
    nxt = pl.BlockSpec((1, HALO, d), lambda b, i: (b, jnp.minimum((i + 1) * nb, last), 0))
    return main, prev, nxt


def _mem_kv_kernel(mem_ref, mnorm_ref, wkv_ref, kg_ref, k_ref, v_ref):
    d = mem_ref.shape[-1]
    hd = d // CROSS_HEADS
    mn = _rms(mem_ref[0], mnorm_ref[...])
    kv = jnp.dot(_bf(mn), wkv_ref[0], preferred_element_type=F32)
    for h in range(CROSS_HEADS):
        kh = _rms(kv[:, h * hd:(h + 1) * hd], kg_ref[0]) * (1.0 / math.sqrt(hd))
        k_ref[0, 0, :, h * hd:(h + 1) * hd] = _bf(kh)
    v_ref[0, 0] = _bf(kv[:, d:])


def _mem_kv(mem, mem_norm, wkv, k_gain):
    b, m, d = mem.shape
    nl = wkv.shape[0]
    out = jax.ShapeDtypeStruct((nl, b, m, d), BF16)
    return pl.pallas_call(
        _mem_kv_kernel,
        grid=(nl, b),
        in_specs=[pl.BlockSpec((1, m, d), lambda l, i: (i, 0, 0)),
                  pl.BlockSpec((1, d), lambda l, i: (0, 0)),
                  pl.BlockSpec((1, d, 2 * d), lambda l, i: (l, 0, 0)),
                  pl.BlockSpec((1, 1, d // CROSS_HEADS), lambda l, i: (l, 0, 0))],
        out_specs=[pl.BlockSpec((1, 1, m, d), lambda l, i: (l, i, 0, 0)),
                   pl.BlockSpec((1, 1, m, d), lambda l, i: (l, i, 0, 0))],
        out_shape=[out, out],
        compiler_params=_params(2),
        name="mem_kv",
    )(mem, mem_norm.reshape(1, d), _bf(wkv), k_gain.reshape(nl, 1, -1))


def _cross_kernel(x_ref, g_ref, wq_ref, qg_ref, k_ref, v_ref, wo_ref, o_ref):
    x = x_ref[0]
    d = x.shape[-1]
    hd = d // CROSS_HEADS
    hn = _bf(_rms(x, g_ref[...]))
    q = jnp.dot(hn, wq_ref[...], preferred_element_type=F32)
    outs = []
    for h in range(CROSS_HEADS):
        sl = slice(h * hd, (h + 1) * hd)
        qh = _rms(q[:, sl], qg_ref[...])
        logits = _dot_nt(qh, k_ref[0, 0, :, sl])
        p = jnp.exp(logits - jnp.max(logits, axis=-1, keepdims=True))
        s = jnp.sum(p, axis=-1, keepdims=True)
        outs.append(jnp.dot(_bf(p), v_ref[0, 0, :, sl], preferred_element_type=F32) / s)
    o = jnp.concatenate(outs, axis=1)
    o_ref[0] = x + jnp.dot(_bf(o), wo_ref[...], preferred_element_type=F32)


def _cross_attention(x, gain, wq, q_gain, k_all, v_all, wo, layer, tq):
    b, t, d = x.shape
    m = k_all.shape[2]
    hd = d // CROSS_HEADS
    return pl.pallas_call(
        _cross_kernel,
        grid=(b, t // tq),
        in_specs=[pl.BlockSpec((1, tq, d), lambda i, j: (i, j, 0)),
                  _const_spec((1, d)),
                  _const_spec((d, d)),
                  _const_spec((1, hd)),
                  pl.BlockSpec((1, 1, m, d), lambda i, j: (layer, i, 0, 0)),
                  pl.BlockSpec((1, 1, m, d), lambda i, j: (layer, i, 0, 0)),
                  _const_spec((d, d))],
        out_specs=pl.BlockSpec((1, tq, d), lambda i, j: (i, j, 0)),
        out_shape=jax.ShapeDtypeStruct((b, t, d), F32),
        compiler_params=_params(2),
        name=f"cross_attn_{layer}",
    )(x, gain.reshape(1, d), _bf(wq), q_gain.reshape(1, hd), k_all, v_all, _bf(wo))


def _ffn_kernel(x_ref, xp_ref, xn_ref, g_ref, wup_ref, cw_ref, cb_ref, wdn_ref, o_ref,
                hn_ref, hh_ref, acc_ref):
    i = pl.program_id(1)
    nt = pl.num_programs(1)
    n_chunks = wdn_ref.shape[0]
    x = x_ref[0]
    gain = g_ref[...]
    hn_ref[...] = _bf(_rms(x, gain))
    hh_ref[...] = _halo_rows(xp_ref, xn_ref, gain, i, nt)
    acc_ref[...] = jnp.zeros_like(acc_ref)

    def up_proj(idx):
        w = wup_ref[idx]
        return (jnp.dot(hn_ref[...], w, preferred_element_type=F32),
                jnp.dot(hh_ref[...], w, preferred_element_type=F32))

    def conv(idx, u, uh):
        cw = cw_ref[idx]
        up = _shift_prev(u, uh[HALO - 1:HALO])
        un = _shift_next(u, uh[HALO:HALO + 1])
        return cb_ref[idx] + up * cw[0:1] + u * cw[1:2] + un * cw[2:3]

    ups = [up_proj(0), up_proj(n_chunks)]
    for c in range(n_chunks):
        nxt = [up_proj(c + 1), up_proj(c + 1 + n_chunks)] if c + 1 < n_chunks else None
        gate = conv(c, *ups[0])
        val = conv(c + n_chunks, *ups[1])
        act = gate * _sigmoid(gate) * val
        acc_ref[...] += jnp.dot(_bf(act), wdn_ref[c], preferred_element_type=F32)
        ups = nxt
    o_ref[0] = x + acc_ref[...]


def _conv_ffn(x, gain, w_up, conv_w, conv_b, w_down, tm):
    b, t, d = x.shape
    ff = w_down.shape[0]
    nc = ff // FF_CHUNK
    wup = _bf(w_up).reshape(d, 2 * nc, FF_CHUNK).transpose(1, 0, 2)
    cw = conv_w.reshape(CONV_WIDTH, 2 * nc, FF_CHUNK).transpose(1, 0, 2)
    cb = conv_b.reshape(2 * nc, 1, FF_CHUNK)
    wdn = _bf(w_down).reshape(nc, FF_CHUNK, d)
    main, prev, nxt = _row_specs(tm, d, t)
    return pl.pallas_call(
        _ffn_kernel,
        grid=(b, t // tm),
        in_specs=[main, prev, nxt,
                  _const_spec((1, d)),
                  _const_spec((2 * nc, d, FF_CHUNK)),
                  _const_spec((2 * nc, CONV_WIDTH, FF_CHUNK)),
                  _const_spec((2 * nc, 1, FF_CHUNK)),
                  _const_spec((nc, FF_CHUNK, d))],
        out_specs=pl.BlockSpec((1, tm, d), lambda i, j: (i, j, 0)),
        out_shape=jax.ShapeDtypeStruct((b, t, d), F32),
        scratch_shapes=[pltpu.VMEM((tm, d), BF16),
                        pltpu.VMEM((2 * HALO, d), BF16),
                        pltpu.VMEM((tm, d), F32)],
        compiler_params=_params(2),
        name="conv_ffn",
    )(x, x, x, gain.reshape(1, d), wup, cw, cb, wdn)


def _ab_in_kernel(x_ref, xp_ref, xn_ref, g_ref, win_ref, mup_ref, mun_ref,
                  w0_ref, w2_ref, a0_ref, a2_ref, g2_ref, kk_ref, ka_ref, rk_ref, bd_ref,
                  vg_ref, ws_ref, sb_ref,
                  r_o, v_o, kk_o, lw0_o, lw1_o, be0_o, be1_o, k0_o, k1_o, g_o, bv_o, b_o):
    i = pl.program_id(1)
    nt = pl.num_programs(1)
    w = RWKV_WIDTH
    x = x_ref[0]
    tm = x.shape[0]
    gain = g_ref[...]
    hn = _bf(_rms(x, gain))
    z = jnp.dot(hn, win_ref[...], preferred_element_type=F32)
    hh = _halo_rows(xp_ref, xn_ref, gain, i, nt)
    zh = jnp.dot(hh, win_ref[:, :RWKV_IN], preferred_element_type=F32)

    za = z[:, :RWKV_IN]
    zp = _shift_prev(za, zh[HALO - 1:HALO])
    zn = _shift_next(za, zh[HALO:HALO + 1])
    zs = za + mup_ref[...] * (zp - za) + mun_ref[...] * (zn - za)
    r = zs[:, 0:w]
    k = zs[:, w:2 * w]
    v = zs[:, 2 * w:3 * w]
    wl = jnp.tanh(zs[:, 3 * w:3 * w + LORA_W])
    al = zs[:, 3 * w + LORA_W:3 * w + 2 * LORA_W]
    gl = zs[:, 3 * w + 2 * LORA_W:]

    wpre = w0_ref[...] + _dot(wl, w2_ref[...])
    t_ = -wpre
    softplus = jnp.maximum(t_, 0.0) + jnp.log1p(jnp.exp(-jnp.abs(t_)))
    lw = -jnp.exp(-softplus - 0.5)
    a = _sigmoid(a0_ref[...] + _dot(al, a2_ref[...]))
    g = _dot(_sigmoid(gl), g2_ref[...])

    bd = bd_ref[...]
    kk0 = k * kk_ref[...]
    kk = kk0 * lax.rsqrt(_dot(kk0 * kk0, bd) + KK_EPS)
    ka = ka_ref[...]
    k0 = k * (1.0 + (a[:, :w] - 1.0) * ka)
    k1 = k * (1.0 + (a[:, w:] - 1.0) * ka)
    bonus = _dot(r * rk_ref[...] * (k0 + k1), bd)

    r_o[0] = r
    v_o[0] = v
    kk_o[0] = kk
    lw0_o[0] = lw[:, :w]
    lw1_o[0] = lw[:, w:]
    be0_o[0] = kk * a[:, :w]
    be1_o[0] = kk * a[:, w:]
    k0_o[0] = k0
    k1_o[0] = k1
    g_o[0] = g
    bv_o[0] = bonus * v

    zg = z[:, RWKV_IN:]
    zg = zg * (0.5 * (1.0 + jnp.tanh(math.sqrt(2.0 / math.pi) * (zg + 0.044715 * (zg * zg * zg)))))
    u = zg[:, :SGU_WIDTH]
    vn = _bf(_rms(zg[:, SGU_WIDTH:], vg_ref[...]))
    gd = SGU_WIDTH // SGU_GROUPS
    for n in range(tm // SGU_CHUNK):
        rows = slice(n * SGU_CHUNK, (n + 1) * SGU_CHUNK)
        for gi in range(SGU_GROUPS):
            cols = slice(gi * gd, (gi + 1) * gd)
            s = jnp.dot(ws_ref[gi], vn[rows, cols], preferred_element_type=F32) + sb_ref[gi]
            b_o[0, rows, cols] = u[rows, cols] * s


def _ab_in(x, gain, w_in, mu_prev, mu_next, w0, w2, a0, a2, g2, k_k, k_a, r_k, v_gain, w_s, s_b, tm):
    b, t, d = x.shape
    w = RWKV_WIDTH
    n_in = w_in.shape[1]
    half = LORA_W // 2

    def lora_cat(m):
        z = jnp.zeros((half, w), m.dtype)
        return _bf(jnp.concatenate([jnp.concatenate([m[0], z], axis=1),
                                    jnp.concatenate([z, m[1]], axis=1)], axis=0))

    head = jnp.arange(w) // HEAD_DIM
    bd = _bf(head[:, None] == head[None, :])
    sb = jnp.broadcast_to(s_b[:, :, None], (SGU_GROUPS, SGU_CHUNK, SGU_WIDTH // SGU_GROUPS))
    main, prev, nxt = _row_specs(tm, d, t)
    out = jax.ShapeDtypeStruct((b, t, w), F32)
    ospec = pl.BlockSpec((1, tm, w), lambda i, j: (i, j, 0))
    return pl.pallas_call(
        _ab_in_kernel,
        grid=(b, t // tm),
        in_specs=[main, prev, nxt,
                  _const_spec((1, d)),
                  _const_spec((d, n_in)),
                  _const_spec((1, RWKV_IN)), _const_spec((1, RWKV_IN)),
                  _const_spec((1, 2 * w)), _const_spec((LORA_W, 2 * w)),
                  _const_spec((1, 2 * w)), _const_spec((LORA_W, 2 * w)),
                  _const_spec((LORA_W, w)),
                  _const_spec((1, w)), _const_spec((1, w)), _const_spec((1, w)),
                  _const_spec((w, w)),
                  _const_spec((1, SGU_WIDTH)),
                  _const_spec((SGU_GROUPS, SGU_CHUNK, SGU_CHUNK)),
                  _const_spec((SGU_GROUPS, SGU_CHUNK, SGU_WIDTH // SGU_GROUPS))],
        out_specs=[ospec] * 12,
        out_shape=[out] * 12,
        compiler_params=_params(2),
        name="ab_in",
    )(x, x, x, gain.reshape(1, d), _bf(w_in), mu_prev.reshape(1, -1), mu_next.reshape(1, -1),
      w0.reshape(1, 2 * w), lora_cat(w2), a0.reshape(1, 2 * w), lora_cat(a2), _bf(g2),
      k_k.reshape(1, w), k_a.reshape(1, w), r_k.reshape(1, w), bd,
      v_gain.reshape(1, -1), _bf(w_s), sb)


def _stack_heads(x):
    even = (_iota(x.shape, 1) & (LANES - 1)) < HEAD_DIM
    return jnp.concatenate([jnp.where(even, x, 0.0), jnp.where(even, 0.0, x)], axis=0)


def _rwkv_prep(r, v, kk, lw, be, kd, reverse):
    c = r.shape[0]
    rr = _iota((c, LANES), 0)
    cc = _iota((c, LANES), 1) & (HEAD_DIM - 1)
    if reverse:
        strict, incl = cc > rr, cc >= rr
    else:
        strict, incl = cc < rr, cc <= rr
    tri = _bf(incl[:, :c])

    l1 = _bf(lw)
    l2 = _bf(lw - l1.astype(F32))
    l3 = _bf(lw - l1.astype(F32) - l2.astype(F32))
    cum = (jnp.dot(tri, l1, preferred_element_type=F32) + jnp.dot(tri, l2, preferred_element_type=F32)
           + jnp.dot(tri, l3, preferred_element_type=F32))
    tot = cum[0:1] if reverse else cum[c - 1:c]
    e_neg = jnp.exp(-cum)
    e_end = jnp.exp(tot - cum)
    return dict(ag=-kk * jnp.exp(cum - lw), rg=r * jnp.exp(cum), bi=be * e_neg, ki=kd * e_neg,
                bee=be * e_end, kee=kd * e_end, gam=jnp.exp(tot), v=v, strict=strict, incl=incl)


def _rwkv_local(chunks):
    c = chunks[0]["v"].shape[0]
    n_tiles = chunks[0]["v"].shape[1] // LANES
    r128 = _iota((LANES, LANES), 0)
    c128 = _iota((LANES, LANES), 1)
    same_head = (r128 < HEAD_DIM) == (c128 < HEAD_DIM)
    diag = r128 == c128
    ctx = [(ch, slice(p * LANES, (p + 1) * LANES)) for ch in chunks for p in range(n_tiles)]

    aa = [_dot_nt(jnp.concatenate([ch["ag"][:, sl], ch["rg"][:, sl]], axis=0),
                  jnp.concatenate([_stack_heads(ch["bi"][:, sl]), _stack_heads(ch["ki"][:, sl])], axis=0))
          for ch, sl in ctx]
    a_ab = [jnp.where(ch["strict"], m[:c, :LANES], 0.0) for (ch, _), m in zip(ctx, aa)]
    a_rb = [jnp.where(ch["incl"], m[c:, :LANES], 0.0) for (ch, _), m in zip(ctx, aa)]
    a_k = [jnp.concatenate([jnp.where(ch["strict"], m[:c, LANES:], 0.0),
                            jnp.where(ch["incl"], m[c:, LANES:], 0.0)], axis=0) for (ch, _), m in zip(ctx, aa)]
    xv = [_dot(m, _stack_heads(ch["v"][:, sl])) for (ch, sl), m in zip(ctx, a_k)]
    x = [jnp.concatenate([ch["ag"][:, sl], m[:c]], axis=1) for (ch, sl), m in zip(ctx, xv)]
    pw = a_ab
    n_round = int(math.log2(c))
    for j in range(n_round):
        x = [xi + _dot(pi, _stack_heads(xi)) for xi, pi in zip(x, pw)]
        if j < n_round - 1:
            pw = [_dot(pi, jnp.where(same_head, jnp.concatenate([pi, pi], axis=0), 0.0)) for pi in pw]
    ry = [jnp.concatenate([ch["rg"][:, sl], m[c:]], axis=1) + _dot(ai, _stack_heads(xi))
          for (ch, sl), m, ai, xi in zip(ctx, xv, a_rb, x)]
    mg = [_dot(jnp.concatenate([ch["bee"][:, sl], ch["kee"][:, sl]], axis=0).T,
               jnp.concatenate([xi, jnp.concatenate([jnp.zeros_like(ch["v"][:, sl]), ch["v"][:, sl]], axis=1)],
                               axis=0))
          for (ch, sl), xi in zip(ctx, x)]
    out = []
    for (ch, sl), ryi, mgi in zip(ctx, ry, mg):
        gam_col = jnp.sum(jnp.where(diag, ch["gam"][:, sl], 0.0), axis=1, keepdims=True)
        out.append((ryi[:, :LANES], ryi[:, LANES:], jnp.where(same_head, mgi[:, :LANES], 0.0),
                    jnp.where(same_head, mgi[:, LANES:], 0.0), gam_col))
    return [out[i * n_tiles:(i + 1) * n_tiles] for i in range(len(chunks))]


def _rwkv_kernel(rf, vf, kkf, lwf, bef, kf, rb, vb, kkb, lwb, beb, kb, yf_o, yb_o, zf_ref, zb_ref):
    @pl.when(pl.program_id(1) == 0)
    def _():
        zf_ref[...] = jnp.zeros_like(zf_ref)
        zb_ref[...] = jnp.zeros_like(zb_ref)

    c = RWKV_CHUNK
    n = rf.shape[1] // c
    n_tiles = rf.shape[2] // LANES
    chunks = []
    for refs, reverse in (((rf, vf, kkf, lwf, bef, kf), False), ((rb, vb, kkb, lwb, beb, kb), True)):
        for ci in range(n):
            chunks.append(_rwkv_prep(*[ref[0, ci * c:(ci + 1) * c, :] for ref in refs], reverse))
    local = _rwkv_local(chunks)

    for d, (z_ref, y_o) in enumerate(((zf_ref, yf_o), (zb_ref, yb_o))):
        order = range(n) if d == 0 else range(n - 1, -1, -1)
        zs = [z_ref[p] for p in range(n_tiles)]
        for ci in order:
            ys = []
            for p in range(n_tiles):
                rt, yl, mt, gt, gam_col = local[d * n + ci][p]
                ys.append(_dot(rt, zs[p]) + yl)
                zs[p] = gam_col * zs[p] + _dot(mt, zs[p]) + gt
            y_o[0, ci * c:(ci + 1) * c, :] = jnp.concatenate(ys, axis=1)
        for p in range(n_tiles):
            z_ref[p] = zs[p]


def _rwkv_scan(r, v, kk, lw0, lw1, be0, be1, k0, k1):
    b, t, w = r.shape
    c = RWKV_CHUNK * RWKV_STEP_CHUNKS
    nc = t // c
    fwd = pl.BlockSpec((1, c, w), lambda i, j: (i, j, 0))
    bwd = pl.BlockSpec((1, c, w), lambda i, j: (i, nc - 1 - j, 0))
    out = jax.ShapeDtypeStruct((b, t, w), F32)
    zshape = pltpu.VMEM((w // LANES, LANES, LANES), F32)
    return pl.pallas_call(
        _rwkv_kernel,
        grid=(b, nc),
        in_specs=[fwd] * 6 + [bwd] * 6,
        out_specs=[fwd, bwd],
        out_shape=[out, out],
        scratch_shapes=[zshape, zshape],
        compiler_params=_params(2),
        name="rwkv_scan",
    )(r, v, kk, lw0, be0, k0, r, v, kk, lw1, be1, k1)


def _ab_out_kernel(x_ref, yf_ref, yb_ref, bv_ref, g_ref, b_ref, og_ref, bd_ref, wo_ref, o_ref):
    y = yf_ref[0] + yb_ref[0]
    ms = _dot(y * y, bd_ref[...]) * (1.0 / HEAD_DIM)
    yn = y * lax.rsqrt(ms + NORM_EPS) * og_ref[...]
    a_out = (yn + bv_ref[0]) * g_ref[0]
    cat = jnp.concatenate([a_out, b_ref[0]], axis=1)
    o_ref[0] = x_ref[0] + jnp.dot(_bf(cat), wo_ref[...], preferred_element_type=F32)


def _ab_out(x, yf, yb, bv, g, b_out, out_gain, w_out, tm):
    b, t, d = x.shape
    w = RWKV_WIDTH
    head = jnp.arange(w) // HEAD_DIM
    bd = _bf(head[:, None] == head[None, :])
    xs = pl.BlockSpec((1, tm, d), lambda i, j: (i, j, 0))
    ws = pl.BlockSpec((1, tm, w), lambda i, j: (i, j, 0))
    return pl.pallas_call(
        _ab_out_kernel,
        grid=(b, t // tm),
        in_specs=[xs, ws, ws, ws, ws, ws, _const_spec((1, w)), _const_spec((w, w)),
                  _const_spec((w_out.shape[0], d))],
        out_specs=xs,
        out_shape=jax.ShapeDtypeStruct((b, t, d), F32),
        compiler_params=_params(2),
        name="ab_out",
    )(x, yf, yb, bv, g, b_out, out_gain.reshape(1, w), bd, _bf(w_out))


def _diff_in_kernel(x_ref, g_ref, w_ref, qg_ref, kg_ref, bd_ref, q_o, k_o, vt_o):
    x = x_ref[0]
    d = x.shape[-1]
    hn = _bf(_rms(x, g_ref[...]))
    z = jnp.dot(hn, w_ref[...], preferred_element_type=F32)
    bd = bd_ref[...]
    half = bd.shape[0]

    def qk_norm(y, gain, scale):
        parts = []
        for j in range(d // half):
            yj = y[:, j * half:(j + 1) * half]
            ms = _dot(yj * yj, bd) * (1.0 / HEAD_DIM)
            parts.append(yj * lax.rsqrt(ms + NORM_EPS) * gain * scale)
        return _bf(jnp.concatenate(parts, axis=1))

    q_o[0] = qk_norm(z[:, :d], qg_ref[...], LOG2E / math.sqrt(HEAD_DIM))
    k_o[0] = qk_norm(z[:, d:2 * d], kg_ref[...], 1.0)
    tm = x.shape[0]
    for h in range(d // DIFF_V_DIM):
        vt_o[0, h, :DIFF_V_DIM, :] = _bf(z[:, 2 * d + h * DIFF_V_DIM:2 * d + (h + 1) * DIFF_V_DIM].T)
        vt_o[0, h, DIFF_V_DIM:, :] = jnp.ones((ONES_ROWS, tm), BF16)


def _diff_in(x, gain, w_in, q_gain, k_gain, tm):
    b, t, d = x.shape
    half = 512
    nh = d // DIFF_V_DIM
    head = jnp.arange(half) // HEAD_DIM
    bd = _bf(head[:, None] == head[None, :])
    xs = pl.BlockSpec((1, tm, d), lambda i, j: (i, j, 0))
    out = jax.ShapeDtypeStruct((b, t, d), BF16)
    vt_rows = DIFF_V_DIM + ONES_ROWS
    return pl.pallas_call(
        _diff_in_kernel,
        grid=(b, t // tm),
        in_specs=[xs, _const_spec((1, d)), _const_spec((d, 3 * d)),
                  _const_spec((1, half)), _const_spec((1, half)), _const_spec((half, half))],
        out_specs=[xs, xs, pl.BlockSpec((1, nh, vt_rows, tm), lambda i, j: (i, 0, 0, j))],
        out_shape=[out, out, jax.ShapeDtypeStruct((b, nh, vt_rows, t), BF16)],
        compiler_params=_params(2),
        name="diff_in",
    )(x, gain.reshape(1, d), _bf(w_in), jnp.tile(q_gain, half // HEAD_DIM).reshape(1, half),
      jnp.tile(k_gain, half // HEAD_DIM).reshape(1, half), bd)


def _diff_attn_kernel(q_ref, k_ref, vt_ref, slope_ref, lam_ref, og_ref, o_ref, bias_ref, s_ref, p_ref, *,
                      lambda_init):
    t, tq = bias_ref.shape

    @pl.when(pl.program_id(2) == 0)
    def _():
        dist = jnp.abs((_iota((t, tq), 0) - pl.program_id(1) * tq - _iota((t, tq), 1)).astype(F32))
        bias_ref[...] = dist * (-LOG2E * slope_ref[0][:, 0:1])

    q = q_ref[0]
    first = _iota(q.shape, 1) < HEAD_DIM
    zero = jnp.zeros_like(q)
    kb = min(KEY_BLOCK, t)
    qb = min(QUERY_BLOCK, tq)
    sub = HALO

    qcs = (jnp.where(first, q, zero), jnp.where(first, zero, q))
    secs = [(c, slice(i * qb, (i + 1) * qb)) for c in range(2) for i in range(tq // qb)]
    n_kb = t // kb

    def score_block(sec, j, m):
        c, qs = sec
        ks = slice(j * kb, (j + 1) * kb)
        s = lax.dot_general(k_ref[0, ks, :], qcs[c][qs], (((1,), (1,)), ((), ())),
                            preferred_element_type=F32) + bias_ref[ks, qs]
        s_ref[c, ks, qs] = s
        return jnp.maximum(m, jnp.max(s.reshape(kb // sub, sub, qb), axis=0))

    def prob_block(sec, j, m):
        c, qs = sec
        ks = slice(j * kb, (j + 1) * kb)
        p_ref[c, ks, qs] = _bf(jnp.exp2(s_ref[c, ks, qs] - m))

    m_init = jnp.full((sub, qb), -jnp.inf, F32)
    m = m_init
    for j in range(n_kb):
        m = score_block(secs[0], j, m)
    m_prev = jnp.max(m, axis=0, keepdims=True)
    outs = []
    for n in range(1, len(secs) + 1):
        m = m_init
        for j in range(n_kb):
            if n < len(secs):
                m = score_block(secs[n], j, m)
            prob_block(secs[n - 1], j, m_prev)
        m_prev = jnp.max(m, axis=0, keepdims=True)
        c, qs = secs[n - 1]
        outs.append(jnp.dot(vt_ref[0, 0], p_ref[c, :, qs], preferred_element_type=F32))
    per_c = len(secs) // 2
    acc0 = jnp.concatenate(outs[:per_c], axis=1)
    acc1 = jnp.concatenate(outs[per_c:], axis=1)
    o0, s0 = acc0[:DIFF_V_DIM], acc0[DIFF_V_DIM:DIFF_V_DIM + 1]
    o1, s1 = acc1[:DIFF_V_DIM], acc1[DIFF_V_DIM:DIFF_V_DIM + 1]
    lv = lam_ref[...]
    lam = (jnp.exp(jnp.sum(lv[0:1] * lv[1:2], axis=-1, keepdims=True))
           - jnp.exp(jnp.sum(lv[2:3] * lv[3:4], axis=-1, keepdims=True)) + lambda_init)
    o = (o0 * (1.0 / s0) - o1 * (lam / s1)).T
    o_ref[0] = _rms(o, og_ref[...]) * (1.0 - lambda_init)


def _diff_attention(q, k, vt, lam_vecs, out_gain, lambda_init, tq):
    b, t, d = q.shape
    nh = d // DIFF_V_DIM
    slopes = 2.0 ** (-8.0 * jnp.arange(1, nh + 1, dtype=F32) / nh)
    slopes = jnp.broadcast_to(slopes[:, None, None], (nh, 1, LANES))
    return pl.pallas_call(
        functools.partial(_diff_attn_kernel, lambda_init=lambda_init),
        grid=(nh, t // tq, b),
        in_specs=[pl.BlockSpec((1, tq, DIFF_V_DIM), lambda h, j, i: (i, j, h)),
                  pl.BlockSpec((1, t, DIFF_V_DIM), lambda h, j, i: (i, 0, h)),
                  pl.BlockSpec((1, 1, vt.shape[2], t), lambda h, j, i: (i, h, 0, 0)),
                  pl.BlockSpec((1, 1, LANES), lambda h, j, i: (h, 0, 0)),
                  _const_spec((4, HEAD_DIM)),
                  _const_spec((1, DIFF_V_DIM))],
        out_specs=pl.BlockSpec((1, tq, DIFF_V_DIM), lambda h, j, i: (i, j, h)),
        out_shape=jax.ShapeDtypeStruct((b, t, d), F32),
        scratch_shapes=[pltpu.VMEM((t, tq), F32), pltpu.VMEM((2, t, tq), F32), pltpu.VMEM((2, t, tq), BF16)],
        compiler_params=_params(3),
        name="diff_attn",
    )(q, k, vt, slopes, lam_vecs, out_gain.reshape(1, DIFF_V_DIM))


def _proj_residual_kernel(x_ref, a_ref, w_ref, o_ref):
    o_ref[0] = x_ref[0] + jnp.dot(_bf(a_ref[0]), w_ref[...], preferred_element_type=F32)


def _proj_residual(x, a, w, tm):
    b, t, d = x.shape
    n = a.shape[-1]
    return pl.pallas_call(
        _proj_residual_kernel,
        grid=(b, t // tm),
        in_specs=[pl.BlockSpec((1, tm, d), lambda i, j: (i, j, 0)),
                  pl.BlockSpec((1, tm, n), lambda i, j: (i, j, 0)),
                  _const_spec((n, d))],
        out_specs=pl.BlockSpec((1, tm, d), lambda i, j: (i, j, 0)),
        out_shape=jax.ShapeDtypeStruct((b, t, d), F32),
        compiler_params=_params(2),
        name="proj_residual",
    )(x, a, _bf(w))


def kernel(x, mem, mem_norm, mix_norm, cross_norm, ffn_norm, ab_w_in, ab_shift_prev, ab_shift_next, rwkv_w0, rwkv_w2, rwkv_a0, rwkv_a2, rwkv_g2, rwkv_k_k, rwkv_k_a, rwkv_r_k, rwkv_out_gain, sgu_v_gain, sgu_w_s, sgu_b, ab_w_out, diff_w_in, diff_q_gain, diff_k_gain, diff_lambda_q1, diff_lambda_k1, diff_lambda_q2, diff_lambda_k2, diff_out_gain, diff_w_out, cross_wq, cross_wkv, cross_q_gain, cross_k_gain, cross_wo, ffn_w_up, ffn_conv_w, ffn_conv_b, ffn_w_down):
    b, t, d = x.shape
    depth = mix_norm.shape[0]
    tm = min(512, t)
    tq = min(512, t)
    assert t % tm == 0 and t % (RWKV_CHUNK * RWKV_STEP_CHUNKS) == 0 and tm % SGU_CHUNK == 0

    k_all, v_all = _mem_kv(mem, mem_norm, cross_wkv, cross_k_gain)
    for l in range(depth):
        if l % 2 == 0:
            e = l // 2
            (r, v, kk, lw0, lw1, be0, be1, k0, k1, g, bv, b_out) = _ab_in(
                x, mix_norm[l], ab_w_in[e], ab_shift_prev[e], ab_shift_next[e],
                rwkv_w0[e], rwkv_w2[e], rwkv_a0[e], rwkv_a2[e], rwkv_g2[e],
                rwkv_k_k[e], rwkv_k_a[e], rwkv_r_k[e], sgu_v_gain[e], sgu_w_s[e], sgu_b[e], tm)
            yf, yb = _rwkv_scan(r, v, kk, lw0, lw1, be0, be1, k0, k1)
            x = _ab_out(x, yf, yb, bv, g, b_out, rwkv_out_gain[e], ab_w_out[e], tm)
        else:
            o = l // 2
            lambda_init = 0.8 - 0.6 * math.exp(-0.3 * l)
            q, k, v = _diff_in(x, mix_norm[l], diff_w_in[o], diff_q_gain[o], diff_k_gain[o], tm)
            lam_vecs = jnp.stack([diff_lambda_q1[o], diff_lambda_k1[o], diff_lambda_q2[o], diff_lambda_k2[o]])
            c_out = _diff_attention(q, k, v, lam_vecs, diff_out_gain[o], lambda_init, tq)
            x = _proj_residual(x, c_out, diff_w_out[o], tm)
        x = _cross_attention(x, cross_norm[l], cross_wq[l], cross_q_gain[l], k_all, v_all, cross_wo[l], l, tm)
        x = _conv_ffn(x, ffn_norm[l], ffn_w_up[l], ffn_conv_w[l], ffn_conv_b[l], ffn_w_down[l], tm)
    return x
```

```python
import functools
import math

import jax
import jax.numpy as jnp
from jax import lax
from jax.experimental import pallas as pl
from jax.experimental.pallas import tpu as pltpu

NORM_EPS = 1e-6
KK_EPS = 1e-12
RWKV_HEADS = 8
HEAD_DIM = 64
RWKV_WIDTH = RWKV_HEADS * HEAD_DIM
LORA_W = 128
RWKV_IN = 3 * RWKV_WIDTH + 3 * LORA_W
SGU_WIDTH = 512
SGU_GROUPS = 4
SGU_CHUNK = 128
DIFF_HEADS = 8
DIFF_V_DIM = 2 * HEAD_DIM
CROSS_HEADS = 4
CONV_WIDTH = 3
RWKV_CHUNK = 64
RWKV_STEP_CHUNKS = 4
LANES = 128
HALO = 8
FF_CHUNK = 256
KEY_BLOCK = 128
QUERY_BLOCK = 256
ONES_ROWS = 16
VMEM_LIMIT = 56 * 1024 * 1024
LOG2E = math.log2(math.e)

F32 = jnp.float32
BF16 = jnp.bfloat16
AB_IN_OUT_DTYPES = (BF16, BF16, BF16, F32, F32, BF16, BF16, BF16, BF16, BF16, BF16, BF16)


def _bf(x):
    return x.astype(BF16)


def _dot(a, b):
    return jnp.dot(_bf(a), _bf(b), preferred_element_type=F32)


def _dot_nt(a, b):
    return lax.dot_general(_bf(a), _bf(b), (((1,), (1,)), ((), ())), preferred_element_type=F32)


def _rms(x, gain):
    return x * lax.rsqrt(jnp.mean(x * x, axis=-1, keepdims=True) + NORM_EPS) * gain


def _sigmoid(x):
    return 1.0 / (1.0 + jnp.exp(-x))


def _iota(shape, dim):
    return lax.broadcasted_iota(jnp.int32, shape, dim)


def _shift_prev(u, halo_row):
    rolled = pltpu.roll(u, 1, 0)
    head = rolled[:HALO]
    head = jnp.where(_iota(head.shape, 0) == 0, halo_row, head)
    return jnp.concatenate([head, rolled[HALO:]], axis=0)


def _shift_next(u, halo_row):
    n = u.shape[0]
    rolled = pltpu.roll(u, n - 1, 0)
    tail = rolled[n - HALO:]
    tail = jnp.where(_iota(tail.shape, 0) == HALO - 1, halo_row, tail)
    return jnp.concatenate([rolled[:n - HALO], tail], axis=0)


def _halo_rows(xp_ref, xn_ref, gain, i, nt):
    xh = jnp.concatenate([xp_ref[0], xn_ref[0]], axis=0)
    hh = _rms(xh, gain)
    row = _iota(hh.shape, 0)
    has_prev = jnp.where(i > 0, 1.0, 0.0)
    has_next = jnp.where(i < nt - 1, 1.0, 0.0)
    return _bf(hh * jnp.where(row < HALO, has_prev, has_next))


def _const_spec(shape):
    zeros = (0,) * len(shape)
    return pl.BlockSpec(shape, lambda *_: zeros, pipeline_mode=pl.Buffered(1))


def _params(n_axes):
    return pltpu.CompilerParams(dimension_semantics=("arbitrary",) * n_axes,
                                vmem_limit_bytes=VMEM_LIMIT)


def _row_specs(tm, d, t):
    nb = tm // HALO
    last = t // HALO - 1
    main = pl.BlockSpec((1, tm, d), lambda b, i: (b, i, 0))
    prev = pl.BlockSpec((1, HALO, d), lambda b, i: (b, jnp.maximum(i * nb - 1, 0), 0))
    nxt = pl.BlockSpec((1, HALO, d), lambda b, i: (b, jnp.minimum((i + 1) * nb, last), 0))
    return main, prev, nxt


def _mem_kv_kernel(mem_ref, mnorm_ref, wkv_ref, kg_ref, k_ref, v_ref):
    d = mem_ref.shape[-1]
    hd = d // CROSS_HEADS
    mn = _rms(mem_ref[0], mnorm_ref[...])
    kv = jnp.dot(_bf(mn), wkv_ref[0], preferred_element_type=F32)
    for h in range(CROSS_HEADS):
        kh = _rms(kv[:, h * hd:(h + 1) * hd], kg_ref[0]) * (1.0 / math.sqrt(hd))
        k_ref[0, 0, :, h * hd:(h + 1) * hd] = _bf(kh)
    v_ref[0, 0] = _bf(kv[:, d:])


def _mem_kv(mem, mem_norm, wkv, k_gain):
    b, m, d = mem.shape
    nl = wkv.shape[0]
    out = jax.ShapeDtypeStruct((nl, b, m, d), BF16)
    return pl.pallas_call(
        _mem_kv_kernel,
        grid=(nl, b),
        in_specs=[pl.BlockSpec((1, m, d), lambda l, i: (i, 0, 0)),
                  pl.BlockSpec((1, d), lambda l, i: (0, 0)),
                  pl.BlockSpec((1, d, 2 * d), lambda l, i: (l, 0, 0)),
                  pl.BlockSpec((1, 1, d // CROSS_HEADS), lambda l, i: (l, 0, 0))],
        out_specs=[pl.BlockSpec((1, 1, m, d), lambda l, i: (l, i, 0, 0)),
                   pl.BlockSpec((1, 1, m, d), lambda l, i: (l, i, 0, 0))],
        out_shape=[out, out],
        compiler_params=_params(2),
        name="mem_kv",
    )(mem, mem_norm.reshape(1, d), _bf(wkv), k_gain.reshape(nl, 1, -1))


def _cross_kernel(x_ref, g_ref, wq_ref, qg_ref, k_ref, v_ref, wo_ref, o_ref):
    x = x_ref[0]
    d = x.shape[-1]
    hd = d // CROSS_HEADS
    hn = _bf(_rms(x, g_ref[...]))
    q = jnp.dot(hn, wq_ref[...], preferred_element_type=F32)
    outs = []
    for h in range(CROSS_HEADS):
        sl = slice(h * hd, (h + 1) * hd)
        qh = _rms(q[:, sl], qg_ref[...])
        logits = _dot_nt(qh, k_ref[0, 0, :, sl])
        p = jnp.exp(logits - jnp.max(logits, axis=-1, keepdims=True))
        s = jnp.sum(p, axis=-1, keepdims=True)
        outs.append(jnp.dot(_bf(p), v_ref[0, 0, :, sl], preferred_element_type=F32) / s)
    o = jnp.concatenate(outs, axis=1)
    o_ref[0] = x + jnp.dot(_bf(o), wo_ref[...], preferred_element_type=F32)


def _cross_attention(x, gain, wq, q_gain, k_all, v_all, wo, layer, tq):
    b, t, d = x.shape
    m = k_all.shape[2]
    hd = d // CROSS_HEADS
    return pl.pallas_call(
        _cross_kernel,
        grid=(b, t // tq),
        in_specs=[pl.BlockSpec((1, tq, d), lambda i, j: (i, j, 0)),
                  _const_spec((1, d)),
                  _const_spec((d, d)),
                  _const_spec((1, hd)),
                  pl.BlockSpec((1, 1, m, d), lambda i, j: (layer, i, 0, 0)),
                  pl.BlockSpec((1, 1, m, d), lambda i, j: (layer, i, 0, 0)),
                  _const_spec((d, d))],
        out_specs=pl.BlockSpec((1, tq, d), lambda i, j: (i, j, 0)),
        out_shape=jax.ShapeDtypeStruct((b, t, d), F32),
        compiler_params=_params(2),
        name=f"cross_attn_{layer}",
    )(x, gain.reshape(1, d), _bf(wq), q_gain.reshape(1, hd), k_all, v_all, _bf(wo))


def _ffn_kernel(x_ref, xp_ref, xn_ref, g_ref, wup_ref, cw_ref, cb_ref, wdn_ref, o_ref,
                hn_ref, hh_ref, acc_ref):
    i = pl.program_id(1)
    nt = pl.num_programs(1)
    n_chunks = wdn_ref.shape[0]
    x = x_ref[0]
    gain = g_ref[...]
    hn_ref[...] = _bf(_rms(x, gain))
    hh_ref[...] = _halo_rows(xp_ref, xn_ref, gain, i, nt)
    acc_ref[...] = jnp.zeros_like(acc_ref)

    def up_proj(idx):
        w = wup_ref[idx]
        return (jnp.dot(hn_ref[...], w, preferred_element_type=F32),
                jnp.dot(hh_ref[...], w, preferred_element_type=F32))

    def conv(idx, u, uh):
        cw = cw_ref[idx]
        up = _shift_prev(u, uh[HALO - 1:HALO])
        un = _shift_next(u, uh[HALO:HALO + 1])
        return cb_ref[idx] + up * cw[0:1] + u * cw[1:2] + un * cw[2:3]

    ups = [up_proj(0), up_proj(n_chunks)]
    for c in range(n_chunks):
        nxt = [up_proj(c + 1), up_proj(c + 1 + n_chunks)] if c + 1 < n_chunks else None
        gate = conv(c, *ups[0])
        val = conv(c + n_chunks, *ups[1])
        act = gate * _sigmoid(gate) * val
        acc_ref[...] += jnp.dot(_bf(act), wdn_ref[c], preferred_element_type=F32)
        ups = nxt
    o_ref[0] = x + acc_ref[...]


def _conv_ffn(x, gain, w_up, conv_w, conv_b, w_down, tm):
    b, t, d = x.shape
    ff = w_down.shape[0]
    nc = ff // FF_CHUNK
    wup = _bf(w_up).reshape(d, 2 * nc, FF_CHUNK).transpose(1, 0, 2)
    cw = conv_w.reshape(CONV_WIDTH, 2 * nc, FF_CHUNK).transpose(1, 0, 2)
    cb = conv_b.reshape(2 * nc, 1, FF_CHUNK)
    wdn = _bf(w_down).reshape(nc, FF_CHUNK, d)
    main, prev, nxt = _row_specs(tm, d, t)
    return pl.pallas_call(
        _ffn_kernel,
        grid=(b, t // tm),
        in_specs=[main, prev, nxt,
                  _const_spec((1, d)),
                  _const_spec((2 * nc, d, FF_CHUNK)),
                  _const_spec((2 * nc, CONV_WIDTH, FF_CHUNK)),
                  _const_spec((2 * nc, 1, FF_CHUNK)),
                  _const_spec((nc, FF_CHUNK, d))],
        out_specs=pl.BlockSpec((1, tm, d), lambda i, j: (i, j, 0)),
        out_shape=jax.ShapeDtypeStruct((b, t, d), F32),
        scratch_shapes=[pltpu.VMEM((tm, d), BF16),
                        pltpu.VMEM((2 * HALO, d), BF16),
                        pltpu.VMEM((tm, d), F32)],
        compiler_params=_params(2),
        name="conv_ffn",
    )(x, x, x, gain.reshape(1, d), wup, cw, cb, wdn)


def _ab_in_kernel(x_ref, xp_ref, xn_ref, g_ref, win_ref, mup_ref, mun_ref,
                  w0_ref, w2_ref, a0_ref, a2_ref, g2_ref, kk_ref, ka_ref, rk_ref, bd_ref,
                  vg_ref, ws_ref, sb_ref,
                  r_o, v_o, kk_o, lw0_o, lw1_o, be0_o, be1_o, k0_o, k1_o, g_o, bv_o, b_o):
    i = pl.program_id(1)
    nt = pl.num_programs(1)
    w = RWKV_WIDTH
    x = x_ref[0]
    tm = x.shape[0]
    gain = g_ref[...]
    hn = _bf(_rms(x, gain))
    z = jnp.dot(hn, win_ref[...], preferred_element_type=F32)
    hh = _halo_rows(xp_ref, xn_ref, gain, i, nt)
    zh = jnp.dot(hh, win_ref[:, :RWKV_IN], preferred_element_type=F32)

    za = z[:, :RWKV_IN]
    zp = _shift_prev(za, zh[HALO - 1:HALO])
    zn = _shift_next(za, zh[HALO:HALO + 1])
    mup, mun = mup_ref[...], mun_ref[...]
    zs = za * (1.0 - mup - mun) + mup * zp + mun * zn
    r = zs[:, 0:w]
    k = zs[:, w:2 * w]
    v = zs[:, 2 * w:3 * w]
    wl = jnp.tanh(zs[:, 3 * w:3 * w + LORA_W])
    al = zs[:, 3 * w + LORA_W:3 * w + 2 * LORA_W]
    gl = zs[:, 3 * w + 2 * LORA_W:]

    wpre = w0_ref[...] + _dot(wl, w2_ref[...])
    lw = -math.exp(-0.5) * _sigmoid(wpre)
    a = _sigmoid(a0_ref[...] + _dot(al, a2_ref[...]))
    g = _dot(_sigmoid(gl), g2_ref[...])

    bd = bd_ref[...]
    kk0 = k * kk_ref[...]
    kk = kk0 * lax.rsqrt(_dot(kk0 * kk0, bd) + KK_EPS)
    ka = ka_ref[...]
    k0 = k * (1.0 + (a[:, :w] - 1.0) * ka)
    k1 = k * (1.0 + (a[:, w:] - 1.0) * ka)
    bonus = _dot(r * rk_ref[...] * (k0 + k1), bd)

    r_o[0] = _bf(r)
    v_o[0] = _bf(v)
    kk_o[0] = _bf(kk)
    lw0_o[0] = lw[:, :w]
    lw1_o[0] = lw[:, w:]
    be0_o[0] = _bf(kk * a[:, :w])
    be1_o[0] = _bf(kk * a[:, w:])
    k0_o[0] = _bf(k0)
    k1_o[0] = _bf(k1)
    g_o[0] = _bf(g)
    bv_o[0] = _bf(bonus * v)

    zg = z[:, RWKV_IN:]
    zg = zg * (0.5 * (1.0 + jnp.tanh(math.sqrt(2.0 / math.pi) * (zg + 0.044715 * (zg * zg * zg)))))
    u = zg[:, :SGU_WIDTH]
    vn = _bf(_rms(zg[:, SGU_WIDTH:], vg_ref[...]))
    gd = SGU_WIDTH // SGU_GROUPS
    for n in range(tm // SGU_CHUNK):
        rows = slice(n * SGU_CHUNK, (n + 1) * SGU_CHUNK)
        for gi in range(SGU_GROUPS):
            cols = slice(gi * gd, (gi + 1) * gd)
            s = jnp.dot(ws_ref[gi], vn[rows, cols], preferred_element_type=F32) + sb_ref[gi]
            b_o[0, rows, cols] = _bf(u[rows, cols] * s)


def _ab_in(x, gain, w_in, mu_prev, mu_next, w0, w2, a0, a2, g2, k_k, k_a, r_k, v_gain, w_s, s_b, tm):
    b, t, d = x.shape
    w = RWKV_WIDTH
    n_in = w_in.shape[1]
    half = LORA_W // 2

    def lora_cat(m):
        z = jnp.zeros((half, w), m.dtype)
        return _bf(jnp.concatenate([jnp.concatenate([m[0], z], axis=1),
                                    jnp.concatenate([z, m[1]], axis=1)], axis=0))

    head = jnp.arange(w) // HEAD_DIM
    bd = _bf(head[:, None] == head[None, :])
    sb = jnp.broadcast_to(s_b[:, :, None], (SGU_GROUPS, SGU_CHUNK, SGU_WIDTH // SGU_GROUPS))
    main, prev, nxt = _row_specs(tm, d, t)
    ospec = pl.BlockSpec((1, tm, w), lambda i, j: (i, j, 0))
    return pl.pallas_call(
        _ab_in_kernel,
        grid=(b, t // tm),
        in_specs=[main, prev, nxt,
                  _const_spec((1, d)),
                  _const_spec((d, n_in)),
                  _const_spec((1, RWKV_IN)), _const_spec((1, RWKV_IN)),
                  _const_spec((1, 2 * w)), _const_spec((LORA_W, 2 * w)),
                  _const_spec((1, 2 * w)), _const_spec((LORA_W, 2 * w)),
                  _const_spec((LORA_W, w)),
                  _const_spec((1, w)), _const_spec((1, w)), _const_spec((1, w)),
                  _const_spec((w, w)),
                  _const_spec((1, SGU_WIDTH)),
                  _const_spec((SGU_GROUPS, SGU_CHUNK, SGU_CHUNK)),
                  _const_spec((SGU_GROUPS, SGU_CHUNK, SGU_WIDTH // SGU_GROUPS))],
        out_specs=[ospec] * 12,
        out_shape=[jax.ShapeDtypeStruct((b, t, w), dt) for dt in AB_IN_OUT_DTYPES],
        compiler_params=_params(2),
        name="ab_in",
    )(x, x, x, gain.reshape(1, d), _bf(w_in), mu_prev.reshape(1, -1), mu_next.reshape(1, -1),
      w0.reshape(1, 2 * w), lora_cat(w2), a0.reshape(1, 2 * w), lora_cat(a2), _bf(g2),
      k_k.reshape(1, w), k_a.reshape(1, w), r_k.reshape(1, w), bd,
      v_gain.reshape(1, -1), _bf(w_s), sb)


def _stack_heads(x):
    even = (_iota(x.shape, 1) & (LANES - 1)) < HEAD_DIM
    return jnp.concatenate([jnp.where(even, x, 0.0), jnp.where(even, 0.0, x)], axis=0)


def _rwkv_prep(r, v, kk, lw, be, kd, reverse):
    c = r.shape[0]
    rr = _iota((c, LANES), 0)
    cc = _iota((c, LANES), 1) & (HEAD_DIM - 1)
    if reverse:
        strict, incl = cc > rr, cc >= rr
    else:
        strict, incl = cc < rr, cc <= rr
    tri = _bf(incl[:, :c])

    l1 = _bf(lw)
    l2 = _bf(lw - l1.astype(F32))
    l3 = _bf(lw - l1.astype(F32) - l2.astype(F32))
    cum = (jnp.dot(tri, l1, preferred_element_type=F32) + jnp.dot(tri, l2, preferred_element_type=F32)
           + jnp.dot(tri, l3, preferred_element_type=F32))
    tot = cum[0:1] if reverse else cum[c - 1:c]
    e_neg = jnp.exp(-cum)
    e_end = jnp.exp(tot - cum)
    return dict(ag=-kk * jnp.exp(cum - lw), rg=r * jnp.exp(cum), bi=be * e_neg, ki=kd * e_neg,
                bee=be * e_end, kee=kd * e_end, gam=jnp.exp(tot), v=v, strict=strict, incl=incl)


def _rwkv_local(chunks):
    c = chunks[0]["v"].shape[0]
    n_tiles = chunks[0]["v"].shape[1] // LANES
    r128 = _iota((LANES, LANES), 0)
    c128 = _iota((LANES, LANES), 1)
    same_head = (r128 < HEAD_DIM) == (c128 < HEAD_DIM)
    diag = r128 == c128
    ctx = [(ch, slice(p * LANES, (p + 1) * LANES)) for ch in chunks for p in range(n_tiles)]

    aa = [_dot_nt(jnp.concatenate([ch["ag"][:, sl], ch["rg"][:, sl]], axis=0),
                  jnp.concatenate([_stack_heads(ch["bi"][:, sl]), _stack_heads(ch["ki"][:, sl])], axis=0))
          for ch, sl in ctx]
    a_ab = [jnp.where(ch["strict"], m[:c, :LANES], 0.0) for (ch, _), m in zip(ctx, aa)]
    a_rb = [jnp.where(ch["incl"], m[c:, :LANES], 0.0) for (ch, _), m in zip(ctx, aa)]
    a_k = [jnp.concatenate([jnp.where(ch["strict"], m[:c, LANES:], 0.0),
                            jnp.where(ch["incl"], m[c:, LANES:], 0.0)], axis=0) for (ch, _), m in zip(ctx, aa)]
    xv = [_dot(m, _stack_heads(ch["v"][:, sl])) for (ch, sl), m in zip(ctx, a_k)]
    x = [jnp.concatenate([ch["ag"][:, sl], m[:c]], axis=1) for (ch, sl), m in zip(ctx, xv)]
    pw = a_ab
    n_round = int(math.log2(c))
    for j in range(n_round):
        x = [xi + _dot(pi, _stack_heads(xi)) for xi, pi in zip(x, pw)]
        if j < n_round - 1:
            pw = [_dot(pi, jnp.where(same_head, jnp.concatenate([pi, pi], axis=0), 0.0)) for pi in pw]
    ry = [jnp.concatenate([ch["rg"][:, sl], m[c:]], axis=1) + _dot(ai, _stack_heads(xi))
          for (ch, sl), m, ai, xi in zip(ctx, xv, a_rb, x)]
    mg = [_dot(jnp.concatenate([ch["bee"][:, sl], ch["kee"][:, sl]], axis=0).T,
               jnp.concatenate([xi, jnp.concatenate([jnp.zeros_like(ch["v"][:, sl]), ch["v"][:, sl]], axis=1)],
                               axis=0))
          for (ch, sl), xi in zip(ctx, x)]
    out = []
    for (ch, sl), ryi, mgi in zip(ctx, ry, mg):
        gam_col = jnp.sum(jnp.where(diag, ch["gam"][:, sl], 0.0), axis=1, keepdims=True)
        out.append((ryi[:, :LANES], ryi[:, LANES:], jnp.where(same_head, mgi[:, :LANES], 0.0),
                    jnp.where(same_head, mgi[:, LANES:], 0.0), gam_col))
    return [out[i * n_tiles:(i + 1) * n_tiles] for i in range(len(chunks))]


def _rwkv_kernel(rf, vf, kkf, lwf, bef, kf, rb, vb, kkb, lwb, beb, kb, yf_o, yb_o, zf_ref, zb_ref):
    @pl.when(pl.program_id(1) == 0)
    def _():
        zf_ref[...] = jnp.zeros_like(zf_ref)
        zb_ref[...] = jnp.zeros_like(zb_ref)

    c = RWKV_CHUNK
    n = rf.shape[1] // c
    n_tiles = rf.shape[2] // LANES
    chunks = []
    for refs, reverse in (((rf, vf, kkf, lwf, bef, kf), False), ((rb, vb, kkb, lwb, beb, kb), True)):
        for ci in range(n):
            chunks.append(_rwkv_prep(*[ref[0, ci * c:(ci + 1) * c, :].astype(F32) for ref in refs], reverse))
    local = _rwkv_local(chunks)

    for d, (z_ref, y_o) in enumerate(((zf_ref, yf_o), (zb_ref, yb_o))):
        order = range(n) if d == 0 else range(n - 1, -1, -1)
        zs = [z_ref[p] for p in range(n_tiles)]
        for ci in order:
            ys = []
            for p in range(n_tiles):
                rt, yl, mt, gt, gam_col = local[d * n + ci][p]
                ys.append(_dot(rt, zs[p]) + yl)
                zs[p] = gam_col * zs[p] + _dot(mt, zs[p]) + gt
            y_o[0, ci * c:(ci + 1) * c, :] = jnp.concatenate(ys, axis=1)
        for p in range(n_tiles):
            z_ref[p] = zs[p]


def _rwkv_scan(r, v, kk, lw0, lw1, be0, be1, k0, k1):
    b, t, w = r.shape
    c = RWKV_CHUNK * RWKV_STEP_CHUNKS
    nc = t // c
    fwd = pl.BlockSpec((1, c, w), lambda i, j: (i, j, 0))
    bwd = pl.BlockSpec((1, c, w), lambda i, j: (i, nc - 1 - j, 0))
    out = jax.ShapeDtypeStruct((b, t, w), F32)
    zshape = pltpu.VMEM((w // LANES, LANES, LANES), F32)
    return pl.pallas_call(
        _rwkv_kernel,
        grid=(b, nc),
        in_specs=[fwd] * 6 + [bwd] * 6,
        out_specs=[fwd, bwd],
        out_shape=[out, out],
        scratch_shapes=[zshape, zshape],
        compiler_params=_params(2),
        name="rwkv_scan",
    )(r, v, kk, lw0, be0, k0, r, v, kk, lw1, be1, k1)


def _ab_out_kernel(x_ref, yf_ref, yb_ref, bv_ref, g_ref, b_ref, og_ref, bd_ref, wo_ref, o_ref):
    y = yf_ref[0] + yb_ref[0]
    ms = _dot(y * y, bd_ref[...]) * (1.0 / HEAD_DIM)
    yn = y * lax.rsqrt(ms + NORM_EPS) * og_ref[...]
    a_out = (yn + bv_ref[0].astype(F32)) * g_ref[0].astype(F32)
    cat = jnp.concatenate([_bf(a_out), b_ref[0]], axis=1)
    o_ref[0] = x_ref[0] + jnp.dot(cat, wo_ref[...], preferred_element_type=F32)


def _ab_out(x, yf, yb, bv, g, b_out, out_gain, w_out, tm):
    b, t, d = x.shape
    w = RWKV_WIDTH
    head = jnp.arange(w) // HEAD_DIM
    bd = _bf(head[:, None] == head[None, :])
    xs = pl.BlockSpec((1, tm, d), lambda i, j: (i, j, 0))
    ws = pl.BlockSpec((1, tm, w), lambda i, j: (i, j, 0))
    return pl.pallas_call(
        _ab_out_kernel,
        grid=(b, t // tm),
        in_specs=[xs, ws, ws, ws, ws, ws, _const_spec((1, w)), _const_spec((w, w)),
                  _const_spec((w_out.shape[0], d))],
        out_specs=xs,
        out_shape=jax.ShapeDtypeStruct((b, t, d), F32),
        compiler_params=_params(2),
        name="ab_out",
    )(x, yf, yb, bv, g, b_out, out_gain.reshape(1, w), bd, _bf(w_out))


def _diff_in_kernel(x_ref, g_ref, w_ref, qg_ref, kg_ref, bd_ref, q_o, k_o, vt_o):
    x = x_ref[0]
    d = x.shape[-1]
    hn = _bf(_rms(x, g_ref[...]))
    z = jnp.dot(hn, w_ref[...], preferred_element_type=F32)
    bd = bd_ref[...]
    half = bd.shape[0]

    def qk_norm(y, gain, scale):
        parts = []
        for j in range(d // half):
            yj = y[:, j * half:(j + 1) * half]
            ms = _dot(yj * yj, bd) * (1.0 / HEAD_DIM)
            parts.append(yj * lax.rsqrt(ms + NORM_EPS) * gain * scale)
        return _bf(jnp.concatenate(parts, axis=1))

    q_o[0] = qk_norm(z[:, :d], qg_ref[...], LOG2E / math.sqrt(HEAD_DIM))
    k_o[0] = qk_norm(z[:, d:2 * d], kg_ref[...], 1.0)
    tm = x.shape[0]
    for h in range(d // DIFF_V_DIM):
        vt_o[0, h, :DIFF_V_DIM, :] = _bf(z[:, 2 * d + h * DIFF_V_DIM:2 * d + (h + 1) * DIFF_V_DIM].T)
        vt_o[0, h, DIFF_V_DIM:, :] = jnp.ones((ONES_ROWS, tm), BF16)


def _diff_in(x, gain, w_in, q_gain, k_gain, tm):
    b, t, d = x.shape
    half = 512
    nh = d // DIFF_V_DIM
    head = jnp.arange(half) // HEAD_DIM
    bd = _bf(head[:, None] == head[None, :])
    xs = pl.BlockSpec((1, tm, d), lambda i, j: (i, j, 0))
    out = jax.ShapeDtypeStruct((b, t, d), BF16)
    vt_rows = DIFF_V_DIM + ONES_ROWS
    return pl.pallas_call(
        _diff_in_kernel,
        grid=(b, t // tm),
        in_specs=[xs, _const_spec((1, d)), _const_spec((d, 3 * d)),
                  _const_spec((1, half)), _const_spec((1, half)), _const_spec((half, half))],
        out_specs=[xs, xs, pl.BlockSpec((1, nh, vt_rows, tm), lambda i, j: (i, 0, 0, j))],
        out_shape=[out, out, jax.ShapeDtypeStruct((b, nh, vt_rows, t), BF16)],
        compiler_params=_params(2),
        name="diff_in",
    )(x, gain.reshape(1, d), _bf(w_in), jnp.tile(q_gain, half // HEAD_DIM).reshape(1, half),
      jnp.tile(k_gain, half // HEAD_DIM).reshape(1, half), bd)


def _diff_attn_kernel(q_ref, k_ref, vt_ref, slope_ref, lam_ref, og_ref, o_ref, bias_ref, s_ref, p_ref, *,
                      lambda_init):
    t, tq = bias_ref.shape

    @pl.when(pl.program_id(2) == 0)
    def _():
        dist = jnp.abs((_iota((t, tq), 0) - pl.program_id(1) * tq - _iota((t, tq), 1)).astype(F32))
        bias_ref[...] = dist * (-LOG2E * slope_ref[0][:, 0:1])

    q = q_ref[0]
    first = _iota(q.shape, 1) < HEAD_DIM
    zero = jnp.zeros_like(q)
    kb = min(KEY_BLOCK, t)
    qb = min(QUERY_BLOCK, tq)
    sub = HALO

    qcs = (jnp.where(first, q, zero), jnp.where(first, zero, q))
    secs = [(c, slice(i * qb, (i + 1) * qb)) for c in range(2) for i in range(tq // qb)]
    n_kb = t // kb

    def score_block(sec, j, m):
        c, qs = sec
        ks = slice(j * kb, (j + 1) * kb)
        s = lax.dot_general(k_ref[0, ks, :], qcs[c][qs], (((1,), (1,)), ((), ())),
                            preferred_element_type=F32) + bias_ref[ks, qs]
        s_ref[c, ks, qs] = s
        return jnp.maximum(m, jnp.max(s.reshape(kb // sub, sub, qb), axis=0))

    def prob_block(sec, j, m):
        c, qs = sec
        ks = slice(j * kb, (j + 1) * kb)
        p_ref[c, ks, qs] = _bf(jnp.exp2(s_ref[c, ks, qs] - m))

    m_init = jnp.full((sub, qb), -jnp.inf, F32)
    m = m_init
    for j in range(n_kb):
        m = score_block(secs[0], j, m)
    m_prev = jnp.max(m, axis=0, keepdims=True)
    outs = []
    for n in range(1, len(secs) + 1):
        m = m_init
        for j in range(n_kb):
            if n < len(secs):
                m = score_block(secs[n], j, m)
            prob_block(secs[n - 1], j, m_prev)
        m_prev = jnp.max(m, axis=0, keepdims=True)
        c, qs = secs[n - 1]
        outs.append(jnp.dot(vt_ref[0, 0], p_ref[c, :, qs], preferred_element_type=F32))
    per_c = len(secs) // 2
    acc0 = jnp.concatenate(outs[:per_c], axis=1)
    acc1 = jnp.concatenate(outs[per_c:], axis=1)
    o0, s0 = acc0[:DIFF_V_DIM], acc0[DIFF_V_DIM:DIFF_V_DIM + 1]
    o1, s1 = acc1[:DIFF_V_DIM], acc1[DIFF_V_DIM:DIFF_V_DIM + 1]
    lv = lam_ref[...]
    lam = (jnp.exp(jnp.sum(lv[0:1] * lv[1:2], axis=-1, keepdims=True))
           - jnp.exp(jnp.sum(lv[2:3] * lv[3:4], axis=-1, keepdims=True)) + lambda_init)
    o = (o0 * (1.0 / s0) - o1 * (lam / s1)).T
    o_ref[0] = _rms(o, og_ref[...]) * (1.0 - lambda_init)


def _diff_attention(q, k, vt, lam_vecs, out_gain, lambda_init, tq):
    b, t, d = q.shape
    nh = d // DIFF_V_DIM
    slopes = 2.0 ** (-8.0 * jnp.arange(1, nh + 1, dtype=F32) / nh)
    slopes = jnp.broadcast_to(slopes[:, None, None], (nh, 1, LANES))
    return pl.pallas_call(
        functools.partial(_diff_attn_kernel, lambda_init=lambda_init),
        grid=(nh, t // tq, b),
        in_specs=[pl.BlockSpec((1, tq, DIFF_V_DIM), lambda h, j, i: (i, j, h)),
                  pl.BlockSpec((1, t, DIFF_V_DIM), lambda h, j, i: (i, 0, h)),
                  pl.BlockSpec((1, 1, vt.shape[2], t), lambda h, j, i: (i, h, 0, 0)),
                  pl.BlockSpec((1, 1, LANES), lambda h, j, i: (h, 0, 0)),
                  _const_spec((4, HEAD_DIM)),
                  _const_spec((1, DIFF_V_DIM))],
        out_specs=pl.BlockSpec((1, tq, DIFF_V_DIM), lambda h, j, i: (i, j, h)),
        out_shape=jax.ShapeDtypeStruct((b, t, d), F32),
        scratch_shapes=[pltpu.VMEM((t, tq), F32), pltpu.VMEM((2, t, tq), F32), pltpu.VMEM((2, t, tq), BF16)],
        compiler_params=_params(3),
        name="diff_attn",
    )(q, k, vt, slopes, lam_vecs, out_gain.reshape(1, DIFF_V_DIM))


def _proj_residual_kernel(x_ref, a_ref, w_ref, o_ref):
    o_ref[0] = x_ref[0] + jnp.dot(_bf(a_ref[0]), w_ref[...], preferred_element_type=F32)


def _proj_residual(x, a, w, tm):
    b, t, d = x.shape
    n = a.shape[-1]
    return pl.pallas_call(
        _proj_residual_kernel,
        grid=(b, t // tm),
        in_specs=[pl.BlockSpec((1, tm, d), lambda i, j: (i, j, 0)),
                  pl.BlockSpec((1, tm, n), lambda i, j: (i, j, 0)),
                  _const_spec((n, d))],
        out_specs=pl.BlockSpec((1, tm, d), lambda i, j: (i, j, 0)),
        out_shape=jax.ShapeDtypeStruct((b, t, d), F32),
        compiler_params=_params(2),
        name="proj_residual",
    )(x, a, _bf(w))


def kernel(x, mem, mem_norm, mix_norm, cross_norm, ffn_norm, ab_w_in, ab_shift_prev, ab_shift_next, rwkv_w0, rwkv_w2, rwkv_a0, rwkv_a2, rwkv_g2, rwkv_k_k, rwkv_k_a, rwkv_r_k, rwkv_out_gain, sgu_v_gain, sgu_w_s, sgu_b, ab_w_out, diff_w_in, diff_q_gain, diff_k_gain, diff_lambda_q1, diff_lambda_k1, diff_lambda_q2, diff_lambda_k2, diff_out_gain, diff_w_out, cross_wq, cross_wkv, cross_q_gain, cross_k_gain, cross_wo, ffn_w_up, ffn_conv_w, ffn_conv_b, ffn_w_down):
    b, t, d = x.shape
    depth = mix_norm.shape[0]
    tm = min(512, t)
    tq = min(512, t)
    assert t % tm == 0 and t % (RWKV_CHUNK * RWKV_STEP_CHUNKS) == 0 and tm % SGU_CHUNK == 0

    k_all, v_all = _mem_kv(mem, mem_norm, cross_wkv, cross_k_gain)
    for l in range(depth):
        if l % 2 == 0:
            e = l // 2
            (r, v, kk, lw0, lw1, be0, be1, k0, k1, g, bv, b_out) = _ab_in(
                x, mix_norm[l], ab_w_in[e], ab_shift_prev[e], ab_shift_next[e],
                rwkv_w0[e], rwkv_w2[e], rwkv_a0[e], rwkv_a2[e], rwkv_g2[e],
                rwkv_k_k[e], rwkv_k_a[e], rwkv_r_k[e], sgu_v_gain[e], sgu_w_s[e], sgu_b[e], tm)
            yf, yb = _rwkv_scan(r, v, kk, lw0, lw1, be0, be1, k0, k1)
            x = _ab_out(x, yf, yb, bv, g, b_out, rwkv_out_gain[e], ab_w_out[e], tm)
        else:
            o = l // 2
            lambda_init = 0.8 - 0.6 * math.exp(-0.3 * l)
            q, k, v = _diff_in(x, mix_norm[l], diff_w_in[o], diff_q_gain[o], diff_k_gain[o], tm)
            lam_vecs = jnp.stack([diff_lambda_q1[o], diff_lambda_k1[o], diff_lambda_q2[o], diff_lambda_k2[o]])
            c_out = _diff_attention(q, k, v, lam_vecs, diff_out_gain[o], lambda_init, tq)
            x = _proj_residual(x, c_out, diff_w_out[o], tm)
        x = _cross_attention(x, cross_norm[l], cross_wq[l], cross_q_gain[l], k_all, v_all, cross_wo[l], l, tm)
        x = _conv_ffn(x, ffn_norm[l], ffn_w_up[l], ffn_conv_w[l], ffn_conv_b[l], ffn_w_down[l], tm)
    return x
```

```python
import functools
import math

import jax
import jax.numpy as jnp
from jax import lax
from jax.experimental import pallas as pl
from jax.experimental.pallas import tpu as pltpu

NORM_EPS = 1e-6
KK_EPS = 1e-12
RWKV_HEADS = 8
HEAD_DIM = 64
RWKV_WIDTH = RWKV_HEADS * HEAD_DIM
LORA_W = 128
RWKV_IN = 3 * RWKV_WIDTH + 3 * LORA_W
SGU_WIDTH = 512
SGU_GROUPS = 4
SGU_CHUNK = 128
DIFF_HEADS = 8
DIFF_V_DIM = 2 * HEAD_DIM
CROSS_HEADS = 4
CONV_WIDTH = 3
RWKV_CHUNK = 64
RWKV_STEP_CHUNKS = 4
LANES = 128
HALO = 8
FF_CHUNK = 256
KEY_BLOCK = 128
QUERY_BLOCK = 256
ONES_ROWS = 16
VMEM_LIMIT = 56 * 1024 * 1024
LOG2E = math.log2(math.e)

F32 = jnp.float32
BF16 = jnp.bfloat16
AB_IN_OUT_DTYPES = (BF16, BF16, BF16, F32, F32, BF16, BF16, BF16, BF16, BF16, BF16, BF16)


def _bf(x):
    return x.astype(BF16)


def _dot(a, b):
    return jnp.dot(_bf(a), _bf(b), preferred_element_type=F32)


def _dot_nt(a, b):
    return lax.dot_general(_bf(a), _bf(b), (((1,), (1,)), ((), ())), preferred_element_type=F32)


def _rms(x, gain):
    return x * lax.rsqrt(jnp.mean(x * x, axis=-1, keepdims=True) + NORM_EPS) * gain


def _sigmoid(x):
    return 1.0 / (1.0 + jnp.exp(-x))


def _iota(shape, dim):
    return lax.broadcasted_iota(jnp.int32, shape, dim)


def _shift_prev(u, halo_row):
    rolled = pltpu.roll(u, 1, 0)
    head = rolled[:HALO]
    head = jnp.where(_iota(head.shape, 0) == 0, halo_row, head)
    return jnp.concatenate([head, rolled[HALO:]], axis=0)


def _shift_next(u, halo_row):
    n = u.shape[0]
    rolled = pltpu.roll(u, n - 1, 0)
    tail = rolled[n - HALO:]
    tail = jnp.where(_iota(tail.shape, 0) == HALO - 1, halo_row, tail)
    return jnp.concatenate([rolled[:n - HALO], tail], axis=0)


def _halo_rows(xp_ref, xn_ref, gain, i, nt):
    xh = jnp.concatenate([xp_ref[0], xn_ref[0]], axis=0)
    hh = _rms(xh, gain)
    row = _iota(hh.shape, 0)
    has_prev = jnp.where(i > 0, 1.0, 0.0)
    has_next = jnp.where(i < nt - 1, 1.0, 0.0)
    return _bf(hh * jnp.where(row < HALO, has_prev, has_next))


def _const_spec(shape):
    zeros = (0,) * len(shape)
    return pl.BlockSpec(shape, lambda *_: zeros, pipeline_mode=pl.Buffered(1))


def _params(n_axes):
    return pltpu.CompilerParams(dimension_semantics=("arbitrary",) * n_axes,
                                vmem_limit_bytes=VMEM_LIMIT)


def _row_specs(tm, d, t):
    nb = tm // HALO
    last = t // HALO - 1
    main = pl.BlockSpec((1, tm, d), lambda b, i: (b, i, 0))
    prev = pl.BlockSpec((1, HALO, d), lambda b, i: (b, jnp.maximum(i * nb - 1, 0), 0))
    nxt = pl.BlockSpec((1, HALO, d), lambda b, i: (b, jnp.minimum((i + 1) * nb, last), 0))
    return main, prev, nxt


def _mem_kv_kernel(mem_ref, mnorm_ref, wkv_ref, kg_ref, k_ref, v_ref):
    d = mem_ref.shape[-1]
    hd = d // CROSS_HEADS
    mn = _rms(mem_ref[0], mnorm_ref[...])
    kv = jnp.dot(_bf(mn), wkv_ref[0], preferred_element_type=F32)
    for h in range(CROSS_HEADS):
        kh = _rms(kv[:, h * hd:(h + 1) * hd], kg_ref[0]) * (1.0 / math.sqrt(hd))
        k_ref[0, 0, :, h * hd:(h + 1) * hd] = _bf(kh)
    v_ref[0, 0] = _bf(kv[:, d:])


def _mem_kv(mem, mem_norm, wkv, k_gain):
    b, m, d = mem.shape
    nl = wkv.shape[0]
    out = jax.ShapeDtypeStruct((nl, b, m, d), BF16)
    return pl.pallas_call(
        _mem_kv_kernel,
        grid=(nl, b),
        in_specs=[pl.BlockSpec((1, m, d), lambda l, i: (i, 0, 0)),
                  pl.BlockSpec((1, d), lambda l, i: (0, 0)),
                  pl.BlockSpec((1, d, 2 * d), lambda l, i: (l, 0, 0)),
                  pl.BlockSpec((1, 1, d // CROSS_HEADS), lambda l, i: (l, 0, 0))],
        out_specs=[pl.BlockSpec((1, 1, m, d), lambda l, i: (l, i, 0, 0)),
                   pl.BlockSpec((1, 1, m, d), lambda l, i: (l, i, 0, 0))],
        out_shape=[out, out],
        compiler_params=_params(2),
        name="mem_kv",
    )(mem, mem_norm.reshape(1, d), _bf(wkv), k_gain.reshape(nl, 1, -1))


def _mix_rwkv_sgu(x, yf_ref, yb_ref, bv_ref, g_ref, b_ref, og_ref, bd_ref, wo_ref):
    y = yf_ref[0] + yb_ref[0]
    ms = _dot(y * y, bd_ref[...]) * (1.0 / HEAD_DIM)
    yn = y * lax.rsqrt(ms + NORM_EPS) * og_ref[...]
    a_out = (yn + bv_ref[0].astype(F32)) * g_ref[0].astype(F32)
    cat = jnp.concatenate([_bf(a_out), b_ref[0]], axis=1)
    return x + jnp.dot(cat, wo_ref[...], preferred_element_type=F32)


def _mix_proj(x, a_ref, w_ref):
    return x + jnp.dot(a_ref[0], w_ref[...], preferred_element_type=F32)


def _cross_kernel(*refs, n_mix, mix_fn):
    x_ref, mix_refs = refs[0], refs[1:1 + n_mix]
    g_ref, wq_ref, qg_ref, k_ref, v_ref, wo_ref, o_ref = refs[1 + n_mix:]
    x = mix_fn(x_ref[0], *mix_refs)
    d = x.shape[-1]
    hd = d // CROSS_HEADS
    hn = _bf(_rms(x, g_ref[...]))
    q = jnp.dot(hn, wq_ref[...], preferred_element_type=F32)
    outs = []
    for h in range(CROSS_HEADS):
        sl = slice(h * hd, (h + 1) * hd)
        qh = _rms(q[:, sl], qg_ref[...])
        logits = _dot_nt(qh, k_ref[0, 0, :, sl])
        p = jnp.exp(logits - jnp.max(logits, axis=-1, keepdims=True))
        s = jnp.sum(p, axis=-1, keepdims=True)
        outs.append(jnp.dot(_bf(p), v_ref[0, 0, :, sl], preferred_element_type=F32) / s)
    o = jnp.concatenate(outs, axis=1)
    o_ref[0] = x + jnp.dot(_bf(o), wo_ref[...], preferred_element_type=F32)


def _cross_attention(x, mix_fn, mix_rows, mix_consts, gain, wq, q_gain, k_all, v_all, wo, layer, tq):
    b, t, d = x.shape
    m = k_all.shape[2]
    hd = d // CROSS_HEADS
    mix_specs = ([pl.BlockSpec((1, tq, a.shape[-1]), lambda i, j: (i, j, 0)) for a in mix_rows]
                 + [_const_spec(a.shape) for a in mix_consts])
    return pl.pallas_call(
        functools.partial(_cross_kernel, n_mix=len(mix_specs), mix_fn=mix_fn),
        grid=(b, t // tq),
        in_specs=[pl.BlockSpec((1, tq, d), lambda i, j: (i, j, 0)),
                  *mix_specs,
                  _const_spec((1, d)),
                  _const_spec((d, d)),
                  _const_spec((1, hd)),
                  pl.BlockSpec((1, 1, m, d), lambda i, j: (layer, i, 0, 0)),
                  pl.BlockSpec((1, 1, m, d), lambda i, j: (layer, i, 0, 0)),
                  _const_spec((d, d))],
        out_specs=pl.BlockSpec((1, tq, d), lambda i, j: (i, j, 0)),
        out_shape=jax.ShapeDtypeStruct((b, t, d), F32),
        compiler_params=_params(2),
        name=f"cross_attn_{layer}",
    )(x, *mix_rows, *mix_consts, gain.reshape(1, d), _bf(wq), q_gain.reshape(1, hd), k_all, v_all, _bf(wo))


def _ffn_kernel(x_ref, xp_ref, xn_ref, g_ref, wup_ref, cw_ref, cb_ref, wdn_ref, o_ref,
                hn_ref, hh_ref, acc_ref):
    i = pl.program_id(1)
    nt = pl.num_programs(1)
    n_chunks = wdn_ref.shape[0]
    x = x_ref[0]
    gain = g_ref[...]
    hn_ref[...] = _bf(_rms(x, gain))
    hh_ref[...] = _halo_rows(xp_ref, xn_ref, gain, i, nt)
    acc_ref[...] = jnp.zeros_like(acc_ref)

    def up_proj(idx):
        w = wup_ref[idx]
        return (jnp.dot(hn_ref[...], w, preferred_element_type=F32),
                jnp.dot(hh_ref[...], w, preferred_element_type=F32))

    def conv(idx, u, uh):
        cw = cw_ref[idx]
        up = _shift_prev(u, uh[HALO - 1:HALO])
        un = _shift_next(u, uh[HALO:HALO + 1])
        return cb_ref[idx] + up * cw[0:1] + u * cw[1:2] + un * cw[2:3]

    ups = [up_proj(0), up_proj(n_chunks)]
    for c in range(n_chunks):
        nxt = [up_proj(c + 1), up_proj(c + 1 + n_chunks)] if c + 1 < n_chunks else None
        gate = conv(c, *ups[0])
        val = conv(c + n_chunks, *ups[1])
        act = gate * _sigmoid(gate) * val
        acc_ref[...] += jnp.dot(_bf(act), wdn_ref[c], preferred_element_type=F32)
        ups = nxt
    o_ref[0] = x + acc_ref[...]


def _conv_ffn(x, gain, w_up, conv_w, conv_b, w_down, tm):
    b, t, d = x.shape
    ff = w_down.shape[0]
    nc = ff // FF_CHUNK
    wup = _bf(w_up).reshape(d, 2 * nc, FF_CHUNK).transpose(1, 0, 2)
    cw = conv_w.reshape(CONV_WIDTH, 2 * nc, FF_CHUNK).transpose(1, 0, 2)
    cb = conv_b.reshape(2 * nc, 1, FF_CHUNK)
    wdn = _bf(w_down).reshape(nc, FF_CHUNK, d)
    main, prev, nxt = _row_specs(tm, d, t)
    return pl.pallas_call(
        _ffn_kernel,
        grid=(b, t // tm),
        in_specs=[main, prev, nxt,
                  _const_spec((1, d)),
                  _const_spec((2 * nc, d, FF_CHUNK)),
                  _const_spec((2 * nc, CONV_WIDTH, FF_CHUNK)),
                  _const_spec((2 * nc, 1, FF_CHUNK)),
                  _const_spec((nc, FF_CHUNK, d))],
        out_specs=pl.BlockSpec((1, tm, d), lambda i, j: (i, j, 0)),
        out_shape=jax.ShapeDtypeStruct((b, t, d), F32),
        scratch_shapes=[pltpu.VMEM((tm, d), BF16),
                        pltpu.VMEM((2 * HALO, d), BF16),
                        pltpu.VMEM((tm, d), F32)],
        compiler_params=_params(2),
        name="conv_ffn",
    )(x, x, x, gain.reshape(1, d), wup, cw, cb, wdn)


def _ab_in_kernel(x_ref, xp_ref, xn_ref, g_ref, win_ref, mup_ref, mun_ref,
                  w0_ref, w2_ref, a0_ref, a2_ref, g2_ref, kk_ref, ka_ref, rk_ref, bd_ref,
                  vg_ref, ws_ref, sb_ref,
                  r_o, v_o, kk_o, lw0_o, lw1_o, be0_o, be1_o, k0_o, k1_o, g_o, bv_o, b_o):
    i = pl.program_id(1)
    nt = pl.num_programs(1)
    w = RWKV_WIDTH
    x = x_ref[0]
    tm = x.shape[0]
    gain = g_ref[...]
    hn = _bf(_rms(x, gain))
    z = jnp.dot(hn, win_ref[...], preferred_element_type=F32)
    hh = _halo_rows(xp_ref, xn_ref, gain, i, nt)
    zh = jnp.dot(hh, win_ref[:, :RWKV_IN], preferred_element_type=F32)

    za = z[:, :RWKV_IN]
    zp = _shift_prev(za, zh[HALO - 1:HALO])
    zn = _shift_next(za, zh[HALO:HALO + 1])
    mup, mun = mup_ref[...], mun_ref[...]
    zs = za * (1.0 - mup - mun) + mup * zp + mun * zn
    r = zs[:, 0:w]
    k = zs[:, w:2 * w]
    v = zs[:, 2 * w:3 * w]
    wl = jnp.tanh(zs[:, 3 * w:3 * w + LORA_W])
    al = zs[:, 3 * w + LORA_W:3 * w + 2 * LORA_W]
    gl = zs[:, 3 * w + 2 * LORA_W:]

    wpre = w0_ref[...] + _dot(wl, w2_ref[...])
    lw = -math.exp(-0.5) * _sigmoid(wpre)
    a = _sigmoid(a0_ref[...] + _dot(al, a2_ref[...]))
    g = _dot(_sigmoid(gl), g2_ref[...])

    bd = bd_ref[...]
    kk0 = k * kk_ref[...]
    kk = kk0 * lax.rsqrt(_dot(kk0 * kk0, bd) + KK_EPS)
    ka = ka_ref[...]
    k0 = k * (1.0 + (a[:, :w] - 1.0) * ka)
    k1 = k * (1.0 + (a[:, w:] - 1.0) * ka)
    bonus = _dot(r * rk_ref[...] * (k0 + k1), bd)

    r_o[0] = _bf(r)
    v_o[0] = _bf(v)
    kk_o[0] = _bf(kk)
    lw0_o[0] = lw[:, :w]
    lw1_o[0] = lw[:, w:]
    be0_o[0] = _bf(kk * a[:, :w])
    be1_o[0] = _bf(kk * a[:, w:])
    k0_o[0] = _bf(k0)
    k1_o[0] = _bf(k1)
    g_o[0] = _bf(g)
    bv_o[0] = _bf(bonus * v)

    zg = z[:, RWKV_IN:]
    zg = zg * (0.5 * (1.0 + jnp.tanh(math.sqrt(2.0 / math.pi) * (zg + 0.044715 * (zg * zg * zg)))))
    u = zg[:, :SGU_WIDTH]
    vn = _bf(_rms(zg[:, SGU_WIDTH:], vg_ref[...]))
    gd = SGU_WIDTH // SGU_GROUPS
    for n in range(tm // SGU_CHUNK):
        rows = slice(n * SGU_CHUNK, (n + 1) * SGU_CHUNK)
        for gi in range(SGU_GROUPS):
            cols = slice(gi * gd, (gi + 1) * gd)
            s = jnp.dot(ws_ref[gi], vn[rows, cols], preferred_element_type=F32) + sb_ref[gi]
            b_o[0, rows, cols] = _bf(u[rows, cols] * s)


def _ab_in(x, gain, w_in, mu_prev, mu_next, w0, w2, a0, a2, g2, k_k, k_a, r_k, v_gain, w_s, s_b, tm):
    b, t, d = x.shape
    w = RWKV_WIDTH
    n_in = w_in.shape[1]
    half = LORA_W // 2

    def lora_cat(m):
        z = jnp.zeros((half, w), m.dtype)
        return _bf(jnp.concatenate([jnp.concatenate([m[0], z], axis=1),
                                    jnp.concatenate([z, m[1]], axis=1)], axis=0))

    head = jnp.arange(w) // HEAD_DIM
    bd = _bf(head[:, None] == head[None, :])
    sb = jnp.broadcast_to(s_b[:, :, None], (SGU_GROUPS, SGU_CHUNK, SGU_WIDTH // SGU_GROUPS))
    main, prev, nxt = _row_specs(tm, d, t)
    ospec = pl.BlockSpec((1, tm, w), lambda i, j: (i, j, 0))
    return pl.pallas_call(
        _ab_in_kernel,
        grid=(b, t // tm),
        in_specs=[main, prev, nxt,
                  _const_spec((1, d)),
                  _const_spec((d, n_in)),
                  _const_spec((1, RWKV_IN)), _const_spec((1, RWKV_IN)),
                  _const_spec((1, 2 * w)), _const_spec((LORA_W, 2 * w)),
                  _const_spec((1, 2 * w)), _const_spec((LORA_W, 2 * w)),
                  _const_spec((LORA_W, w)),
                  _const_spec((1, w)), _const_spec((1, w)), _const_spec((1, w)),
                  _const_spec((w, w)),
                  _const_spec((1, SGU_WIDTH)),
                  _const_spec((SGU_GROUPS, SGU_CHUNK, SGU_CHUNK)),
                  _const_spec((SGU_GROUPS, SGU_CHUNK, SGU_WIDTH // SGU_GROUPS))],
        out_specs=[ospec] * 12,
        out_shape=[jax.ShapeDtypeStruct((b, t, w), dt) for dt in AB_IN_OUT_DTYPES],
        compiler_params=_params(2),
        name="ab_in",
    )(x, x, x, gain.reshape(1, d), _bf(w_in), mu_prev.reshape(1, -1), mu_next.reshape(1, -1),
      w0.reshape(1, 2 * w), lora_cat(w2), a0.reshape(1, 2 * w), lora_cat(a2), _bf(g2),
      k_k.reshape(1, w), k_a.reshape(1, w), r_k.reshape(1, w), bd,
      v_gain.reshape(1, -1), _bf(w_s), sb)


def _stack_heads(x):
    even = (_iota(x.shape, 1) & (LANES - 1)) < HEAD_DIM
    return jnp.concatenate([jnp.where(even, x, 0.0), jnp.where(even, 0.0, x)], axis=0)


def _rwkv_prep(r, v, kk, lw, be, kd, reverse):
    c = r.shape[0]
    rr = _iota((c, LANES), 0)
    cc = _iota((c, LANES), 1) & (HEAD_DIM - 1)
    if reverse:
        strict, incl = cc > rr, cc >= rr
    else:
        strict, incl = cc < rr, cc <= rr
    tri = _bf(incl[:, :c])

    l1 = _bf(lw)
    l2 = _bf(lw - l1.astype(F32))
    l3 = _bf(lw - l1.astype(F32) - l2.astype(F32))
    cum = (jnp.dot(tri, l1, preferred_element_type=F32) + jnp.dot(tri, l2, preferred_element_type=F32)
           + jnp.dot(tri, l3, preferred_element_type=F32))
    tot = cum[0:1] if reverse else cum[c - 1:c]
    e_neg = jnp.exp(-cum)
    e_end = jnp.exp(tot - cum)
    return dict(ag=-kk * jnp.exp(cum - lw), rg=r * jnp.exp(cum), bi=be * e_neg, ki=kd * e_neg,
                bee=be * e_end, kee=kd * e_end, gam=jnp.exp(tot), v=v, strict=strict, incl=incl)


def _rwkv_local(chunks):
    c = chunks[0]["v"].shape[0]
    n_tiles = chunks[0]["v"].shape[1] // LANES
    r128 = _iota((LANES, LANES), 0)
    c128 = _iota((LANES, LANES), 1)
    same_head = (r128 < HEAD_DIM) == (c128 < HEAD_DIM)
    diag = r128 == c128
    ctx = [(ch, slice(p * LANES, (p + 1) * LANES)) for ch in chunks for p in range(n_tiles)]

    aa = [_dot_nt(jnp.concatenate([ch["ag"][:, sl], ch["rg"][:, sl]], axis=0),
                  jnp.concatenate([_stack_heads(ch["bi"][:, sl]), _stack_heads(ch["ki"][:, sl])], axis=0))
          for ch, sl in ctx]
    a_ab = [jnp.where(ch["strict"], m[:c, :LANES], 0.0) for (ch, _), m in zip(ctx, aa)]
    a_rb = [jnp.where(ch["incl"], m[c:, :LANES], 0.0) for (ch, _), m in zip(ctx, aa)]
    a_k = [jnp.concatenate([jnp.where(ch["strict"], m[:c, LANES:], 0.0),
                            jnp.where(ch["incl"], m[c:, LANES:], 0.0)], axis=0) for (ch, _), m in zip(ctx, aa)]
    xv = [_dot(m, _stack_heads(ch["v"][:, sl])) for (ch, sl), m in zip(ctx, a_k)]
    x = [jnp.concatenate([ch["ag"][:, sl], m[:c]], axis=1) for (ch, sl), m in zip(ctx, xv)]
    pw = a_ab
    n_round = int(math.log2(c))
    for j in range(n_round):
        x = [xi + _dot(pi, _stack_heads(xi)) for xi, pi in zip(x, pw)]
        if j < n_round - 1:
            pw = [_dot(pi, jnp.where(same_head, jnp.concatenate([pi, pi], axis=0), 0.0)) for pi in pw]
    ry = [jnp.concatenate([ch["rg"][:, sl], m[c:]], axis=1) + _dot(ai, _stack_heads(xi))
          for (ch, sl), m, ai, xi in zip(ctx, xv, a_rb, x)]
    mg = [_dot(jnp.concatenate([ch["bee"][:, sl], ch["kee"][:, sl]], axis=0).T,
               jnp.concatenate([xi, jnp.concatenate([jnp.zeros_like(ch["v"][:, sl]), ch["v"][:, sl]], axis=1)],
                               axis=0))
          for (ch, sl), xi in zip(ctx, x)]
    out = []
    for (ch, sl), ryi, mgi in zip(ctx, ry, mg):
        gam_col = jnp.sum(jnp.where(diag, ch["gam"][:, sl], 0.0), axis=1, keepdims=True)
        out.append((ryi[:, :LANES], ryi[:, LANES:], jnp.where(same_head, mgi[:, :LANES], 0.0),
                    jnp.where(same_head, mgi[:, LANES:], 0.0), gam_col))
    return [out[i * n_tiles:(i + 1) * n_tiles] for i in range(len(chunks))]


def _rwkv_kernel(rf, vf, kkf, lwf, bef, kf, rb, vb, kkb, lwb, beb, kb, yf_o, yb_o, zf_ref, zb_ref):
    @pl.when(pl.program_id(1) == 0)
    def _():
        zf_ref[...] = jnp.zeros_like(zf_ref)
        zb_ref[...] = jnp.zeros_like(zb_ref)

    c = RWKV_CHUNK
    n = rf.shape[1] // c
    n_tiles = rf.shape[2] // LANES
    chunks = []
    for refs, reverse in (((rf, vf, kkf, lwf, bef, kf), False), ((rb, vb, kkb, lwb, beb, kb), True)):
        for ci in range(n):
            chunks.append(_rwkv_prep(*[ref[0, ci * c:(ci + 1) * c, :].astype(F32) for ref in refs], reverse))
    local = _rwkv_local(chunks)

    for d, (z_ref, y_o) in enumerate(((zf_ref, yf_o), (zb_ref, yb_o))):
        order = range(n) if d == 0 else range(n - 1, -1, -1)
        zs = [z_ref[p] for p in range(n_tiles)]
        for ci in order:
            ys = []
            for p in range(n_tiles):
                rt, yl, mt, gt, gam_col = local[d * n + ci][p]
                yz = _dot(jnp.concatenate([rt, mt], axis=0), zs[p])
                ys.append(yz[:c] + yl)
                zs[p] = gam_col * zs[p] + yz[c:] + gt
            y_o[0, ci * c:(ci + 1) * c, :] = jnp.concatenate(ys, axis=1)
        for p in range(n_tiles):
            z_ref[p] = zs[p]


def _rwkv_scan(r, v, kk, lw0, lw1, be0, be1, k0, k1):
    b, t, w = r.shape
    c = RWKV_CHUNK * RWKV_STEP_CHUNKS
    nc = t // c
    fwd = pl.BlockSpec((1, c, w), lambda i, j: (i, j, 0))
    bwd = pl.BlockSpec((1, c, w), lambda i, j: (i, nc - 1 - j, 0))
    out = jax.ShapeDtypeStruct((b, t, w), F32)
    zshape = pltpu.VMEM((w // LANES, LANES, LANES), F32)
    return pl.pallas_call(
        _rwkv_kernel,
        grid=(b, nc),
        in_specs=[fwd] * 6 + [bwd] * 6,
        out_specs=[fwd, bwd],
        out_shape=[out, out],
        scratch_shapes=[zshape, zshape],
        compiler_params=_params(2),
        name="rwkv_scan",
    )(r, v, kk, lw0, be0, k0, r, v, kk, lw1, be1, k1)


def _diff_in_kernel(x_ref, g_ref, w_ref, qg_ref, kg_ref, bd_ref, q_o, k_o, vt_o):
    x = x_ref[0]
    d = x.shape[-1]
    hn = _bf(_rms(x, g_ref[...]))
    z = jnp.dot(hn, w_ref[...], preferred_element_type=F32)
    bd = bd_ref[...]
    half = bd.shape[0]

    def qk_norm(y, gain, scale):
        parts = []
        for j in range(d // half):
            yj = y[:, j * half:(j + 1) * half]
            ms = _dot(yj * yj, bd) * (1.0 / HEAD_DIM)
            parts.append(yj * lax.rsqrt(ms + NORM_EPS) * gain * scale)
        return _bf(jnp.concatenate(parts, axis=1))

    q_o[0] = qk_norm(z[:, :d], qg_ref[...], LOG2E / math.sqrt(HEAD_DIM))
    k_o[0] = qk_norm(z[:, d:2 * d], kg_ref[...], 1.0)
    tm = x.shape[0]
    for h in range(d // DIFF_V_DIM):
        vt_o[0, h, :DIFF_V_DIM, :] = _bf(z[:, 2 * d + h * DIFF_V_DIM:2 * d + (h + 1) * DIFF_V_DIM].T)
        vt_o[0, h, DIFF_V_DIM:, :] = jnp.ones((ONES_ROWS, tm), BF16)


def _diff_in(x, gain, w_in, q_gain, k_gain, tm):
    b, t, d = x.shape
    half = 512
    nh = d // DIFF_V_DIM
    head = jnp.arange(half) // HEAD_DIM
    bd = _bf(head[:, None] == head[None, :])
    xs = pl.BlockSpec((1, tm, d), lambda i, j: (i, j, 0))
    out = jax.ShapeDtypeStruct((b, t, d), BF16)
    vt_rows = DIFF_V_DIM + ONES_ROWS
    return pl.pallas_call(
        _diff_in_kernel,
        grid=(b, t // tm),
        in_specs=[xs, _const_spec((1, d)), _const_spec((d, 3 * d)),
                  _const_spec((1, half)), _const_spec((1, half)), _const_spec((half, half))],
        out_specs=[xs, xs, pl.BlockSpec((1, nh, vt_rows, tm), lambda i, j: (i, 0, 0, j))],
        out_shape=[out, out, jax.ShapeDtypeStruct((b, nh, vt_rows, t), BF16)],
        compiler_params=_params(2),
        name="diff_in",
    )(x, gain.reshape(1, d), _bf(w_in), jnp.tile(q_gain, half // HEAD_DIM).reshape(1, half),
      jnp.tile(k_gain, half // HEAD_DIM).reshape(1, half), bd)


def _diff_attn_kernel(q_ref, k_ref, vt_ref, slope_ref, lam_ref, og_ref, o_ref, bias_ref, s_ref, p_ref, *,
                      lambda_init):
    t, tq = bias_ref.shape

    @pl.when(pl.program_id(2) == 0)
    def _():
        dist = jnp.abs((_iota((t, tq), 0) - pl.program_id(1) * tq - _iota((t, tq), 1)).astype(F32))
        bias_ref[...] = dist * (-LOG2E * slope_ref[0][:, 0:1])

    q = q_ref[0]
    first = _iota(q.shape, 1) < HEAD_DIM
    zero = jnp.zeros_like(q)
    kb = min(KEY_BLOCK, t)
    qb = min(QUERY_BLOCK, tq)
    sub = HALO

    qcs = (jnp.where(first, q, zero), jnp.where(first, zero, q))
    secs = [(c, slice(i * qb, (i + 1) * qb)) for c in range(2) for i in range(tq // qb)]
    n_kb = t // kb

    def score_block(sec, j, m):
        c, qs = sec
        ks = slice(j * kb, (j + 1) * kb)
        s = lax.dot_general(k_ref[0, ks, :], qcs[c][qs], (((1,), (1,)), ((), ())),
                            preferred_element_type=F32) + bias_ref[ks, qs]
        s_ref[c, ks, qs] = s
        return jnp.maximum(m, jnp.max(s.reshape(kb // sub, sub, qb), axis=0))

    def prob_block(sec, j, m):
        c, qs = sec
        ks = slice(j * kb, (j + 1) * kb)
        p_ref[c, ks, qs] = _bf(jnp.exp2(s_ref[c, ks, qs] - m))

    m_init = jnp.full((sub, qb), -jnp.inf, F32)
    m = m_init
    for j in range(n_kb):
        m = score_block(secs[0], j, m)
    m_prev = jnp.max(m, axis=0, keepdims=True)
    outs = []
    for n in range(1, len(secs) + 1):
        m = m_init
        for j in range(n_kb):
            if n < len(secs):
                m = score_block(secs[n], j, m)
            prob_block(secs[n - 1], j, m_prev)
        m_prev = jnp.max(m, axis=0, keepdims=True)
        c, qs = secs[n - 1]
        outs.append(jnp.dot(vt_ref[0, 0], p_ref[c, :, qs], preferred_element_type=F32))
    per_c = len(secs) // 2
    acc0 = jnp.concatenate(outs[:per_c], axis=1)
    acc1 = jnp.concatenate(outs[per_c:], axis=1)
    o0, s0 = acc0[:DIFF_V_DIM], acc0[DIFF_V_DIM:DIFF_V_DIM + 1]
    o1, s1 = acc1[:DIFF_V_DIM], acc1[DIFF_V_DIM:DIFF_V_DIM + 1]
    lv = lam_ref[...]
    lam = (jnp.exp(jnp.sum(lv[0:1] * lv[1:2], axis=-1, keepdims=True))
           - jnp.exp(jnp.sum(lv[2:3] * lv[3:4], axis=-1, keepdims=True)) + lambda_init)
    o = (o0 * (1.0 / s0) - o1 * (lam / s1)).T
    o_ref[0] = _bf(_rms(o, og_ref[...]) * (1.0 - lambda_init))


def _diff_attention(q, k, vt, lam_vecs, out_gain, lambda_init, tq):
    b, t, d = q.shape
    nh = d // DIFF_V_DIM
    slopes = 2.0 ** (-8.0 * jnp.arange(1, nh + 1, dtype=F32) / nh)
    slopes = jnp.broadcast_to(slopes[:, None, None], (nh, 1, LANES))
    return pl.pallas_call(
        functools.partial(_diff_attn_kernel, lambda_init=lambda_init),
        grid=(nh, t // tq, b),
        in_specs=[pl.BlockSpec((1, tq, DIFF_V_DIM), lambda h, j, i: (i, j, h)),
                  pl.BlockSpec((1, t, DIFF_V_DIM), lambda h, j, i: (i, 0, h)),
                  pl.BlockSpec((1, 1, vt.shape[2], t), lambda h, j, i: (i, h, 0, 0)),
                  pl.BlockSpec((1, 1, LANES), lambda h, j, i: (h, 0, 0)),
                  _const_spec((4, HEAD_DIM)),
                  _const_spec((1, DIFF_V_DIM))],
        out_specs=pl.BlockSpec((1, tq, DIFF_V_DIM), lambda h, j, i: (i, j, h)),
        out_shape=jax.ShapeDtypeStruct((b, t, d), BF16),
        scratch_shapes=[pltpu.VMEM((t, tq), F32), pltpu.VMEM((2, t, tq), F32), pltpu.VMEM((2, t, tq), BF16)],
        compiler_params=_params(3),
        name="diff_attn",
    )(q, k, vt, slopes, lam_vecs, out_gain.reshape(1, DIFF_V_DIM))


def kernel(x, mem, mem_norm, mix_norm, cross_norm, ffn_norm, ab_w_in, ab_shift_prev, ab_shift_next, rwkv_w0, rwkv_w2, rwkv_a0, rwkv_a2, rwkv_g2, rwkv_k_k, rwkv_k_a, rwkv_r_k, rwkv_out_gain, sgu_v_gain, sgu_w_s, sgu_b, ab_w_out, diff_w_in, diff_q_gain, diff_k_gain, diff_lambda_q1, diff_lambda_k1, diff_lambda_q2, diff_lambda_k2, diff_out_gain, diff_w_out, cross_wq, cross_wkv, cross_q_gain, cross_k_gain, cross_wo, ffn_w_up, ffn_conv_w, ffn_conv_b, ffn_w_down):
    b, t, d = x.shape
    depth = mix_norm.shape[0]
    tm = min(512, t)
    tq = min(512, t)
    assert t % tm == 0 and t % (RWKV_CHUNK * RWKV_STEP_CHUNKS) == 0 and tm % SGU_CHUNK == 0

    k_all, v_all = _mem_kv(mem, mem_norm, cross_wkv, cross_k_gain)
    for l in range(depth):
        if l % 2 == 0:
            e = l // 2
            (r, v, kk, lw0, lw1, be0, be1, k0, k1, g, bv, b_out) = _ab_in(
                x, mix_norm[l], ab_w_in[e], ab_shift_prev[e], ab_shift_next[e],
                rwkv_w0[e], rwkv_w2[e], rwkv_a0[e], rwkv_a2[e], rwkv_g2[e],
                rwkv_k_k[e], rwkv_k_a[e], rwkv_r_k[e], sgu_v_gain[e], sgu_w_s[e], sgu_b[e], tm)
            yf, yb = _rwkv_scan(r, v, kk, lw0, lw1, be0, be1, k0, k1)
            head = jnp.arange(RWKV_WIDTH) // HEAD_DIM
            mix = (_mix_rwkv_sgu, (yf, yb, bv, g, b_out),
                   (rwkv_out_gain[e].reshape(1, RWKV_WIDTH), _bf(head[:, None] == head[None, :]), _bf(ab_w_out[e])))
        else:
            o = l // 2
            lambda_init = 0.8 - 0.6 * math.exp(-0.3 * l)
            q, k, vt = _diff_in(x, mix_norm[l], diff_w_in[o], diff_q_gain[o], diff_k_gain[o], tm)
            lam_vecs = jnp.stack([diff_lambda_q1[o], diff_lambda_k1[o], diff_lambda_q2[o], diff_lambda_k2[o]])
            c_out = _diff_attention(q, k, vt, lam_vecs, diff_out_gain[o], lambda_init, tq)
            mix = (_mix_proj, (c_out,), (_bf(diff_w_out[o]),))
        x = _cross_attention(x, *mix, cross_norm[l], cross_wq[l], cross_q_gain[l], k_all, v_all, cross_wo[l], l, tm)
        x = _conv_ffn(x, ffn_norm[l], ffn_w_up[l], ffn_conv_w[l], ffn_conv_b[l], ffn_w_down[l], tm)
    return x
```

```python
import functools
import math

import jax
import jax.numpy as jnp
from jax import lax
from jax.experimental import pallas as pl
from jax.experimental.pallas import tpu as pltpu

NORM_EPS = 1e-6
KK_EPS = 1e-12
RWKV_HEADS = 8
HEAD_DIM = 64
RWKV_WIDTH = RWKV_HEADS * HEAD_DIM
LORA_W = 128
RWKV_IN = 3 * RWKV_WIDTH + 3 * LORA_W
SGU_WIDTH = 512
SGU_GROUPS = 4
SGU_CHUNK = 128
DIFF_HEADS = 8
DIFF_V_DIM = 2 * HEAD_DIM
CROSS_HEADS = 4
CONV_WIDTH = 3
RWKV_CHUNK = 64
RWKV_STEP_CHUNKS = 4
LANES = 128
HALO = 8
FF_CHUNK = 256
KEY_BLOCK = 128
QUERY_BLOCK = 256
ONES_ROWS = 16
SAFE_EXP2_SPAN = 100.0
VMEM_LIMIT = 56 * 1024 * 1024
LOG2E = math.log2(math.e)

F32 = jnp.float32
BF16 = jnp.bfloat16
AB_IN_OUT_DTYPES = (BF16, BF16, BF16, F32, F32, BF16, BF16, BF16, BF16, BF16, BF16, BF16)


def _bf(x):
    return x.astype(BF16)


def _dot(a, b):
    return jnp.dot(_bf(a), _bf(b), preferred_element_type=F32)


def _dot_nt(a, b):
    return lax.dot_general(_bf(a), _bf(b), (((1,), (1,)), ((), ())), preferred_element_type=F32)


def _rms(x, gain):
    return x * lax.rsqrt(jnp.mean(x * x, axis=-1, keepdims=True) + NORM_EPS) * gain


def _sigmoid(x):
    return 1.0 / (1.0 + jnp.exp(-x))


def _iota(shape, dim):
    return lax.broadcasted_iota(jnp.int32, shape, dim)


def _shift_prev(u, halo_row):
    rolled = pltpu.roll(u, 1, 0)
    head = rolled[:HALO]
    head = jnp.where(_iota(head.shape, 0) == 0, halo_row, head)
    return jnp.concatenate([head, rolled[HALO:]], axis=0)


def _shift_next(u, halo_row):
    n = u.shape[0]
    rolled = pltpu.roll(u, n - 1, 0)
    tail = rolled[n - HALO:]
    tail = jnp.where(_iota(tail.shape, 0) == HALO - 1, halo_row, tail)
    return jnp.concatenate([rolled[:n - HALO], tail], axis=0)


def _halo_rows(xp_ref, xn_ref, gain, i, nt):
    xh = jnp.concatenate([xp_ref[0], xn_ref[0]], axis=0)
    hh = _rms(xh, gain)
    row = _iota(hh.shape, 0)
    has_prev = jnp.where(i > 0, 1.0, 0.0)
    has_next = jnp.where(i < nt - 1, 1.0, 0.0)
    return _bf(hh * jnp.where(row < HALO, has_prev, has_next))


def _const_spec(shape):
    zeros = (0,) * len(shape)
    return pl.BlockSpec(shape, lambda *_: zeros, pipeline_mode=pl.Buffered(1))


def _params(n_axes):
    return pltpu.CompilerParams(dimension_semantics=("arbitrary",) * n_axes,
                                vmem_limit_bytes=VMEM_LIMIT)


def _row_specs(tm, d, t):
    nb = tm // HALO
    last = t // HALO - 1
    main = pl.BlockSpec((1, tm, d), lambda b, i: (b, i, 0))
    prev = pl.BlockSpec((1, HALO, d), lambda b, i: (b, jnp.maximum(i * nb - 1, 0), 0))
    nxt = pl.BlockSpec((1, HALO, d), lambda b, i: (b, jnp.minimum((i + 1) * nb, last), 0))
    return main, prev, nxt


def _mem_kv_kernel(mem_ref, mnorm_ref, wkv_ref, kg_ref, k_ref, v_ref):
    d = mem_ref.shape[-1]
    hd = d // CROSS_HEADS
    mn = _rms(mem_ref[0], mnorm_ref[...])
    kv = jnp.dot(_bf(mn), wkv_ref[0], preferred_element_type=F32)
    for h in range(CROSS_HEADS):
        kh = _rms(kv[:, h * hd:(h + 1) * hd], kg_ref[0]) * (1.0 / math.sqrt(hd))
        k_ref[0, 0, :, h * hd:(h + 1) * hd] = _bf(kh)
    v_ref[0, 0] = _bf(kv[:, d:])


def _mem_kv(mem, mem_norm, wkv, k_gain):
    b, m, d = mem.shape
    nl = wkv.shape[0]
    out = jax.ShapeDtypeStruct((nl, b, m, d), BF16)
    return pl.pallas_call(
        _mem_kv_kernel,
        grid=(nl, b),
        in_specs=[pl.BlockSpec((1, m, d), lambda l, i: (i, 0, 0)),
                  pl.BlockSpec((1, d), lambda l, i: (0, 0)),
                  pl.BlockSpec((1, d, 2 * d), lambda l, i: (l, 0, 0)),
                  pl.BlockSpec((1, 1, d // CROSS_HEADS), lambda l, i: (l, 0, 0))],
        out_specs=[pl.BlockSpec((1, 1, m, d), lambda l, i: (l, i, 0, 0)),
                   pl.BlockSpec((1, 1, m, d), lambda l, i: (l, i, 0, 0))],
        out_shape=[out, out],
        compiler_params=_params(2),
        name="mem_kv",
    )(mem, mem_norm.reshape(1, d), _bf(wkv), k_gain.reshape(nl, 1, -1))


def _mix_rwkv_sgu(x, yf_ref, yb_ref, bv_ref, g_ref, b_ref, og_ref, bd_ref, wo_ref):
    y = yf_ref[0] + yb_ref[0]
    ms = _dot(y * y, bd_ref[...]) * (1.0 / HEAD_DIM)
    yn = y * lax.rsqrt(ms + NORM_EPS) * og_ref[...]
    a_out = (yn + bv_ref[0].astype(F32)) * g_ref[0].astype(F32)
    cat = jnp.concatenate([_bf(a_out), b_ref[0]], axis=1)
    return x + jnp.dot(cat, wo_ref[...], preferred_element_type=F32)


def _mix_proj(x, a_ref, w_ref):
    return x + jnp.dot(a_ref[0], w_ref[...], preferred_element_type=F32)


def _cross_kernel(*refs, n_mix, mix_fn):
    x_ref, mix_refs = refs[0], refs[1:1 + n_mix]
    g_ref, wq_ref, qg_ref, k_ref, v_ref, wo_ref, o_ref = refs[1 + n_mix:]
    x = mix_fn(x_ref[0], *mix_refs)
    d = x.shape[-1]
    hd = d // CROSS_HEADS
    hn = _bf(_rms(x, g_ref[...]))
    q = jnp.dot(hn, wq_ref[...], preferred_element_type=F32)
    outs = []
    for h in range(CROSS_HEADS):
        sl = slice(h * hd, (h + 1) * hd)
        qh = _rms(q[:, sl], qg_ref[...])
        logits = _dot_nt(qh, k_ref[0, 0, :, sl])
        p = jnp.exp(logits - jnp.max(logits, axis=-1, keepdims=True))
        s = jnp.sum(p, axis=-1, keepdims=True)
        outs.append(jnp.dot(_bf(p), v_ref[0, 0, :, sl], preferred_element_type=F32) / s)
    o = jnp.concatenate(outs, axis=1)
    o_ref[0] = x + jnp.dot(_bf(o), wo_ref[...], preferred_element_type=F32)


def _cross_attention(x, mix_fn, mix_rows, mix_consts, gain, wq, q_gain, k_all, v_all, wo, layer, tq):
    b, t, d = x.shape
    m = k_all.shape[2]
    hd = d // CROSS_HEADS
    mix_specs = ([pl.BlockSpec((1, tq, a.shape[-1]), lambda i, j: (i, j, 0)) for a in mix_rows]
                 + [_const_spec(a.shape) for a in mix_consts])
    return pl.pallas_call(
        functools.partial(_cross_kernel, n_mix=len(mix_specs), mix_fn=mix_fn),
        grid=(b, t // tq),
        in_specs=[pl.BlockSpec((1, tq, d), lambda i, j: (i, j, 0)),
                  *mix_specs,
                  _const_spec((1, d)),
                  _const_spec((d, d)),
                  _const_spec((1, hd)),
                  pl.BlockSpec((1, 1, m, d), lambda i, j: (layer, i, 0, 0)),
                  pl.BlockSpec((1, 1, m, d), lambda i, j: (layer, i, 0, 0)),
                  _const_spec((d, d))],
        out_specs=pl.BlockSpec((1, tq, d), lambda i, j: (i, j, 0)),
        out_shape=jax.ShapeDtypeStruct((b, t, d), F32),
        compiler_params=_params(2),
        name=f"cross_attn_{layer}",
    )(x, *mix_rows, *mix_consts, gain.reshape(1, d), _bf(wq), q_gain.reshape(1, hd), k_all, v_all, _bf(wo))


def _ffn_kernel(x_ref, xp_ref, xn_ref, g_ref, wup_ref, cw_ref, cb_ref, wdn_ref, o_ref,
                hn_ref, hh_ref, acc_ref):
    i = pl.program_id(1)
    nt = pl.num_programs(1)
    n_chunks = wdn_ref.shape[0]
    x = x_ref[0]
    gain = g_ref[...]
    hn_ref[...] = _bf(_rms(x, gain))
    hh_ref[...] = _halo_rows(xp_ref, xn_ref, gain, i, nt)
    acc_ref[...] = jnp.zeros_like(acc_ref)

    def up_proj(idx):
        w = wup_ref[idx]
        return (jnp.dot(hn_ref[...], w, preferred_element_type=F32),
                jnp.dot(hh_ref[...], w, preferred_element_type=F32))

    def conv(idx, u, uh):
        cw = cw_ref[idx]
        up = _shift_prev(u, uh[HALO - 1:HALO])
        un = _shift_next(u, uh[HALO:HALO + 1])
        return cb_ref[idx] + up * cw[0:1] + u * cw[1:2] + un * cw[2:3]

    ups = [up_proj(0), up_proj(n_chunks)]
    for c in range(n_chunks):
        nxt = [up_proj(c + 1), up_proj(c + 1 + n_chunks)] if c + 1 < n_chunks else None
        gate = conv(c, *ups[0])
        val = conv(c + n_chunks, *ups[1])
        act = gate * _sigmoid(gate) * val
        acc_ref[...] += jnp.dot(_bf(act), wdn_ref[c], preferred_element_type=F32)
        ups = nxt
    o_ref[0] = x + acc_ref[...]


def _conv_ffn(x, gain, w_up, conv_w, conv_b, w_down, tm):
    b, t, d = x.shape
    ff = w_down.shape[0]
    nc = ff // FF_CHUNK
    wup = _bf(w_up).reshape(d, 2 * nc, FF_CHUNK).transpose(1, 0, 2)
    cw = conv_w.reshape(CONV_WIDTH, 2 * nc, FF_CHUNK).transpose(1, 0, 2)
    cb = conv_b.reshape(2 * nc, 1, FF_CHUNK)
    wdn = _bf(w_down).reshape(nc, FF_CHUNK, d)
    main, prev, nxt = _row_specs(tm, d, t)
    return pl.pallas_call(
        _ffn_kernel,
        grid=(b, t // tm),
        in_specs=[main, prev, nxt,
                  _const_spec((1, d)),
                  _const_spec((2 * nc, d, FF_CHUNK)),
                  _const_spec((2 * nc, CONV_WIDTH, FF_CHUNK)),
                  _const_spec((2 * nc, 1, FF_CHUNK)),
                  _const_spec((nc, FF_CHUNK, d))],
        out_specs=pl.BlockSpec((1, tm, d), lambda i, j: (i, j, 0)),
        out_shape=jax.ShapeDtypeStruct((b, t, d), F32),
        scratch_shapes=[pltpu.VMEM((tm, d), BF16),
                        pltpu.VMEM((2 * HALO, d), BF16),
                        pltpu.VMEM((tm, d), F32)],
        compiler_params=_params(2),
        name="conv_ffn",
    )(x, x, x, gain.reshape(1, d), wup, cw, cb, wdn)


def _ab_in_kernel(x_ref, xp_ref, xn_ref, g_ref, win_ref, mup_ref, mun_ref,
                  w0_ref, w2_ref, a0_ref, a2_ref, g2_ref, kk_ref, ka_ref, rk_ref, bd_ref,
                  vg_ref, ws_ref, sb_ref,
                  r_o, v_o, kk_o, lw0_o, lw1_o, be0_o, be1_o, k0_o, k1_o, g_o, bv_o, b_o):
    i = pl.program_id(1)
    nt = pl.num_programs(1)
    w = RWKV_WIDTH
    x = x_ref[0]
    tm = x.shape[0]
    gain = g_ref[...]
    hn = _bf(_rms(x, gain))
    z = jnp.dot(hn, win_ref[...], preferred_element_type=F32)
    hh = _halo_rows(xp_ref, xn_ref, gain, i, nt)
    zh = jnp.dot(hh, win_ref[:, :RWKV_IN], preferred_element_type=F32)

    za = z[:, :RWKV_IN]
    zp = _shift_prev(za, zh[HALO - 1:HALO])
    zn = _shift_next(za, zh[HALO:HALO + 1])
    mup, mun = mup_ref[...], mun_ref[...]
    zs = za * (1.0 - mup - mun) + mup * zp + mun * zn
    r = zs[:, 0:w]
    k = zs[:, w:2 * w]
    v = zs[:, 2 * w:3 * w]
    wl = jnp.tanh(zs[:, 3 * w:3 * w + LORA_W])
    al = zs[:, 3 * w + LORA_W:3 * w + 2 * LORA_W]
    gl = zs[:, 3 * w + 2 * LORA_W:]

    wpre = w0_ref[...] + _dot(wl, w2_ref[...])
    lw = -math.exp(-0.5) * _sigmoid(wpre)
    a = _sigmoid(a0_ref[...] + _dot(al, a2_ref[...]))
    g = _dot(_sigmoid(gl), g2_ref[...])

    bd = bd_ref[...]
    kk0 = k * kk_ref[...]
    kk = kk0 * lax.rsqrt(_dot(kk0 * kk0, bd) + KK_EPS)
    ka = ka_ref[...]
    k0 = k * (1.0 + (a[:, :w] - 1.0) * ka)
    k1 = k * (1.0 + (a[:, w:] - 1.0) * ka)
    bonus = _dot(r * rk_ref[...] * (k0 + k1), bd)

    r_o[0] = _bf(r)
    v_o[0] = _bf(v)
    kk_o[0] = _bf(kk)
    lw0_o[0] = lw[:, :w]
    lw1_o[0] = lw[:, w:]
    be0_o[0] = _bf(kk * a[:, :w])
    be1_o[0] = _bf(kk * a[:, w:])
    k0_o[0] = _bf(k0)
    k1_o[0] = _bf(k1)
    g_o[0] = _bf(g)
    bv_o[0] = _bf(bonus * v)

    zg = z[:, RWKV_IN:]
    zg = zg * (0.5 * (1.0 + jnp.tanh(math.sqrt(2.0 / math.pi) * (zg + 0.044715 * (zg * zg * zg)))))
    u = zg[:, :SGU_WIDTH]
    vn = _bf(_rms(zg[:, SGU_WIDTH:], vg_ref[...]))
    gd = SGU_WIDTH // SGU_GROUPS
    for n in range(tm // SGU_CHUNK):
        rows = slice(n * SGU_CHUNK, (n + 1) * SGU_CHUNK)
        for gi in range(SGU_GROUPS):
            cols = slice(gi * gd, (gi + 1) * gd)
            s = jnp.dot(ws_ref[gi], vn[rows, cols], preferred_element_type=F32) + sb_ref[gi]
            b_o[0, rows, cols] = _bf(u[rows, cols] * s)


def _ab_in(x, gain, w_in, mu_prev, mu_next, w0, w2, a0, a2, g2, k_k, k_a, r_k, v_gain, w_s, s_b, tm):
    b, t, d = x.shape
    w = RWKV_WIDTH
    n_in = w_in.shape[1]
    half = LORA_W // 2

    def lora_cat(m):
        z = jnp.zeros((half, w), m.dtype)
        return _bf(jnp.concatenate([jnp.concatenate([m[0], z], axis=1),
                                    jnp.concatenate([z, m[1]], axis=1)], axis=0))

    head = jnp.arange(w) // HEAD_DIM
    bd = _bf(head[:, None] == head[None, :])
    sb = jnp.broadcast_to(s_b[:, :, None], (SGU_GROUPS, SGU_CHUNK, SGU_WIDTH // SGU_GROUPS))
    main, prev, nxt = _row_specs(tm, d, t)
    ospec = pl.BlockSpec((1, tm, w), lambda i, j: (i, j, 0))
    return pl.pallas_call(
        _ab_in_kernel,
        grid=(b, t // tm),
        in_specs=[main, prev, nxt,
                  _const_spec((1, d)),
                  _const_spec((d, n_in)),
                  _const_spec((1, RWKV_IN)), _const_spec((1, RWKV_IN)),
                  _const_spec((1, 2 * w)), _const_spec((LORA_W, 2 * w)),
                  _const_spec((1, 2 * w)), _const_spec((LORA_W, 2 * w)),
                  _const_spec((LORA_W, w)),
                  _const_spec((1, w)), _const_spec((1, w)), _const_spec((1, w)),
                  _const_spec((w, w)),
                  _const_spec((1, SGU_WIDTH)),
                  _const_spec((SGU_GROUPS, SGU_CHUNK, SGU_CHUNK)),
                  _const_spec((SGU_GROUPS, SGU_CHUNK, SGU_WIDTH // SGU_GROUPS))],
        out_specs=[ospec] * 12,
        out_shape=[jax.ShapeDtypeStruct((b, t, w), dt) for dt in AB_IN_OUT_DTYPES],
        compiler_params=_params(2),
        name="ab_in",
    )(x, x, x, gain.reshape(1, d), _bf(w_in), mu_prev.reshape(1, -1), mu_next.reshape(1, -1),
      w0.reshape(1, 2 * w), lora_cat(w2), a0.reshape(1, 2 * w), lora_cat(a2), _bf(g2),
      k_k.reshape(1, w), k_a.reshape(1, w), r_k.reshape(1, w), bd,
      v_gain.reshape(1, -1), _bf(w_s), sb)


def _stack_heads(x):
    even = (_iota(x.shape, 1) & (LANES - 1)) < HEAD_DIM
    return jnp.concatenate([jnp.where(even, x, 0.0), jnp.where(even, 0.0, x)], axis=0)


def _rwkv_prep(r, v, kk, lw, be, kd, reverse):
    c = r.shape[0]
    rr = _iota((c, LANES), 0)
    cc = _iota((c, LANES), 1) & (HEAD_DIM - 1)
    if reverse:
        strict, incl = cc > rr, cc >= rr
    else:
        strict, incl = cc < rr, cc <= rr
    tri = _bf(incl[:, :c])

    l1 = _bf(lw)
    l2 = _bf(lw - l1.astype(F32))
    l3 = _bf(lw - l1.astype(F32) - l2.astype(F32))
    cum = (jnp.dot(tri, l1, preferred_element_type=F32) + jnp.dot(tri, l2, preferred_element_type=F32)
           + jnp.dot(tri, l3, preferred_element_type=F32))
    tot = cum[0:1] if reverse else cum[c - 1:c]
    e_neg = jnp.exp(-cum)
    e_end = jnp.exp(tot - cum)
    return dict(ag=-kk * jnp.exp(cum - lw), rg=r * jnp.exp(cum), bi=be * e_neg, ki=kd * e_neg,
                bee=be * e_end, kee=kd * e_end, gam=jnp.exp(tot), v=v, strict=strict, incl=incl)


def _rwkv_local(chunks):
    c = chunks[0]["v"].shape[0]
    n_tiles = chunks[0]["v"].shape[1] // LANES
    r128 = _iota((LANES, LANES), 0)
    c128 = _iota((LANES, LANES), 1)
    same_head = (r128 < HEAD_DIM) == (c128 < HEAD_DIM)
    diag = r128 == c128
    ctx = [(ch, slice(p * LANES, (p + 1) * LANES)) for ch in chunks for p in range(n_tiles)]

    aa = [_dot_nt(jnp.concatenate([ch["ag"][:, sl], ch["rg"][:, sl]], axis=0),
                  jnp.concatenate([_stack_heads(ch["bi"][:, sl]), _stack_heads(ch["ki"][:, sl])], axis=0))
          for ch, sl in ctx]
    a_ab = [jnp.where(ch["strict"], m[:c, :LANES], 0.0) for (ch, _), m in zip(ctx, aa)]
    a_rb = [jnp.where(ch["incl"], m[c:, :LANES], 0.0) for (ch, _), m in zip(ctx, aa)]
    a_k = [jnp.concatenate([jnp.where(ch["strict"], m[:c, LANES:], 0.0),
                            jnp.where(ch["incl"], m[c:, LANES:], 0.0)], axis=0) for (ch, _), m in zip(ctx, aa)]
    xv = [_dot(m, _stack_heads(ch["v"][:, sl])) for (ch, sl), m in zip(ctx, a_k)]
    x = [jnp.concatenate([ch["ag"][:, sl], m[:c]], axis=1) for (ch, sl), m in zip(ctx, xv)]
    pw = a_ab
    n_round = int(math.log2(c))
    for j in range(n_round):
        x = [xi + _dot(pi, _stack_heads(xi)) for xi, pi in zip(x, pw)]
        if j < n_round - 1:
            pw = [_dot(pi, jnp.where(same_head, jnp.concatenate([pi, pi], axis=0), 0.0)) for pi in pw]
    ry = [jnp.concatenate([ch["rg"][:, sl], m[c:]], axis=1) + _dot(ai, _stack_heads(xi))
          for (ch, sl), m, ai, xi in zip(ctx, xv, a_rb, x)]
    mg = [_dot(jnp.concatenate([ch["bee"][:, sl], ch["kee"][:, sl]], axis=0).T,
               jnp.concatenate([xi, jnp.concatenate([jnp.zeros_like(ch["v"][:, sl]), ch["v"][:, sl]], axis=1)],
                               axis=0))
          for (ch, sl), xi in zip(ctx, x)]
    out = []
    for (ch, sl), ryi, mgi in zip(ctx, ry, mg):
        gam_col = jnp.sum(jnp.where(diag, ch["gam"][:, sl], 0.0), axis=1, keepdims=True)
        out.append((ryi[:, :LANES], ryi[:, LANES:], jnp.where(same_head, mgi[:, :LANES], 0.0),
                    jnp.where(same_head, mgi[:, LANES:], 0.0), gam_col))
    return [out[i * n_tiles:(i + 1) * n_tiles] for i in range(len(chunks))]


def _rwkv_kernel(rf, vf, kkf, lwf, bef, kf, rb, vb, kkb, lwb, beb, kb, yf_o, yb_o, zf_ref, zb_ref):
    @pl.when(pl.program_id(1) == 0)
    def _():
        zf_ref[...] = jnp.zeros_like(zf_ref)
        zb_ref[...] = jnp.zeros_like(zb_ref)

    c = RWKV_CHUNK
    n = rf.shape[1] // c
    n_tiles = rf.shape[2] // LANES
    chunks = []
    for refs, reverse in (((rf, vf, kkf, lwf, bef, kf), False), ((rb, vb, kkb, lwb, beb, kb), True)):
        for ci in range(n):
            chunks.append(_rwkv_prep(*[ref[0, ci * c:(ci + 1) * c, :].astype(F32) for ref in refs], reverse))
    local = _rwkv_local(chunks)

    for d, (z_ref, y_o) in enumerate(((zf_ref, yf_o), (zb_ref, yb_o))):
        order = range(n) if d == 0 else range(n - 1, -1, -1)
        zs = [z_ref[p] for p in range(n_tiles)]
        for ci in order:
            ys = []
            for p in range(n_tiles):
                rt, yl, mt, gt, gam_col = local[d * n + ci][p]
                yz = _dot(jnp.concatenate([rt, mt], axis=0), zs[p])
                ys.append(yz[:c] + yl)
                zs[p] = gam_col * zs[p] + yz[c:] + gt
            y_o[0, ci * c:(ci + 1) * c, :] = jnp.concatenate(ys, axis=1)
        for p in range(n_tiles):
            z_ref[p] = zs[p]


def _rwkv_scan(r, v, kk, lw0, lw1, be0, be1, k0, k1):
    b, t, w = r.shape
    c = RWKV_CHUNK * RWKV_STEP_CHUNKS
    nc = t // c
    fwd = pl.BlockSpec((1, c, w), lambda i, j: (i, j, 0))
    bwd = pl.BlockSpec((1, c, w), lambda i, j: (i, nc - 1 - j, 0))
    out = jax.ShapeDtypeStruct((b, t, w), F32)
    zshape = pltpu.VMEM((w // LANES, LANES, LANES), F32)
    return pl.pallas_call(
        _rwkv_kernel,
        grid=(b, nc),
        in_specs=[fwd] * 6 + [bwd] * 6,
        out_specs=[fwd, bwd],
        out_shape=[out, out],
        scratch_shapes=[zshape, zshape],
        compiler_params=_params(2),
        name="rwkv_scan",
    )(r, v, kk, lw0, be0, k0, r, v, kk, lw1, be1, k1)


def _diff_in_kernel(x_ref, g_ref, w_ref, qg_ref, kg_ref, bd_ref, qt_o, k_o, vt_o):
    x = x_ref[0]
    d = x.shape[-1]
    hn = _bf(_rms(x, g_ref[...]))
    z = jnp.dot(hn, w_ref[...], preferred_element_type=F32)
    bd = bd_ref[...]
    half = bd.shape[0]

    def qk_norm(y, gain, scale):
        parts = []
        for j in range(d // half):
            yj = y[:, j * half:(j + 1) * half]
            ms = _dot(yj * yj, bd) * (1.0 / HEAD_DIM)
            parts.append(yj * lax.rsqrt(ms + NORM_EPS) * gain * scale)
        return jnp.concatenate(parts, axis=1)

    q = qk_norm(z[:, :d], qg_ref[...], LOG2E / math.sqrt(HEAD_DIM))
    k_o[0] = _bf(qk_norm(z[:, d:2 * d], kg_ref[...], 1.0))
    tm = x.shape[0]
    for h in range(d // DIFF_V_DIM):
        cols = slice(h * DIFF_V_DIM, (h + 1) * DIFF_V_DIM)
        qt_o[0, h] = _bf(q[:, cols].T)
        vt_o[0, h, :DIFF_V_DIM, :] = _bf(z[:, 2 * d:][:, cols].T)
        vt_o[0, h, DIFF_V_DIM:, :] = jnp.ones((ONES_ROWS, tm), BF16)


def _diff_in(x, gain, w_in, q_gain, k_gain, tm):
    b, t, d = x.shape
    half = 512
    nh = d // DIFF_V_DIM
    head = jnp.arange(half) // HEAD_DIM
    bd = _bf(head[:, None] == head[None, :])
    xs = pl.BlockSpec((1, tm, d), lambda i, j: (i, j, 0))
    out = jax.ShapeDtypeStruct((b, t, d), BF16)
    vt_rows = DIFF_V_DIM + ONES_ROWS
    return pl.pallas_call(
        _diff_in_kernel,
        grid=(b, t // tm),
        in_specs=[xs, _const_spec((1, d)), _const_spec((d, 3 * d)),
                  _const_spec((1, half)), _const_spec((1, half)), _const_spec((half, half))],
        out_specs=[pl.BlockSpec((1, nh, DIFF_V_DIM, tm), lambda i, j: (i, 0, 0, j)), xs,
                   pl.BlockSpec((1, nh, vt_rows, tm), lambda i, j: (i, 0, 0, j))],
        out_shape=[jax.ShapeDtypeStruct((b, nh, DIFF_V_DIM, t), BF16), out,
                   jax.ShapeDtypeStruct((b, nh, vt_rows, t), BF16)],
        compiler_params=_params(2),
        name="diff_in",
    )(x, gain.reshape(1, d), _bf(w_in), jnp.tile(q_gain, half // HEAD_DIM).reshape(1, half),
      jnp.tile(k_gain, half // HEAD_DIM).reshape(1, half), bd)


def _diff_attn_kernel(qt_ref, k_ref, vt_ref, hc_ref, lam_ref, og_ref, o_ref, bias_ref, p_ref, *s_scratch,
                      lambda_init, bounded):
    t, tq = bias_ref.shape

    @pl.when(pl.program_id(2) == 0)
    def _():
        dist = jnp.abs((_iota((t, tq), 0) - pl.program_id(1) * tq - _iota((t, tq), 1)).astype(F32))
        bias = dist * (-LOG2E * hc_ref[0, 0:1, 0:1])
        bias_ref[...] = bias - hc_ref[0, 1:2, 0:1] if bounded else bias

    q = qt_ref[0, 0]
    first = _iota(q.shape, 0) < HEAD_DIM
    zero = jnp.zeros_like(q)
    kb = min(KEY_BLOCK, t)
    qb = min(QUERY_BLOCK, tq)
    sub = HALO

    qcs = (jnp.where(first, q, zero), jnp.where(first, zero, q))
    secs = [(c, slice(i * qb, (i + 1) * qb)) for c in range(2) for i in range(tq // qb)]
    n_kb = t // kb
    outs = []

    def logits(sec, j):
        c, qs = sec
        ks = slice(j * kb, (j + 1) * kb)
        return jnp.dot(k_ref[0, ks, :], qcs[c][:, qs], preferred_element_type=F32) + bias_ref[ks, qs]

    if bounded:
        for sec in secs:
            c, qs = sec
            for j in range(n_kb):
                p_ref[c, j * kb:(j + 1) * kb, qs] = _bf(jnp.exp2(logits(sec, j)))
            outs.append(jnp.dot(vt_ref[0, 0], p_ref[c, :, qs], preferred_element_type=F32))
    else:
        s_ref, = s_scratch

        def score_block(sec, j, m):
            c, qs = sec
            s = logits(sec, j)
            s_ref[c, j * kb:(j + 1) * kb, qs] = s
            return jnp.maximum(m, jnp.max(s.reshape(kb // sub, sub, qb), axis=0))

        def prob_block(sec, j, m):
            c, qs = sec
            ks = slice(j * kb, (j + 1) * kb)
            p_ref[c, ks, qs] = _bf(jnp.exp2(s_ref[c, ks, qs] - m))

        m_init = jnp.full((sub, qb), -jnp.inf, F32)
        m = m_init
        for j in range(n_kb):
            m = score_block(secs[0], j, m)
        m_prev = jnp.max(m, axis=0, keepdims=True)
        for n in range(1, len(secs) + 1):
            m = m_init
            for j in range(n_kb):
                if n < len(secs):
                    m = score_block(secs[n], j, m)
                prob_block(secs[n - 1], j, m_prev)
            m_prev = jnp.max(m, axis=0, keepdims=True)
            c, qs = secs[n - 1]
            outs.append(jnp.dot(vt_ref[0, 0], p_ref[c, :, qs], preferred_element_type=F32))
    per_c = len(secs) // 2
    acc0 = jnp.concatenate(outs[:per_c], axis=1)
    acc1 = jnp.concatenate(outs[per_c:], axis=1)
    o0, s0 = acc0[:DIFF_V_DIM], acc0[DIFF_V_DIM:DIFF_V_DIM + 1]
    o1, s1 = acc1[:DIFF_V_DIM], acc1[DIFF_V_DIM:DIFF_V_DIM + 1]
    lv = lam_ref[...]
    lam = (jnp.exp(jnp.sum(lv[0:1] * lv[1:2], axis=-1, keepdims=True))
           - jnp.exp(jnp.sum(lv[2:3] * lv[3:4], axis=-1, keepdims=True)) + lambda_init)
    o = (o0 * (1.0 / s0) - o1 * (lam / s1)).T
    o_ref[0] = _bf(_rms(o, og_ref[...]) * (1.0 - lambda_init))


def _diff_attention(qt, k, vt, q_gain, k_gain, lam_vecs, out_gain, lambda_init, tq):
    b, t, d = k.shape
    nh = d // DIFF_V_DIM
    slopes = 2.0 ** (-8.0 * jnp.arange(1, nh + 1, dtype=F32) / nh)
    bound = 1.02 * math.sqrt(HEAD_DIM) * LOG2E * jnp.max(jnp.abs(q_gain)) * jnp.max(jnp.abs(k_gain))
    head_consts = jnp.stack([jnp.broadcast_to(slopes[:, None], (nh, LANES)),
                             jnp.broadcast_to(bound, (nh, LANES))], axis=1)

    def run(bounded):
        scratch = [pltpu.VMEM((t, tq), F32), pltpu.VMEM((2, t, tq), BF16)]
        if not bounded:
            scratch.append(pltpu.VMEM((2, t, tq), F32))
        return pl.pallas_call(
            functools.partial(_diff_attn_kernel, lambda_init=lambda_init, bounded=bounded),
            grid=(nh, t // tq, b),
            in_specs=[pl.BlockSpec((1, 1, DIFF_V_DIM, tq), lambda h, j, i: (i, h, 0, j)),
                      pl.BlockSpec((1, t, DIFF_V_DIM), lambda h, j, i: (i, 0, h)),
                      pl.BlockSpec((1, 1, vt.shape[2], t), lambda h, j, i: (i, h, 0, 0)),
                      pl.BlockSpec((1, 2, LANES), lambda h, j, i: (h, 0, 0)),
                      _const_spec((4, HEAD_DIM)),
                      _const_spec((1, DIFF_V_DIM))],
            out_specs=pl.BlockSpec((1, tq, DIFF_V_DIM), lambda h, j, i: (i, j, h)),
            out_shape=jax.ShapeDtypeStruct((b, t, d), BF16),
            scratch_shapes=scratch,
            compiler_params=_params(3),
            name="diff_attn_bounded" if bounded else "diff_attn",
        )(qt, k, vt, head_consts, lam_vecs, out_gain.reshape(1, DIFF_V_DIM))

    return lax.cond(2.0 * bound <= SAFE_EXP2_SPAN, lambda: run(True), lambda: run(False))


def kernel(x, mem, mem_norm, mix_norm, cross_norm, ffn_norm, ab_w_in, ab_shift_prev, ab_shift_next, rwkv_w0, rwkv_w2, rwkv_a0, rwkv_a2, rwkv_g2, rwkv_k_k, rwkv_k_a, rwkv_r_k, rwkv_out_gain, sgu_v_gain, sgu_w_s, sgu_b, ab_w_out, diff_w_in, diff_q_gain, diff_k_gain, diff_lambda_q1, diff_lambda_k1, diff_lambda_q2, diff_lambda_k2, diff_out_gain, diff_w_out, cross_wq, cross_wkv, cross_q_gain, cross_k_gain, cross_wo, ffn_w_up, ffn_conv_w, ffn_conv_b, ffn_w_down):
    b, t, d = x.shape
    depth = mix_norm.shape[0]
    tm = min(512, t)
    tq = min(512, t)
    assert t % tm == 0 and t % (RWKV_CHUNK * RWKV_STEP_CHUNKS) == 0 and tm % SGU_CHUNK == 0

    k_all, v_all = _mem_kv(mem, mem_norm, cross_wkv, cross_k_gain)
    for l in range(depth):
        if l % 2 == 0:
            e = l // 2
            (r, v, kk, lw0, lw1, be0, be1, k0, k1, g, bv, b_out) = _ab_in(
                x, mix_norm[l], ab_w_in[e], ab_shift_prev[e], ab_shift_next[e],
                rwkv_w0[e], rwkv_w2[e], rwkv_a0[e], rwkv_a2[e], rwkv_g2[e],
                rwkv_k_k[e], rwkv_k_a[e], rwkv_r_k[e], sgu_v_gain[e], sgu_w_s[e], sgu_b[e], tm)
            yf, yb = _rwkv_scan(r, v, kk, lw0, lw1, be0, be1, k0, k1)
            head = jnp.arange(RWKV_WIDTH) // HEAD_DIM
            mix = (_mix_rwkv_sgu, (yf, yb, bv, g, b_out),
                   (rwkv_out_gain[e].reshape(1, RWKV_WIDTH), _bf(head[:, None] == head[None, :]), _bf(ab_w_out[e])))
        else:
            o = l // 2
            lambda_init = 0.8 - 0.6 * math.exp(-0.3 * l)
            q, k, vt = _diff_in(x, mix_norm[l], diff_w_in[o], diff_q_gain[o], diff_k_gain[o], tm)
            lam_vecs = jnp.stack([diff_lambda_q1[o], diff_lambda_k1[o], diff_lambda_q2[o], diff_lambda_k2[o]])
            c_out = _diff_attention(q, k, vt, diff_q_gain[o], diff_k_gain[o], lam_vecs, diff_out_gain[o],
                                    lambda_init, tq)
            mix = (_mix_proj, (c_out,), (_bf(diff_w_out[o]),))
        x = _cross_attention(x, *mix, cross_norm[l], cross_wq[l], cross_q_gain[l], k_all, v_all, cross_wo[l], l, tm)
        x = _conv_ffn(x, ffn_norm[l], ffn_w_up[l], ffn_conv_w[l], ffn_conv_b[l], ffn_w_down[l], tm)
    return x
```

```python
import functools
import math

import jax
import jax.numpy as jnp
from jax import lax
from jax.experimental import pallas as pl
from jax.experimental.pallas import tpu as pltpu

NORM_EPS = 1e-6
KK_EPS = 1e-12
RWKV_HEADS = 8
HEAD_DIM = 64
RWKV_WIDTH = RWKV_HEADS * HEAD_DIM
LORA_W = 128
RWKV_IN = 3 * RWKV_WIDTH + 3 * LORA_W
SGU_WIDTH = 512
SGU_GROUPS = 4
SGU_CHUNK = 128
DIFF_HEADS = 8
DIFF_V_DIM = 2 * HEAD_DIM
CROSS_HEADS = 4
CONV_WIDTH = 3
RWKV_CHUNK = 64
RWKV_STEP_CHUNKS = 4
LANES = 128
HALO = 8
FF_CHUNK = 256
KEY_BLOCK = 128
QUERY_BLOCK = 256
ONES_ROWS = 16
SAFE_EXP2_SPAN = 100.0
VMEM_LIMIT = 56 * 1024 * 1024
LOG2E = math.log2(math.e)

F32 = jnp.float32
BF16 = jnp.bfloat16
AB_IN_OUT_DTYPES = (BF16, BF16, BF16, F32, F32, BF16, BF16, BF16, BF16, BF16, BF16, BF16)


def _bf(x):
    return x.astype(BF16)


def _dot(a, b):
    return jnp.dot(_bf(a), _bf(b), preferred_element_type=F32)


def _dot_nt(a, b):
    return lax.dot_general(_bf(a), _bf(b), (((1,), (1,)), ((), ())), preferred_element_type=F32)


def _rms(x, gain):
    return x * lax.rsqrt(jnp.mean(x * x, axis=-1, keepdims=True) + NORM_EPS) * gain


def _sigmoid(x):
    return 1.0 / (1.0 + jnp.exp(-x))


def _iota(shape, dim):
    return lax.broadcasted_iota(jnp.int32, shape, dim)


def _shift_prev(u, halo_row):
    rolled = pltpu.roll(u, 1, 0)
    head = rolled[:HALO]
    head = jnp.where(_iota(head.shape, 0) == 0, halo_row, head)
    return jnp.concatenate([head, rolled[HALO:]], axis=0)


def _shift_next(u, halo_row):
    n = u.shape[0]
    rolled = pltpu.roll(u, n - 1, 0)
    tail = rolled[n - HALO:]
    tail = jnp.where(_iota(tail.shape, 0) == HALO - 1, halo_row, tail)
    return jnp.concatenate([rolled[:n - HALO], tail], axis=0)


def _halo_rows(xp_ref, xn_ref, gain, i, nt):
    xh = jnp.concatenate([xp_ref[0], xn_ref[0]], axis=0)
    hh = _rms(xh, gain)
    row = _iota(hh.shape, 0)
    has_prev = jnp.where(i > 0, 1.0, 0.0)
    has_next = jnp.where(i < nt - 1, 1.0, 0.0)
    return _bf(hh * jnp.where(row < HALO, has_prev, has_next))


def _const_spec(shape):
    zeros = (0,) * len(shape)
    return pl.BlockSpec(shape, lambda *_: zeros, pipeline_mode=pl.Buffered(1))


def _params(n_axes):
    return pltpu.CompilerParams(dimension_semantics=("arbitrary",) * n_axes,
                                vmem_limit_bytes=VMEM_LIMIT)


def _row_specs(tm, d, t):
    nb = tm // HALO
    last = t // HALO - 1
    main = pl.BlockSpec((1, tm, d), lambda b, i: (b, i, 0))
    prev = pl.BlockSpec((1, HALO, d), lambda b, i: (b, jnp.maximum(i * nb - 1, 0), 0))
    nxt = pl.BlockSpec((1, HALO, d), lambda b, i: (b, jnp.minimum((i + 1) * nb, last), 0))
    return main, prev, nxt


def _mem_kv_kernel(mem_ref, mnorm_ref, wkv_ref, kg_ref, k_ref, v_ref):
    d = mem_ref.shape[-1]
    hd = d // CROSS_HEADS
    mn = _rms(mem_ref[0], mnorm_ref[...])
    kv = jnp.dot(_bf(mn), wkv_ref[0], preferred_element_type=F32)
    for h in range(CROSS_HEADS):
        kh = _rms(kv[:, h * hd:(h + 1) * hd], kg_ref[0]) * (1.0 / math.sqrt(hd))
        k_ref[0, 0, :, h * hd:(h + 1) * hd] = _bf(kh)
    v_ref[0, 0] = _bf(kv[:, d:])


def _mem_kv(mem, mem_norm, wkv, k_gain):
    b, m, d = mem.shape
    nl = wkv.shape[0]
    out = jax.ShapeDtypeStruct((nl, b, m, d), BF16)
    return pl.pallas_call(
        _mem_kv_kernel,
        grid=(nl, b),
        in_specs=[pl.BlockSpec((1, m, d), lambda l, i: (i, 0, 0)),
                  pl.BlockSpec((1, d), lambda l, i: (0, 0)),
                  pl.BlockSpec((1, d, 2 * d), lambda l, i: (l, 0, 0)),
                  pl.BlockSpec((1, 1, d // CROSS_HEADS), lambda l, i: (l, 0, 0))],
        out_specs=[pl.BlockSpec((1, 1, m, d), lambda l, i: (l, i, 0, 0)),
                   pl.BlockSpec((1, 1, m, d), lambda l, i: (l, i, 0, 0))],
        out_shape=[out, out],
        compiler_params=_params(2),
        name="mem_kv",
    )(mem, mem_norm.reshape(1, d), _bf(wkv), k_gain.reshape(nl, 1, -1))


def _mix_rwkv_sgu(x, yf_ref, yb_ref, bv_ref, g_ref, b_ref, og_ref, bd_ref, wo_ref):
    y = yf_ref[0] + yb_ref[0]
    ms = _dot(y * y, bd_ref[...]) * (1.0 / HEAD_DIM)
    yn = y * lax.rsqrt(ms + NORM_EPS) * og_ref[...]
    a_out = (yn + bv_ref[0].astype(F32)) * g_ref[0].astype(F32)
    cat = jnp.concatenate([_bf(a_out), b_ref[0]], axis=1)
    return x + jnp.dot(cat, wo_ref[...], preferred_element_type=F32)


def _mix_proj(x, a_ref, w_ref):
    return x + jnp.dot(a_ref[0], w_ref[...], preferred_element_type=F32)


def _cross_kernel(*refs, n_mix, mix_fn):
    x_ref, mix_refs = refs[0], refs[1:1 + n_mix]
    g_ref, wq_ref, qg_ref, k_ref, v_ref, wo_ref, o_ref = refs[1 + n_mix:]
    x = mix_fn(x_ref[0], *mix_refs)
    d = x.shape[-1]
    hd = d // CROSS_HEADS
    hn = _bf(_rms(x, g_ref[...]))
    q = jnp.dot(hn, wq_ref[...], preferred_element_type=F32)
    outs = []
    for h in range(CROSS_HEADS):
        sl = slice(h * hd, (h + 1) * hd)
        qh = _rms(q[:, sl], qg_ref[...])
        logits = _dot_nt(qh, k_ref[0, 0, :, sl])
        p = jnp.exp(logits - jnp.max(logits, axis=-1, keepdims=True))
        s = jnp.sum(p, axis=-1, keepdims=True)
        outs.append(jnp.dot(_bf(p), v_ref[0, 0, :, sl], preferred_element_type=F32) / s)
    o = jnp.concatenate(outs, axis=1)
    o_ref[0] = x + jnp.dot(_bf(o), wo_ref[...], preferred_element_type=F32)


def _cross_attention(x, mix_fn, mix_rows, mix_consts, gain, wq, q_gain, k_all, v_all, wo, layer, tq):
    b, t, d = x.shape
    m = k_all.shape[2]
    hd = d // CROSS_HEADS
    mix_specs = ([pl.BlockSpec((1, tq, a.shape[-1]), lambda i, j: (i, j, 0)) for a in mix_rows]
                 + [_const_spec(a.shape) for a in mix_consts])
    return pl.pallas_call(
        functools.partial(_cross_kernel, n_mix=len(mix_specs), mix_fn=mix_fn),
        grid=(b, t // tq),
        in_specs=[pl.BlockSpec((1, tq, d), lambda i, j: (i, j, 0)),
                  *mix_specs,
                  _const_spec((1, d)),
                  _const_spec((d, d)),
                  _const_spec((1, hd)),
                  pl.BlockSpec((1, 1, m, d), lambda i, j: (layer, i, 0, 0)),
                  pl.BlockSpec((1, 1, m, d), lambda i, j: (layer, i, 0, 0)),
                  _const_spec((d, d))],
        out_specs=pl.BlockSpec((1, tq, d), lambda i, j: (i, j, 0)),
        out_shape=jax.ShapeDtypeStruct((b, t, d), F32),
        compiler_params=_params(2),
        name=f"cross_attn_{layer}",
    )(x, *mix_rows, *mix_consts, gain.reshape(1, d), _bf(wq), q_gain.reshape(1, hd), k_all, v_all, _bf(wo))


def _ffn_kernel(x_ref, xp_ref, xn_ref, g_ref, wup_ref, cw_ref, cb_ref, wdn_ref, o_ref,
                hn_ref, hh_ref, acc_ref):
    i = pl.program_id(1)
    nt = pl.num_programs(1)
    n_chunks = wdn_ref.shape[0]
    x = x_ref[0]
    gain = g_ref[...]
    hn_ref[...] = _bf(_rms(x, gain))
    hh_ref[...] = _halo_rows(xp_ref, xn_ref, gain, i, nt)
    acc_ref[...] = jnp.zeros_like(acc_ref)

    def up_proj(idx):
        w = wup_ref[idx]
        return (jnp.dot(hn_ref[...], w, preferred_element_type=F32),
                jnp.dot(hh_ref[...], w, preferred_element_type=F32))

    def conv(idx, u, uh):
        cw = cw_ref[idx]
        up = _shift_prev(u, uh[HALO - 1:HALO])
        un = _shift_next(u, uh[HALO:HALO + 1])
        return cb_ref[idx] + up * cw[0:1] + u * cw[1:2] + un * cw[2:3]

    ups = [up_proj(0), up_proj(n_chunks)]
    for c in range(n_chunks):
        nxt = [up_proj(c + 1), up_proj(c + 1 + n_chunks)] if c + 1 < n_chunks else None
        gate = conv(c, *ups[0])
        val = conv(c + n_chunks, *ups[1])
        act = gate * _sigmoid(gate) * val
        acc_ref[...] += jnp.dot(_bf(act), wdn_ref[c], preferred_element_type=F32)
        ups = nxt
    o_ref[0] = x + acc_ref[...]


def _conv_ffn(x, gain, w_up, conv_w, conv_b, w_down, tm):
    b, t, d = x.shape
    ff = w_down.shape[0]
    nc = ff // FF_CHUNK
    wup = _bf(w_up).reshape(d, 2 * nc, FF_CHUNK).transpose(1, 0, 2)
    cw = conv_w.reshape(CONV_WIDTH, 2 * nc, FF_CHUNK).transpose(1, 0, 2)
    cb = conv_b.reshape(2 * nc, 1, FF_CHUNK)
    wdn = _bf(w_down).reshape(nc, FF_CHUNK, d)
    main, prev, nxt = _row_specs(tm, d, t)
    return pl.pallas_call(
        _ffn_kernel,
        grid=(b, t // tm),
        in_specs=[main, prev, nxt,
                  _const_spec((1, d)),
                  _const_spec((2 * nc, d, FF_CHUNK)),
                  _const_spec((2 * nc, CONV_WIDTH, FF_CHUNK)),
                  _const_spec((2 * nc, 1, FF_CHUNK)),
                  _const_spec((nc, FF_CHUNK, d))],
        out_specs=pl.BlockSpec((1, tm, d), lambda i, j: (i, j, 0)),
        out_shape=jax.ShapeDtypeStruct((b, t, d), F32),
        scratch_shapes=[pltpu.VMEM((tm, d), BF16),
                        pltpu.VMEM((2 * HALO, d), BF16),
                        pltpu.VMEM((tm, d), F32)],
        compiler_params=_params(2),
        name="conv_ffn",
    )(x, x, x, gain.reshape(1, d), wup, cw, cb, wdn)


def _ab_in_kernel(x_ref, xp_ref, xn_ref, g_ref, win_ref, mup_ref, mun_ref,
                  w0_ref, w2_ref, a0_ref, a2_ref, g2_ref, kk_ref, ka_ref, rk_ref, bd_ref,
                  vg_ref, ws_ref, sb_ref,
                  r_o, v_o, kk_o, lw0_o, lw1_o, be0_o, be1_o, k0_o, k1_o, g_o, bv_o, b_o):
    i = pl.program_id(1)
    nt = pl.num_programs(1)
    w = RWKV_WIDTH
    x = x_ref[0]
    tm = x.shape[0]
    gain = g_ref[...]
    hn = _bf(_rms(x, gain))
    z = jnp.dot(hn, win_ref[...], preferred_element_type=F32)
    hh = _halo_rows(xp_ref, xn_ref, gain, i, nt)
    zh = jnp.dot(hh, win_ref[:, :RWKV_IN], preferred_element_type=F32)

    za = z[:, :RWKV_IN]
    zp = _shift_prev(za, zh[HALO - 1:HALO])
    zn = _shift_next(za, zh[HALO:HALO + 1])
    mup, mun = mup_ref[...], mun_ref[...]
    zs = za * (1.0 - mup - mun) + mup * zp + mun * zn
    r = zs[:, 0:w]
    k = zs[:, w:2 * w]
    v = zs[:, 2 * w:3 * w]
    wl = jnp.tanh(zs[:, 3 * w:3 * w + LORA_W])
    al = zs[:, 3 * w + LORA_W:3 * w + 2 * LORA_W]
    gl = zs[:, 3 * w + 2 * LORA_W:]

    wpre = w0_ref[...] + _dot(wl, w2_ref[...])
    lw = -math.exp(-0.5) * _sigmoid(wpre)
    a = _sigmoid(a0_ref[...] + _dot(al, a2_ref[...]))
    g = _dot(_sigmoid(gl), g2_ref[...])

    bd = bd_ref[...]
    kk0 = k * kk_ref[...]
    kk = kk0 * lax.rsqrt(_dot(kk0 * kk0, bd) + KK_EPS)
    ka = ka_ref[...]
    k0 = k * (1.0 + (a[:, :w] - 1.0) * ka)
    k1 = k * (1.0 + (a[:, w:] - 1.0) * ka)
    bonus = _dot(r * rk_ref[...] * (k0 + k1), bd)

    r_o[0] = _bf(r)
    v_o[0] = _bf(v)
    kk_o[0] = _bf(kk)
    lw0_o[0] = lw[:, :w]
    lw1_o[0] = lw[:, w:]
    be0_o[0] = _bf(kk * a[:, :w])
    be1_o[0] = _bf(kk * a[:, w:])
    k0_o[0] = _bf(k0)
    k1_o[0] = _bf(k1)
    g_o[0] = _bf(g)
    bv_o[0] = _bf(bonus * v)

    zg = z[:, RWKV_IN:]
    zg = zg * (0.5 * (1.0 + jnp.tanh(math.sqrt(2.0 / math.pi) * (zg + 0.044715 * (zg * zg * zg)))))
    u = zg[:, :SGU_WIDTH]
    vn = _bf(_rms(zg[:, SGU_WIDTH:], vg_ref[...]))
    gd = SGU_WIDTH // SGU_GROUPS
    for n in range(tm // SGU_CHUNK):
        rows = slice(n * SGU_CHUNK, (n + 1) * SGU_CHUNK)
        for gi in range(SGU_GROUPS):
            cols = slice(gi * gd, (gi + 1) * gd)
            s = jnp.dot(ws_ref[gi], vn[rows, cols], preferred_element_type=F32) + sb_ref[gi]
            b_o[0, rows, cols] = _bf(u[rows, cols] * s)


def _ab_in(x, gain, w_in, mu_prev, mu_next, w0, w2, a0, a2, g2, k_k, k_a, r_k, v_gain, w_s, s_b, tm):
    b, t, d = x.shape
    w = RWKV_WIDTH
    n_in = w_in.shape[1]
    half = LORA_W // 2

    def lora_cat(m):
        z = jnp.zeros((half, w), m.dtype)
        return _bf(jnp.concatenate([jnp.concatenate([m[0], z], axis=1),
                                    jnp.concatenate([z, m[1]], axis=1)], axis=0))

    head = jnp.arange(w) // HEAD_DIM
    bd = _bf(head[:, None] == head[None, :])
    sb = jnp.broadcast_to(s_b[:, :, None], (SGU_GROUPS, SGU_CHUNK, SGU_WIDTH // SGU_GROUPS))
    main, prev, nxt = _row_specs(tm, d, t)
    ospec = pl.BlockSpec((1, tm, w), lambda i, j: (i, j, 0))
    return pl.pallas_call(
        _ab_in_kernel,
        grid=(b, t // tm),
        in_specs=[main, prev, nxt,
                  _const_spec((1, d)),
                  _const_spec((d, n_in)),
                  _const_spec((1, RWKV_IN)), _const_spec((1, RWKV_IN)),
                  _const_spec((1, 2 * w)), _const_spec((LORA_W, 2 * w)),
                  _const_spec((1, 2 * w)), _const_spec((LORA_W, 2 * w)),
                  _const_spec((LORA_W, w)),
                  _const_spec((1, w)), _const_spec((1, w)), _const_spec((1, w)),
                  _const_spec((w, w)),
                  _const_spec((1, SGU_WIDTH)),
                  _const_spec((SGU_GROUPS, SGU_CHUNK, SGU_CHUNK)),
                  _const_spec((SGU_GROUPS, SGU_CHUNK, SGU_WIDTH // SGU_GROUPS))],
        out_specs=[ospec] * 12,
        out_shape=[jax.ShapeDtypeStruct((b, t, w), dt) for dt in AB_IN_OUT_DTYPES],
        compiler_params=_params(2),
        name="ab_in",
    )(x, x, x, gain.reshape(1, d), _bf(w_in), mu_prev.reshape(1, -1), mu_next.reshape(1, -1),
      w0.reshape(1, 2 * w), lora_cat(w2), a0.reshape(1, 2 * w), lora_cat(a2), _bf(g2),
      k_k.reshape(1, w), k_a.reshape(1, w), r_k.reshape(1, w), bd,
      v_gain.reshape(1, -1), _bf(w_s), sb)


def _stack_heads(x):
    even = (_iota(x.shape, 1) & (LANES - 1)) < HEAD_DIM
    return jnp.concatenate([jnp.where(even, x, 0.0), jnp.where(even, 0.0, x)], axis=0)


def _rwkv_prep(r, v, kk, lw, be, kd, reverse):
    c = r.shape[0]
    rr = _iota((c, LANES), 0)
    cc = _iota((c, LANES), 1) & (HEAD_DIM - 1)
    if reverse:
        strict, incl = cc > rr, cc >= rr
    else:
        strict, incl = cc < rr, cc <= rr
    tri = _bf(incl[:, :c])

    l1 = _bf(lw)
    l2 = _bf(lw - l1.astype(F32))
    l3 = _bf(lw - l1.astype(F32) - l2.astype(F32))
    cum = (jnp.dot(tri, l1, preferred_element_type=F32) + jnp.dot(tri, l2, preferred_element_type=F32)
           + jnp.dot(tri, l3, preferred_element_type=F32))
    tot = cum[0:1] if reverse else cum[c - 1:c]
    e_neg = jnp.exp(-cum)
    e_end = jnp.exp(tot - cum)
    return dict(ag=-kk * jnp.exp(cum - lw), rg=r * jnp.exp(cum), bi=be * e_neg, ki=kd * e_neg,
                bee=be * e_end, kee=kd * e_end, gam=jnp.exp(tot), v=v, strict=strict, incl=incl)


def _rwkv_local(chunks):
    c = chunks[0]["v"].shape[0]
    n_tiles = chunks[0]["v"].shape[1] // LANES
    r128 = _iota((LANES, LANES), 0)
    c128 = _iota((LANES, LANES), 1)
    same_head = (r128 < HEAD_DIM) == (c128 < HEAD_DIM)
    diag = r128 == c128
    ctx = [(ch, slice(p * LANES, (p + 1) * LANES)) for ch in chunks for p in range(n_tiles)]

    aa = [_dot_nt(jnp.concatenate([ch["ag"][:, sl], ch["rg"][:, sl]], axis=0),
                  jnp.concatenate([_stack_heads(ch["bi"][:, sl]), _stack_heads(ch["ki"][:, sl])], axis=0))
          for ch, sl in ctx]
    a_ab = [jnp.where(ch["strict"], m[:c, :LANES], 0.0) for (ch, _), m in zip(ctx, aa)]
    a_rb = [jnp.where(ch["incl"], m[c:, :LANES], 0.0) for (ch, _), m in zip(ctx, aa)]
    a_k = [jnp.concatenate([jnp.where(ch["strict"], m[:c, LANES:], 0.0),
                            jnp.where(ch["incl"], m[c:, LANES:], 0.0)], axis=0) for (ch, _), m in zip(ctx, aa)]
    xv = [_dot(m, _stack_heads(ch["v"][:, sl])) for (ch, sl), m in zip(ctx, a_k)]
    x = [jnp.concatenate([ch["ag"][:, sl], m[:c]], axis=1) for (ch, sl), m in zip(ctx, xv)]
    def block_diag(pi):
        return jnp.where(same_head, jnp.concatenate([pi, pi], axis=0), 0.0)

    n_round = int(math.log2(c))
    rr = a_ab
    pw = [_dot(pi, block_diag(pi)) for pi in a_ab]
    for k in range(1, n_round):
        if k < n_round - 1:
            res = [_dot(jnp.concatenate([ri, pi], axis=0), block_diag(pi)) for ri, pi in zip(rr, pw)]
            rr = [ri + pi + m[:c] for ri, pi, m in zip(rr, pw, res)]
            pw = [m[c:] for m in res]
        else:
            rr = [ri + pi + _dot(ri, block_diag(pi)) for ri, pi in zip(rr, pw)]
    x = [xi + _dot(ri, _stack_heads(xi)) for xi, ri in zip(x, rr)]
    ry = [jnp.concatenate([ch["rg"][:, sl], m[c:]], axis=1) + _dot(ai, _stack_heads(xi))
          for (ch, sl), m, ai, xi in zip(ctx, xv, a_rb, x)]
    mg = [_dot(jnp.concatenate([ch["bee"][:, sl], ch["kee"][:, sl]], axis=0).T,
               jnp.concatenate([xi, jnp.concatenate([jnp.zeros_like(ch["v"][:, sl]), ch["v"][:, sl]], axis=1)],
                               axis=0))
          for (ch, sl), xi in zip(ctx, x)]
    out = []
    for (ch, sl), ryi, mgi in zip(ctx, ry, mg):
        gam_col = jnp.sum(jnp.where(diag, ch["gam"][:, sl], 0.0), axis=1, keepdims=True)
        out.append((ryi[:, :LANES], ryi[:, LANES:], jnp.where(same_head, mgi[:, :LANES], 0.0),
                    jnp.where(same_head, mgi[:, LANES:], 0.0), gam_col))
    return [out[i * n_tiles:(i + 1) * n_tiles] for i in range(len(chunks))]


def _rwkv_kernel(rf, vf, kkf, lwf, bef, kf, rb, vb, kkb, lwb, beb, kb, yf_o, yb_o, zf_ref, zb_ref):
    @pl.when(pl.program_id(1) == 0)
    def _():
        zf_ref[...] = jnp.zeros_like(zf_ref)
        zb_ref[...] = jnp.zeros_like(zb_ref)

    c = RWKV_CHUNK
    n = rf.shape[1] // c
    n_tiles = rf.shape[2] // LANES
    chunks = []
    for refs, reverse in (((rf, vf, kkf, lwf, bef, kf), False), ((rb, vb, kkb, lwb, beb, kb), True)):
        for ci in range(n):
            chunks.append(_rwkv_prep(*[ref[0, ci * c:(ci + 1) * c, :].astype(F32) for ref in refs], reverse))
    local = _rwkv_local(chunks)

    for d, (z_ref, y_o) in enumerate(((zf_ref, yf_o), (zb_ref, yb_o))):
        order = range(n) if d == 0 else range(n - 1, -1, -1)
        zs = [z_ref[p] for p in range(n_tiles)]
        for ci in order:
            ys = []
            for p in range(n_tiles):
                rt, yl, mt, gt, gam_col = local[d * n + ci][p]
                yz = _dot(jnp.concatenate([rt, mt], axis=0), zs[p])
                ys.append(yz[:c] + yl)
                zs[p] = gam_col * zs[p] + yz[c:] + gt
            y_o[0, ci * c:(ci + 1) * c, :] = jnp.concatenate(ys, axis=1)
        for p in range(n_tiles):
            z_ref[p] = zs[p]


def _rwkv_scan(r, v, kk, lw0, lw1, be0, be1, k0, k1):
    b, t, w = r.shape
    c = RWKV_CHUNK * RWKV_STEP_CHUNKS
    nc = t // c
    fwd = pl.BlockSpec((1, c, w), lambda i, j: (i, j, 0))
    bwd = pl.BlockSpec((1, c, w), lambda i, j: (i, nc - 1 - j, 0))
    out = jax.ShapeDtypeStruct((b, t, w), F32)
    zshape = pltpu.VMEM((w // LANES, LANES, LANES), F32)
    return pl.pallas_call(
        _rwkv_kernel,
        grid=(b, nc),
        in_specs=[fwd] * 6 + [bwd] * 6,
        out_specs=[fwd, bwd],
        out_shape=[out, out],
        scratch_shapes=[zshape, zshape],
        compiler_params=_params(2),
        name="rwkv_scan",
    )(r, v, kk, lw0, be0, k0, r, v, kk, lw1, be1, k1)


def _diff_in_kernel(x_ref, g_ref, w_ref, qg_ref, kg_ref, bd_ref, qt_o, k_o, vt_o):
    x = x_ref[0]
    d = x.shape[-1]
    hn = _bf(_rms(x, g_ref[...]))
    z = jnp.dot(hn, w_ref[...], preferred_element_type=F32)
    bd = bd_ref[...]
    half = bd.shape[0]

    def qk_norm(y, gain, scale):
        parts = []
        for j in range(d // half):
            yj = y[:, j * half:(j + 1) * half]
            ms = _dot(yj * yj, bd) * (1.0 / HEAD_DIM)
            parts.append(yj * lax.rsqrt(ms + NORM_EPS) * gain * scale)
        return jnp.concatenate(parts, axis=1)

    q = qk_norm(z[:, :d], qg_ref[...], LOG2E / math.sqrt(HEAD_DIM))
    k_o[0] = _bf(qk_norm(z[:, d:2 * d], kg_ref[...], 1.0))
    tm = x.shape[0]
    for h in range(d // DIFF_V_DIM):
        cols = slice(h * DIFF_V_DIM, (h + 1) * DIFF_V_DIM)
        qt_o[0, h] = _bf(q[:, cols].T)
        vt_o[0, h, :DIFF_V_DIM, :] = _bf(z[:, 2 * d:][:, cols].T)
        vt_o[0, h, DIFF_V_DIM:, :] = jnp.ones((ONES_ROWS, tm), BF16)


def _diff_in(x, gain, w_in, q_gain, k_gain, tm):
    b, t, d = x.shape
    half = 512
    nh = d // DIFF_V_DIM
    head = jnp.arange(half) // HEAD_DIM
    bd = _bf(head[:, None] == head[None, :])
    xs = pl.BlockSpec((1, tm, d), lambda i, j: (i, j, 0))
    out = jax.ShapeDtypeStruct((b, t, d), BF16)
    vt_rows = DIFF_V_DIM + ONES_ROWS
    return pl.pallas_call(
        _diff_in_kernel,
        grid=(b, t // tm),
        in_specs=[xs, _const_spec((1, d)), _const_spec((d, 3 * d)),
                  _const_spec((1, half)), _const_spec((1, half)), _const_spec((half, half))],
        out_specs=[pl.BlockSpec((1, nh, DIFF_V_DIM, tm), lambda i, j: (i, 0, 0, j)), xs,
                   pl.BlockSpec((1, nh, vt_rows, tm), lambda i, j: (i, 0, 0, j))],
        out_shape=[jax.ShapeDtypeStruct((b, nh, DIFF_V_DIM, t), BF16), out,
                   jax.ShapeDtypeStruct((b, nh, vt_rows, t), BF16)],
        compiler_params=_params(2),
        name="diff_in",
    )(x, gain.reshape(1, d), _bf(w_in), jnp.tile(q_gain, half // HEAD_DIM).reshape(1, half),
      jnp.tile(k_gain, half // HEAD_DIM).reshape(1, half), bd)


def _diff_attn_kernel(qt_ref, k_ref, vt_ref, hc_ref, lam_ref, og_ref, o_ref, bias_ref, p_ref, *s_scratch,
                      lambda_init, bounded):
    t, tq = bias_ref.shape

    @pl.when(pl.program_id(2) == 0)
    def _():
        dist = jnp.abs((_iota((t, tq), 0) - pl.program_id(1) * tq - _iota((t, tq), 1)).astype(F32))
        bias = dist * (-LOG2E * hc_ref[0, 0:1, 0:1])
        bias_ref[...] = bias - hc_ref[0, 1:2, 0:1] if bounded else bias

    q = qt_ref[0, 0]
    first = _iota(q.shape, 0) < HEAD_DIM
    zero = jnp.zeros_like(q)
    kb = min(KEY_BLOCK, t)
    qb = min(QUERY_BLOCK, tq)
    sub = HALO

    qcs = (jnp.where(first, q, zero), jnp.where(first, zero, q))
    secs = [(c, slice(i * qb, (i + 1) * qb)) for c in range(2) for i in range(tq // qb)]
    n_kb = t // kb
    outs = []

    def logits(sec, j):
        c, qs = sec
        ks = slice(j * kb, (j + 1) * kb)
        return jnp.dot(k_ref[0, ks, :], qcs[c][:, qs], preferred_element_type=F32) + bias_ref[ks, qs]

    if bounded:
        for sec in secs:
            c, qs = sec
            for j in range(n_kb):
                p_ref[c, j * kb:(j + 1) * kb, qs] = _bf(jnp.exp2(logits(sec, j)))
            outs.append(jnp.dot(vt_ref[0, 0], p_ref[c, :, qs], preferred_element_type=F32))
    else:
        s_ref, = s_scratch

        def score_block(sec, j, m):
            c, qs = sec
            s = logits(sec, j)
            s_ref[c, j * kb:(j + 1) * kb, qs] = s
            return jnp.maximum(m, jnp.max(s.reshape(kb // sub, sub, qb), axis=0))

        def prob_block(sec, j, m):
            c, qs = sec
            ks = slice(j * kb, (j + 1) * kb)
            p_ref[c, ks, qs] = _bf(jnp.exp2(s_ref[c, ks, qs] - m))

        m_init = jnp.full((sub, qb), -jnp.inf, F32)
        m = m_init
        for j in range(n_kb):
            m = score_block(secs[0], j, m)
        m_prev = jnp.max(m, axis=0, keepdims=True)
        for n in range(1, len(secs) + 1):
            m = m_init
            for j in range(n_kb):
                if n < len(secs):
                    m = score_block(secs[n], j, m)
                prob_block(secs[n - 1], j, m_prev)
            m_prev = jnp.max(m, axis=0, keepdims=True)
            c, qs = secs[n - 1]
            outs.append(jnp.dot(vt_ref[0, 0], p_ref[c, :, qs], preferred_element_type=F32))
    per_c = len(secs) // 2
    acc0 = jnp.concatenate(outs[:per_c], axis=1)
    acc1 = jnp.concatenate(outs[per_c:], axis=1)
    o0, s0 = acc0[:DIFF_V_DIM], acc0[DIFF_V_DIM:DIFF_V_DIM + 1]
    o1, s1 = acc1[:DIFF_V_DIM], acc1[DIFF_V_DIM:DIFF_V_DIM + 1]
    lv = lam_ref[...]
    lam = (jnp.exp(jnp.sum(lv[0:1] * lv[1:2], axis=-1, keepdims=True))
           - jnp.exp(jnp.sum(lv[2:3] * lv[3:4], axis=-1, keepdims=True)) + lambda_init)
    o = (o0 * (1.0 / s0) - o1 * (lam / s1)).T
    o_ref[0] = _bf(_rms(o, og_ref[...]) * (1.0 - lambda_init))


def _diff_attention(qt, k, vt, q_gain, k_gain, lam_vecs, out_gain, lambda_init, tq):
    b, t, d = k.shape
    nh = d // DIFF_V_DIM
    slopes = 2.0 ** (-8.0 * jnp.arange(1, nh + 1, dtype=F32) / nh)
    bound = 1.02 * math.sqrt(HEAD_DIM) * LOG2E * jnp.max(jnp.abs(q_gain)) * jnp.max(jnp.abs(k_gain))
    head_consts = jnp.stack([jnp.broadcast_to(slopes[:, None], (nh, LANES)),
                             jnp.broadcast_to(bound, (nh, LANES))], axis=1)

    def run(bounded):
        scratch = [pltpu.VMEM((t, tq), F32), pltpu.VMEM((2, t, tq), BF16)]
        if not bounded:
            scratch.append(pltpu.VMEM((2, t, tq), F32))
        return pl.pallas_call(
            functools.partial(_diff_attn_kernel, lambda_init=lambda_init, bounded=bounded),
            grid=(nh, t // tq, b),
            in_specs=[pl.BlockSpec((1, 1, DIFF_V_DIM, tq), lambda h, j, i: (i, h, 0, j)),
                      pl.BlockSpec((1, t, DIFF_V_DIM), lambda h, j, i: (i, 0, h)),
                      pl.BlockSpec((1, 1, vt.shape[2], t), lambda h, j, i: (i, h, 0, 0)),
                      pl.BlockSpec((1, 2, LANES), lambda h, j, i: (h, 0, 0)),
                      _const_spec((4, HEAD_DIM)),
                      _const_spec((1, DIFF_V_DIM))],
            out_specs=pl.BlockSpec((1, tq, DIFF_V_DIM), lambda h, j, i: (i, j, h)),
            out_shape=jax.ShapeDtypeStruct((b, t, d), BF16),
            scratch_shapes=scratch,
            compiler_params=_params(3),
            name="diff_attn_bounded" if bounded else "diff_attn",
        )(qt, k, vt, head_consts, lam_vecs, out_gain.reshape(1, DIFF_V_DIM))

    return lax.cond(2.0 * bound <= SAFE_EXP2_SPAN, lambda: run(True), lambda: run(False))


def kernel(x, mem, mem_norm, mix_norm, cross_norm, ffn_norm, ab_w_in, ab_shift_prev, ab_shift_next, rwkv_w0, rwkv_w2, rwkv_a0, rwkv_a2, rwkv_g2, rwkv_k_k, rwkv_k_a, rwkv_r_k, rwkv_out_gain, sgu_v_gain, sgu_w_s, sgu_b, ab_w_out, diff_w_in, diff_q_gain, diff_k_gain, diff_lambda_q1, diff_lambda_k1, diff_lambda_q2, diff_lambda_k2, diff_out_gain, diff_w_out, cross_wq, cross_wkv, cross_q_gain, cross_k_gain, cross_wo, ffn_w_up, ffn_conv_w, ffn_conv_b, ffn_w_down):
    b, t, d = x.shape
    depth = mix_norm.shape[0]
    tm = min(512, t)
    tq = min(512, t)
    assert t % tm == 0 and t % (RWKV_CHUNK * RWKV_STEP_CHUNKS) == 0 and tm % SGU_CHUNK == 0

    k_all, v_all = _mem_kv(mem, mem_norm, cross_wkv, cross_k_gain)
    for l in range(depth):
        if l % 2 == 0:
            e = l // 2
            (r, v, kk, lw0, lw1, be0, be1, k0, k1, g, bv, b_out) = _ab_in(
                x, mix_norm[l], ab_w_in[e], ab_shift_prev[e], ab_shift_next[e],
                rwkv_w0[e], rwkv_w2[e], rwkv_a0[e], rwkv_a2[e], rwkv_g2[e],
                rwkv_k_k[e], rwkv_k_a[e], rwkv_r_k[e], sgu_v_gain[e], sgu_w_s[e], sgu_b[e], tm)
            yf, yb = _rwkv_scan(r, v, kk, lw0, lw1, be0, be1, k0, k1)
            head = jnp.arange(RWKV_WIDTH) // HEAD_DIM
            mix = (_mix_rwkv_sgu, (yf, yb, bv, g, b_out),
                   (rwkv_out_gain[e].reshape(1, RWKV_WIDTH), _bf(head[:, None] == head[None, :]), _bf(ab_w_out[e])))
        else:
            o = l // 2
            lambda_init = 0.8 - 0.6 * math.exp(-0.3 * l)
            q, k, vt = _diff_in(x, mix_norm[l], diff_w_in[o], diff_q_gain[o], diff_k_gain[o], tm)
            lam_vecs = jnp.stack([diff_lambda_q1[o], diff_lambda_k1[o], diff_lambda_q2[o], diff_lambda_k2[o]])
            c_out = _diff_attention(q, k, vt, diff_q_gain[o], diff_k_gain[o], lam_vecs, diff_out_gain[o],
                                    lambda_init, tq)
            mix = (_mix_proj, (c_out,), (_bf(diff_w_out[o]),))
        x = _cross_attention(x, *mix, cross_norm[l], cross_wq[l], cross_q_gain[l], k_all, v_all, cross_wo[l], l, tm)
        x = _conv_ffn(x, ffn_norm[l], ffn_w_up[l], ffn_conv_w[l], ffn_conv_b[l], ffn_w_down[l], tm)
    return x
```

```python
import functools
import math

import jax
import jax.numpy as jnp
from jax import lax
from jax.experimental import pallas as pl
from jax.experimental.pallas import tpu as pltpu

NORM_EPS = 1e-6
KK_EPS = 1e-12
RWKV_HEADS = 8
HEAD_DIM = 64
RWKV_WIDTH = RWKV_HEADS * HEAD_DIM
LORA_W = 128
RWKV_IN = 3 * RWKV_WIDTH + 3 * LORA_W
SGU_WIDTH = 512
SGU_GROUPS = 4
SGU_CHUNK = 128
DIFF_HEADS = 8
DIFF_V_DIM = 2 * HEAD_DIM
CROSS_HEADS = 4
CONV_WIDTH = 3
RWKV_CHUNK = 64
RWKV_STEP_CHUNKS = 4
LANES = 128
HALO = 8
FF_CHUNK = 256
KEY_BLOCK = 128
QUERY_BLOCK = 256
ONES_ROWS = 16
SAFE_EXP2_SPAN = 100.0
VMEM_LIMIT = 56 * 1024 * 1024
LOG2E = math.log2(math.e)

F32 = jnp.float32
BF16 = jnp.bfloat16
AB_IN_OUT_DTYPES = (BF16, BF16, BF16, F32, F32, BF16, BF16, BF16, BF16, BF16, BF16, BF16)


def _bf(x):
    return x.astype(BF16)


def _dot(a, b):
    return jnp.dot(_bf(a), _bf(b), preferred_element_type=F32)


def _dot_nt(a, b):
    return lax.dot_general(_bf(a), _bf(b), (((1,), (1,)), ((), ())), preferred_element_type=F32)


def _rms(x, gain):
    return x * lax.rsqrt(jnp.mean(x * x, axis=-1, keepdims=True) + NORM_EPS) * gain


def _sigmoid(x):
    return 1.0 / (1.0 + jnp.exp(-x))


def _iota(shape, dim):
    return lax.broadcasted_iota(jnp.int32, shape, dim)


def _shift_prev(u, halo_row):
    rolled = pltpu.roll(u, 1, 0)
    head = rolled[:HALO]
    head = jnp.where(_iota(head.shape, 0) == 0, halo_row, head)
    return jnp.concatenate([head, rolled[HALO:]], axis=0)


def _shift_next(u, halo_row):
    n = u.shape[0]
    rolled = pltpu.roll(u, n - 1, 0)
    tail = rolled[n - HALO:]
    tail = jnp.where(_iota(tail.shape, 0) == HALO - 1, halo_row, tail)
    return jnp.concatenate([rolled[:n - HALO], tail], axis=0)


def _halo_rows(xp_ref, xn_ref, gain, i, nt):
    xh = jnp.concatenate([xp_ref[0], xn_ref[0]], axis=0)
    hh = _rms(xh, gain)
    row = _iota(hh.shape, 0)
    has_prev = jnp.where(i > 0, 1.0, 0.0)
    has_next = jnp.where(i < nt - 1, 1.0, 0.0)
    return _bf(hh * jnp.where(row < HALO, has_prev, has_next))


def _const_spec(shape):
    zeros = (0,) * len(shape)
    return pl.BlockSpec(shape, lambda *_: zeros, pipeline_mode=pl.Buffered(1))


def _params(n_axes):
    return pltpu.CompilerParams(dimension_semantics=("arbitrary",) * n_axes,
                                vmem_limit_bytes=VMEM_LIMIT)


def _row_specs(tm, d, t):
    nb = tm // HALO
    last = t // HALO - 1
    main = pl.BlockSpec((1, tm, d), lambda b, i: (b, i, 0))
    prev = pl.BlockSpec((1, HALO, d), lambda b, i: (b, jnp.maximum(i * nb - 1, 0), 0))
    nxt = pl.BlockSpec((1, HALO, d), lambda b, i: (b, jnp.minimum((i + 1) * nb, last), 0))
    return main, prev, nxt


def _mem_kv_kernel(mem_ref, mnorm_ref, wkv_ref, kg_ref, k_ref, v_ref):
    d = mem_ref.shape[-1]
    hd = d // CROSS_HEADS
    mn = _rms(mem_ref[0], mnorm_ref[...])
    kv = jnp.dot(_bf(mn), wkv_ref[0], preferred_element_type=F32)
    for h in range(CROSS_HEADS):
        kh = _rms(kv[:, h * hd:(h + 1) * hd], kg_ref[0]) * (1.0 / math.sqrt(hd))
        k_ref[0, 0, :, h * hd:(h + 1) * hd] = _bf(kh)
    v_ref[0, 0] = _bf(kv[:, d:])


def _mem_kv(mem, mem_norm, wkv, k_gain):
    b, m, d = mem.shape
    nl = wkv.shape[0]
    out = jax.ShapeDtypeStruct((nl, b, m, d), BF16)
    return pl.pallas_call(
        _mem_kv_kernel,
        grid=(nl, b),
        in_specs=[pl.BlockSpec((1, m, d), lambda l, i: (i, 0, 0)),
                  pl.BlockSpec((1, d), lambda l, i: (0, 0)),
                  pl.BlockSpec((1, d, 2 * d), lambda l, i: (l, 0, 0)),
                  pl.BlockSpec((1, 1, d // CROSS_HEADS), lambda l, i: (l, 0, 0))],
        out_specs=[pl.BlockSpec((1, 1, m, d), lambda l, i: (l, i, 0, 0)),
                   pl.BlockSpec((1, 1, m, d), lambda l, i: (l, i, 0, 0))],
        out_shape=[out, out],
        compiler_params=_params(2),
        name="mem_kv",
    )(mem, mem_norm.reshape(1, d), _bf(wkv), k_gain.reshape(nl, 1, -1))


def _mix_rwkv_sgu(x, yf_ref, yb_ref, bv_ref, g_ref, b_ref, og_ref, bd_ref, wo_ref):
    y = yf_ref[0] + yb_ref[0]
    ms = _dot(y * y, bd_ref[...]) * (1.0 / HEAD_DIM)
    yn = y * lax.rsqrt(ms + NORM_EPS) * og_ref[...]
    a_out = (yn + bv_ref[0].astype(F32)) * g_ref[0].astype(F32)
    cat = jnp.concatenate([_bf(a_out), b_ref[0]], axis=1)
    return x + jnp.dot(cat, wo_ref[...], preferred_element_type=F32)


def _mix_proj(x, a_ref, w_ref):
    return x + jnp.dot(a_ref[0], w_ref[...], preferred_element_type=F32)


def _cross_kernel(*refs, n_mix, mix_fn):
    x_ref, mix_refs = refs[0], refs[1:1 + n_mix]
    g_ref, wq_ref, qg_ref, k_ref, v_ref, wo_ref, o_ref = refs[1 + n_mix:]
    x = mix_fn(x_ref[0], *mix_refs)
    d = x.shape[-1]
    hd = d // CROSS_HEADS
    hn = _bf(_rms(x, g_ref[...]))
    q = jnp.dot(hn, wq_ref[...], preferred_element_type=F32)
    outs = []
    for h in range(CROSS_HEADS):
        sl = slice(h * hd, (h + 1) * hd)
        qh = _rms(q[:, sl], qg_ref[...])
        logits = _dot_nt(qh, k_ref[0, 0, :, sl])
        p = jnp.exp(logits - jnp.max(logits, axis=-1, keepdims=True))
        s = jnp.sum(p, axis=-1, keepdims=True)
        outs.append(jnp.dot(_bf(p), v_ref[0, 0, :, sl], preferred_element_type=F32) / s)
    o = jnp.concatenate(outs, axis=1)
    o_ref[0] = x + jnp.dot(_bf(o), wo_ref[...], preferred_element_type=F32)


def _cross_attention(x, mix_fn, mix_rows, mix_consts, gain, wq, q_gain, k_all, v_all, wo, layer, tq):
    b, t, d = x.shape
    m = k_all.shape[2]
    hd = d // CROSS_HEADS
    mix_specs = ([pl.BlockSpec((1, tq, a.shape[-1]), lambda i, j: (i, j, 0)) for a in mix_rows]
                 + [_const_spec(a.shape) for a in mix_consts])
    return pl.pallas_call(
        functools.partial(_cross_kernel, n_mix=len(mix_specs), mix_fn=mix_fn),
        grid=(b, t // tq),
        in_specs=[pl.BlockSpec((1, tq, d), lambda i, j: (i, j, 0)),
                  *mix_specs,
                  _const_spec((1, d)),
                  _const_spec((d, d)),
                  _const_spec((1, hd)),
                  pl.BlockSpec((1, 1, m, d), lambda i, j: (layer, i, 0, 0)),
                  pl.BlockSpec((1, 1, m, d), lambda i, j: (layer, i, 0, 0)),
                  _const_spec((d, d))],
        out_specs=pl.BlockSpec((1, tq, d), lambda i, j: (i, j, 0)),
        out_shape=jax.ShapeDtypeStruct((b, t, d), F32),
        compiler_params=_params(2),
        name=f"cross_attn_{layer}",
    )(x, *mix_rows, *mix_consts, gain.reshape(1, d), _bf(wq), q_gain.reshape(1, hd), k_all, v_all, _bf(wo))


def _ffn_kernel(x_ref, xp_ref, xn_ref, g_ref, wup_ref, cw_ref, cb_ref, wdn_ref, o_ref,
                hn_ref, acc_ref):
    i = pl.program_id(1)
    nt = pl.num_programs(1)
    n_chunks = wdn_ref.shape[0]
    x = x_ref[0]
    tm = x.shape[0]
    gain = g_ref[...]
    hn_ref[:tm] = _bf(_rms(x, gain))
    hn_ref[tm:] = _halo_rows(xp_ref, xn_ref, gain, i, nt)
    acc_ref[...] = jnp.zeros_like(acc_ref)

    def up_proj(idx):
        u = jnp.dot(hn_ref[...], wup_ref[idx], preferred_element_type=F32)
        return u[:tm], u[tm:]

    def conv(idx, u, uh):
        cw = cw_ref[idx]
        up = _shift_prev(u, uh[HALO - 1:HALO])
        un = _shift_next(u, uh[HALO:HALO + 1])
        return cb_ref[idx] + up * cw[0:1] + u * cw[1:2] + un * cw[2:3]

    ups = [up_proj(0), up_proj(n_chunks)]
    for c in range(n_chunks):
        nxt = [up_proj(c + 1), up_proj(c + 1 + n_chunks)] if c + 1 < n_chunks else None
        gate = conv(c, *ups[0])
        val = conv(c + n_chunks, *ups[1])
        act = gate * _sigmoid(gate) * val
        acc_ref[...] += jnp.dot(_bf(act), wdn_ref[c], preferred_element_type=F32)
        ups = nxt
    o_ref[0] = x + acc_ref[...]


def _conv_ffn(x, gain, w_up, conv_w, conv_b, w_down, tm):
    b, t, d = x.shape
    ff = w_down.shape[0]
    nc = ff // FF_CHUNK
    wup = _bf(w_up).reshape(d, 2 * nc, FF_CHUNK).transpose(1, 0, 2)
    cw = conv_w.reshape(CONV_WIDTH, 2 * nc, FF_CHUNK).transpose(1, 0, 2)
    cb = conv_b.reshape(2 * nc, 1, FF_CHUNK)
    wdn = _bf(w_down).reshape(nc, FF_CHUNK, d)
    main, prev, nxt = _row_specs(tm, d, t)
    return pl.pallas_call(
        _ffn_kernel,
        grid=(b, t // tm),
        in_specs=[main, prev, nxt,
                  _const_spec((1, d)),
                  _const_spec((2 * nc, d, FF_CHUNK)),
                  _const_spec((2 * nc, CONV_WIDTH, FF_CHUNK)),
                  _const_spec((2 * nc, 1, FF_CHUNK)),
                  _const_spec((nc, FF_CHUNK, d))],
        out_specs=pl.BlockSpec((1, tm, d), lambda i, j: (i, j, 0)),
        out_shape=jax.ShapeDtypeStruct((b, t, d), F32),
        scratch_shapes=[pltpu.VMEM((tm + 2 * HALO, d), BF16),
                        pltpu.VMEM((tm, d), F32)],
        compiler_params=_params(2),
        name="conv_ffn",
    )(x, x, x, gain.reshape(1, d), wup, cw, cb, wdn)


def _ab_in_kernel(x_ref, xp_ref, xn_ref, g_ref, win_ref, mup_ref, mun_ref,
                  w0_ref, w2_ref, a0_ref, a2_ref, g2_ref, kk_ref, ka_ref, rk_ref, bd_ref,
                  vg_ref, ws_ref, sb_ref,
                  r_o, v_o, kk_o, lw0_o, lw1_o, be0_o, be1_o, k0_o, k1_o, g_o, bv_o, b_o):
    i = pl.program_id(1)
    nt = pl.num_programs(1)
    w = RWKV_WIDTH
    x = x_ref[0]
    tm = x.shape[0]
    gain = g_ref[...]
    hn = _bf(_rms(x, gain))
    hx = jnp.concatenate([hn, _halo_rows(xp_ref, xn_ref, gain, i, nt)], axis=0)
    zr = jnp.dot(hx, win_ref[:, :RWKV_IN], preferred_element_type=F32)
    za, zh = zr[:tm], zr[tm:]
    zp = _shift_prev(za, zh[HALO - 1:HALO])
    zn = _shift_next(za, zh[HALO:HALO + 1])
    mup, mun = mup_ref[...], mun_ref[...]
    zs = za * (1.0 - mup - mun) + mup * zp + mun * zn
    r = zs[:, 0:w]
    k = zs[:, w:2 * w]
    v = zs[:, 2 * w:3 * w]
    wl = jnp.tanh(zs[:, 3 * w:3 * w + LORA_W])
    al = zs[:, 3 * w + LORA_W:3 * w + 2 * LORA_W]
    gl = zs[:, 3 * w + 2 * LORA_W:]

    wpre = w0_ref[...] + _dot(wl, w2_ref[...])
    lw = -math.exp(-0.5) * _sigmoid(wpre)
    a = _sigmoid(a0_ref[...] + _dot(al, a2_ref[...]))
    g = _dot(_sigmoid(gl), g2_ref[...])

    bd = bd_ref[...]
    kk0 = k * kk_ref[...]
    kk = kk0 * lax.rsqrt(_dot(kk0 * kk0, bd) + KK_EPS)
    ka = ka_ref[...]
    k0 = k * (1.0 + (a[:, :w] - 1.0) * ka)
    k1 = k * (1.0 + (a[:, w:] - 1.0) * ka)
    bonus = _dot(r * rk_ref[...] * (k0 + k1), bd)

    r_o[0] = _bf(r)
    v_o[0] = _bf(v)
    kk_o[0] = _bf(kk)
    lw0_o[0] = lw[:, :w]
    lw1_o[0] = lw[:, w:]
    be0_o[0] = _bf(kk * a[:, :w])
    be1_o[0] = _bf(kk * a[:, w:])
    k0_o[0] = _bf(k0)
    k1_o[0] = _bf(k1)
    g_o[0] = _bf(g)
    bv_o[0] = _bf(bonus * v)

    zg = jnp.dot(hn, win_ref[:, RWKV_IN:], preferred_element_type=F32)
    zg = zg * (0.5 * (1.0 + jnp.tanh(math.sqrt(2.0 / math.pi) * (zg + 0.044715 * (zg * zg * zg)))))
    u = zg[:, :SGU_WIDTH]
    vn = _bf(_rms(zg[:, SGU_WIDTH:], vg_ref[...]))
    gd = SGU_WIDTH // SGU_GROUPS
    for n in range(tm // SGU_CHUNK):
        rows = slice(n * SGU_CHUNK, (n + 1) * SGU_CHUNK)
        for gi in range(SGU_GROUPS):
            cols = slice(gi * gd, (gi + 1) * gd)
            s = jnp.dot(ws_ref[gi], vn[rows, cols], preferred_element_type=F32) + sb_ref[gi]
            b_o[0, rows, cols] = _bf(u[rows, cols] * s)


def _ab_in(x, gain, w_in, mu_prev, mu_next, w0, w2, a0, a2, g2, k_k, k_a, r_k, v_gain, w_s, s_b, tm):
    b, t, d = x.shape
    w = RWKV_WIDTH
    n_in = w_in.shape[1]
    half = LORA_W // 2

    def lora_cat(m):
        z = jnp.zeros((half, w), m.dtype)
        return _bf(jnp.concatenate([jnp.concatenate([m[0], z], axis=1),
                                    jnp.concatenate([z, m[1]], axis=1)], axis=0))

    head = jnp.arange(w) // HEAD_DIM
    bd = _bf(head[:, None] == head[None, :])
    sb = jnp.broadcast_to(s_b[:, :, None], (SGU_GROUPS, SGU_CHUNK, SGU_WIDTH // SGU_GROUPS))
    main, prev, nxt = _row_specs(tm, d, t)
    ospec = pl.BlockSpec((1, tm, w), lambda i, j: (i, j, 0))
    return pl.pallas_call(
        _ab_in_kernel,
        grid=(b, t // tm),
        in_specs=[main, prev, nxt,
                  _const_spec((1, d)),
                  _const_spec((d, n_in)),
                  _const_spec((1, RWKV_IN)), _const_spec((1, RWKV_IN)),
                  _const_spec((1, 2 * w)), _const_spec((LORA_W, 2 * w)),
                  _const_spec((1, 2 * w)), _const_spec((LORA_W, 2 * w)),
                  _const_spec((LORA_W, w)),
                  _const_spec((1, w)), _const_spec((1, w)), _const_spec((1, w)),
                  _const_spec((w, w)),
                  _const_spec((1, SGU_WIDTH)),
                  _const_spec((SGU_GROUPS, SGU_CHUNK, SGU_CHUNK)),
                  _const_spec((SGU_GROUPS, SGU_CHUNK, SGU_WIDTH // SGU_GROUPS))],
        out_specs=[ospec] * 12,
        out_shape=[jax.ShapeDtypeStruct((b, t, w), dt) for dt in AB_IN_OUT_DTYPES],
        compiler_params=_params(2),
        name="ab_in",
    )(x, x, x, gain.reshape(1, d), _bf(w_in), mu_prev.reshape(1, -1), mu_next.reshape(1, -1),
      w0.reshape(1, 2 * w), lora_cat(w2), a0.reshape(1, 2 * w), lora_cat(a2), _bf(g2),
      k_k.reshape(1, w), k_a.reshape(1, w), r_k.reshape(1, w), bd,
      v_gain.reshape(1, -1), _bf(w_s), sb)


def _stack_heads(x):
    even = (_iota(x.shape, 1) & (LANES - 1)) < HEAD_DIM
    return jnp.concatenate([jnp.where(even, x, 0.0), jnp.where(even, 0.0, x)], axis=0)


def _rwkv_prep(r, v, kk, lw, be, kd, reverse):
    c = r.shape[0]
    rr = _iota((c, LANES), 0)
    cc = _iota((c, LANES), 1) & (HEAD_DIM - 1)
    if reverse:
        strict, incl = cc > rr, cc >= rr
    else:
        strict, incl = cc < rr, cc <= rr
    tri = _bf(incl[:, :c])

    l1 = _bf(lw)
    l2 = _bf(lw - l1.astype(F32))
    l3 = _bf(lw - l1.astype(F32) - l2.astype(F32))
    cum = (jnp.dot(tri, l1, preferred_element_type=F32) + jnp.dot(tri, l2, preferred_element_type=F32)
           + jnp.dot(tri, l3, preferred_element_type=F32))
    tot = cum[0:1] if reverse else cum[c - 1:c]
    e_neg = jnp.exp(-cum)
    e_end = jnp.exp(tot - cum)
    return dict(ag=-kk * jnp.exp(cum - lw), rg=r * jnp.exp(cum), bi=be * e_neg, ki=kd * e_neg,
                bee=be * e_end, kee=kd * e_end, gam=jnp.exp(tot), v=v, strict=strict, incl=incl)


def _rwkv_local(chunks):
    c = chunks[0]["v"].shape[0]
    n_tiles = chunks[0]["v"].shape[1] // LANES
    r128 = _iota((LANES, LANES), 0)
    c128 = _iota((LANES, LANES), 1)
    same_head = (r128 < HEAD_DIM) == (c128 < HEAD_DIM)
    diag = r128 == c128
    ctx = [(ch, slice(p * LANES, (p + 1) * LANES)) for ch in chunks for p in range(n_tiles)]

    aa = [_dot_nt(jnp.concatenate([ch["ag"][:, sl], ch["rg"][:, sl]], axis=0),
                  jnp.concatenate([_stack_heads(ch["bi"][:, sl]), _stack_heads(ch["ki"][:, sl])], axis=0))
          for ch, sl in ctx]
    a_ab = [jnp.where(ch["strict"], m[:c, :LANES], 0.0) for (ch, _), m in zip(ctx, aa)]
    a_rb = [jnp.where(ch["incl"], m[c:, :LANES], 0.0) for (ch, _), m in zip(ctx, aa)]
    a_k = [jnp.concatenate([jnp.where(ch["strict"], m[:c, LANES:], 0.0),
                            jnp.where(ch["incl"], m[c:, LANES:], 0.0)], axis=0) for (ch, _), m in zip(ctx, aa)]
    xv = [_dot(m, _stack_heads(ch["v"][:, sl])) for (ch, sl), m in zip(ctx, a_k)]
    x = [jnp.concatenate([ch["ag"][:, sl], m[:c]], axis=1) for (ch, sl), m in zip(ctx, xv)]
    def block_diag(pi):
        return jnp.where(same_head, jnp.concatenate([pi, pi], axis=0), 0.0)

    n_round = int(math.log2(c))
    rr = a_ab
    pw = [_dot(pi, block_diag(pi)) for pi in a_ab]
    for k in range(1, n_round):
        if k < n_round - 1:
            res = [_dot(jnp.concatenate([ri, pi], axis=0), block_diag(pi)) for ri, pi in zip(rr, pw)]
            rr = [ri + pi + m[:c] for ri, pi, m in zip(rr, pw, res)]
            pw = [m[c:] for m in res]
        else:
            rr = [ri + pi + _dot(ri, block_diag(pi)) for ri, pi in zip(rr, pw)]
    x = [xi + _dot(ri, _stack_heads(xi)) for xi, ri in zip(x, rr)]
    ry = [jnp.concatenate([ch["rg"][:, sl], m[c:]], axis=1) + _dot(ai, _stack_heads(xi))
          for (ch, sl), m, ai, xi in zip(ctx, xv, a_rb, x)]
    mg = [_dot(jnp.concatenate([ch["bee"][:, sl], ch["kee"][:, sl]], axis=0).T,
               jnp.concatenate([xi, jnp.concatenate([jnp.zeros_like(ch["v"][:, sl]), ch["v"][:, sl]], axis=1)],
                               axis=0))
          for (ch, sl), xi in zip(ctx, x)]
    out = []
    for (ch, sl), ryi, mgi in zip(ctx, ry, mg):
        gam_col = jnp.sum(jnp.where(diag, ch["gam"][:, sl], 0.0), axis=1, keepdims=True)
        out.append((ryi[:, :LANES], ryi[:, LANES:], jnp.where(same_head, mgi[:, :LANES], 0.0),
                    jnp.where(same_head, mgi[:, LANES:], 0.0), gam_col))
    return [out[i * n_tiles:(i + 1) * n_tiles] for i in range(len(chunks))]


def _rwkv_kernel(rf, vf, kkf, lwf, bef, kf, rb, vb, kkb, lwb, beb, kb, yf_o, yb_o, zf_ref, zb_ref):
    @pl.when(pl.program_id(1) == 0)
    def _():
        zf_ref[...] = jnp.zeros_like(zf_ref)
        zb_ref[...] = jnp.zeros_like(zb_ref)

    c = RWKV_CHUNK
    n = rf.shape[1] // c
    n_tiles = rf.shape[2] // LANES
    chunks = []
    for refs, reverse in (((rf, vf, kkf, lwf, bef, kf), False), ((rb, vb, kkb, lwb, beb, kb), True)):
        for ci in range(n):
            chunks.append(_rwkv_prep(*[ref[0, ci * c:(ci + 1) * c, :].astype(F32) for ref in refs], reverse))
    local = _rwkv_local(chunks)

    for d, (z_ref, y_o) in enumerate(((zf_ref, yf_o), (zb_ref, yb_o))):
        order = range(n) if d == 0 else range(n - 1, -1, -1)
        zs = [z_ref[p] for p in range(n_tiles)]
        for ci in order:
            ys = []
            for p in range(n_tiles):
                rt, yl, mt, gt, gam_col = local[d * n + ci][p]
                yz = _dot(jnp.concatenate([rt, mt], axis=0), zs[p])
                ys.append(yz[:c] + yl)
                zs[p] = gam_col * zs[p] + yz[c:] + gt
            y_o[0, ci * c:(ci + 1) * c, :] = jnp.concatenate(ys, axis=1)
        for p in range(n_tiles):
            z_ref[p] = zs[p]


def _rwkv_scan(r, v, kk, lw0, lw1, be0, be1, k0, k1):
    b, t, w = r.shape
    c = RWKV_CHUNK * RWKV_STEP_CHUNKS
    nc = t // c
    fwd = pl.BlockSpec((1, c, w), lambda i, j: (i, j, 0))
    bwd = pl.BlockSpec((1, c, w), lambda i, j: (i, nc - 1 - j, 0))
    out = jax.ShapeDtypeStruct((b, t, w), F32)
    zshape = pltpu.VMEM((w // LANES, LANES, LANES), F32)
    return pl.pallas_call(
        _rwkv_kernel,
        grid=(b, nc),
        in_specs=[fwd] * 6 + [bwd] * 6,
        out_specs=[fwd, bwd],
        out_shape=[out, out],
        scratch_shapes=[zshape, zshape],
        compiler_params=_params(2),
        name="rwkv_scan",
    )(r, v, kk, lw0, be0, k0, r, v, kk, lw1, be1, k1)


def _diff_in_kernel(x_ref, g_ref, w_ref, qg_ref, kg_ref, bd_ref, qt_o, k_o, vt_o):
    x = x_ref[0]
    d = x.shape[-1]
    hn = _bf(_rms(x, g_ref[...]))
    z = jnp.dot(hn, w_ref[...], preferred_element_type=F32)
    bd = bd_ref[...]
    half = bd.shape[0]

    def qk_norm(y, gain, scale):
        parts = []
        for j in range(d // half):
            yj = y[:, j * half:(j + 1) * half]
            ms = _dot(yj * yj, bd) * (1.0 / HEAD_DIM)
            parts.append(yj * lax.rsqrt(ms + NORM_EPS) * gain * scale)
        return jnp.concatenate(parts, axis=1)

    q = qk_norm(z[:, :d], qg_ref[...], LOG2E / math.sqrt(HEAD_DIM))
    k_o[0] = _bf(qk_norm(z[:, d:2 * d], kg_ref[...], 1.0))
    tm = x.shape[0]
    for h in range(d // DIFF_V_DIM):
        cols = slice(h * DIFF_V_DIM, (h + 1) * DIFF_V_DIM)
        qt_o[0, h] = _bf(q[:, cols].T)
        vt_o[0, h, :DIFF_V_DIM, :] = _bf(z[:, 2 * d:][:, cols].T)
        vt_o[0, h, DIFF_V_DIM:, :] = jnp.ones((ONES_ROWS, tm), BF16)


def _diff_in(x, gain, w_in, q_gain, k_gain, tm):
    b, t, d = x.shape
    half = 512
    nh = d // DIFF_V_DIM
    head = jnp.arange(half) // HEAD_DIM
    bd = _bf(head[:, None] == head[None, :])
    xs = pl.BlockSpec((1, tm, d), lambda i, j: (i, j, 0))
    out = jax.ShapeDtypeStruct((b, t, d), BF16)
    vt_rows = DIFF_V_DIM + ONES_ROWS
    return pl.pallas_call(
        _diff_in_kernel,
        grid=(b, t // tm),
        in_specs=[xs, _const_spec((1, d)), _const_spec((d, 3 * d)),
                  _const_spec((1, half)), _const_spec((1, half)), _const_spec((half, half))],
        out_specs=[pl.BlockSpec((1, nh, DIFF_V_DIM, tm), lambda i, j: (i, 0, 0, j)), xs,
                   pl.BlockSpec((1, nh, vt_rows, tm), lambda i, j: (i, 0, 0, j))],
        out_shape=[jax.ShapeDtypeStruct((b, nh, DIFF_V_DIM, t), BF16), out,
                   jax.ShapeDtypeStruct((b, nh, vt_rows, t), BF16)],
        compiler_params=_params(2),
        name="diff_in",
    )(x, gain.reshape(1, d), _bf(w_in), jnp.tile(q_gain, half // HEAD_DIM).reshape(1, half),
      jnp.tile(k_gain, half // HEAD_DIM).reshape(1, half), bd)


def _diff_attn_kernel(qt_ref, k_ref, vt_ref, hc_ref, lam_ref, og_ref, o_ref, bias_ref, p_ref, *s_scratch,
                      lambda_init, bounded):
    t, tq = bias_ref.shape

    @pl.when(pl.program_id(2) == 0)
    def _():
        dist = jnp.abs((_iota((t, tq), 0) - pl.program_id(1) * tq - _iota((t, tq), 1)).astype(F32))
        bias = dist * (-LOG2E * hc_ref[0, 0:1, 0:1])
        bias_ref[...] = bias - hc_ref[0, 1:2, 0:1] if bounded else bias

    q = qt_ref[0, 0]
    first = _iota(q.shape, 0) < HEAD_DIM
    zero = jnp.zeros_like(q)
    kb = min(KEY_BLOCK, t)
    qb = min(QUERY_BLOCK, tq)
    sub = HALO

    qcs = (jnp.where(first, q, zero), jnp.where(first, zero, q))
    secs = [(c, slice(i * qb, (i + 1) * qb)) for c in range(2) for i in range(tq // qb)]
    n_kb = t // kb
    outs = []

    def logits(sec, j):
        c, qs = sec
        ks = slice(j * kb, (j + 1) * kb)
        return jnp.dot(k_ref[0, ks, :], qcs[c][:, qs], preferred_element_type=F32) + bias_ref[ks, qs]

    if bounded:
        for sec in secs:
            c, qs = sec
            for j in range(n_kb):
                p_ref[c, j * kb:(j + 1) * kb, qs] = _bf(jnp.exp2(logits(sec, j)))
            outs.append(jnp.dot(vt_ref[0, 0], p_ref[c, :, qs], preferred_element_type=F32))
    else:
        s_ref, = s_scratch

        def score_block(sec, j, m):
            c, qs = sec
            s = logits(sec, j)
            s_ref[c, j * kb:(j + 1) * kb, qs] = s
            return jnp.maximum(m, jnp.max(s.reshape(kb // sub, sub, qb), axis=0))

        def prob_block(sec, j, m):
            c, qs = sec
            ks = slice(j * kb, (j + 1) * kb)
            p_ref[c, ks, qs] = _bf(jnp.exp2(s_ref[c, ks, qs] - m))

        m_init = jnp.full((sub, qb), -jnp.inf, F32)
        m = m_init
        for j in range(n_kb):
            m = score_block(secs[0], j, m)
        m_prev = jnp.max(m, axis=0, keepdims=True)
        for n in range(1, len(secs) + 1):
            m = m_init
            for j in range(n_kb):
                if n < len(secs):
                    m = score_block(secs[n], j, m)
                prob_block(secs[n - 1], j, m_prev)
            m_prev = jnp.max(m, axis=0, keepdims=True)
            c, qs = secs[n - 1]
            outs.append(jnp.dot(vt_ref[0, 0], p_ref[c, :, qs], preferred_element_type=F32))
    per_c = len(secs) // 2
    acc0 = jnp.concatenate(outs[:per_c], axis=1)
    acc1 = jnp.concatenate(outs[per_c:], axis=1)
    o0, s0 = acc0[:DIFF_V_DIM], acc0[DIFF_V_DIM:DIFF_V_DIM + 1]
    o1, s1 = acc1[:DIFF_V_DIM], acc1[DIFF_V_DIM:DIFF_V_DIM + 1]
    lv = lam_ref[...]
    lam = (jnp.exp(jnp.sum(lv[0:1] * lv[1:2], axis=-1, keepdims=True))
           - jnp.exp(jnp.sum(lv[2:3] * lv[3:4], axis=-1, keepdims=True)) + lambda_init)
    o = (o0 * (1.0 / s0) - o1 * (lam / s1)).T
    o_ref[0] = _bf(_rms(o, og_ref[...]) * (1.0 - lambda_init))


def _diff_attention(qt, k, vt, q_gain, k_gain, lam_vecs, out_gain, lambda_init, tq):
    b, t, d = k.shape
    nh = d // DIFF_V_DIM
    slopes = 2.0 ** (-8.0 * jnp.arange(1, nh + 1, dtype=F32) / nh)
    bound = 1.02 * math.sqrt(HEAD_DIM) * LOG2E * jnp.max(jnp.abs(q_gain)) * jnp.max(jnp.abs(k_gain))
    head_consts = jnp.stack([jnp.broadcast_to(slopes[:, None], (nh, LANES)),
                             jnp.broadcast_to(bound, (nh, LANES))], axis=1)

    def run(bounded):
        scratch = [pltpu.VMEM((t, tq), F32), pltpu.VMEM((2, t, tq), BF16)]
        if not bounded:
            scratch.append(pltpu.VMEM((2, t, tq), F32))
        return pl.pallas_call(
            functools.partial(_diff_attn_kernel, lambda_init=lambda_init, bounded=bounded),
            grid=(nh, t // tq, b),
            in_specs=[pl.BlockSpec((1, 1, DIFF_V_DIM, tq), lambda h, j, i: (i, h, 0, j)),
                      pl.BlockSpec((1, t, DIFF_V_DIM), lambda h, j, i: (i, 0, h)),
                      pl.BlockSpec((1, 1, vt.shape[2], t), lambda h, j, i: (i, h, 0, 0)),
                      pl.BlockSpec((1, 2, LANES), lambda h, j, i: (h, 0, 0)),
                      _const_spec((4, HEAD_DIM)),
                      _const_spec((1, DIFF_V_DIM))],
            out_specs=pl.BlockSpec((1, tq, DIFF_V_DIM), lambda h, j, i: (i, j, h)),
            out_shape=jax.ShapeDtypeStruct((b, t, d), BF16),
            scratch_shapes=scratch,
            compiler_params=_params(3),
            name="diff_attn_bounded" if bounded else "diff_attn",
        )(qt, k, vt, head_consts, lam_vecs, out_gain.reshape(1, DIFF_V_DIM))

    return lax.cond(2.0 * bound <= SAFE_EXP2_SPAN, lambda: run(True), lambda: run(False))


def kernel(x, mem, mem_norm, mix_norm, cross_norm, ffn_norm, ab_w_in, ab_shift_prev, ab_shift_next, rwkv_w0, rwkv_w2, rwkv_a0, rwkv_a2, rwkv_g2, rwkv_k_k, rwkv_k_a, rwkv_r_k, rwkv_out_gain, sgu_v_gain, sgu_w_s, sgu_b, ab_w_out, diff_w_in, diff_q_gain, diff_k_gain, diff_lambda_q1, diff_lambda_k1, diff_lambda_q2, diff_lambda_k2, diff_out_gain, diff_w_out, cross_wq, cross_wkv, cross_q_gain, cross_k_gain, cross_wo, ffn_w_up, ffn_conv_w, ffn_conv_b, ffn_w_down):
    b, t, d = x.shape
    depth = mix_norm.shape[0]
    tm = min(512, t)
    tq = min(512, t)
    assert t % tm == 0 and t % (RWKV_CHUNK * RWKV_STEP_CHUNKS) == 0 and tm % SGU_CHUNK == 0

    k_all, v_all = _mem_kv(mem, mem_norm, cross_wkv, cross_k_gain)
    for l in range(depth):
        if l % 2 == 0:
            e = l // 2
            (r, v, kk, lw0, lw1, be0, be1, k0, k1, g, bv, b_out) = _ab_in(
                x, mix_norm[l], ab_w_in[e], ab_shift_prev[e], ab_shift_next[e],
                rwkv_w0[e], rwkv_w2[e], rwkv_a0[e], rwkv_a2[e], rwkv_g2[e],
                rwkv_k_k[e], rwkv_k_a[e], rwkv_r_k[e], sgu_v_gain[e], sgu_w_s[e], sgu_b[e], tm)
            yf, yb = _rwkv_scan(r, v, kk, lw0, lw1, be0, be1, k0, k1)
            head = jnp.arange(RWKV_WIDTH) // HEAD_DIM
            mix = (_mix_rwkv_sgu, (yf, yb, bv, g, b_out),
                   (rwkv_out_gain[e].reshape(1, RWKV_WIDTH), _bf(head[:, None] == head[None, :]), _bf(ab_w_out[e])))
        else:
            o = l // 2
            lambda_init = 0.8 - 0.6 * math.exp(-0.3 * l)
            q, k, vt = _diff_in(x, mix_norm[l], diff_w_in[o], diff_q_gain[o], diff_k_gain[o], tm)
            lam_vecs = jnp.stack([diff_lambda_q1[o], diff_lambda_k1[o], diff_lambda_q2[o], diff_lambda_k2[o]])
            c_out = _diff_attention(q, k, vt, diff_q_gain[o], diff_k_gain[o], lam_vecs, diff_out_gain[o],
                                    lambda_init, tq)
            mix = (_mix_proj, (c_out,), (_bf(diff_w_out[o]),))
        x = _cross_attention(x, *mix, cross_norm[l], cross_wq[l], cross_q_gain[l], k_all, v_all, cross_wo[l], l, tm)
        x = _conv_ffn(x, ffn_norm[l], ffn_w_up[l], ffn_conv_w[l], ffn_conv_b[l], ffn_w_down[l], tm)
    return x
```

```python
import functools
import math

import jax
import jax.numpy as jnp
from jax import lax
from jax.experimental import pallas as pl
from jax.experimental.pallas import tpu as pltpu

NORM_EPS = 1e-6
KK_EPS = 1e-12
RWKV_HEADS = 8
HEAD_DIM = 64
RWKV_WIDTH = RWKV_HEADS * HEAD_DIM
LORA_W = 128
RWKV_IN = 3 * RWKV_WIDTH + 3 * LORA_W
SGU_WIDTH = 512
SGU_GROUPS = 4
SGU_CHUNK = 128
DIFF_HEADS = 8
DIFF_V_DIM = 2 * HEAD_DIM
CROSS_HEADS = 4
CONV_WIDTH = 3
RWKV_CHUNK = 64
RWKV_STEP_CHUNKS = 4
LANES = 128
HALO = 8
FF_CHUNK = 256
MXU_ROWS = 256
KEY_BLOCK = 128
QUERY_BLOCK = 256
ONES_ROWS = 16
SAFE_EXP2_SPAN = 100.0
VMEM_LIMIT = 56 * 1024 * 1024
LOG2E = math.log2(math.e)

F32 = jnp.float32
BF16 = jnp.bfloat16
AB_IN_OUT_DTYPES = (BF16, BF16, BF16, F32, F32, BF16, BF16, BF16, BF16, BF16, BF16, BF16)


def _bf(x):
    return x.astype(BF16)


def _dot(a, b):
    return jnp.dot(_bf(a), _bf(b), preferred_element_type=F32)


def _dot_nt(a, b):
    return lax.dot_general(_bf(a), _bf(b), (((1,), (1,)), ((), ())), preferred_element_type=F32)


def _dot_row_pieces(a, b):
    m = a.shape[0]
    starts = list(range(0, m - m % MXU_ROWS, MXU_ROWS)) or [0]
    ends = starts[1:] + [m]
    return jnp.concatenate([_dot(a[s:e], b) for s, e in zip(starts, ends)], axis=0)


def _rms(x, gain):
    return x * lax.rsqrt(jnp.mean(x * x, axis=-1, keepdims=True) + NORM_EPS) * gain


def _sigmoid(x):
    return 1.0 / (1.0 + jnp.exp(-x))


def _iota(shape, dim):
    return lax.broadcasted_iota(jnp.int32, shape, dim)


def _shift_prev(u, halo_row):
    rolled = pltpu.roll(u, 1, 0)
    head = rolled[:HALO]
    head = jnp.where(_iota(head.shape, 0) == 0, halo_row, head)
    return jnp.concatenate([head, rolled[HALO:]], axis=0)


def _shift_next(u, halo_row):
    n = u.shape[0]
    rolled = pltpu.roll(u, n - 1, 0)
    tail = rolled[n - HALO:]
    tail = jnp.where(_iota(tail.shape, 0) == HALO - 1, halo_row, tail)
    return jnp.concatenate([rolled[:n - HALO], tail], axis=0)


def _halo_rows(xp_ref, xn_ref, gain, i, nt):
    xh = jnp.concatenate([xp_ref[0], xn_ref[0]], axis=0)
    hh = _rms(xh, gain)
    row = _iota(hh.shape, 0)
    has_prev = jnp.where(i > 0, 1.0, 0.0)
    has_next = jnp.where(i < nt - 1, 1.0, 0.0)
    return _bf(hh * jnp.where(row < HALO, has_prev, has_next))


def _const_spec(shape):
    zeros = (0,) * len(shape)
    return pl.BlockSpec(shape, lambda *_: zeros, pipeline_mode=pl.Buffered(1))


def _params(n_axes):
    return pltpu.CompilerParams(dimension_semantics=("arbitrary",) * n_axes,
                                vmem_limit_bytes=VMEM_LIMIT)


def _row_specs(tm, d, t):
    nb = tm // HALO
    last = t // HALO - 1
    main = pl.BlockSpec((1, tm, d), lambda b, i: (b, i, 0))
    prev = pl.BlockSpec((1, HALO, d), lambda b, i: (b, jnp.maximum(i * nb - 1, 0), 0))
    nxt = pl.BlockSpec((1, HALO, d), lambda b, i: (b, jnp.minimum((i + 1) * nb, last), 0))
    return main, prev, nxt


def _mem_kv_kernel(mem_ref, mnorm_ref, wkv_ref, kg_ref, k_ref, v_ref):
    d = mem_ref.shape[-1]
    hd = d // CROSS_HEADS
    mn = _rms(mem_ref[0], mnorm_ref[...])
    kv = jnp.dot(_bf(mn), wkv_ref[0], preferred_element_type=F32)
    for h in range(CROSS_HEADS):
        kh = _rms(kv[:, h * hd:(h + 1) * hd], kg_ref[0]) * (1.0 / math.sqrt(hd))
        k_ref[0, 0, :, h * hd:(h + 1) * hd] = _bf(kh)
    v_ref[0, 0] = _bf(kv[:, d:])


def _mem_kv(mem, mem_norm, wkv, k_gain):
    b, m, d = mem.shape
    nl = wkv.shape[0]
    out = jax.ShapeDtypeStruct((nl, b, m, d), BF16)
    return pl.pallas_call(
        _mem_kv_kernel,
        grid=(nl, b),
        in_specs=[pl.BlockSpec((1, m, d), lambda l, i: (i, 0, 0)),
                  pl.BlockSpec((1, d), lambda l, i: (0, 0)),
                  pl.BlockSpec((1, d, 2 * d), lambda l, i: (l, 0, 0)),
                  pl.BlockSpec((1, 1, d // CROSS_HEADS), lambda l, i: (l, 0, 0))],
        out_specs=[pl.BlockSpec((1, 1, m, d), lambda l, i: (l, i, 0, 0)),
                   pl.BlockSpec((1, 1, m, d), lambda l, i: (l, i, 0, 0))],
        out_shape=[out, out],
        compiler_params=_params(2),
        name="mem_kv",
    )(mem, mem_norm.reshape(1, d), _bf(wkv), k_gain.reshape(nl, 1, -1))


def _mix_rwkv_sgu(x, yf_ref, yb_ref, bv_ref, g_ref, b_ref, og_ref, bd_ref, wo_ref):
    y = yf_ref[0] + yb_ref[0]
    ms = _dot(y * y, bd_ref[...]) * (1.0 / HEAD_DIM)
    yn = y * lax.rsqrt(ms + NORM_EPS) * og_ref[...]
    a_out = (yn + bv_ref[0].astype(F32)) * g_ref[0].astype(F32)
    cat = jnp.concatenate([_bf(a_out), b_ref[0]], axis=1)
    return x + _dot(cat, wo_ref[...])


def _mix_proj(x, a_ref, w_ref):
    return x + _dot(a_ref[0], w_ref[...])


def _cross_kernel(*refs, n_mix, mix_fn):
    x_ref, mix_refs = refs[0], refs[1:1 + n_mix]
    g_ref, wq_ref, qg_ref, k_ref, v_ref, wo_ref, o_ref = refs[1 + n_mix:]
    x = mix_fn(x_ref[0], *mix_refs)
    d = x.shape[-1]
    hd = d // CROSS_HEADS
    hn = _bf(_rms(x, g_ref[...]))
    q = _dot(hn, wq_ref[...])
    outs = []
    for h in range(CROSS_HEADS):
        sl = slice(h * hd, (h + 1) * hd)
        qh = _rms(q[:, sl], qg_ref[...])
        logits = _dot_nt(qh, k_ref[0, 0, :, sl])
        p = jnp.exp(logits - jnp.max(logits, axis=-1, keepdims=True))
        s = jnp.sum(p, axis=-1, keepdims=True)
        outs.append(_dot(p, v_ref[0, 0, :, sl]) / s)
    o = jnp.concatenate(outs, axis=1)
    o_ref[0] = x + _dot(o, wo_ref[...])


def _cross_attention(x, mix_fn, mix_rows, mix_consts, gain, wq, q_gain, k_all, v_all, wo, layer, tq):
    b, t, d = x.shape
    m = k_all.shape[2]
    hd = d // CROSS_HEADS
    mix_specs = ([pl.BlockSpec((1, tq, a.shape[-1]), lambda i, j: (i, j, 0)) for a in mix_rows]
                 + [_const_spec(a.shape) for a in mix_consts])
    return pl.pallas_call(
        functools.partial(_cross_kernel, n_mix=len(mix_specs), mix_fn=mix_fn),
        grid=(b, t // tq),
        in_specs=[pl.BlockSpec((1, tq, d), lambda i, j: (i, j, 0)),
                  *mix_specs,
                  _const_spec((1, d)),
                  _const_spec((d, d)),
                  _const_spec((1, hd)),
                  pl.BlockSpec((1, 1, m, d), lambda i, j: (layer, i, 0, 0)),
                  pl.BlockSpec((1, 1, m, d), lambda i, j: (layer, i, 0, 0)),
                  _const_spec((d, d))],
        out_specs=pl.BlockSpec((1, tq, d), lambda i, j: (i, j, 0)),
        out_shape=jax.ShapeDtypeStruct((b, t, d), F32),
        compiler_params=_params(2),
        name=f"cross_attn_{layer}",
    )(x, *mix_rows, *mix_consts, gain.reshape(1, d), _bf(wq), q_gain.reshape(1, hd), k_all, v_all, _bf(wo))


def _ffn_kernel(x_ref, xp_ref, xn_ref, g_ref, wup_ref, cw_ref, cb_ref, wdn_ref, o_ref,
                hn_ref, acc_ref):
    i = pl.program_id(1)
    nt = pl.num_programs(1)
    n_chunks = wdn_ref.shape[0]
    x = x_ref[0]
    tm = x.shape[0]
    gain = g_ref[...]
    hn_ref[:tm] = _bf(_rms(x, gain))
    hn_ref[tm:] = _halo_rows(xp_ref, xn_ref, gain, i, nt)
    acc_ref[...] = jnp.zeros_like(acc_ref)

    def up_proj(idx):
        u = _dot_row_pieces(hn_ref[...], wup_ref[idx])
        return u[:tm], u[tm:]

    def conv(idx, u, uh):
        cw = cw_ref[idx]
        up = _shift_prev(u, uh[HALO - 1:HALO])
        un = _shift_next(u, uh[HALO:HALO + 1])
        return cb_ref[idx] + up * cw[0:1] + u * cw[1:2] + un * cw[2:3]

    ups = [up_proj(0), up_proj(n_chunks)]
    for c in range(n_chunks):
        nxt = [up_proj(c + 1), up_proj(c + 1 + n_chunks)] if c + 1 < n_chunks else None
        gate = conv(c, *ups[0])
        val = conv(c + n_chunks, *ups[1])
        acc_ref[...] += _dot_row_pieces(gate * _sigmoid(gate) * val, wdn_ref[c])
        ups = nxt
    o_ref[0] = x + acc_ref[...]


def _conv_ffn(x, gain, w_up, conv_w, conv_b, w_down, tm):
    b, t, d = x.shape
    ff = w_down.shape[0]
    nc = ff // FF_CHUNK
    wup = _bf(w_up).reshape(d, 2 * nc, FF_CHUNK).transpose(1, 0, 2)
    cw = conv_w.reshape(CONV_WIDTH, 2 * nc, FF_CHUNK).transpose(1, 0, 2)
    cb = conv_b.reshape(2 * nc, 1, FF_CHUNK)
    wdn = _bf(w_down).reshape(nc, FF_CHUNK, d)
    main, prev, nxt = _row_specs(tm, d, t)
    return pl.pallas_call(
        _ffn_kernel,
        grid=(b, t // tm),
        in_specs=[main, prev, nxt,
                  _const_spec((1, d)),
                  _const_spec((2 * nc, d, FF_CHUNK)),
                  _const_spec((2 * nc, CONV_WIDTH, FF_CHUNK)),
                  _const_spec((2 * nc, 1, FF_CHUNK)),
                  _const_spec((nc, FF_CHUNK, d))],
        out_specs=pl.BlockSpec((1, tm, d), lambda i, j: (i, j, 0)),
        out_shape=jax.ShapeDtypeStruct((b, t, d), F32),
        scratch_shapes=[pltpu.VMEM((tm + 2 * HALO, d), BF16),
                        pltpu.VMEM((tm, d), F32)],
        compiler_params=_params(2),
        name="conv_ffn",
    )(x, x, x, gain.reshape(1, d), wup, cw, cb, wdn)


def _ab_in_kernel(x_ref, xp_ref, xn_ref, g_ref, win_ref, mup_ref, mun_ref,
                  w0_ref, w2_ref, a0_ref, a2_ref, g2_ref, kk_ref, ka_ref, rk_ref, bd_ref,
                  vg_ref, ws_ref, sb_ref,
                  r_o, v_o, kk_o, lw0_o, lw1_o, be0_o, be1_o, k0_o, k1_o, g_o, bv_o, b_o):
    i = pl.program_id(1)
    nt = pl.num_programs(1)
    w = RWKV_WIDTH
    x = x_ref[0]
    tm = x.shape[0]
    gain = g_ref[...]
    hn = _bf(_rms(x, gain))
    hx = jnp.concatenate([hn, _halo_rows(xp_ref, xn_ref, gain, i, nt)], axis=0)
    zr = _dot(hx, win_ref[:, :RWKV_IN])
    za, zh = zr[:tm], zr[tm:]
    zp = _shift_prev(za, zh[HALO - 1:HALO])
    zn = _shift_next(za, zh[HALO:HALO + 1])
    mup, mun = mup_ref[...], mun_ref[...]
    zs = za * (1.0 - mup - mun) + mup * zp + mun * zn
    r = zs[:, 0:w]
    k = zs[:, w:2 * w]
    v = zs[:, 2 * w:3 * w]
    wl = jnp.tanh(zs[:, 3 * w:3 * w + LORA_W])
    al = zs[:, 3 * w + LORA_W:3 * w + 2 * LORA_W]
    gl = zs[:, 3 * w + 2 * LORA_W:]

    wpre = w0_ref[...] + _dot(wl, w2_ref[...])
    lw = -math.exp(-0.5) * _sigmoid(wpre)
    a = _sigmoid(a0_ref[...] + _dot(al, a2_ref[...]))
    g = _dot(_sigmoid(gl), g2_ref[...])

    bd = bd_ref[...]
    kk0 = k * kk_ref[...]
    kk = kk0 * lax.rsqrt(_dot(kk0 * kk0, bd) + KK_EPS)
    ka = ka_ref[...]
    k0 = k * (1.0 + (a[:, :w] - 1.0) * ka)
    k1 = k * (1.0 + (a[:, w:] - 1.0) * ka)
    bonus = _dot(r * rk_ref[...] * (k0 + k1), bd)

    r_o[0] = _bf(r)
    v_o[0] = _bf(v)
    kk_o[0] = _bf(kk)
    lw0_o[0] = lw[:, :w]
    lw1_o[0] = lw[:, w:]
    be0_o[0] = _bf(kk * a[:, :w])
    be1_o[0] = _bf(kk * a[:, w:])
    k0_o[0] = _bf(k0)
    k1_o[0] = _bf(k1)
    g_o[0] = _bf(g)
    bv_o[0] = _bf(bonus * v)

    zg = _dot(hn, win_ref[:, RWKV_IN:])
    zg = zg * (0.5 * (1.0 + jnp.tanh(math.sqrt(2.0 / math.pi) * (zg + 0.044715 * (zg * zg * zg)))))
    u = zg[:, :SGU_WIDTH]
    vn = _bf(_rms(zg[:, SGU_WIDTH:], vg_ref[...]))
    gd = SGU_WIDTH // SGU_GROUPS
    for n in range(tm // SGU_CHUNK):
        rows = slice(n * SGU_CHUNK, (n + 1) * SGU_CHUNK)
        for gi in range(SGU_GROUPS):
            cols = slice(gi * gd, (gi + 1) * gd)
            s = jnp.dot(ws_ref[gi], vn[rows, cols], preferred_element_type=F32) + sb_ref[gi]
            b_o[0, rows, cols] = _bf(u[rows, cols] * s)


def _ab_in(x, gain, w_in, mu_prev, mu_next, w0, w2, a0, a2, g2, k_k, k_a, r_k, v_gain, w_s, s_b, tm):
    b, t, d = x.shape
    w = RWKV_WIDTH
    n_in = w_in.shape[1]
    half = LORA_W // 2

    def lora_cat(m):
        z = jnp.zeros((half, w), m.dtype)
        return _bf(jnp.concatenate([jnp.concatenate([m[0], z], axis=1),
                                    jnp.concatenate([z, m[1]], axis=1)], axis=0))

    head = jnp.arange(w) // HEAD_DIM
    bd = _bf(head[:, None] == head[None, :])
    sb = jnp.broadcast_to(s_b[:, :, None], (SGU_GROUPS, SGU_CHUNK, SGU_WIDTH // SGU_GROUPS))
    main, prev, nxt = _row_specs(tm, d, t)
    ospec = pl.BlockSpec((1, tm, w), lambda i, j: (i, j, 0))
    return pl.pallas_call(
        _ab_in_kernel,
        grid=(b, t // tm),
        in_specs=[main, prev, nxt,
                  _const_spec((1, d)),
                  _const_spec((d, n_in)),
                  _const_spec((1, RWKV_IN)), _const_spec((1, RWKV_IN)),
                  _const_spec((1, 2 * w)), _const_spec((LORA_W, 2 * w)),
                  _const_spec((1, 2 * w)), _const_spec((LORA_W, 2 * w)),
                  _const_spec((LORA_W, w)),
                  _const_spec((1, w)), _const_spec((1, w)), _const_spec((1, w)),
                  _const_spec((w, w)),
                  _const_spec((1, SGU_WIDTH)),
                  _const_spec((SGU_GROUPS, SGU_CHUNK, SGU_CHUNK)),
                  _const_spec((SGU_GROUPS, SGU_CHUNK, SGU_WIDTH // SGU_GROUPS))],
        out_specs=[ospec] * 12,
        out_shape=[jax.ShapeDtypeStruct((b, t, w), dt) for dt in AB_IN_OUT_DTYPES],
        compiler_params=_params(2),
        name="ab_in",
    )(x, x, x, gain.reshape(1, d), _bf(w_in), mu_prev.reshape(1, -1), mu_next.reshape(1, -1),
      w0.reshape(1, 2 * w), lora_cat(w2), a0.reshape(1, 2 * w), lora_cat(a2), _bf(g2),
      k_k.reshape(1, w), k_a.reshape(1, w), r_k.reshape(1, w), bd,
      v_gain.reshape(1, -1), _bf(w_s), sb)


def _stack_heads(x):
    even = (_iota(x.shape, 1) & (LANES - 1)) < HEAD_DIM
    return jnp.concatenate([jnp.where(even, x, 0.0), jnp.where(even, 0.0, x)], axis=0)


def _rwkv_prep(r, v, kk, lw, be, kd, reverse):
    c = r.shape[0]
    rr = _iota((c, LANES), 0)
    cc = _iota((c, LANES), 1) & (HEAD_DIM - 1)
    if reverse:
        strict, incl = cc > rr, cc >= rr
    else:
        strict, incl = cc < rr, cc <= rr
    tri = _bf(incl[:, :c])

    l1 = _bf(lw)
    l2 = _bf(lw - l1.astype(F32))
    l3 = _bf(lw - l1.astype(F32) - l2.astype(F32))
    cum = (jnp.dot(tri, l1, preferred_element_type=F32) + jnp.dot(tri, l2, preferred_element_type=F32)
           + jnp.dot(tri, l3, preferred_element_type=F32))
    tot = cum[0:1] if reverse else cum[c - 1:c]
    e_neg = jnp.exp(-cum)
    e_end = jnp.exp(tot - cum)
    return dict(ag=-kk * jnp.exp(cum - lw), rg=r * jnp.exp(cum), bi=be * e_neg, ki=kd * e_neg,
                bee=be * e_end, kee=kd * e_end, gam=jnp.exp(tot), v=v, strict=strict, incl=incl)


def _rwkv_local(chunks):
    c = chunks[0]["v"].shape[0]
    n_tiles = chunks[0]["v"].shape[1] // LANES
    r128 = _iota((LANES, LANES), 0)
    c128 = _iota((LANES, LANES), 1)
    same_head = (r128 < HEAD_DIM) == (c128 < HEAD_DIM)
    diag = r128 == c128
    ctx = [(ch, slice(p * LANES, (p + 1) * LANES)) for ch in chunks for p in range(n_tiles)]

    aa = [_dot_nt(jnp.concatenate([ch["ag"][:, sl], ch["rg"][:, sl]], axis=0),
                  jnp.concatenate([_stack_heads(ch["bi"][:, sl]), _stack_heads(ch["ki"][:, sl])], axis=0))
          for ch, sl in ctx]
    a_ab = [jnp.where(ch["strict"], m[:c, :LANES], 0.0) for (ch, _), m in zip(ctx, aa)]
    a_rb = [jnp.where(ch["incl"], m[c:, :LANES], 0.0) for (ch, _), m in zip(ctx, aa)]
    a_k = [jnp.concatenate([jnp.where(ch["strict"], m[:c, LANES:], 0.0),
                            jnp.where(ch["incl"], m[c:, LANES:], 0.0)], axis=0) for (ch, _), m in zip(ctx, aa)]
    xv = [_dot(m, _stack_heads(ch["v"][:, sl])) for (ch, sl), m in zip(ctx, a_k)]
    x = [jnp.concatenate([ch["ag"][:, sl], m[:c]], axis=1) for (ch, sl), m in zip(ctx, xv)]
    def block_diag(pi):
        return jnp.where(same_head, jnp.concatenate([pi, pi], axis=0), 0.0)

    n_round = int(math.log2(c))
    rr = a_ab
    pw = [_dot(pi, block_diag(pi)) for pi in a_ab]
    for k in range(1, n_round):
        if k < n_round - 1:
            res = [_dot(jnp.concatenate([ri, pi], axis=0), block_diag(pi)) for ri, pi in zip(rr, pw)]
            rr = [ri + pi + m[:c] for ri, pi, m in zip(rr, pw, res)]
            pw = [m[c:] for m in res]
        else:
            rr = [ri + pi + _dot(ri, block_diag(pi)) for ri, pi in zip(rr, pw)]
    x = [xi + _dot(ri, _stack_heads(xi)) for xi, ri in zip(x, rr)]
    ry = [jnp.concatenate([ch["rg"][:, sl], m[c:]], axis=1) + _dot(ai, _stack_heads(xi))
          for (ch, sl), m, ai, xi in zip(ctx, xv, a_rb, x)]
    mg = [_dot(jnp.concatenate([ch["bee"][:, sl], ch["kee"][:, sl]], axis=0).T,
               jnp.concatenate([xi, jnp.concatenate([jnp.zeros_like(ch["v"][:, sl]), ch["v"][:, sl]], axis=1)],
                               axis=0))
          for (ch, sl), xi in zip(ctx, x)]
    out = []
    for (ch, sl), ryi, mgi in zip(ctx, ry, mg):
        gam_col = jnp.sum(jnp.where(diag, ch["gam"][:, sl], 0.0), axis=1, keepdims=True)
        out.append((ryi[:, :LANES], ryi[:, LANES:], jnp.where(same_head, mgi[:, :LANES], 0.0),
                    jnp.where(same_head, mgi[:, LANES:], 0.0), gam_col))
    return [out[i * n_tiles:(i + 1) * n_tiles] for i in range(len(chunks))]


def _rwkv_kernel(rf, vf, kkf, lwf, bef, kf, rb, vb, kkb, lwb, beb, kb, yf_o, yb_o, zf_ref, zb_ref):
    @pl.when(pl.program_id(1) == 0)
    def _():
        zf_ref[...] = jnp.zeros_like(zf_ref)
        zb_ref[...] = jnp.zeros_like(zb_ref)

    c = RWKV_CHUNK
    n = rf.shape[1] // c
    n_tiles = rf.shape[2] // LANES
    chunks = []
    for refs, reverse in (((rf, vf, kkf, lwf, bef, kf), False), ((rb, vb, kkb, lwb, beb, kb), True)):
        for ci in range(n):
            chunks.append(_rwkv_prep(*[ref[0, ci * c:(ci + 1) * c, :].astype(F32) for ref in refs], reverse))
    local = _rwkv_local(chunks)

    for d, (z_ref, y_o) in enumerate(((zf_ref, yf_o), (zb_ref, yb_o))):
        order = range(n) if d == 0 else range(n - 1, -1, -1)
        zs = [z_ref[p] for p in range(n_tiles)]
        for ci in order:
            ys = []
            for p in range(n_tiles):
                rt, yl, mt, gt, gam_col = local[d * n + ci][p]
                yz = _dot(jnp.concatenate([rt, mt], axis=0), zs[p])
                ys.append(yz[:c] + yl)
                zs[p] = gam_col * zs[p] + yz[c:] + gt
            y_o[0, ci * c:(ci + 1) * c, :] = jnp.concatenate(ys, axis=1)
        for p in range(n_tiles):
            z_ref[p] = zs[p]


def _rwkv_scan(r, v, kk, lw0, lw1, be0, be1, k0, k1):
    b, t, w = r.shape
    c = RWKV_CHUNK * RWKV_STEP_CHUNKS
    nc = t // c
    fwd = pl.BlockSpec((1, c, w), lambda i, j: (i, j, 0))
    bwd = pl.BlockSpec((1, c, w), lambda i, j: (i, nc - 1 - j, 0))
    out = jax.ShapeDtypeStruct((b, t, w), F32)
    zshape = pltpu.VMEM((w // LANES, LANES, LANES), F32)
    return pl.pallas_call(
        _rwkv_kernel,
        grid=(b, nc),
        in_specs=[fwd] * 6 + [bwd] * 6,
        out_specs=[fwd, bwd],
        out_shape=[out, out],
        scratch_shapes=[zshape, zshape],
        compiler_params=_params(2),
        name="rwkv_scan",
    )(r, v, kk, lw0, be0, k0, r, v, kk, lw1, be1, k1)


def _diff_in_kernel(x_ref, g_ref, w_ref, qg_ref, kg_ref, bd_ref, qt_o, k_o, vt_o):
    x = x_ref[0]
    d = x.shape[-1]
    hn = _bf(_rms(x, g_ref[...]))
    z = _dot(hn, w_ref[...])
    bd = bd_ref[...]
    half = bd.shape[0]

    def qk_norm(y, gain, scale):
        parts = []
        for j in range(d // half):
            yj = y[:, j * half:(j + 1) * half]
            ms = _dot(yj * yj, bd) * (1.0 / HEAD_DIM)
            parts.append(yj * lax.rsqrt(ms + NORM_EPS) * gain * scale)
        return jnp.concatenate(parts, axis=1)

    q = qk_norm(z[:, :d], qg_ref[...], LOG2E / math.sqrt(HEAD_DIM))
    k_o[0] = _bf(qk_norm(z[:, d:2 * d], kg_ref[...], 1.0))
    tm = x.shape[0]
    for h in range(d // DIFF_V_DIM):
        cols = slice(h * DIFF_V_DIM, (h + 1) * DIFF_V_DIM)
        qt_o[0, h] = _bf(q[:, cols].T)
        vt_o[0, h, :DIFF_V_DIM, :] = _bf(z[:, 2 * d:][:, cols].T)
        vt_o[0, h, DIFF_V_DIM:, :] = jnp.ones((ONES_ROWS, tm), BF16)


def _diff_in(x, gain, w_in, q_gain, k_gain, tm):
    b, t, d = x.shape
    half = 512
    nh = d // DIFF_V_DIM
    head = jnp.arange(half) // HEAD_DIM
    bd = _bf(head[:, None] == head[None, :])
    xs = pl.BlockSpec((1, tm, d), lambda i, j: (i, j, 0))
    out = jax.ShapeDtypeStruct((b, t, d), BF16)
    vt_rows = DIFF_V_DIM + ONES_ROWS
    return pl.pallas_call(
        _diff_in_kernel,
        grid=(b, t // tm),
        in_specs=[xs, _const_spec((1, d)), _const_spec((d, 3 * d)),
                  _const_spec((1, half)), _const_spec((1, half)), _const_spec((half, half))],
        out_specs=[pl.BlockSpec((1, nh, DIFF_V_DIM, tm), lambda i, j: (i, 0, 0, j)), xs,
                   pl.BlockSpec((1, nh, vt_rows, tm), lambda i, j: (i, 0, 0, j))],
        out_shape=[jax.ShapeDtypeStruct((b, nh, DIFF_V_DIM, t), BF16), out,
                   jax.ShapeDtypeStruct((b, nh, vt_rows, t), BF16)],
        compiler_params=_params(2),
        name="diff_in",
    )(x, gain.reshape(1, d), _bf(w_in), jnp.tile(q_gain, half // HEAD_DIM).reshape(1, half),
      jnp.tile(k_gain, half // HEAD_DIM).reshape(1, half), bd)


def _diff_attn_kernel(qt_ref, k_ref, vt_ref, hc_ref, lam_ref, og_ref, o_ref, bias_ref, p_ref, *s_scratch,
                      lambda_init, bounded):
    t, tq = bias_ref.shape

    @pl.when(pl.program_id(2) == 0)
    def _():
        dist = jnp.abs((_iota((t, tq), 0) - pl.program_id(1) * tq - _iota((t, tq), 1)).astype(F32))
        bias = dist * (-LOG2E * hc_ref[0, 0:1, 0:1])
        bias_ref[...] = bias - hc_ref[0, 1:2, 0:1] if bounded else bias

    q = qt_ref[0, 0]
    first = _iota(q.shape, 0) < HEAD_DIM
    zero = jnp.zeros_like(q)
    kb = min(KEY_BLOCK, t)
    qb = min(QUERY_BLOCK, tq)
    sub = HALO

    qcs = (jnp.where(first, q, zero), jnp.where(first, zero, q))
    secs = [(c, slice(i * qb, (i + 1) * qb)) for c in range(2) for i in range(tq // qb)]
    n_kb = t // kb
    outs = []

    def logits(sec, j):
        c, qs = sec
        ks = slice(j * kb, (j + 1) * kb)
        return jnp.dot(k_ref[0, ks, :], qcs[c][:, qs], preferred_element_type=F32) + bias_ref[ks, qs]

    if bounded:
        for sec in secs:
            c, qs = sec
            for j in range(n_kb):
                p_ref[c, j * kb:(j + 1) * kb, qs] = _bf(jnp.exp2(logits(sec, j)))
            outs.append(jnp.dot(vt_ref[0, 0], p_ref[c, :, qs], preferred_element_type=F32))
    else:
        s_ref, = s_scratch

        def score_block(sec, j, m):
            c, qs = sec
            s = logits(sec, j)
            s_ref[c, j * kb:(j + 1) * kb, qs] = s
            return jnp.maximum(m, jnp.max(s.reshape(kb // sub, sub, qb), axis=0))

        def prob_block(sec, j, m):
            c, qs = sec
            ks = slice(j * kb, (j + 1) * kb)
            p_ref[c, ks, qs] = _bf(jnp.exp2(s_ref[c, ks, qs] - m))

        m_init = jnp.full((sub, qb), -jnp.inf, F32)
        m = m_init
        for j in range(n_kb):
            m = score_block(secs[0], j, m)
        m_prev = jnp.max(m, axis=0, keepdims=True)
        for n in range(1, len(secs) + 1):
            m = m_init
            for j in range(n_kb):
                if n < len(secs):
                    m = score_block(secs[n], j, m)
                prob_block(secs[n - 1], j, m_prev)
            m_prev = jnp.max(m, axis=0, keepdims=True)
            c, qs = secs[n - 1]
            outs.append(jnp.dot(vt_ref[0, 0], p_ref[c, :, qs], preferred_element_type=F32))
    per_c = len(secs) // 2
    acc0 = jnp.concatenate(outs[:per_c], axis=1)
    acc1 = jnp.concatenate(outs[per_c:], axis=1)
    o0, s0 = acc0[:DIFF_V_DIM], acc0[DIFF_V_DIM:DIFF_V_DIM + 1]
    o1, s1 = acc1[:DIFF_V_DIM], acc1[DIFF_V_DIM:DIFF_V_DIM + 1]
    lv = lam_ref[...]
    lam = (jnp.exp(jnp.sum(lv[0:1] * lv[1:2], axis=-1, keepdims=True))
           - jnp.exp(jnp.sum(lv[2:3] * lv[3:4], axis=-1, keepdims=True)) + lambda_init)
    o = (o0 * (1.0 / s0) - o1 * (lam / s1)).T
    o_ref[0] = _bf(_rms(o, og_ref[...]) * (1.0 - lambda_init))


def _diff_attention(qt, k, vt, q_gain, k_gain, lam_vecs, out_gain, lambda_init, tq):
    b, t, d = k.shape
    nh = d // DIFF_V_DIM
    slopes = 2.0 ** (-8.0 * jnp.arange(1, nh + 1, dtype=F32) / nh)
    bound = 1.02 * math.sqrt(HEAD_DIM) * LOG2E * jnp.max(jnp.abs(q_gain)) * jnp.max(jnp.abs(k_gain))
    head_consts = jnp.stack([jnp.broadcast_to(slopes[:, None], (nh, LANES)),
                             jnp.broadcast_to(bound, (nh, LANES))], axis=1)

    def run(bounded):
        scratch = [pltpu.VMEM((t, tq), F32), pltpu.VMEM((2, t, tq), BF16)]
        if not bounded:
            scratch.append(pltpu.VMEM((2, t, tq), F32))
        return pl.pallas_call(
            functools.partial(_diff_attn_kernel, lambda_init=lambda_init, bounded=bounded),
            grid=(nh, t // tq, b),
            in_specs=[pl.BlockSpec((1, 1, DIFF_V_DIM, tq), lambda h, j, i: (i, h, 0, j)),
                      pl.BlockSpec((1, t, DIFF_V_DIM), lambda h, j, i: (i, 0, h)),
                      pl.BlockSpec((1, 1, vt.shape[2], t), lambda h, j, i: (i, h, 0, 0)),
                      pl.BlockSpec((1, 2, LANES), lambda h, j, i: (h, 0, 0)),
                      _const_spec((4, HEAD_DIM)),
                      _const_spec((1, DIFF_V_DIM))],
            out_specs=pl.BlockSpec((1, tq, DIFF_V_DIM), lambda h, j, i: (i, j, h)),
            out_shape=jax.ShapeDtypeStruct((b, t, d), BF16),
            scratch_shapes=scratch,
            compiler_params=_params(3),
            name="diff_attn_bounded" if bounded else "diff_attn",
        )(qt, k, vt, head_consts, lam_vecs, out_gain.reshape(1, DIFF_V_DIM))

    return lax.cond(2.0 * bound <= SAFE_EXP2_SPAN, lambda: run(True), lambda: run(False))


def kernel(x, mem, mem_norm, mix_norm, cross_norm, ffn_norm, ab_w_in, ab_shift_prev, ab_shift_next, rwkv_w0, rwkv_w2, rwkv_a0, rwkv_a2, rwkv_g2, rwkv_k_k, rwkv_k_a, rwkv_r_k, rwkv_out_gain, sgu_v_gain, sgu_w_s, sgu_b, ab_w_out, diff_w_in, diff_q_gain, diff_k_gain, diff_lambda_q1, diff_lambda_k1, diff_lambda_q2, diff_lambda_k2, diff_out_gain, diff_w_out, cross_wq, cross_wkv, cross_q_gain, cross_k_gain, cross_wo, ffn_w_up, ffn_conv_w, ffn_conv_b, ffn_w_down):
    b, t, d = x.shape
    depth = mix_norm.shape[0]
    tm = min(512, t)
    tq = min(512, t)
    assert t % tm == 0 and t % (RWKV_CHUNK * RWKV_STEP_CHUNKS) == 0 and tm % SGU_CHUNK == 0

    k_all, v_all = _mem_kv(mem, mem_norm, cross_wkv, cross_k_gain)
    for l in range(depth):
        if l % 2 == 0:
            e = l // 2
            (r, v, kk, lw0, lw1, be0, be1, k0, k1, g, bv, b_out) = _ab_in(
                x, mix_norm[l], ab_w_in[e], ab_shift_prev[e], ab_shift_next[e],
                rwkv_w0[e], rwkv_w2[e], rwkv_a0[e], rwkv_a2[e], rwkv_g2[e],
                rwkv_k_k[e], rwkv_k_a[e], rwkv_r_k[e], sgu_v_gain[e], sgu_w_s[e], sgu_b[e], tm)
            yf, yb = _rwkv_scan(r, v, kk, lw0, lw1, be0, be1, k0, k1)
            head = jnp.arange(RWKV_WIDTH) // HEAD_DIM
            mix = (_mix_rwkv_sgu, (yf, yb, bv, g, b_out),
                   (rwkv_out_gain[e].reshape(1, RWKV_WIDTH), _bf(head[:, None] == head[None, :]), _bf(ab_w_out[e])))
        else:
            o = l // 2
            lambda_init = 0.8 - 0.6 * math.exp(-0.3 * l)
            q, k, vt = _diff_in(x, mix_norm[l], diff_w_in[o], diff_q_gain[o], diff_k_gain[o], tm)
            lam_vecs = jnp.stack([diff_lambda_q1[o], diff_lambda_k1[o], diff_lambda_q2[o], diff_lambda_k2[o]])
            c_out = _diff_attention(q, k, vt, diff_q_gain[o], diff_k_gain[o], lam_vecs, diff_out_gain[o],
                                    lambda_init, tq)
            mix = (_mix_proj, (c_out,), (_bf(diff_w_out[o]),))
        x = _cross_attention(x, *mix, cross_norm[l], cross_wq[l], cross_q_gain[l], k_all, v_all, cross_wo[l], l, tm)
        x = _conv_ffn(x, ffn_norm[l], ffn_w_up[l], ffn_conv_w[l], ffn_conv_b[l], ffn_w_down[l], tm)
    return x
```

```python
import functools
import math

import jax
import jax.numpy as jnp
from jax import lax
from jax.experimental import pallas as pl
from jax.experimental.pallas import tpu as pltpu

NORM_EPS = 1e-6
KK_EPS = 1e-12
RWKV_HEADS = 8
HEAD_DIM = 64
RWKV_WIDTH = RWKV_HEADS * HEAD_DIM
LORA_W = 128
RWKV_IN = 3 * RWKV_WIDTH + 3 * LORA_W
SGU_WIDTH = 512
SGU_GROUPS = 4
SGU_CHUNK = 128
DIFF_HEADS = 8
DIFF_V_DIM = 2 * HEAD_DIM
CROSS_HEADS = 4
CONV_WIDTH = 3
RWKV_CHUNK = 64
RWKV_STEP_CHUNKS = 4
LANES = 128
HALO = 8
FF_CHUNK = 256
MXU_ROWS = 256
KEY_BLOCK = 128
QUERY_BLOCK = 256
ONES_ROWS = 16
SAFE_EXP2_SPAN = 100.0
VMEM_LIMIT = 56 * 1024 * 1024
LOG2E = math.log2(math.e)

F32 = jnp.float32
BF16 = jnp.bfloat16
AB_IN_OUT_DTYPES = (BF16, BF16, BF16, F32, F32, BF16, BF16, BF16, BF16, BF16, BF16, BF16)


def _bf(x):
    return x.astype(BF16)


def _dot(a, b):
    return jnp.dot(_bf(a), _bf(b), preferred_element_type=F32)


def _dot_nt(a, b):
    return lax.dot_general(_bf(a), _bf(b), (((1,), (1,)), ((), ())), preferred_element_type=F32)


def _dot_row_pieces(a, b):
    m = a.shape[0]
    starts = list(range(0, m - m % MXU_ROWS, MXU_ROWS)) or [0]
    ends = starts[1:] + [m]
    return jnp.concatenate([_dot(a[s:e], b) for s, e in zip(starts, ends)], axis=0)


def _rms(x, gain):
    return x * lax.rsqrt(jnp.mean(x * x, axis=-1, keepdims=True) + NORM_EPS) * gain


def _sigmoid(x):
    return 1.0 / (1.0 + jnp.exp(-x))


def _iota(shape, dim):
    return lax.broadcasted_iota(jnp.int32, shape, dim)


def _shift_prev(u, halo_row):
    rolled = pltpu.roll(u, 1, 0)
    head = rolled[:HALO]
    head = jnp.where(_iota(head.shape, 0) == 0, halo_row, head)
    return jnp.concatenate([head, rolled[HALO:]], axis=0)


def _shift_next(u, halo_row):
    n = u.shape[0]
    rolled = pltpu.roll(u, n - 1, 0)
    tail = rolled[n - HALO:]
    tail = jnp.where(_iota(tail.shape, 0) == HALO - 1, halo_row, tail)
    return jnp.concatenate([rolled[:n - HALO], tail], axis=0)


def _halo_rows(xp_ref, xn_ref, gain, i, nt):
    xh = jnp.concatenate([xp_ref[0], xn_ref[0]], axis=0)
    hh = _rms(xh, gain)
    row = _iota(hh.shape, 0)
    has_prev = jnp.where(i > 0, 1.0, 0.0)
    has_next = jnp.where(i < nt - 1, 1.0, 0.0)
    return _bf(hh * jnp.where(row < HALO, has_prev, has_next))


def _const_spec(shape):
    zeros = (0,) * len(shape)
    return pl.BlockSpec(shape, lambda *_: zeros, pipeline_mode=pl.Buffered(1))


def _params(n_axes):
    return pltpu.CompilerParams(dimension_semantics=("arbitrary",) * n_axes,
                                vmem_limit_bytes=VMEM_LIMIT)


def _row_specs(tm, d, t):
    nb = tm // HALO
    last = t // HALO - 1
    main = pl.BlockSpec((1, tm, d), lambda b, i: (b, i, 0))
    prev = pl.BlockSpec((1, HALO, d), lambda b, i: (b, jnp.maximum(i * nb - 1, 0), 0))
    nxt = pl.BlockSpec((1, HALO, d), lambda b, i: (b, jnp.minimum((i + 1) * nb, last), 0))
    return main, prev, nxt


def _mem_kv_kernel(mem_ref, mnorm_ref, wkv_ref, kg_ref, k_ref, v_ref):
    d = mem_ref.shape[-1]
    hd = d // CROSS_HEADS
    mn = _rms(mem_ref[0], mnorm_ref[...])
    kv = jnp.dot(_bf(mn), wkv_ref[0], preferred_element_type=F32)
    for h in range(CROSS_HEADS):
        kh = _rms(kv[:, h * hd:(h + 1) * hd], kg_ref[0]) * (1.0 / math.sqrt(hd))
        k_ref[0, 0, :, h * hd:(h + 1) * hd] = _bf(kh)
    v_ref[0, 0] = _bf(kv[:, d:])


def _mem_kv(mem, mem_norm, wkv, k_gain):
    b, m, d = mem.shape
    nl = wkv.shape[0]
    out = jax.ShapeDtypeStruct((nl, b, m, d), BF16)
    return pl.pallas_call(
        _mem_kv_kernel,
        grid=(nl, b),
        in_specs=[pl.BlockSpec((1, m, d), lambda l, i: (i, 0, 0)),
                  pl.BlockSpec((1, d), lambda l, i: (0, 0)),
                  pl.BlockSpec((1, d, 2 * d), lambda l, i: (l, 0, 0)),
                  pl.BlockSpec((1, 1, d // CROSS_HEADS), lambda l, i: (l, 0, 0))],
        out_specs=[pl.BlockSpec((1, 1, m, d), lambda l, i: (l, i, 0, 0)),
                   pl.BlockSpec((1, 1, m, d), lambda l, i: (l, i, 0, 0))],
        out_shape=[out, out],
        compiler_params=_params(2),
        name="mem_kv",
    )(mem, mem_norm.reshape(1, d), _bf(wkv), k_gain.reshape(nl, 1, -1))


def _mix_rwkv_sgu(x, yf_ref, yb_ref, bv_ref, g_ref, b_ref, og_ref, bd_ref, wo_ref):
    y = yf_ref[0] + yb_ref[0]
    ms = _dot(y * y, bd_ref[...]) * (1.0 / HEAD_DIM)
    yn = y * lax.rsqrt(ms + NORM_EPS) * og_ref[...]
    a_out = (yn + bv_ref[0].astype(F32)) * g_ref[0].astype(F32)
    cat = jnp.concatenate([_bf(a_out), b_ref[0]], axis=1)
    return x + _dot_row_pieces(cat, wo_ref[...])


def _mix_proj(x, a_ref, w_ref):
    return x + _dot_row_pieces(a_ref[0], w_ref[...])


def _cross_kernel(*refs, n_mix, mix_fn):
    x_ref, mix_refs = refs[0], refs[1:1 + n_mix]
    g_ref, wq_ref, qg_ref, k_ref, v_ref, wo_ref, o_ref = refs[1 + n_mix:]
    x = mix_fn(x_ref[0], *mix_refs)
    d = x.shape[-1]
    hd = d // CROSS_HEADS
    hn = _bf(_rms(x, g_ref[...]))
    q = _dot_row_pieces(hn, wq_ref[...])
    outs = []
    for h in range(CROSS_HEADS):
        sl = slice(h * hd, (h + 1) * hd)
        qh = _rms(q[:, sl], qg_ref[...])
        logits = _dot_nt(qh, k_ref[0, 0, :, sl])
        p = jnp.exp(logits - jnp.max(logits, axis=-1, keepdims=True))
        s = jnp.sum(p, axis=-1, keepdims=True)
        outs.append(_dot(p, v_ref[0, 0, :, sl]) / s)
    o = jnp.concatenate(outs, axis=1)
    o_ref[0] = x + _dot_row_pieces(o, wo_ref[...])


def _cross_attention(x, mix_fn, mix_rows, mix_consts, gain, wq, q_gain, k_all, v_all, wo, layer, tq):
    b, t, d = x.shape
    m = k_all.shape[2]
    hd = d // CROSS_HEADS
    mix_specs = ([pl.BlockSpec((1, tq, a.shape[-1]), lambda i, j: (i, j, 0)) for a in mix_rows]
                 + [_const_spec(a.shape) for a in mix_consts])
    return pl.pallas_call(
        functools.partial(_cross_kernel, n_mix=len(mix_specs), mix_fn=mix_fn),
        grid=(b, t // tq),
        in_specs=[pl.BlockSpec((1, tq, d), lambda i, j: (i, j, 0)),
                  *mix_specs,
                  _const_spec((1, d)),
                  _const_spec((d, d)),
                  _const_spec((1, hd)),
                  pl.BlockSpec((1, 1, m, d), lambda i, j: (layer, i, 0, 0)),
                  pl.BlockSpec((1, 1, m, d), lambda i, j: (layer, i, 0, 0)),
                  _const_spec((d, d))],
        out_specs=pl.BlockSpec((1, tq, d), lambda i, j: (i, j, 0)),
        out_shape=jax.ShapeDtypeStruct((b, t, d), F32),
        compiler_params=_params(2),
        name=f"cross_attn_{layer}",
    )(x, *mix_rows, *mix_consts, gain.reshape(1, d), _bf(wq), q_gain.reshape(1, hd), k_all, v_all, _bf(wo))


def _ffn_kernel(x_ref, xp_ref, xn_ref, g_ref, wup_ref, cw_ref, cb_ref, wdn_ref, o_ref,
                hn_ref, acc_ref):
    i = pl.program_id(1)
    nt = pl.num_programs(1)
    n_chunks = wdn_ref.shape[0]
    x = x_ref[0]
    tm = x.shape[0]
    gain = g_ref[...]
    hn_ref[:tm] = _bf(_rms(x, gain))
    hn_ref[tm:] = _halo_rows(xp_ref, xn_ref, gain, i, nt)
    acc_ref[...] = jnp.zeros_like(acc_ref)

    def up_proj(idx):
        u = _dot_row_pieces(hn_ref[...], wup_ref[idx])
        return u[:tm], u[tm:]

    def conv(idx, u, uh):
        cw = cw_ref[idx]
        up = _shift_prev(u, uh[HALO - 1:HALO])
        un = _shift_next(u, uh[HALO:HALO + 1])
        return cb_ref[idx] + up * cw[0:1] + u * cw[1:2] + un * cw[2:3]

    ups = [up_proj(0), up_proj(n_chunks)]
    for c in range(n_chunks):
        nxt = [up_proj(c + 1), up_proj(c + 1 + n_chunks)] if c + 1 < n_chunks else None
        gate = conv(c, *ups[0])
        val = conv(c + n_chunks, *ups[1])
        acc_ref[...] += _dot_row_pieces(gate * _sigmoid(gate) * val, wdn_ref[c])
        ups = nxt
    o_ref[0] = x + acc_ref[...]


def _conv_ffn(x, gain, w_up, conv_w, conv_b, w_down, tm):
    b, t, d = x.shape
    ff = w_down.shape[0]
    nc = ff // FF_CHUNK
    wup = _bf(w_up).reshape(d, 2 * nc, FF_CHUNK).transpose(1, 0, 2)
    cw = conv_w.reshape(CONV_WIDTH, 2 * nc, FF_CHUNK).transpose(1, 0, 2)
    cb = conv_b.reshape(2 * nc, 1, FF_CHUNK)
    wdn = _bf(w_down).reshape(nc, FF_CHUNK, d)
    main, prev, nxt = _row_specs(tm, d, t)
    return pl.pallas_call(
        _ffn_kernel,
        grid=(b, t // tm),
        in_specs=[main, prev, nxt,
                  _const_spec((1, d)),
                  _const_spec((2 * nc, d, FF_CHUNK)),
                  _const_spec((2 * nc, CONV_WIDTH, FF_CHUNK)),
                  _const_spec((2 * nc, 1, FF_CHUNK)),
                  _const_spec((nc, FF_CHUNK, d))],
        out_specs=pl.BlockSpec((1, tm, d), lambda i, j: (i, j, 0)),
        out_shape=jax.ShapeDtypeStruct((b, t, d), F32),
        scratch_shapes=[pltpu.VMEM((tm + 2 * HALO, d), BF16),
                        pltpu.VMEM((tm, d), F32)],
        compiler_params=_params(2),
        name="conv_ffn",
    )(x, x, x, gain.reshape(1, d), wup, cw, cb, wdn)


def _ab_in_kernel(x_ref, xp_ref, xn_ref, g_ref, win_ref, mup_ref, mun_ref,
                  w0_ref, w2_ref, a0_ref, a2_ref, g2_ref, kk_ref, ka_ref, rk_ref, bd_ref,
                  vg_ref, ws_ref, sb_ref,
                  r_o, v_o, kk_o, lw0_o, lw1_o, be0_o, be1_o, k0_o, k1_o, g_o, bv_o, b_o):
    i = pl.program_id(1)
    nt = pl.num_programs(1)
    w = RWKV_WIDTH
    x = x_ref[0]
    tm = x.shape[0]
    gain = g_ref[...]
    hn = _bf(_rms(x, gain))
    hx = jnp.concatenate([hn, _halo_rows(xp_ref, xn_ref, gain, i, nt)], axis=0)
    zr = _dot_row_pieces(hx, win_ref[:, :RWKV_IN])
    za, zh = zr[:tm], zr[tm:]
    zp = _shift_prev(za, zh[HALO - 1:HALO])
    zn = _shift_next(za, zh[HALO:HALO + 1])
    mup, mun = mup_ref[...], mun_ref[...]
    zs = za * (1.0 - mup - mun) + mup * zp + mun * zn
    r = zs[:, 0:w]
    k = zs[:, w:2 * w]
    v = zs[:, 2 * w:3 * w]
    wl = jnp.tanh(zs[:, 3 * w:3 * w + LORA_W])
    al = zs[:, 3 * w + LORA_W:3 * w + 2 * LORA_W]
    gl = zs[:, 3 * w + 2 * LORA_W:]

    wpre = w0_ref[...] + _dot(wl, w2_ref[...])
    lw = -math.exp(-0.5) * _sigmoid(wpre)
    a = _sigmoid(a0_ref[...] + _dot(al, a2_ref[...]))
    g = _dot(_sigmoid(gl), g2_ref[...])

    bd = bd_ref[...]
    kk0 = k * kk_ref[...]
    kk = kk0 * lax.rsqrt(_dot(kk0 * kk0, bd) + KK_EPS)
    ka = ka_ref[...]
    k0 = k * (1.0 + (a[:, :w] - 1.0) * ka)
    k1 = k * (1.0 + (a[:, w:] - 1.0) * ka)
    bonus = _dot(r * rk_ref[...] * (k0 + k1), bd)

    r_o[0] = _bf(r)
    v_o[0] = _bf(v)
    kk_o[0] = _bf(kk)
    lw0_o[0] = lw[:, :w]
    lw1_o[0] = lw[:, w:]
    be0_o[0] = _bf(kk * a[:, :w])
    be1_o[0] = _bf(kk * a[:, w:])
    k0_o[0] = _bf(k0)
    k1_o[0] = _bf(k1)
    g_o[0] = _bf(g)
    bv_o[0] = _bf(bonus * v)

    zg = _dot_row_pieces(hn, win_ref[:, RWKV_IN:])
    zg = zg * (0.5 * (1.0 + jnp.tanh(math.sqrt(2.0 / math.pi) * (zg + 0.044715 * (zg * zg * zg)))))
    u = zg[:, :SGU_WIDTH]
    vn = _bf(_rms(zg[:, SGU_WIDTH:], vg_ref[...]))
    gd = SGU_WIDTH // SGU_GROUPS
    for n in range(tm // SGU_CHUNK):
        rows = slice(n * SGU_CHUNK, (n + 1) * SGU_CHUNK)
        for gi in range(SGU_GROUPS):
            cols = slice(gi * gd, (gi + 1) * gd)
            s = jnp.dot(ws_ref[gi], vn[rows, cols], preferred_element_type=F32) + sb_ref[gi]
            b_o[0, rows, cols] = _bf(u[rows, cols] * s)


def _ab_in(x, gain, w_in, mu_prev, mu_next, w0, w2, a0, a2, g2, k_k, k_a, r_k, v_gain, w_s, s_b, tm):
    b, t, d = x.shape
    w = RWKV_WIDTH
    n_in = w_in.shape[1]
    half = LORA_W // 2

    def lora_cat(m):
        z = jnp.zeros((half, w), m.dtype)
        return _bf(jnp.concatenate([jnp.concatenate([m[0], z], axis=1),
                                    jnp.concatenate([z, m[1]], axis=1)], axis=0))

    head = jnp.arange(w) // HEAD_DIM
    bd = _bf(head[:, None] == head[None, :])
    sb = jnp.broadcast_to(s_b[:, :, None], (SGU_GROUPS, SGU_CHUNK, SGU_WIDTH // SGU_GROUPS))
    main, prev, nxt = _row_specs(tm, d, t)
    ospec = pl.BlockSpec((1, tm, w), lambda i, j: (i, j, 0))
    return pl.pallas_call(
        _ab_in_kernel,
        grid=(b, t // tm),
        in_specs=[main, prev, nxt,
                  _const_spec((1, d)),
                  _const_spec((d, n_in)),
                  _const_spec((1, RWKV_IN)), _const_spec((1, RWKV_IN)),
                  _const_spec((1, 2 * w)), _const_spec((LORA_W, 2 * w)),
                  _const_spec((1, 2 * w)), _const_spec((LORA_W, 2 * w)),
                  _const_spec((LORA_W, w)),
                  _const_spec((1, w)), _const_spec((1, w)), _const_spec((1, w)),
                  _const_spec((w, w)),
                  _const_spec((1, SGU_WIDTH)),
                  _const_spec((SGU_GROUPS, SGU_CHUNK, SGU_CHUNK)),
                  _const_spec((SGU_GROUPS, SGU_CHUNK, SGU_WIDTH // SGU_GROUPS))],
        out_specs=[ospec] * 12,
        out_shape=[jax.ShapeDtypeStruct((b, t, w), dt) for dt in AB_IN_OUT_DTYPES],
        compiler_params=_params(2),
        name="ab_in",
    )(x, x, x, gain.reshape(1, d), _bf(w_in), mu_prev.reshape(1, -1), mu_next.reshape(1, -1),
      w0.reshape(1, 2 * w), lora_cat(w2), a0.reshape(1, 2 * w), lora_cat(a2), _bf(g2),
      k_k.reshape(1, w), k_a.reshape(1, w), r_k.reshape(1, w), bd,
      v_gain.reshape(1, -1), _bf(w_s), sb)


def _stack_heads(x):
    even = (_iota(x.shape, 1) & (LANES - 1)) < HEAD_DIM
    return jnp.concatenate([jnp.where(even, x, 0.0), jnp.where(even, 0.0, x)], axis=0)


def _rwkv_prep(r, v, kk, lw, be, kd, reverse):
    c = r.shape[0]
    rr = _iota((c, LANES), 0)
    cc = _iota((c, LANES), 1) & (HEAD_DIM - 1)
    if reverse:
        strict, incl = cc > rr, cc >= rr
    else:
        strict, incl = cc < rr, cc <= rr
    tri = _bf(incl[:, :c])

    l1 = _bf(lw)
    l2 = _bf(lw - l1.astype(F32))
    l3 = _bf(lw - l1.astype(F32) - l2.astype(F32))
    cum = (jnp.dot(tri, l1, preferred_element_type=F32) + jnp.dot(tri, l2, preferred_element_type=F32)
           + jnp.dot(tri, l3, preferred_element_type=F32))
    tot = cum[0:1] if reverse else cum[c - 1:c]
    e_neg = jnp.exp(-cum)
    e_end = jnp.exp(tot - cum)
    return dict(ag=-kk * jnp.exp(cum - lw), rg=r * jnp.exp(cum), bi=be * e_neg, ki=kd * e_neg,
                bee=be * e_end, kee=kd * e_end, gam=jnp.exp(tot), v=v, strict=strict, incl=incl)


def _rwkv_local(chunks):
    c = chunks[0]["v"].shape[0]
    n_tiles = chunks[0]["v"].shape[1] // LANES
    r128 = _iota((LANES, LANES), 0)
    c128 = _iota((LANES, LANES), 1)
    same_head = (r128 < HEAD_DIM) == (c128 < HEAD_DIM)
    diag = r128 == c128
    ctx = [(ch, slice(p * LANES, (p + 1) * LANES)) for ch in chunks for p in range(n_tiles)]

    aa = [_dot_nt(jnp.concatenate([ch["ag"][:, sl], ch["rg"][:, sl]], axis=0),
                  jnp.concatenate([_stack_heads(ch["bi"][:, sl]), _stack_heads(ch["ki"][:, sl])], axis=0))
          for ch, sl in ctx]
    a_ab = [jnp.where(ch["strict"], m[:c, :LANES], 0.0) for (ch, _), m in zip(ctx, aa)]
    a_rb = [jnp.where(ch["incl"], m[c:, :LANES], 0.0) for (ch, _), m in zip(ctx, aa)]
    a_k = [jnp.concatenate([jnp.where(ch["strict"], m[:c, LANES:], 0.0),
                            jnp.where(ch["incl"], m[c:, LANES:], 0.0)], axis=0) for (ch, _), m in zip(ctx, aa)]
    xv = [_dot(m, _stack_heads(ch["v"][:, sl])) for (ch, sl), m in zip(ctx, a_k)]
    x = [jnp.concatenate([ch["ag"][:, sl], m[:c]], axis=1) for (ch, sl), m in zip(ctx, xv)]
    def block_diag(pi):
        return jnp.where(same_head, jnp.concatenate([pi, pi], axis=0), 0.0)

    n_round = int(math.log2(c))
    rr = a_ab
    pw = [_dot(pi, block_diag(pi)) for pi in a_ab]
    for k in range(1, n_round):
        if k < n_round - 1:
            res = [_dot(jnp.concatenate([ri, pi], axis=0), block_diag(pi)) for ri, pi in zip(rr, pw)]
            rr = [ri + pi + m[:c] for ri, pi, m in zip(rr, pw, res)]
            pw = [m[c:] for m in res]
        else:
            rr = [ri + pi + _dot(ri, block_diag(pi)) for ri, pi in zip(rr, pw)]
    x = [xi + _dot(ri, _stack_heads(xi)) for xi, ri in zip(x, rr)]
    ry = [jnp.concatenate([ch["rg"][:, sl], m[c:]], axis=1) + _dot(ai, _stack_heads(xi))
          for (ch, sl), m, ai, xi in zip(ctx, xv, a_rb, x)]
    mg = [_dot(jnp.concatenate([ch["bee"][:, sl], ch["kee"][:, sl]], axis=0).T,
               jnp.concatenate([xi, jnp.concatenate([jnp.zeros_like(ch["v"][:, sl]), ch["v"][:, sl]], axis=1)],
                               axis=0))
          for (ch, sl), xi in zip(ctx, x)]
    out = []
    for (ch, sl), ryi, mgi in zip(ctx, ry, mg):
        gam_col = jnp.sum(jnp.where(diag, ch["gam"][:, sl], 0.0), axis=1, keepdims=True)
        out.append((ryi[:, :LANES], ryi[:, LANES:], jnp.where(same_head, mgi[:, :LANES], 0.0),
                    jnp.where(same_head, mgi[:, LANES:], 0.0), gam_col))
    return [out[i * n_tiles:(i + 1) * n_tiles] for i in range(len(chunks))]


def _rwkv_kernel(rf, vf, kkf, lwf, bef, kf, rb, vb, kkb, lwb, beb, kb, yf_o, yb_o, zf_ref, zb_ref):
    @pl.when(pl.program_id(1) == 0)
    def _():
        zf_ref[...] = jnp.zeros_like(zf_ref)
        zb_ref[...] = jnp.zeros_like(zb_ref)

    c = RWKV_CHUNK
    n = rf.shape[1] // c
    n_tiles = rf.shape[2] // LANES
    chunks = []
    for refs, reverse in (((rf, vf, kkf, lwf, bef, kf), False), ((rb, vb, kkb, lwb, beb, kb), True)):
        for ci in range(n):
            chunks.append(_rwkv_prep(*[ref[0, ci * c:(ci + 1) * c, :].astype(F32) for ref in refs], reverse))
    local = _rwkv_local(chunks)

    for d, (z_ref, y_o) in enumerate(((zf_ref, yf_o), (zb_ref, yb_o))):
        order = range(n) if d == 0 else range(n - 1, -1, -1)
        zs = [z_ref[p] for p in range(n_tiles)]
        for ci in order:
            ys = []
            for p in range(n_tiles):
                rt, yl, mt, gt, gam_col = local[d * n + ci][p]
                yz = _dot(jnp.concatenate([rt, mt], axis=0), zs[p])
                ys.append(yz[:c] + yl)
                zs[p] = gam_col * zs[p] + yz[c:] + gt
            y_o[0, ci * c:(ci + 1) * c, :] = jnp.concatenate(ys, axis=1)
        for p in range(n_tiles):
            z_ref[p] = zs[p]


def _rwkv_scan(r, v, kk, lw0, lw1, be0, be1, k0, k1):
    b, t, w = r.shape
    c = RWKV_CHUNK * RWKV_STEP_CHUNKS
    nc = t // c
    fwd = pl.BlockSpec((1, c, w), lambda i, j: (i, j, 0))
    bwd = pl.BlockSpec((1, c, w), lambda i, j: (i, nc - 1 - j, 0))
    out = jax.ShapeDtypeStruct((b, t, w), F32)
    zshape = pltpu.VMEM((w // LANES, LANES, LANES), F32)
    return pl.pallas_call(
        _rwkv_kernel,
        grid=(b, nc),
        in_specs=[fwd] * 6 + [bwd] * 6,
        out_specs=[fwd, bwd],
        out_shape=[out, out],
        scratch_shapes=[zshape, zshape],
        compiler_params=_params(2),
        name="rwkv_scan",
    )(r, v, kk, lw0, be0, k0, r, v, kk, lw1, be1, k1)


def _diff_in_kernel(x_ref, g_ref, w_ref, qg_ref, kg_ref, bd_ref, qt_o, k_o, vt_o):
    x = x_ref[0]
    d = x.shape[-1]
    hn = _bf(_rms(x, g_ref[...]))
    z = _dot_row_pieces(hn, w_ref[...])
    bd = bd_ref[...]
    half = bd.shape[0]

    def qk_norm(y, gain, scale):
        parts = []
        for j in range(d // half):
            yj = y[:, j * half:(j + 1) * half]
            ms = _dot(yj * yj, bd) * (1.0 / HEAD_DIM)
            parts.append(yj * lax.rsqrt(ms + NORM_EPS) * gain * scale)
        return jnp.concatenate(parts, axis=1)

    q = qk_norm(z[:, :d], qg_ref[...], LOG2E / math.sqrt(HEAD_DIM))
    k_o[0] = _bf(qk_norm(z[:, d:2 * d], kg_ref[...], 1.0))
    tm = x.shape[0]
    for h in range(d // DIFF_V_DIM):
        cols = slice(h * DIFF_V_DIM, (h + 1) * DIFF_V_DIM)
        qt_o[0, h] = _bf(q[:, cols].T)
        vt_o[0, h, :DIFF_V_DIM, :] = _bf(z[:, 2 * d:][:, cols].T)
        vt_o[0, h, DIFF_V_DIM:, :] = jnp.ones((ONES_ROWS, tm), BF16)


def _diff_in(x, gain, w_in, q_gain, k_gain, tm):
    b, t, d = x.shape
    half = 512
    nh = d // DIFF_V_DIM
    head = jnp.arange(half) // HEAD_DIM
    bd = _bf(head[:, None] == head[None, :])
    xs = pl.BlockSpec((1, tm, d), lambda i, j: (i, j, 0))
    out = jax.ShapeDtypeStruct((b, t, d), BF16)
    vt_rows = DIFF_V_DIM + ONES_ROWS
    return pl.pallas_call(
        _diff_in_kernel,
        grid=(b, t // tm),
        in_specs=[xs, _const_spec((1, d)), _const_spec((d, 3 * d)),
                  _const_spec((1, half)), _const_spec((1, half)), _const_spec((half, half))],
        out_specs=[pl.BlockSpec((1, nh, DIFF_V_DIM, tm), lambda i, j: (i, 0, 0, j)), xs,
                   pl.BlockSpec((1, nh, vt_rows, tm), lambda i, j: (i, 0, 0, j))],
        out_shape=[jax.ShapeDtypeStruct((b, nh, DIFF_V_DIM, t), BF16), out,
                   jax.ShapeDtypeStruct((b, nh, vt_rows, t), BF16)],
        compiler_params=_params(2),
        name="diff_in",
    )(x, gain.reshape(1, d), _bf(w_in), jnp.tile(q_gain, half // HEAD_DIM).reshape(1, half),
      jnp.tile(k_gain, half // HEAD_DIM).reshape(1, half), bd)


def _diff_attn_kernel(qt_ref, k_ref, vt_ref, hc_ref, lam_ref, og_ref, o_ref, bias_ref, p_ref, *s_scratch,
                      lambda_init, bounded):
    t, tq = bias_ref.shape

    @pl.when(pl.program_id(2) == 0)
    def _():
        dist = jnp.abs((_iota((t, tq), 0) - pl.program_id(1) * tq - _iota((t, tq), 1)).astype(F32))
        bias = dist * (-LOG2E * hc_ref[0, 0:1, 0:1])
        bias_ref[...] = bias - hc_ref[0, 1:2, 0:1] if bounded else bias

    q = qt_ref[0, 0]
    first = _iota(q.shape, 0) < HEAD_DIM
    zero = jnp.zeros_like(q)
    kb = min(KEY_BLOCK, t)
    qb = min(QUERY_BLOCK, tq)
    sub = HALO

    qcs = (jnp.where(first, q, zero), jnp.where(first, zero, q))
    secs = [(c, slice(i * qb, (i + 1) * qb)) for c in range(2) for i in range(tq // qb)]
    n_kb = t // kb
    outs = []

    def logits(sec, j):
        c, qs = sec
        ks = slice(j * kb, (j + 1) * kb)
        return jnp.dot(k_ref[0, ks, :], qcs[c][:, qs], preferred_element_type=F32) + bias_ref[ks, qs]

    if bounded:
        for sec in secs:
            c, qs = sec
            for j in range(n_kb):
                p_ref[c, j * kb:(j + 1) * kb, qs] = _bf(jnp.exp2(logits(sec, j)))
            outs.append(jnp.dot(vt_ref[0, 0], p_ref[c, :, qs], preferred_element_type=F32))
    else:
        s_ref, = s_scratch

        def score_block(sec, j, m):
            c, qs = sec
            s = logits(sec, j)
            s_ref[c, j * kb:(j + 1) * kb, qs] = s
            return jnp.maximum(m, jnp.max(s.reshape(kb // sub, sub, qb), axis=0))

        def prob_block(sec, j, m):
            c, qs = sec
            ks = slice(j * kb, (j + 1) * kb)
            p_ref[c, ks, qs] = _bf(jnp.exp2(s_ref[c, ks, qs] - m))

        m_init = jnp.full((sub, qb), -jnp.inf, F32)
        m = m_init
        for j in range(n_kb):
            m = score_block(secs[0], j, m)
        m_prev = jnp.max(m, axis=0, keepdims=True)
        for n in range(1, len(secs) + 1):
            m = m_init
            for j in range(n_kb):
                if n < len(secs):
                    m = score_block(secs[n], j, m)
                prob_block(secs[n - 1], j, m_prev)
            m_prev = jnp.max(m, axis=0, keepdims=True)
            c, qs = secs[n - 1]
            outs.append(jnp.dot(vt_ref[0, 0], p_ref[c, :, qs], preferred_element_type=F32))
    per_c = len(secs) // 2
    acc0 = jnp.concatenate(outs[:per_c], axis=1)
    acc1 = jnp.concatenate(outs[per_c:], axis=1)
    o0, s0 = acc0[:DIFF_V_DIM], acc0[DIFF_V_DIM:DIFF_V_DIM + 1]
    o1, s1 = acc1[:DIFF_V_DIM], acc1[DIFF_V_DIM:DIFF_V_DIM + 1]
    lv = lam_ref[...]
    lam = (jnp.exp(jnp.sum(lv[0:1] * lv[1:2], axis=-1, keepdims=True))
           - jnp.exp(jnp.sum(lv[2:3] * lv[3:4], axis=-1, keepdims=True)) + lambda_init)
    o = (o0 * (1.0 / s0) - o1 * (lam / s1)).T
    o_ref[0] = _bf(_rms(o, og_ref[...]) * (1.0 - lambda_init))


def _diff_attention(qt, k, vt, q_gain, k_gain, lam_vecs, out_gain, lambda_init, tq):
    b, t, d = k.shape
    nh = d // DIFF_V_DIM
    slopes = 2.0 ** (-8.0 * jnp.arange(1, nh + 1, dtype=F32) / nh)
    bound = 1.02 * math.sqrt(HEAD_DIM) * LOG2E * jnp.max(jnp.abs(q_gain)) * jnp.max(jnp.abs(k_gain))
    head_consts = jnp.stack([jnp.broadcast_to(slopes[:, None], (nh, LANES)),
                             jnp.broadcast_to(bound, (nh, LANES))], axis=1)

    def run(bounded):
        scratch = [pltpu.VMEM((t, tq), F32), pltpu.VMEM((2, t, tq), BF16)]
        if not bounded:
            scratch.append(pltpu.VMEM((2, t, tq), F32))
        return pl.pallas_call(
            functools.partial(_diff_attn_kernel, lambda_init=lambda_init, bounded=bounded),
            grid=(nh, t // tq, b),
            in_specs=[pl.BlockSpec((1, 1, DIFF_V_DIM, tq), lambda h, j, i: (i, h, 0, j)),
                      pl.BlockSpec((1, t, DIFF_V_DIM), lambda h, j, i: (i, 0, h)),
                      pl.BlockSpec((1, 1, vt.shape[2], t), lambda h, j, i: (i, h, 0, 0)),
                      pl.BlockSpec((1, 2, LANES), lambda h, j, i: (h, 0, 0)),
                      _const_spec((4, HEAD_DIM)),
                      _const_spec((1, DIFF_V_DIM))],
            out_specs=pl.BlockSpec((1, tq, DIFF_V_DIM), lambda h, j, i: (i, j, h)),
            out_shape=jax.ShapeDtypeStruct((b, t, d), BF16),
            scratch_shapes=scratch,
            compiler_params=_params(3),
            name="diff_attn_bounded" if bounded else "diff_attn",
        )(qt, k, vt, head_consts, lam_vecs, out_gain.reshape(1, DIFF_V_DIM))

    return lax.cond(2.0 * bound <= SAFE_EXP2_SPAN, lambda: run(True), lambda: run(False))


def kernel(x, mem, mem_norm, mix_norm, cross_norm, ffn_norm, ab_w_in, ab_shift_prev, ab_shift_next, rwkv_w0, rwkv_w2, rwkv_a0, rwkv_a2, rwkv_g2, rwkv_k_k, rwkv_k_a, rwkv_r_k, rwkv_out_gain, sgu_v_gain, sgu_w_s, sgu_b, ab_w_out, diff_w_in, diff_q_gain, diff_k_gain, diff_lambda_q1, diff_lambda_k1, diff_lambda_q2, diff_lambda_k2, diff_out_gain, diff_w_out, cross_wq, cross_wkv, cross_q_gain, cross_k_gain, cross_wo, ffn_w_up, ffn_conv_w, ffn_conv_b, ffn_w_down):
    b, t, d = x.shape
    depth = mix_norm.shape[0]
    tm = min(512, t)
    tq = min(512, t)
    assert t % tm == 0 and t % (RWKV_CHUNK * RWKV_STEP_CHUNKS) == 0 and tm % SGU_CHUNK == 0

    k_all, v_all = _mem_kv(mem, mem_norm, cross_wkv, cross_k_gain)
    for l in range(depth):
        if l % 2 == 0:
            e = l // 2
            (r, v, kk, lw0, lw1, be0, be1, k0, k1, g, bv, b_out) = _ab_in(
                x, mix_norm[l], ab_w_in[e], ab_shift_prev[e], ab_shift_next[e],
                rwkv_w0[e], rwkv_w2[e], rwkv_a0[e], rwkv_a2[e], rwkv_g2[e],
                rwkv_k_k[e], rwkv_k_a[e], rwkv_r_k[e], sgu_v_gain[e], sgu_w_s[e], sgu_b[e], tm)
            yf, yb = _rwkv_scan(r, v, kk, lw0, lw1, be0, be1, k0, k1)
            head = jnp.arange(RWKV_WIDTH) // HEAD_DIM
            mix = (_mix_rwkv_sgu, (yf, yb, bv, g, b_out),
                   (rwkv_out_gain[e].reshape(1, RWKV_WIDTH), _bf(head[:, None] == head[None, :]), _bf(ab_w_out[e])))
        else:
            o = l // 2
            lambda_init = 0.8 - 0.6 * math.exp(-0.3 * l)
            q, k, vt = _diff_in(x, mix_norm[l], diff_w_in[o], diff_q_gain[o], diff_k_gain[o], tm)
            lam_vecs = jnp.stack([diff_lambda_q1[o], diff_lambda_k1[o], diff_lambda_q2[o], diff_lambda_k2[o]])
            c_out = _diff_attention(q, k, vt, diff_q_gain[o], diff_k_gain[o], lam_vecs, diff_out_gain[o],
                                    lambda_init, tq)
            mix = (_mix_proj, (c_out,), (_bf(diff_w_out[o]),))
        x = _cross_attention(x, *mix, cross_norm[l], cross_wq[l], cross_q_gain[l], k_all, v_all, cross_wo[l], l, tm)
        x = _conv_ffn(x, ffn_norm[l], ffn_w_up[l], ffn_conv_w[l], ffn_conv_b[l], ffn_w_down[l], tm)
    return x
```

```python
import functools
import math

import jax
import jax.numpy as jnp
from jax import lax
from jax.experimental import pallas as pl
from jax.experimental.pallas import tpu as pltpu

NORM_EPS = 1e-6
KK_EPS = 1e-12
RWKV_HEADS = 8
HEAD_DIM = 64
RWKV_WIDTH = RWKV_HEADS * HEAD_DIM
LORA_W = 128
RWKV_IN = 3 * RWKV_WIDTH + 3 * LORA_W
SGU_WIDTH = 512
SGU_GROUPS = 4
SGU_CHUNK = 128
DIFF_HEADS = 8
DIFF_V_DIM = 2 * HEAD_DIM
CROSS_HEADS = 4
CONV_WIDTH = 3
RWKV_CHUNK = 64
RWKV_STEP_CHUNKS = 4
LANES = 128
HALO = 8
FF_CHUNK = 256
MXU_ROWS = 256
KEY_BLOCK = 128
QUERY_BLOCK = 256
ONES_ROWS = 16
SAFE_EXP2_SPAN = 100.0
VMEM_LIMIT = 56 * 1024 * 1024
LOG2E = math.log2(math.e)

F32 = jnp.float32
BF16 = jnp.bfloat16
AB_IN_OUT_DTYPES = (BF16, BF16, BF16, F32, F32, BF16, BF16, BF16, BF16, BF16, BF16, BF16)


def _bf(x):
    return x.astype(BF16)


def _dot(a, b):
    return jnp.dot(_bf(a), _bf(b), preferred_element_type=F32)


def _dot_nt(a, b):
    return lax.dot_general(_bf(a), _bf(b), (((1,), (1,)), ((), ())), preferred_element_type=F32)


def _dot_row_pieces(a, b):
    m = a.shape[0]
    starts = list(range(0, m - m % MXU_ROWS, MXU_ROWS)) or [0]
    ends = starts[1:] + [m]
    return jnp.concatenate([_dot(a[s:e], b) for s, e in zip(starts, ends)], axis=0)


def _rms(x, gain):
    return x * lax.rsqrt(jnp.mean(x * x, axis=-1, keepdims=True) + NORM_EPS) * gain


def _sigmoid(x):
    return 1.0 / (1.0 + jnp.exp(-x))


def _iota(shape, dim):
    return lax.broadcasted_iota(jnp.int32, shape, dim)


def _shift_prev(u, halo_row):
    rolled = pltpu.roll(u, 1, 0)
    head = rolled[:HALO]
    head = jnp.where(_iota(head.shape, 0) == 0, halo_row, head)
    return jnp.concatenate([head, rolled[HALO:]], axis=0)


def _shift_next(u, halo_row):
    n = u.shape[0]
    rolled = pltpu.roll(u, n - 1, 0)
    tail = rolled[n - HALO:]
    tail = jnp.where(_iota(tail.shape, 0) == HALO - 1, halo_row, tail)
    return jnp.concatenate([rolled[:n - HALO], tail], axis=0)


def _halo_rows(xp_ref, xn_ref, gain, i, nt):
    xh = jnp.concatenate([xp_ref[0], xn_ref[0]], axis=0)
    hh = _rms(xh, gain)
    row = _iota(hh.shape, 0)
    has_prev = jnp.where(i > 0, 1.0, 0.0)
    has_next = jnp.where(i < nt - 1, 1.0, 0.0)
    return _bf(hh * jnp.where(row < HALO, has_prev, has_next))


def _const_spec(shape):
    zeros = (0,) * len(shape)
    return pl.BlockSpec(shape, lambda *_: zeros, pipeline_mode=pl.Buffered(1))


def _params(n_axes):
    return pltpu.CompilerParams(dimension_semantics=("arbitrary",) * n_axes,
                                vmem_limit_bytes=VMEM_LIMIT)


def _row_specs(tm, d, t):
    nb = tm // HALO
    last = t // HALO - 1
    main = pl.BlockSpec((1, tm, d), lambda b, i: (b, i, 0))
    prev = pl.BlockSpec((1, HALO, d), lambda b, i: (b, jnp.maximum(i * nb - 1, 0), 0))
    nxt = pl.BlockSpec((1, HALO, d), lambda b, i: (b, jnp.minimum((i + 1) * nb, last), 0))
    return main, prev, nxt


def _mem_kv_kernel(mem_ref, mnorm_ref, wkv_ref, kg_ref, k_ref, v_ref):
    d = mem_ref.shape[-1]
    hd = d // CROSS_HEADS
    mn = _rms(mem_ref[0], mnorm_ref[...])
    kv = jnp.dot(_bf(mn), wkv_ref[0], preferred_element_type=F32)
    for h in range(CROSS_HEADS):
        kh = _rms(kv[:, h * hd:(h + 1) * hd], kg_ref[0]) * (1.0 / math.sqrt(hd))
        k_ref[0, 0, :, h * hd:(h + 1) * hd] = _bf(kh)
    v_ref[0, 0] = _bf(kv[:, d:])


def _mem_kv(mem, mem_norm, wkv, k_gain):
    b, m, d = mem.shape
    nl = wkv.shape[0]
    out = jax.ShapeDtypeStruct((nl, b, m, d), BF16)
    return pl.pallas_call(
        _mem_kv_kernel,
        grid=(nl, b),
        in_specs=[pl.BlockSpec((1, m, d), lambda l, i: (i, 0, 0)),
                  pl.BlockSpec((1, d), lambda l, i: (0, 0)),
                  pl.BlockSpec((1, d, 2 * d), lambda l, i: (l, 0, 0)),
                  pl.BlockSpec((1, 1, d // CROSS_HEADS), lambda l, i: (l, 0, 0))],
        out_specs=[pl.BlockSpec((1, 1, m, d), lambda l, i: (l, i, 0, 0)),
                   pl.BlockSpec((1, 1, m, d), lambda l, i: (l, i, 0, 0))],
        out_shape=[out, out],
        compiler_params=_params(2),
        name="mem_kv",
    )(mem, mem_norm.reshape(1, d), _bf(wkv), k_gain.reshape(nl, 1, -1))


def _mix_rwkv_sgu(x, yf_ref, yb_ref, bv_ref, g_ref, b_ref, og_ref, bd_ref, wo_ref):
    y = yf_ref[0] + yb_ref[0]
    ms = _dot(y * y, bd_ref[...]) * (1.0 / HEAD_DIM)
    yn = y * lax.rsqrt(ms + NORM_EPS) * og_ref[...]
    a_out = (yn + bv_ref[0].astype(F32)) * g_ref[0].astype(F32)
    cat = jnp.concatenate([_bf(a_out), b_ref[0]], axis=1)
    return x + _dot(cat, wo_ref[...])


def _mix_proj(x, at_ref, w_ref):
    a = jnp.concatenate([at_ref[0, h].astype(F32).T for h in range(at_ref.shape[1])], axis=1)
    return x + _dot(a, w_ref[...])


def _cross_kernel(*refs, n_mix, mix_fn):
    x_ref, mix_refs = refs[0], refs[1:1 + n_mix]
    g_ref, wq_ref, qg_ref, k_ref, v_ref, wo_ref, o_ref = refs[1 + n_mix:]
    x = mix_fn(x_ref[0], *mix_refs)
    d = x.shape[-1]
    hd = d // CROSS_HEADS
    hn = _bf(_rms(x, g_ref[...]))
    q = _dot(hn, wq_ref[...])
    outs = []
    for h in range(CROSS_HEADS):
        sl = slice(h * hd, (h + 1) * hd)
        qh = _rms(q[:, sl], qg_ref[...])
        logits = _dot_nt(qh, k_ref[0, 0, :, sl])
        p = jnp.exp(logits - jnp.max(logits, axis=-1, keepdims=True))
        s = jnp.sum(p, axis=-1, keepdims=True)
        outs.append(_dot(p, v_ref[0, 0, :, sl]) / s)
    o = jnp.concatenate(outs, axis=1)
    o_ref[0] = x + _dot(o, wo_ref[...])


def _cross_attention(x, mix_fn, mix_rows, mix_consts, gain, wq, q_gain, k_all, v_all, wo, layer, tq):
    b, t, d = x.shape
    m = k_all.shape[2]
    hd = d // CROSS_HEADS

    def row_spec(a):
        if a.ndim == 3:
            return pl.BlockSpec((1, tq, a.shape[-1]), lambda i, j: (i, j, 0))
        return pl.BlockSpec((1, a.shape[1], a.shape[2], tq), lambda i, j: (i, 0, 0, j))

    mix_specs = [row_spec(a) for a in mix_rows] + [_const_spec(a.shape) for a in mix_consts]
    return pl.pallas_call(
        functools.partial(_cross_kernel, n_mix=len(mix_specs), mix_fn=mix_fn),
        grid=(b, t // tq),
        in_specs=[pl.BlockSpec((1, tq, d), lambda i, j: (i, j, 0)),
                  *mix_specs,
                  _const_spec((1, d)),
                  _const_spec((d, d)),
                  _const_spec((1, hd)),
                  pl.BlockSpec((1, 1, m, d), lambda i, j: (layer, i, 0, 0)),
                  pl.BlockSpec((1, 1, m, d), lambda i, j: (layer, i, 0, 0)),
                  _const_spec((d, d))],
        out_specs=pl.BlockSpec((1, tq, d), lambda i, j: (i, j, 0)),
        out_shape=jax.ShapeDtypeStruct((b, t, d), F32),
        compiler_params=_params(2),
        name=f"cross_attn_{layer}",
    )(x, *mix_rows, *mix_consts, gain.reshape(1, d), _bf(wq), q_gain.reshape(1, hd), k_all, v_all, _bf(wo))


def _ffn_kernel(x_ref, xp_ref, xn_ref, g_ref, wup_ref, cw_ref, cb_ref, wdn_ref, o_ref,
                hn_ref, acc_ref):
    i = pl.program_id(1)
    nt = pl.num_programs(1)
    n_chunks = wdn_ref.shape[0]
    x = x_ref[0]
    tm = x.shape[0]
    gain = g_ref[...]
    hn_ref[:tm] = _bf(_rms(x, gain))
    hn_ref[tm:] = _halo_rows(xp_ref, xn_ref, gain, i, nt)
    acc_ref[...] = jnp.zeros_like(acc_ref)

    def up_proj(idx):
        u = _dot_row_pieces(hn_ref[...], wup_ref[idx])
        return u[:tm], u[tm:]

    def conv(idx, u, uh):
        cw = cw_ref[idx]
        up = _shift_prev(u, uh[HALO - 1:HALO])
        un = _shift_next(u, uh[HALO:HALO + 1])
        return cb_ref[idx] + up * cw[0:1] + u * cw[1:2] + un * cw[2:3]

    ups = [up_proj(0), up_proj(n_chunks)]
    for c in range(n_chunks):
        nxt = [up_proj(c + 1), up_proj(c + 1 + n_chunks)] if c + 1 < n_chunks else None
        gate = conv(c, *ups[0])
        val = conv(c + n_chunks, *ups[1])
        acc_ref[...] += _dot_row_pieces(gate * _sigmoid(gate) * val, wdn_ref[c])
        ups = nxt
    o_ref[0] = x + acc_ref[...]


def _conv_ffn(x, gain, w_up, conv_w, conv_b, w_down, tm):
    b, t, d = x.shape
    ff = w_down.shape[0]
    nc = ff // FF_CHUNK
    wup = _bf(w_up).reshape(d, 2 * nc, FF_CHUNK).transpose(1, 0, 2)
    cw = conv_w.reshape(CONV_WIDTH, 2 * nc, FF_CHUNK).transpose(1, 0, 2)
    cb = conv_b.reshape(2 * nc, 1, FF_CHUNK)
    wdn = _bf(w_down).reshape(nc, FF_CHUNK, d)
    main, prev, nxt = _row_specs(tm, d, t)
    return pl.pallas_call(
        _ffn_kernel,
        grid=(b, t // tm),
        in_specs=[main, prev, nxt,
                  _const_spec((1, d)),
                  _const_spec((2 * nc, d, FF_CHUNK)),
                  _const_spec((2 * nc, CONV_WIDTH, FF_CHUNK)),
                  _const_spec((2 * nc, 1, FF_CHUNK)),
                  _const_spec((nc, FF_CHUNK, d))],
        out_specs=pl.BlockSpec((1, tm, d), lambda i, j: (i, j, 0)),
        out_shape=jax.ShapeDtypeStruct((b, t, d), F32),
        scratch_shapes=[pltpu.VMEM((tm + 2 * HALO, d), BF16),
                        pltpu.VMEM((tm, d), F32)],
        compiler_params=_params(2),
        name="conv_ffn",
    )(x, x, x, gain.reshape(1, d), wup, cw, cb, wdn)


def _ab_in_kernel(x_ref, xp_ref, xn_ref, g_ref, win_ref, mup_ref, mun_ref,
                  w0_ref, w2_ref, a0_ref, a2_ref, g2_ref, kk_ref, ka_ref, rk_ref, bd_ref,
                  vg_ref, ws_ref, sb_ref,
                  r_o, v_o, kk_o, lw0_o, lw1_o, be0_o, be1_o, k0_o, k1_o, g_o, bv_o, b_o):
    i = pl.program_id(1)
    nt = pl.num_programs(1)
    w = RWKV_WIDTH
    x = x_ref[0]
    tm = x.shape[0]
    gain = g_ref[...]
    hn = _bf(_rms(x, gain))
    hx = jnp.concatenate([hn, _halo_rows(xp_ref, xn_ref, gain, i, nt)], axis=0)
    zr = _dot(hx, win_ref[:, :RWKV_IN])
    za, zh = zr[:tm], zr[tm:]
    zp = _shift_prev(za, zh[HALO - 1:HALO])
    zn = _shift_next(za, zh[HALO:HALO + 1])
    mup, mun = mup_ref[...], mun_ref[...]
    zs = za * (1.0 - mup - mun) + mup * zp + mun * zn
    r = zs[:, 0:w]
    k = zs[:, w:2 * w]
    v = zs[:, 2 * w:3 * w]
    wl = jnp.tanh(zs[:, 3 * w:3 * w + LORA_W])
    al = zs[:, 3 * w + LORA_W:3 * w + 2 * LORA_W]
    gl = zs[:, 3 * w + 2 * LORA_W:]

    wpre = w0_ref[...] + _dot(wl, w2_ref[...])
    lw = -math.exp(-0.5) * _sigmoid(wpre)
    a = _sigmoid(a0_ref[...] + _dot(al, a2_ref[...]))
    g = _dot(_sigmoid(gl), g2_ref[...])

    bd = bd_ref[...]
    kk0 = k * kk_ref[...]
    kk = kk0 * lax.rsqrt(_dot(kk0 * kk0, bd) + KK_EPS)
    ka = ka_ref[...]
    k0 = k * (1.0 + (a[:, :w] - 1.0) * ka)
    k1 = k * (1.0 + (a[:, w:] - 1.0) * ka)
    bonus = _dot(r * rk_ref[...] * (k0 + k1), bd)

    r_o[0] = _bf(r)
    v_o[0] = _bf(v)
    kk_o[0] = _bf(kk)
    lw0_o[0] = lw[:, :w]
    lw1_o[0] = lw[:, w:]
    be0_o[0] = _bf(kk * a[:, :w])
    be1_o[0] = _bf(kk * a[:, w:])
    k0_o[0] = _bf(k0)
    k1_o[0] = _bf(k1)
    g_o[0] = _bf(g)
    bv_o[0] = _bf(bonus * v)

    zg = _dot(hn, win_ref[:, RWKV_IN:])
    zg = zg * (0.5 * (1.0 + jnp.tanh(math.sqrt(2.0 / math.pi) * (zg + 0.044715 * (zg * zg * zg)))))
    u = zg[:, :SGU_WIDTH]
    vn = _bf(_rms(zg[:, SGU_WIDTH:], vg_ref[...]))
    gd = SGU_WIDTH // SGU_GROUPS
    for n in range(tm // SGU_CHUNK):
        rows = slice(n * SGU_CHUNK, (n + 1) * SGU_CHUNK)
        for gi in range(SGU_GROUPS):
            cols = slice(gi * gd, (gi + 1) * gd)
            s = jnp.dot(ws_ref[gi], vn[rows, cols], preferred_element_type=F32) + sb_ref[gi]
            b_o[0, rows, cols] = _bf(u[rows, cols] * s)


def _ab_in(x, gain, w_in, mu_prev, mu_next, w0, w2, a0, a2, g2, k_k, k_a, r_k, v_gain, w_s, s_b, tm):
    b, t, d = x.shape
    w = RWKV_WIDTH
    n_in = w_in.shape[1]
    half = LORA_W // 2

    def lora_cat(m):
        z = jnp.zeros((half, w), m.dtype)
        return _bf(jnp.concatenate([jnp.concatenate([m[0], z], axis=1),
                                    jnp.concatenate([z, m[1]], axis=1)], axis=0))

    head = jnp.arange(w) // HEAD_DIM
    bd = _bf(head[:, None] == head[None, :])
    sb = jnp.broadcast_to(s_b[:, :, None], (SGU_GROUPS, SGU_CHUNK, SGU_WIDTH // SGU_GROUPS))
    main, prev, nxt = _row_specs(tm, d, t)
    ospec = pl.BlockSpec((1, tm, w), lambda i, j: (i, j, 0))
    return pl.pallas_call(
        _ab_in_kernel,
        grid=(b, t // tm),
        in_specs=[main, prev, nxt,
                  _const_spec((1, d)),
                  _const_spec((d, n_in)),
                  _const_spec((1, RWKV_IN)), _const_spec((1, RWKV_IN)),
                  _const_spec((1, 2 * w)), _const_spec((LORA_W, 2 * w)),
                  _const_spec((1, 2 * w)), _const_spec((LORA_W, 2 * w)),
                  _const_spec((LORA_W, w)),
                  _const_spec((1, w)), _const_spec((1, w)), _const_spec((1, w)),
                  _const_spec((w, w)),
                  _const_spec((1, SGU_WIDTH)),
                  _const_spec((SGU_GROUPS, SGU_CHUNK, SGU_CHUNK)),
                  _const_spec((SGU_GROUPS, SGU_CHUNK, SGU_WIDTH // SGU_GROUPS))],
        out_specs=[ospec] * 12,
        out_shape=[jax.ShapeDtypeStruct((b, t, w), dt) for dt in AB_IN_OUT_DTYPES],
        compiler_params=_params(2),
        name="ab_in",
    )(x, x, x, gain.reshape(1, d), _bf(w_in), mu_prev.reshape(1, -1), mu_next.reshape(1, -1),
      w0.reshape(1, 2 * w), lora_cat(w2), a0.reshape(1, 2 * w), lora_cat(a2), _bf(g2),
      k_k.reshape(1, w), k_a.reshape(1, w), r_k.reshape(1, w), bd,
      v_gain.reshape(1, -1), _bf(w_s), sb)


def _stack_heads(x):
    even = (_iota(x.shape, 1) & (LANES - 1)) < HEAD_DIM
    return jnp.concatenate([jnp.where(even, x, 0.0), jnp.where(even, 0.0, x)], axis=0)


def _rwkv_prep(r, v, kk, lw, be, kd, reverse):
    c = r.shape[0]
    rr = _iota((c, LANES), 0)
    cc = _iota((c, LANES), 1) & (HEAD_DIM - 1)
    if reverse:
        strict, incl = cc > rr, cc >= rr
    else:
        strict, incl = cc < rr, cc <= rr
    tri = _bf(incl[:, :c])

    l1 = _bf(lw)
    l2 = _bf(lw - l1.astype(F32))
    l3 = _bf(lw - l1.astype(F32) - l2.astype(F32))
    cum = (jnp.dot(tri, l1, preferred_element_type=F32) + jnp.dot(tri, l2, preferred_element_type=F32)
           + jnp.dot(tri, l3, preferred_element_type=F32))
    tot = cum[0:1] if reverse else cum[c - 1:c]
    e_neg = jnp.exp(-cum)
    e_end = jnp.exp(tot - cum)
    return dict(ag=-kk * jnp.exp(cum - lw), rg=r * jnp.exp(cum), bi=be * e_neg, ki=kd * e_neg,
                bee=be * e_end, kee=kd * e_end, gam=jnp.exp(tot), v=v, strict=strict, incl=incl)


def _rwkv_local(chunks):
    c = chunks[0]["v"].shape[0]
    n_tiles = chunks[0]["v"].shape[1] // LANES
    r128 = _iota((LANES, LANES), 0)
    c128 = _iota((LANES, LANES), 1)
    same_head = (r128 < HEAD_DIM) == (c128 < HEAD_DIM)
    diag = r128 == c128
    ctx = [(ch, slice(p * LANES, (p + 1) * LANES)) for ch in chunks for p in range(n_tiles)]

    aa = [_dot_nt(jnp.concatenate([ch["ag"][:, sl], ch["rg"][:, sl]], axis=0),
                  jnp.concatenate([_stack_heads(ch["bi"][:, sl]), _stack_heads(ch["ki"][:, sl])], axis=0))
          for ch, sl in ctx]
    a_ab = [jnp.where(ch["strict"], m[:c, :LANES], 0.0) for (ch, _), m in zip(ctx, aa)]
    a_rb = [jnp.where(ch["incl"], m[c:, :LANES], 0.0) for (ch, _), m in zip(ctx, aa)]
    a_k = [jnp.concatenate([jnp.where(ch["strict"], m[:c, LANES:], 0.0),
                            jnp.where(ch["incl"], m[c:, LANES:], 0.0)], axis=0) for (ch, _), m in zip(ctx, aa)]
    xv = [_dot(m, _stack_heads(ch["v"][:, sl])) for (ch, sl), m in zip(ctx, a_k)]
    x = [jnp.concatenate([ch["ag"][:, sl], m[:c]], axis=1) for (ch, sl), m in zip(ctx, xv)]
    def block_diag(pi):
        return jnp.where(same_head, jnp.concatenate([pi, pi], axis=0), 0.0)

    n_round = int(math.log2(c))
    rr = a_ab
    pw = [_dot(pi, block_diag(pi)) for pi in a_ab]
    for k in range(1, n_round):
        if k < n_round - 1:
            res = [_dot(jnp.concatenate([ri, pi], axis=0), block_diag(pi)) for ri, pi in zip(rr, pw)]
            rr = [ri + pi + m[:c] for ri, pi, m in zip(rr, pw, res)]
            pw = [m[c:] for m in res]
        else:
            rr = [ri + pi + _dot(ri, block_diag(pi)) for ri, pi in zip(rr, pw)]
    x = [xi + _dot(ri, _stack_heads(xi)) for xi, ri in zip(x, rr)]
    ry = [jnp.concatenate([ch["rg"][:, sl], m[c:]], axis=1) + _dot(ai, _stack_heads(xi))
          for (ch, sl), m, ai, xi in zip(ctx, xv, a_rb, x)]
    mg = [_dot(jnp.concatenate([ch["bee"][:, sl], ch["kee"][:, sl]], axis=0).T,
               jnp.concatenate([xi, jnp.concatenate([jnp.zeros_like(ch["v"][:, sl]), ch["v"][:, sl]], axis=1)],
                               axis=0))
          for (ch, sl), xi in zip(ctx, x)]
    out = []
    for (ch, sl), ryi, mgi in zip(ctx, ry, mg):
        gam_col = jnp.sum(jnp.where(diag, ch["gam"][:, sl], 0.0), axis=1, keepdims=True)
        out.append((ryi[:, :LANES], ryi[:, LANES:], jnp.where(same_head, mgi[:, :LANES], 0.0),
                    jnp.where(same_head, mgi[:, LANES:], 0.0), gam_col))
    return [out[i * n_tiles:(i + 1) * n_tiles] for i in range(len(chunks))]


def _rwkv_kernel(rf, vf, kkf, lwf, bef, kf, rb, vb, kkb, lwb, beb, kb, yf_o, yb_o, zf_ref, zb_ref):
    @pl.when(pl.program_id(1) == 0)
    def _():
        zf_ref[...] = jnp.zeros_like(zf_ref)
        zb_ref[...] = jnp.zeros_like(zb_ref)

    c = RWKV_CHUNK
    n = rf.shape[1] // c
    n_tiles = rf.shape[2] // LANES
    chunks = []
    for refs, reverse in (((rf, vf, kkf, lwf, bef, kf), False), ((rb, vb, kkb, lwb, beb, kb), True)):
        for ci in range(n):
            chunks.append(_rwkv_prep(*[ref[0, ci * c:(ci + 1) * c, :].astype(F32) for ref in refs], reverse))
    local = _rwkv_local(chunks)

    for d, (z_ref, y_o) in enumerate(((zf_ref, yf_o), (zb_ref, yb_o))):
        order = range(n) if d == 0 else range(n - 1, -1, -1)
        zs = [z_ref[p] for p in range(n_tiles)]
        for ci in order:
            ys = []
            for p in range(n_tiles):
                rt, yl, mt, gt, gam_col = local[d * n + ci][p]
                yz = _dot(jnp.concatenate([rt, mt], axis=0), zs[p])
                ys.append(yz[:c] + yl)
                zs[p] = gam_col * zs[p] + yz[c:] + gt
            y_o[0, ci * c:(ci + 1) * c, :] = jnp.concatenate(ys, axis=1)
        for p in range(n_tiles):
            z_ref[p] = zs[p]


def _rwkv_scan(r, v, kk, lw0, lw1, be0, be1, k0, k1):
    b, t, w = r.shape
    c = RWKV_CHUNK * RWKV_STEP_CHUNKS
    nc = t // c
    fwd = pl.BlockSpec((1, c, w), lambda i, j: (i, j, 0))
    bwd = pl.BlockSpec((1, c, w), lambda i, j: (i, nc - 1 - j, 0))
    out = jax.ShapeDtypeStruct((b, t, w), F32)
    zshape = pltpu.VMEM((w // LANES, LANES, LANES), F32)
    return pl.pallas_call(
        _rwkv_kernel,
        grid=(b, nc),
        in_specs=[fwd] * 6 + [bwd] * 6,
        out_specs=[fwd, bwd],
        out_shape=[out, out],
        scratch_shapes=[zshape, zshape],
        compiler_params=_params(2),
        name="rwkv_scan",
    )(r, v, kk, lw0, be0, k0, r, v, kk, lw1, be1, k1)


def _diff_in_kernel(x_ref, g_ref, w_ref, qg_ref, kg_ref, bd_ref, qt_o, k_o, vt_o):
    x = x_ref[0]
    d = x.shape[-1]
    hn = _bf(_rms(x, g_ref[...]))
    z = _dot(hn, w_ref[...])
    bd = bd_ref[...]
    half = bd.shape[0]

    def qk_norm(y, gain, scale):
        parts = []
        for j in range(d // half):
            yj = y[:, j * half:(j + 1) * half]
            ms = _dot(yj * yj, bd) * (1.0 / HEAD_DIM)
            parts.append(yj * lax.rsqrt(ms + NORM_EPS) * gain * scale)
        return jnp.concatenate(parts, axis=1)

    q = qk_norm(z[:, :d], qg_ref[...], LOG2E / math.sqrt(HEAD_DIM))
    k_o[0] = _bf(qk_norm(z[:, d:2 * d], kg_ref[...], 1.0))
    tm = x.shape[0]
    for h in range(d // DIFF_V_DIM):
        cols = slice(h * DIFF_V_DIM, (h + 1) * DIFF_V_DIM)
        qt_o[0, h] = _bf(q[:, cols].T)
        vt_o[0, h, :DIFF_V_DIM, :] = _bf(z[:, 2 * d:][:, cols].T)
        vt_o[0, h, DIFF_V_DIM:, :] = jnp.ones((ONES_ROWS, tm), BF16)


def _diff_in(x, gain, w_in, q_gain, k_gain, tm):
    b, t, d = x.shape
    half = 512
    nh = d // DIFF_V_DIM
    head = jnp.arange(half) // HEAD_DIM
    bd = _bf(head[:, None] == head[None, :])
    xs = pl.BlockSpec((1, tm, d), lambda i, j: (i, j, 0))
    out = jax.ShapeDtypeStruct((b, t, d), BF16)
    vt_rows = DIFF_V_DIM + ONES_ROWS
    return pl.pallas_call(
        _diff_in_kernel,
        grid=(b, t // tm),
        in_specs=[xs, _const_spec((1, d)), _const_spec((d, 3 * d)),
                  _const_spec((1, half)), _const_spec((1, half)), _const_spec((half, half))],
        out_specs=[pl.BlockSpec((1, nh, DIFF_V_DIM, tm), lambda i, j: (i, 0, 0, j)), xs,
                   pl.BlockSpec((1, nh, vt_rows, tm), lambda i, j: (i, 0, 0, j))],
        out_shape=[jax.ShapeDtypeStruct((b, nh, DIFF_V_DIM, t), BF16), out,
                   jax.ShapeDtypeStruct((b, nh, vt_rows, t), BF16)],
        compiler_params=_params(2),
        name="diff_in",
    )(x, gain.reshape(1, d), _bf(w_in), jnp.tile(q_gain, half // HEAD_DIM).reshape(1, half),
      jnp.tile(k_gain, half // HEAD_DIM).reshape(1, half), bd)


def _diff_attn_kernel(qt_ref, k_ref, vt_ref, hc_ref, lam_ref, og_ref, o_ref, bias_ref, p_ref, *s_scratch,
                      lambda_init, bounded):
    t, tq = bias_ref.shape

    @pl.when(pl.program_id(2) == 0)
    def _():
        dist = jnp.abs((_iota((t, tq), 0) - pl.program_id(1) * tq - _iota((t, tq), 1)).astype(F32))
        bias = dist * (-LOG2E * hc_ref[0, 0:1, 0:1])
        bias_ref[...] = bias - hc_ref[0, 1:2, 0:1] if bounded else bias

    q = qt_ref[0, 0]
    first = _iota(q.shape, 0) < HEAD_DIM
    zero = jnp.zeros_like(q)
    kb = min(KEY_BLOCK, t)
    qb = min(QUERY_BLOCK, tq)
    sub = HALO

    qcs = (jnp.where(first, q, zero), jnp.where(first, zero, q))
    secs = [(c, slice(i * qb, (i + 1) * qb)) for i in range(tq // qb) for c in range(2)]
    n_kb = t // kb
    outs = []
    lv = lam_ref[...]
    lam = (jnp.exp(jnp.sum(lv[0:1] * lv[1:2], axis=-1, keepdims=True))
           - jnp.exp(jnp.sum(lv[2:3] * lv[3:4], axis=-1, keepdims=True)) + lambda_init)

    def finish(qs):
        acc1, acc0 = outs.pop(), outs.pop()
        o0, s0 = acc0[:DIFF_V_DIM], acc0[DIFF_V_DIM:DIFF_V_DIM + 1]
        o1, s1 = acc1[:DIFF_V_DIM], acc1[DIFF_V_DIM:DIFF_V_DIM + 1]
        o = o0 * (1.0 / s0) - o1 * (lam / s1)
        inv = lax.rsqrt(jnp.mean(o * o, axis=0, keepdims=True) + NORM_EPS)
        o_ref[0, 0, :, qs] = _bf(o * inv * (og_ref[...] * (1.0 - lambda_init)))

    def logits(sec, j):
        c, qs = sec
        ks = slice(j * kb, (j + 1) * kb)
        return jnp.dot(k_ref[0, ks, :], qcs[c][:, qs], preferred_element_type=F32) + bias_ref[ks, qs]

    if bounded:
        for sec in secs:
            c, qs = sec
            for j in range(n_kb):
                p_ref[c, j * kb:(j + 1) * kb, qs] = _bf(jnp.exp2(logits(sec, j)))
            outs.append(jnp.dot(vt_ref[0, 0], p_ref[c, :, qs], preferred_element_type=F32))
            if c == 1:
                finish(qs)
    else:
        s_ref, = s_scratch

        def score_block(sec, j, m):
            c, qs = sec
            s = logits(sec, j)
            s_ref[c, j * kb:(j + 1) * kb, qs] = s
            return jnp.maximum(m, jnp.max(s.reshape(kb // sub, sub, qb), axis=0))

        def prob_block(sec, j, m):
            c, qs = sec
            ks = slice(j * kb, (j + 1) * kb)
            p_ref[c, ks, qs] = _bf(jnp.exp2(s_ref[c, ks, qs] - m))

        m_init = jnp.full((sub, qb), -jnp.inf, F32)
        m = m_init
        for j in range(n_kb):
            m = score_block(secs[0], j, m)
        m_prev = jnp.max(m, axis=0, keepdims=True)
        for n in range(1, len(secs) + 1):
            m = m_init
            for j in range(n_kb):
                if n < len(secs):
                    m = score_block(secs[n], j, m)
                prob_block(secs[n - 1], j, m_prev)
            m_prev = jnp.max(m, axis=0, keepdims=True)
            c, qs = secs[n - 1]
            outs.append(jnp.dot(vt_ref[0, 0], p_ref[c, :, qs], preferred_element_type=F32))
            if c == 1:
                finish(qs)


def _diff_attention(qt, k, vt, q_gain, k_gain, lam_vecs, out_gain, lambda_init, tq):
    b, t, d = k.shape
    nh = d // DIFF_V_DIM
    slopes = 2.0 ** (-8.0 * jnp.arange(1, nh + 1, dtype=F32) / nh)
    bound = 1.02 * math.sqrt(HEAD_DIM) * LOG2E * jnp.max(jnp.abs(q_gain)) * jnp.max(jnp.abs(k_gain))
    head_consts = jnp.stack([jnp.broadcast_to(slopes[:, None], (nh, LANES)),
                             jnp.broadcast_to(bound, (nh, LANES))], axis=1)

    def run(bounded):
        scratch = [pltpu.VMEM((t, tq), F32), pltpu.VMEM((2, t, tq), BF16)]
        if not bounded:
            scratch.append(pltpu.VMEM((2, t, tq), F32))
        return pl.pallas_call(
            functools.partial(_diff_attn_kernel, lambda_init=lambda_init, bounded=bounded),
            grid=(nh, t // tq, b),
            in_specs=[pl.BlockSpec((1, 1, DIFF_V_DIM, tq), lambda h, j, i: (i, h, 0, j)),
                      pl.BlockSpec((1, t, DIFF_V_DIM), lambda h, j, i: (i, 0, h)),
                      pl.BlockSpec((1, 1, vt.shape[2], t), lambda h, j, i: (i, h, 0, 0)),
                      pl.BlockSpec((1, 2, LANES), lambda h, j, i: (h, 0, 0)),
                      _const_spec((4, HEAD_DIM)),
                      _const_spec((DIFF_V_DIM, 1))],
            out_specs=pl.BlockSpec((1, 1, DIFF_V_DIM, tq), lambda h, j, i: (i, h, 0, j)),
            out_shape=jax.ShapeDtypeStruct((b, nh, DIFF_V_DIM, t), BF16),
            scratch_shapes=scratch,
            compiler_params=_params(3),
            name="diff_attn_bounded" if bounded else "diff_attn",
        )(qt, k, vt, head_consts, lam_vecs, out_gain.reshape(DIFF_V_DIM, 1))

    return lax.cond(2.0 * bound <= SAFE_EXP2_SPAN, lambda: run(True), lambda: run(False))


def kernel(x, mem, mem_norm, mix_norm, cross_norm, ffn_norm, ab_w_in, ab_shift_prev, ab_shift_next, rwkv_w0, rwkv_w2, rwkv_a0, rwkv_a2, rwkv_g2, rwkv_k_k, rwkv_k_a, rwkv_r_k, rwkv_out_gain, sgu_v_gain, sgu_w_s, sgu_b, ab_w_out, diff_w_in, diff_q_gain, diff_k_gain, diff_lambda_q1, diff_lambda_k1, diff_lambda_q2, diff_lambda_k2, diff_out_gain, diff_w_out, cross_wq, cross_wkv, cross_q_gain, cross_k_gain, cross_wo, ffn_w_up, ffn_conv_w, ffn_conv_b, ffn_w_down):
    b, t, d = x.shape
    depth = mix_norm.shape[0]
    tm = min(512, t)
    tq = min(512, t)
    assert t % tm == 0 and t % (RWKV_CHUNK * RWKV_STEP_CHUNKS) == 0 and tm % SGU_CHUNK == 0

    k_all, v_all = _mem_kv(mem, mem_norm, cross_wkv, cross_k_gain)
    for l in range(depth):
        if l % 2 == 0:
            e = l // 2
            (r, v, kk, lw0, lw1, be0, be1, k0, k1, g, bv, b_out) = _ab_in(
                x, mix_norm[l], ab_w_in[e], ab_shift_prev[e], ab_shift_next[e],
                rwkv_w0[e], rwkv_w2[e], rwkv_a0[e], rwkv_a2[e], rwkv_g2[e],
                rwkv_k_k[e], rwkv_k_a[e], rwkv_r_k[e], sgu_v_gain[e], sgu_w_s[e], sgu_b[e], tm)
            yf, yb = _rwkv_scan(r, v, kk, lw0, lw1, be0, be1, k0, k1)
            head = jnp.arange(RWKV_WIDTH) // HEAD_DIM
            mix = (_mix_rwkv_sgu, (yf, yb, bv, g, b_out),
                   (rwkv_out_gain[e].reshape(1, RWKV_WIDTH), _bf(head[:, None] == head[None, :]), _bf(ab_w_out[e])))
        else:
            o = l // 2
            lambda_init = 0.8 - 0.6 * math.exp(-0.3 * l)
            q, k, vt = _diff_in(x, mix_norm[l], diff_w_in[o], diff_q_gain[o], diff_k_gain[o], tm)
            lam_vecs = jnp.stack([diff_lambda_q1[o], diff_lambda_k1[o], diff_lambda_q2[o], diff_lambda_k2[o]])
            c_out = _diff_attention(q, k, vt, diff_q_gain[o], diff_k_gain[o], lam_vecs, diff_out_gain[o],
                                    lambda_init, tq)
            mix = (_mix_proj, (c_out,), (_bf(diff_w_out[o]),))
        x = _cross_attention(x, *mix, cross_norm[l], cross_wq[l], cross_q_gain[l], k_all, v_all, cross_wo[l], l, tm)
        x = _conv_ffn(x, ffn_norm[l], ffn_w_up[l], ffn_conv_w[l], ffn_conv_b[l], ffn_w_down[l], tm)
    return x
```

```python
import functools
import math

import jax
import jax.numpy as jnp
from jax import lax
from jax.experimental import pallas as pl
from jax.experimental.pallas import tpu as pltpu

NORM_EPS = 1e-6
KK_EPS = 1e-12
RWKV_HEADS = 8
HEAD_DIM = 64
RWKV_WIDTH = RWKV_HEADS * HEAD_DIM
LORA_W = 128
RWKV_IN = 3 * RWKV_WIDTH + 3 * LORA_W
SGU_WIDTH = 512
SGU_GROUPS = 4
SGU_CHUNK = 128
DIFF_HEADS = 8
DIFF_V_DIM = 2 * HEAD_DIM
CROSS_HEADS = 4
CONV_WIDTH = 3
RWKV_CHUNK = 64
RWKV_STEP_CHUNKS = 2
LANES = 128
HALO = 8
FF_CHUNK = 256
MXU_ROWS = 256
KEY_BLOCK = 128
QUERY_BLOCK = 256
ONES_ROWS = 16
SAFE_EXP2_SPAN = 100.0
VMEM_LIMIT = 56 * 1024 * 1024
LOG2E = math.log2(math.e)

F32 = jnp.float32
BF16 = jnp.bfloat16
AB_IN_OUT_DTYPES = (BF16, BF16, BF16, F32, F32, BF16, BF16, BF16, BF16, BF16, BF16, BF16)


def _bf(x):
    return x.astype(BF16)


def _dot(a, b):
    return jnp.dot(_bf(a), _bf(b), preferred_element_type=F32)


def _dot_nt(a, b):
    return lax.dot_general(_bf(a), _bf(b), (((1,), (1,)), ((), ())), preferred_element_type=F32)


def _dot_row_pieces(a, b):
    m = a.shape[0]
    starts = list(range(0, m - m % MXU_ROWS, MXU_ROWS)) or [0]
    ends = starts[1:] + [m]
    return jnp.concatenate([_dot(a[s:e], b) for s, e in zip(starts, ends)], axis=0)


def _rms(x, gain):
    return x * lax.rsqrt(jnp.mean(x * x, axis=-1, keepdims=True) + NORM_EPS) * gain


def _sigmoid(x):
    return 1.0 / (1.0 + jnp.exp(-x))


def _iota(shape, dim):
    return lax.broadcasted_iota(jnp.int32, shape, dim)


def _shift_prev(u, halo_row):
    rolled = pltpu.roll(u, 1, 0)
    head = rolled[:HALO]
    head = jnp.where(_iota(head.shape, 0) == 0, halo_row, head)
    return jnp.concatenate([head, rolled[HALO:]], axis=0)


def _shift_next(u, halo_row):
    n = u.shape[0]
    rolled = pltpu.roll(u, n - 1, 0)
    tail = rolled[n - HALO:]
    tail = jnp.where(_iota(tail.shape, 0) == HALO - 1, halo_row, tail)
    return jnp.concatenate([rolled[:n - HALO], tail], axis=0)


def _halo_rows(xp_ref, xn_ref, gain, i, nt):
    xh = jnp.concatenate([xp_ref[0], xn_ref[0]], axis=0)
    hh = _rms(xh, gain)
    row = _iota(hh.shape, 0)
    has_prev = jnp.where(i > 0, 1.0, 0.0)
    has_next = jnp.where(i < nt - 1, 1.0, 0.0)
    return _bf(hh * jnp.where(row < HALO, has_prev, has_next))


def _const_spec(shape):
    zeros = (0,) * len(shape)
    return pl.BlockSpec(shape, lambda *_: zeros, pipeline_mode=pl.Buffered(1))


def _params(n_axes):
    return pltpu.CompilerParams(dimension_semantics=("arbitrary",) * n_axes,
                                vmem_limit_bytes=VMEM_LIMIT)


def _row_specs(tm, d, t):
    nb = tm // HALO
    last = t // HALO - 1
    main = pl.BlockSpec((1, tm, d), lambda b, i: (b, i, 0))
    prev = pl.BlockSpec((1, HALO, d), lambda b, i: (b, jnp.maximum(i * nb - 1, 0), 0))
    nxt = pl.BlockSpec((1, HALO, d), lambda b, i: (b, jnp.minimum((i + 1) * nb, last), 0))
    return main, prev, nxt


def _mem_kv_kernel(mem_ref, mnorm_ref, wkv_ref, kg_ref, k_ref, v_ref):
    d = mem_ref.shape[-1]
    hd = d // CROSS_HEADS
    mn = _rms(mem_ref[0], mnorm_ref[...])
    kv = jnp.dot(_bf(mn), wkv_ref[0], preferred_element_type=F32)
    for h in range(CROSS_HEADS):
        kh = _rms(kv[:, h * hd:(h + 1) * hd], kg_ref[0]) * (1.0 / math.sqrt(hd))
        k_ref[0, 0, :, h * hd:(h + 1) * hd] = _bf(kh)
    v_ref[0, 0] = _bf(kv[:, d:])


def _mem_kv(mem, mem_norm, wkv, k_gain):
    b, m, d = mem.shape
    nl = wkv.shape[0]
    out = jax.ShapeDtypeStruct((nl, b, m, d), BF16)
    return pl.pallas_call(
        _mem_kv_kernel,
        grid=(nl, b),
        in_specs=[pl.BlockSpec((1, m, d), lambda l, i: (i, 0, 0)),
                  pl.BlockSpec((1, d), lambda l, i: (0, 0)),
                  pl.BlockSpec((1, d, 2 * d), lambda l, i: (l, 0, 0)),
                  pl.BlockSpec((1, 1, d // CROSS_HEADS), lambda l, i: (l, 0, 0))],
        out_specs=[pl.BlockSpec((1, 1, m, d), lambda l, i: (l, i, 0, 0)),
                   pl.BlockSpec((1, 1, m, d), lambda l, i: (l, i, 0, 0))],
        out_shape=[out, out],
        compiler_params=_params(2),
        name="mem_kv",
    )(mem, mem_norm.reshape(1, d), _bf(wkv), k_gain.reshape(nl, 1, -1))


def _mix_rwkv_sgu(x, yf_ref, yb_ref, bv_ref, g_ref, b_ref, og_ref, bd_ref, wo_ref):
    y = yf_ref[0] + yb_ref[0]
    ms = _dot(y * y, bd_ref[...]) * (1.0 / HEAD_DIM)
    yn = y * lax.rsqrt(ms + NORM_EPS) * og_ref[...]
    a_out = (yn + bv_ref[0].astype(F32)) * g_ref[0].astype(F32)
    cat = jnp.concatenate([_bf(a_out), b_ref[0]], axis=1)
    return x + _dot(cat, wo_ref[...])


def _mix_proj(x, at_ref, w_ref):
    a = jnp.concatenate([at_ref[0, h].astype(F32).T for h in range(at_ref.shape[1])], axis=1)
    return x + _dot(a, w_ref[...])


def _cross_kernel(*refs, n_mix, mix_fn):
    x_ref, mix_refs = refs[0], refs[1:1 + n_mix]
    g_ref, wq_ref, qg_ref, k_ref, v_ref, wo_ref, o_ref = refs[1 + n_mix:]
    x = mix_fn(x_ref[0], *mix_refs)
    d = x.shape[-1]
    hd = d // CROSS_HEADS
    hn = _bf(_rms(x, g_ref[...]))
    q = _dot(hn, wq_ref[...])
    outs = []
    for h in range(CROSS_HEADS):
        sl = slice(h * hd, (h + 1) * hd)
        qh = _rms(q[:, sl], qg_ref[...])
        logits = _dot_nt(qh, k_ref[0, 0, :, sl])
        p = jnp.exp(logits - jnp.max(logits, axis=-1, keepdims=True))
        s = jnp.sum(p, axis=-1, keepdims=True)
        outs.append(_dot(p, v_ref[0, 0, :, sl]) / s)
    o = jnp.concatenate(outs, axis=1)
    o_ref[0] = x + _dot(o, wo_ref[...])


def _cross_attention(x, mix_fn, mix_rows, mix_consts, gain, wq, q_gain, k_all, v_all, wo, layer, tq):
    b, t, d = x.shape
    m = k_all.shape[2]
    hd = d // CROSS_HEADS

    def row_spec(a):
        if a.ndim == 3:
            return pl.BlockSpec((1, tq, a.shape[-1]), lambda i, j: (i, j, 0))
        return pl.BlockSpec((1, a.shape[1], a.shape[2], tq), lambda i, j: (i, 0, 0, j))

    mix_specs = [row_spec(a) for a in mix_rows] + [_const_spec(a.shape) for a in mix_consts]
    return pl.pallas_call(
        functools.partial(_cross_kernel, n_mix=len(mix_specs), mix_fn=mix_fn),
        grid=(b, t // tq),
        in_specs=[pl.BlockSpec((1, tq, d), lambda i, j: (i, j, 0)),
                  *mix_specs,
                  _const_spec((1, d)),
                  _const_spec((d, d)),
                  _const_spec((1, hd)),
                  pl.BlockSpec((1, 1, m, d), lambda i, j: (layer, i, 0, 0)),
                  pl.BlockSpec((1, 1, m, d), lambda i, j: (layer, i, 0, 0)),
                  _const_spec((d, d))],
        out_specs=pl.BlockSpec((1, tq, d), lambda i, j: (i, j, 0)),
        out_shape=jax.ShapeDtypeStruct((b, t, d), F32),
        compiler_params=_params(2),
        name=f"cross_attn_{layer}",
    )(x, *mix_rows, *mix_consts, gain.reshape(1, d), _bf(wq), q_gain.reshape(1, hd), k_all, v_all, _bf(wo))


def _ffn_kernel(x_ref, xp_ref, xn_ref, g_ref, wup_ref, cw_ref, cb_ref, wdn_ref, o_ref,
                hn_ref, acc_ref):
    i = pl.program_id(1)
    nt = pl.num_programs(1)
    n_chunks = wdn_ref.shape[0]
    x = x_ref[0]
    tm = x.shape[0]
    gain = g_ref[...]
    hn_ref[:tm] = _bf(_rms(x, gain))
    hn_ref[tm:] = _halo_rows(xp_ref, xn_ref, gain, i, nt)
    acc_ref[...] = jnp.zeros_like(acc_ref)

    def up_proj(idx):
        u = _dot_row_pieces(hn_ref[...], wup_ref[idx])
        return u[:tm], u[tm:]

    def conv(idx, u, uh):
        cw = cw_ref[idx]
        up = _shift_prev(u, uh[HALO - 1:HALO])
        un = _shift_next(u, uh[HALO:HALO + 1])
        return cb_ref[idx] + up * cw[0:1] + u * cw[1:2] + un * cw[2:3]

    ups = [up_proj(0), up_proj(n_chunks)]
    for c in range(n_chunks):
        nxt = [up_proj(c + 1), up_proj(c + 1 + n_chunks)] if c + 1 < n_chunks else None
        gate = conv(c, *ups[0])
        val = conv(c + n_chunks, *ups[1])
        acc_ref[...] += _dot_row_pieces(gate * _sigmoid(gate) * val, wdn_ref[c])
        ups = nxt
    o_ref[0] = x + acc_ref[...]


def _conv_ffn(x, gain, w_up, conv_w, conv_b, w_down, tm):
    b, t, d = x.shape
    ff = w_down.shape[0]
    nc = ff // FF_CHUNK
    wup = _bf(w_up).reshape(d, 2 * nc, FF_CHUNK).transpose(1, 0, 2)
    cw = conv_w.reshape(CONV_WIDTH, 2 * nc, FF_CHUNK).transpose(1, 0, 2)
    cb = conv_b.reshape(2 * nc, 1, FF_CHUNK)
    wdn = _bf(w_down).reshape(nc, FF_CHUNK, d)
    main, prev, nxt = _row_specs(tm, d, t)
    return pl.pallas_call(
        _ffn_kernel,
        grid=(b, t // tm),
        in_specs=[main, prev, nxt,
                  _const_spec((1, d)),
                  _const_spec((2 * nc, d, FF_CHUNK)),
                  _const_spec((2 * nc, CONV_WIDTH, FF_CHUNK)),
                  _const_spec((2 * nc, 1, FF_CHUNK)),
                  _const_spec((nc, FF_CHUNK, d))],
        out_specs=pl.BlockSpec((1, tm, d), lambda i, j: (i, j, 0)),
        out_shape=jax.ShapeDtypeStruct((b, t, d), F32),
        scratch_shapes=[pltpu.VMEM((tm + 2 * HALO, d), BF16),
                        pltpu.VMEM((tm, d), F32)],
        compiler_params=_params(2),
        name="conv_ffn",
    )(x, x, x, gain.reshape(1, d), wup, cw, cb, wdn)


def _ab_in_kernel(x_ref, xp_ref, xn_ref, g_ref, win_ref, mup_ref, mun_ref,
                  w0_ref, w2_ref, a0_ref, a2_ref, g2_ref, kk_ref, ka_ref, rk_ref, bd_ref,
                  vg_ref, ws_ref, sb_ref,
                  r_o, v_o, kk_o, lw0_o, lw1_o, be0_o, be1_o, k0_o, k1_o, g_o, bv_o, b_o):
    i = pl.program_id(1)
    nt = pl.num_programs(1)
    w = RWKV_WIDTH
    x = x_ref[0]
    tm = x.shape[0]
    gain = g_ref[...]
    hn = _bf(_rms(x, gain))
    hx = jnp.concatenate([hn, _halo_rows(xp_ref, xn_ref, gain, i, nt)], axis=0)
    zr = _dot(hx, win_ref[:, :RWKV_IN])
    za, zh = zr[:tm], zr[tm:]
    zp = _shift_prev(za, zh[HALO - 1:HALO])
    zn = _shift_next(za, zh[HALO:HALO + 1])
    mup, mun = mup_ref[...], mun_ref[...]
    zs = za * (1.0 - mup - mun) + mup * zp + mun * zn
    r = zs[:, 0:w]
    k = zs[:, w:2 * w]
    v = zs[:, 2 * w:3 * w]
    wl = jnp.tanh(zs[:, 3 * w:3 * w + LORA_W])
    al = zs[:, 3 * w + LORA_W:3 * w + 2 * LORA_W]
    gl = zs[:, 3 * w + 2 * LORA_W:]

    wpre = w0_ref[...] + _dot(wl, w2_ref[...])
    lw = -math.exp(-0.5) * _sigmoid(wpre)
    a = _sigmoid(a0_ref[...] + _dot(al, a2_ref[...]))
    g = _dot(_sigmoid(gl), g2_ref[...])

    bd = bd_ref[...]
    kk0 = k * kk_ref[...]
    kk = kk0 * lax.rsqrt(_dot(kk0 * kk0, bd) + KK_EPS)
    ka = ka_ref[...]
    k0 = k * (1.0 + (a[:, :w] - 1.0) * ka)
    k1 = k * (1.0 + (a[:, w:] - 1.0) * ka)
    bonus = _dot(r * rk_ref[...] * (k0 + k1), bd)

    r_o[0] = _bf(r)
    v_o[0] = _bf(v)
    kk_o[0] = _bf(kk)
    lw0_o[0] = lw[:, :w]
    lw1_o[0] = lw[:, w:]
    be0_o[0] = _bf(kk * a[:, :w])
    be1_o[0] = _bf(kk * a[:, w:])
    k0_o[0] = _bf(k0)
    k1_o[0] = _bf(k1)
    g_o[0] = _bf(g)
    bv_o[0] = _bf(bonus * v)

    zg = _dot(hn, win_ref[:, RWKV_IN:])
    zg = zg * (0.5 * (1.0 + jnp.tanh(math.sqrt(2.0 / math.pi) * (zg + 0.044715 * (zg * zg * zg)))))
    u = zg[:, :SGU_WIDTH]
    vn = _bf(_rms(zg[:, SGU_WIDTH:], vg_ref[...]))
    gd = SGU_WIDTH // SGU_GROUPS
    for n in range(tm // SGU_CHUNK):
        rows = slice(n * SGU_CHUNK, (n + 1) * SGU_CHUNK)
        for gi in range(SGU_GROUPS):
            cols = slice(gi * gd, (gi + 1) * gd)
            s = jnp.dot(ws_ref[gi], vn[rows, cols], preferred_element_type=F32) + sb_ref[gi]
            b_o[0, rows, cols] = _bf(u[rows, cols] * s)


def _ab_in(x, gain, w_in, mu_prev, mu_next, w0, w2, a0, a2, g2, k_k, k_a, r_k, v_gain, w_s, s_b, tm):
    b, t, d = x.shape
    w = RWKV_WIDTH
    n_in = w_in.shape[1]
    half = LORA_W // 2

    def lora_cat(m):
        z = jnp.zeros((half, w), m.dtype)
        return _bf(jnp.concatenate([jnp.concatenate([m[0], z], axis=1),
                                    jnp.concatenate([z, m[1]], axis=1)], axis=0))

    head = jnp.arange(w) // HEAD_DIM
    bd = _bf(head[:, None] == head[None, :])
    sb = jnp.broadcast_to(s_b[:, :, None], (SGU_GROUPS, SGU_CHUNK, SGU_WIDTH // SGU_GROUPS))
    main, prev, nxt = _row_specs(tm, d, t)
    ospec = pl.BlockSpec((1, tm, w), lambda i, j: (i, j, 0))
    return pl.pallas_call(
        _ab_in_kernel,
        grid=(b, t // tm),
        in_specs=[main, prev, nxt,
                  _const_spec((1, d)),
                  _const_spec((d, n_in)),
                  _const_spec((1, RWKV_IN)), _const_spec((1, RWKV_IN)),
                  _const_spec((1, 2 * w)), _const_spec((LORA_W, 2 * w)),
                  _const_spec((1, 2 * w)), _const_spec((LORA_W, 2 * w)),
                  _const_spec((LORA_W, w)),
                  _const_spec((1, w)), _const_spec((1, w)), _const_spec((1, w)),
                  _const_spec((w, w)),
                  _const_spec((1, SGU_WIDTH)),
                  _const_spec((SGU_GROUPS, SGU_CHUNK, SGU_CHUNK)),
                  _const_spec((SGU_GROUPS, SGU_CHUNK, SGU_WIDTH // SGU_GROUPS))],
        out_specs=[ospec] * 12,
        out_shape=[jax.ShapeDtypeStruct((b, t, w), dt) for dt in AB_IN_OUT_DTYPES],
        compiler_params=_params(2),
        name="ab_in",
    )(x, x, x, gain.reshape(1, d), _bf(w_in), mu_prev.reshape(1, -1), mu_next.reshape(1, -1),
      w0.reshape(1, 2 * w), lora_cat(w2), a0.reshape(1, 2 * w), lora_cat(a2), _bf(g2),
      k_k.reshape(1, w), k_a.reshape(1, w), r_k.reshape(1, w), bd,
      v_gain.reshape(1, -1), _bf(w_s), sb)


def _stack_heads(x):
    even = (_iota(x.shape, 1) & (LANES - 1)) < HEAD_DIM
    return jnp.concatenate([jnp.where(even, x, 0.0), jnp.where(even, 0.0, x)], axis=0)


def _rwkv_prep(r, v, kk, lw, be, kd, reverse):
    c = r.shape[0]
    rr = _iota((c, LANES), 0)
    cc = _iota((c, LANES), 1) & (HEAD_DIM - 1)
    if reverse:
        strict, incl = cc > rr, cc >= rr
    else:
        strict, incl = cc < rr, cc <= rr
    tri = _bf(incl[:, :c])

    l1 = _bf(lw)
    l2 = _bf(lw - l1.astype(F32))
    l3 = _bf(lw - l1.astype(F32) - l2.astype(F32))
    cum = (jnp.dot(tri, l1, preferred_element_type=F32) + jnp.dot(tri, l2, preferred_element_type=F32)
           + jnp.dot(tri, l3, preferred_element_type=F32))
    tot = cum[0:1] if reverse else cum[c - 1:c]
    e_neg = jnp.exp(-cum)
    e_end = jnp.exp(tot - cum)
    return dict(ag=-kk * jnp.exp(cum - lw), rg=r * jnp.exp(cum), bi=be * e_neg, ki=kd * e_neg,
                bee=be * e_end, kee=kd * e_end, gam=jnp.exp(tot), v=v, strict=strict, incl=incl)


def _rwkv_local(chunks):
    c = chunks[0]["v"].shape[0]
    n_tiles = chunks[0]["v"].shape[1] // LANES
    r128 = _iota((LANES, LANES), 0)
    c128 = _iota((LANES, LANES), 1)
    same_head = (r128 < HEAD_DIM) == (c128 < HEAD_DIM)
    diag = r128 == c128
    ctx = [(ch, slice(p * LANES, (p + 1) * LANES)) for ch in chunks for p in range(n_tiles)]

    aa = [_dot_nt(jnp.concatenate([ch["ag"][:, sl], ch["rg"][:, sl]], axis=0),
                  jnp.concatenate([_stack_heads(ch["bi"][:, sl]), _stack_heads(ch["ki"][:, sl])], axis=0))
          for ch, sl in ctx]
    a_ab = [jnp.where(ch["strict"], m[:c, :LANES], 0.0) for (ch, _), m in zip(ctx, aa)]
    a_rb = [jnp.where(ch["incl"], m[c:, :LANES], 0.0) for (ch, _), m in zip(ctx, aa)]
    a_k = [jnp.concatenate([jnp.where(ch["strict"], m[:c, LANES:], 0.0),
                            jnp.where(ch["incl"], m[c:, LANES:], 0.0)], axis=0) for (ch, _), m in zip(ctx, aa)]
    xv = [_dot(m, _stack_heads(ch["v"][:, sl])) for (ch, sl), m in zip(ctx, a_k)]
    x = [jnp.concatenate([ch["ag"][:, sl], m[:c]], axis=1) for (ch, sl), m in zip(ctx, xv)]
    def block_diag(pi):
        return jnp.where(same_head, jnp.concatenate([pi, pi], axis=0), 0.0)

    n_round = int(math.log2(c))
    rr = a_ab
    pw = [_dot(pi, block_diag(pi)) for pi in a_ab]
    for k in range(1, n_round):
        if k < n_round - 1:
            res = [_dot(jnp.concatenate([ri, pi], axis=0), block_diag(pi)) for ri, pi in zip(rr, pw)]
            rr = [ri + pi + m[:c] for ri, pi, m in zip(rr, pw, res)]
            pw = [m[c:] for m in res]
        else:
            rr = [ri + pi + _dot(ri, block_diag(pi)) for ri, pi in zip(rr, pw)]
    x = [xi + _dot(ri, _stack_heads(xi)) for xi, ri in zip(x, rr)]
    ry = [jnp.concatenate([ch["rg"][:, sl], m[c:]], axis=1) + _dot(ai, _stack_heads(xi))
          for (ch, sl), m, ai, xi in zip(ctx, xv, a_rb, x)]
    mg = [_dot(jnp.concatenate([ch["bee"][:, sl], ch["kee"][:, sl]], axis=0).T,
               jnp.concatenate([xi, jnp.concatenate([jnp.zeros_like(ch["v"][:, sl]), ch["v"][:, sl]], axis=1)],
                               axis=0))
          for (ch, sl), xi in zip(ctx, x)]
    out = []
    for (ch, sl), ryi, mgi in zip(ctx, ry, mg):
        gam_col = jnp.sum(jnp.where(diag, ch["gam"][:, sl], 0.0), axis=1, keepdims=True)
        out.append((ryi[:, :LANES], ryi[:, LANES:], jnp.where(same_head, mgi[:, :LANES], 0.0),
                    jnp.where(same_head, mgi[:, LANES:], 0.0), gam_col))
    return [out[i * n_tiles:(i + 1) * n_tiles] for i in range(len(chunks))]


def _rwkv_kernel(rf, vf, kkf, lwf, bef, kf, rb, vb, kkb, lwb, beb, kb, yf_o, yb_o, zf_ref, zb_ref):
    @pl.when(pl.program_id(1) == 0)
    def _():
        zf_ref[...] = jnp.zeros_like(zf_ref)
        zb_ref[...] = jnp.zeros_like(zb_ref)

    c = RWKV_CHUNK
    n = rf.shape[1] // c
    n_tiles = rf.shape[2] // LANES
    chunks = []
    for refs, reverse in (((rf, vf, kkf, lwf, bef, kf), False), ((rb, vb, kkb, lwb, beb, kb), True)):
        for ci in range(n):
            chunks.append(_rwkv_prep(*[ref[0, ci * c:(ci + 1) * c, :].astype(F32) for ref in refs], reverse))
    local = _rwkv_local(chunks)

    for d, (z_ref, y_o) in enumerate(((zf_ref, yf_o), (zb_ref, yb_o))):
        order = range(n) if d == 0 else range(n - 1, -1, -1)
        zs = [z_ref[p] for p in range(n_tiles)]
        for ci in order:
            ys = []
            for p in range(n_tiles):
                rt, yl, mt, gt, gam_col = local[d * n + ci][p]
                yz = _dot(jnp.concatenate([rt, mt], axis=0), zs[p])
                ys.append(yz[:c] + yl)
                zs[p] = gam_col * zs[p] + yz[c:] + gt
            y_o[0, ci * c:(ci + 1) * c, :] = jnp.concatenate(ys, axis=1)
        for p in range(n_tiles):
            z_ref[p] = zs[p]


def _rwkv_scan(r, v, kk, lw0, lw1, be0, be1, k0, k1):
    b, t, w = r.shape
    c = RWKV_CHUNK * RWKV_STEP_CHUNKS
    nc = t // c
    fwd = pl.BlockSpec((1, c, w), lambda i, j: (i, j, 0))
    bwd = pl.BlockSpec((1, c, w), lambda i, j: (i, nc - 1 - j, 0))
    out = jax.ShapeDtypeStruct((b, t, w), F32)
    zshape = pltpu.VMEM((w // LANES, LANES, LANES), F32)
    return pl.pallas_call(
        _rwkv_kernel,
        grid=(b, nc),
        in_specs=[fwd] * 6 + [bwd] * 6,
        out_specs=[fwd, bwd],
        out_shape=[out, out],
        scratch_shapes=[zshape, zshape],
        compiler_params=_params(2),
        name="rwkv_scan",
    )(r, v, kk, lw0, be0, k0, r, v, kk, lw1, be1, k1)


def _diff_in_kernel(x_ref, g_ref, w_ref, qgt_ref, kg_ref, bd_ref, qt_o, k_o, vt_o):
    x = x_ref[0]
    d = x.shape[-1]
    hn = _bf(_rms(x, g_ref[...]))
    z = _dot(hn, w_ref[...])
    bd = bd_ref[...]
    half = bd.shape[0]

    def qk_norm(y, gain, scale):
        parts = []
        for j in range(d // half):
            yj = y[:, j * half:(j + 1) * half]
            ms = _dot(yj * yj, bd) * (1.0 / HEAD_DIM)
            parts.append(yj * lax.rsqrt(ms + NORM_EPS) * gain * scale)
        return jnp.concatenate(parts, axis=1)

    k_o[0] = _bf(qk_norm(z[:, d:2 * d], kg_ref[...], 1.0))
    tm = x.shape[0]
    q_scale = qgt_ref[...] * (LOG2E / math.sqrt(HEAD_DIM))
    for h in range(d // DIFF_V_DIM):
        cols = slice(h * DIFF_V_DIM, (h + 1) * DIFF_V_DIM)
        qt = z[:, cols].T.reshape(2, HEAD_DIM, tm)
        inv = lax.rsqrt(jnp.mean(qt * qt, axis=1, keepdims=True) + NORM_EPS)
        qt_o[0, h] = _bf((qt * inv).reshape(DIFF_V_DIM, tm) * q_scale)
        vt_o[0, h, :DIFF_V_DIM, :] = _bf(z[:, 2 * d:][:, cols].T)
        vt_o[0, h, DIFF_V_DIM:, :] = jnp.ones((ONES_ROWS, tm), BF16)


def _diff_in(x, gain, w_in, q_gain, k_gain, tm):
    b, t, d = x.shape
    half = 512
    nh = d // DIFF_V_DIM
    head = jnp.arange(half) // HEAD_DIM
    bd = _bf(head[:, None] == head[None, :])
    xs = pl.BlockSpec((1, tm, d), lambda i, j: (i, j, 0))
    out = jax.ShapeDtypeStruct((b, t, d), BF16)
    vt_rows = DIFF_V_DIM + ONES_ROWS
    return pl.pallas_call(
        _diff_in_kernel,
        grid=(b, t // tm),
        in_specs=[xs, _const_spec((1, d)), _const_spec((d, 3 * d)),
                  _const_spec((DIFF_V_DIM, 1)), _const_spec((1, half)), _const_spec((half, half))],
        out_specs=[pl.BlockSpec((1, nh, DIFF_V_DIM, tm), lambda i, j: (i, 0, 0, j)), xs,
                   pl.BlockSpec((1, nh, vt_rows, tm), lambda i, j: (i, 0, 0, j))],
        out_shape=[jax.ShapeDtypeStruct((b, nh, DIFF_V_DIM, t), BF16), out,
                   jax.ShapeDtypeStruct((b, nh, vt_rows, t), BF16)],
        compiler_params=_params(2),
        name="diff_in",
    )(x, gain.reshape(1, d), _bf(w_in), jnp.tile(q_gain, 2).reshape(DIFF_V_DIM, 1),
      jnp.tile(k_gain, half // HEAD_DIM).reshape(1, half), bd)


def _diff_attn_kernel(qt_ref, k_ref, vt_ref, hc_ref, lam_ref, og_ref, o_ref, bias_ref, p_ref, *s_scratch,
                      lambda_init, bounded):
    t, tq = bias_ref.shape

    @pl.when(pl.program_id(2) == 0)
    def _():
        dist = jnp.abs((_iota((t, tq), 0) - pl.program_id(1) * tq - _iota((t, tq), 1)).astype(F32))
        bias = dist * (-LOG2E * hc_ref[0, 0:1, 0:1])
        bias_ref[...] = bias - hc_ref[0, 1:2, 0:1] if bounded else bias

    q = qt_ref[0, 0]
    first = _iota(q.shape, 0) < HEAD_DIM
    zero = jnp.zeros_like(q)
    kb = min(KEY_BLOCK, t)
    qb = min(QUERY_BLOCK, tq)
    sub = HALO

    qcs = (jnp.where(first, q, zero), jnp.where(first, zero, q))
    secs = [(c, slice(i * qb, (i + 1) * qb)) for i in range(tq // qb) for c in range(2)]
    n_kb = t // kb
    outs = []
    lv = lam_ref[...]
    lam = (jnp.exp(jnp.sum(lv[0:1] * lv[1:2], axis=-1, keepdims=True))
           - jnp.exp(jnp.sum(lv[2:3] * lv[3:4], axis=-1, keepdims=True)) + lambda_init)

    def finish(qs):
        acc1, acc0 = outs.pop(), outs.pop()
        o0, s0 = acc0[:DIFF_V_DIM], acc0[DIFF_V_DIM:DIFF_V_DIM + 1]
        o1, s1 = acc1[:DIFF_V_DIM], acc1[DIFF_V_DIM:DIFF_V_DIM + 1]
        o = o0 * (1.0 / s0) - o1 * (lam / s1)
        inv = lax.rsqrt(jnp.mean(o * o, axis=0, keepdims=True) + NORM_EPS)
        o_ref[0, 0, :, qs] = _bf(o * inv * (og_ref[...] * (1.0 - lambda_init)))

    def logits(sec, j):
        c, qs = sec
        ks = slice(j * kb, (j + 1) * kb)
        return jnp.dot(k_ref[0, ks, :], qcs[c][:, qs], preferred_element_type=F32) + bias_ref[ks, qs]

    if bounded:
        for sec in secs:
            c, qs = sec
            for j in range(n_kb):
                p_ref[c, j * kb:(j + 1) * kb, qs] = _bf(jnp.exp2(logits(sec, j)))
            outs.append(jnp.dot(vt_ref[0, 0], p_ref[c, :, qs], preferred_element_type=F32))
            if c == 1:
                finish(qs)
    else:
        s_ref, = s_scratch

        def score_block(sec, j, m):
            c, qs = sec
            s = logits(sec, j)
            s_ref[c, j * kb:(j + 1) * kb, qs] = s
            return jnp.maximum(m, jnp.max(s.reshape(kb // sub, sub, qb), axis=0))

        def prob_block(sec, j, m):
            c, qs = sec
            ks = slice(j * kb, (j + 1) * kb)
            p_ref[c, ks, qs] = _bf(jnp.exp2(s_ref[c, ks, qs] - m))

        m_init = jnp.full((sub, qb), -jnp.inf, F32)
        m = m_init
        for j in range(n_kb):
            m = score_block(secs[0], j, m)
        m_prev = jnp.max(m, axis=0, keepdims=True)
        for n in range(1, len(secs) + 1):
            m = m_init
            for j in range(n_kb):
                if n < len(secs):
                    m = score_block(secs[n], j, m)
                prob_block(secs[n - 1], j, m_prev)
            m_prev = jnp.max(m, axis=0, keepdims=True)
            c, qs = secs[n - 1]
            outs.append(jnp.dot(vt_ref[0, 0], p_ref[c, :, qs], preferred_element_type=F32))
            if c == 1:
                finish(qs)


def _diff_attention(qt, k, vt, q_gain, k_gain, lam_vecs, out_gain, lambda_init, tq):
    b, t, d = k.shape
    nh = d // DIFF_V_DIM
    slopes = 2.0 ** (-8.0 * jnp.arange(1, nh + 1, dtype=F32) / nh)
    bound = 1.02 * math.sqrt(HEAD_DIM) * LOG2E * jnp.max(jnp.abs(q_gain)) * jnp.max(jnp.abs(k_gain))
    head_consts = jnp.stack([jnp.broadcast_to(slopes[:, None], (nh, LANES)),
                             jnp.broadcast_to(bound, (nh, LANES))], axis=1)

    def run(bounded):
        scratch = [pltpu.VMEM((t, tq), F32), pltpu.VMEM((2, t, tq), BF16)]
        if not bounded:
            scratch.append(pltpu.VMEM((2, t, tq), F32))
        return pl.pallas_call(
            functools.partial(_diff_attn_kernel, lambda_init=lambda_init, bounded=bounded),
            grid=(nh, t // tq, b),
            in_specs=[pl.BlockSpec((1, 1, DIFF_V_DIM, tq), lambda h, j, i: (i, h, 0, j)),
                      pl.BlockSpec((1, t, DIFF_V_DIM), lambda h, j, i: (i, 0, h)),
                      pl.BlockSpec((1, 1, vt.shape[2], t), lambda h, j, i: (i, h, 0, 0)),
                      pl.BlockSpec((1, 2, LANES), lambda h, j, i: (h, 0, 0)),
                      _const_spec((4, HEAD_DIM)),
                      _const_spec((DIFF_V_DIM, 1))],
            out_specs=pl.BlockSpec((1, 1, DIFF_V_DIM, tq), lambda h, j, i: (i, h, 0, j)),
            out_shape=jax.ShapeDtypeStruct((b, nh, DIFF_V_DIM, t), BF16),
            scratch_shapes=scratch,
            compiler_params=_params(3),
            name="diff_attn_bounded" if bounded else "diff_attn",
        )(qt, k, vt, head_consts, lam_vecs, out_gain.reshape(DIFF_V_DIM, 1))

    return lax.cond(2.0 * bound <= SAFE_EXP2_SPAN, lambda: run(True), lambda: run(False))


def kernel(x, mem, mem_norm, mix_norm, cross_norm, ffn_norm, ab_w_in, ab_shift_prev, ab_shift_next, rwkv_w0, rwkv_w2, rwkv_a0, rwkv_a2, rwkv_g2, rwkv_k_k, rwkv_k_a, rwkv_r_k, rwkv_out_gain, sgu_v_gain, sgu_w_s, sgu_b, ab_w_out, diff_w_in, diff_q_gain, diff_k_gain, diff_lambda_q1, diff_lambda_k1, diff_lambda_q2, diff_lambda_k2, diff_out_gain, diff_w_out, cross_wq, cross_wkv, cross_q_gain, cross_k_gain, cross_wo, ffn_w_up, ffn_conv_w, ffn_conv_b, ffn_w_down):
    b, t, d = x.shape
    depth = mix_norm.shape[0]
    tm = min(512, t)
    tq = min(512, t)
    assert t % tm == 0 and t % (RWKV_CHUNK * RWKV_STEP_CHUNKS) == 0 and tm % SGU_CHUNK == 0

    k_all, v_all = _mem_kv(mem, mem_norm, cross_wkv, cross_k_gain)
    for l in range(depth):
        if l % 2 == 0:
            e = l // 2
            (r, v, kk, lw0, lw1, be0, be1, k0, k1, g, bv, b_out) = _ab_in(
                x, mix_norm[l], ab_w_in[e], ab_shift_prev[e], ab_shift_next[e],
                rwkv_w0[e], rwkv_w2[e], rwkv_a0[e], rwkv_a2[e], rwkv_g2[e],
                rwkv_k_k[e], rwkv_k_a[e], rwkv_r_k[e], sgu_v_gain[e], sgu_w_s[e], sgu_b[e], tm)
            yf, yb = _rwkv_scan(r, v, kk, lw0, lw1, be0, be1, k0, k1)
            head = jnp.arange(RWKV_WIDTH) // HEAD_DIM
            mix = (_mix_rwkv_sgu, (yf, yb, bv, g, b_out),
                   (rwkv_out_gain[e].reshape(1, RWKV_WIDTH), _bf(head[:, None] == head[None, :]), _bf(ab_w_out[e])))
        else:
            o = l // 2
            lambda_init = 0.8 - 0.6 * math.exp(-0.3 * l)
            q, k, vt = _diff_in(x, mix_norm[l], diff_w_in[o], diff_q_gain[o], diff_k_gain[o], tm)
            lam_vecs = jnp.stack([diff_lambda_q1[o], diff_lambda_k1[o], diff_lambda_q2[o], diff_lambda_k2[o]])
            c_out = _diff_attention(q, k, vt, diff_q_gain[o], diff_k_gain[o], lam_vecs, diff_out_gain[o],
                                    lambda_init, tq)
            mix = (_mix_proj, (c_out,), (_bf(diff_w_out[o]),))
        x = _cross_attention(x, *mix, cross_norm[l], cross_wq[l], cross_q_gain[l], k_all, v_all, cross_wo[l], l, tm)
        x = _conv_ffn(x, ffn_norm[l], ffn_w_up[l], ffn_conv_w[l], ffn_conv_b[l], ffn_w_down[l], tm)
    return x
```

```python
import functools
import math

import jax
import jax.numpy as jnp
from jax import lax
from jax.experimental import pallas as pl
from jax.experimental.pallas import tpu as pltpu

NORM_EPS = 1e-6
KK_EPS = 1e-12
RWKV_HEADS = 8
HEAD_DIM = 64
RWKV_WIDTH = RWKV_HEADS * HEAD_DIM
LORA_W = 128
RWKV_IN = 3 * RWKV_WIDTH + 3 * LORA_W
SGU_WIDTH = 512
SGU_GROUPS = 4
SGU_CHUNK = 128
DIFF_HEADS = 8
DIFF_V_DIM = 2 * HEAD_DIM
CROSS_HEADS = 4
CONV_WIDTH = 3
RWKV_CHUNK = 64
RWKV_STEP_CHUNKS = 2
LANES = 128
HALO = 8
FF_CHUNK = 256
MXU_ROWS = 256
KEY_BLOCK = 128
QUERY_BLOCK = 256
ONES_ROWS = 16
SAFE_EXP2_SPAN = 100.0
VMEM_LIMIT = 56 * 1024 * 1024
LOG2E = math.log2(math.e)

F32 = jnp.float32
BF16 = jnp.bfloat16
AB_IN_OUT_DTYPES = (BF16, BF16, BF16, F32, F32, BF16, BF16, BF16, BF16, BF16, BF16, BF16)


def _bf(x):
    return x.astype(BF16)


def _dot(a, b):
    return jnp.dot(_bf(a), _bf(b), preferred_element_type=F32)


def _dot_nt(a, b):
    return lax.dot_general(_bf(a), _bf(b), (((1,), (1,)), ((), ())), preferred_element_type=F32)


def _dot_row_pieces(a, b):
    m = a.shape[0]
    starts = list(range(0, m - m % MXU_ROWS, MXU_ROWS)) or [0]
    ends = starts[1:] + [m]
    return jnp.concatenate([_dot(a[s:e], b) for s, e in zip(starts, ends)], axis=0)


def _rms(x, gain):
    return x * lax.rsqrt(jnp.mean(x * x, axis=-1, keepdims=True) + NORM_EPS) * gain


def _sigmoid(x):
    return 1.0 / (1.0 + jnp.exp(-x))


def _iota(shape, dim):
    return lax.broadcasted_iota(jnp.int32, shape, dim)


def _shift_prev(u, halo_row):
    rolled = pltpu.roll(u, 1, 0)
    head = rolled[:HALO]
    head = jnp.where(_iota(head.shape, 0) == 0, halo_row, head)
    return jnp.concatenate([head, rolled[HALO:]], axis=0)


def _shift_next(u, halo_row):
    n = u.shape[0]
    rolled = pltpu.roll(u, n - 1, 0)
    tail = rolled[n - HALO:]
    tail = jnp.where(_iota(tail.shape, 0) == HALO - 1, halo_row, tail)
    return jnp.concatenate([rolled[:n - HALO], tail], axis=0)


def _halo_rows(xp_ref, xn_ref, gain, i, nt):
    xh = jnp.concatenate([xp_ref[0], xn_ref[0]], axis=0)
    hh = _rms(xh, gain)
    row = _iota(hh.shape, 0)
    has_prev = jnp.where(i > 0, 1.0, 0.0)
    has_next = jnp.where(i < nt - 1, 1.0, 0.0)
    return _bf(hh * jnp.where(row < HALO, has_prev, has_next))


def _const_spec(shape):
    zeros = (0,) * len(shape)
    return pl.BlockSpec(shape, lambda *_: zeros, pipeline_mode=pl.Buffered(1))


def _params(n_axes):
    return pltpu.CompilerParams(dimension_semantics=("arbitrary",) * n_axes,
                                vmem_limit_bytes=VMEM_LIMIT)


def _row_specs(tm, d, t):
    nb = tm // HALO
    last = t // HALO - 1
    main = pl.BlockSpec((1, tm, d), lambda b, i: (b, i, 0))
    prev = pl.BlockSpec((1, HALO, d), lambda b, i: (b, jnp.maximum(i * nb - 1, 0), 0))
    nxt = pl.BlockSpec((1, HALO, d), lambda b, i: (b, jnp.minimum((i + 1) * nb, last), 0))
    return main, prev, nxt


def _mem_kv_kernel(mem_ref, mnorm_ref, wkv_ref, kg_ref, k_ref, v_ref):
    d = mem_ref.shape[-1]
    hd = d // CROSS_HEADS
    mn = _rms(mem_ref[0], mnorm_ref[...])
    kv = jnp.dot(_bf(mn), wkv_ref[0], preferred_element_type=F32)
    for h in range(CROSS_HEADS):
        kh = _rms(kv[:, h * hd:(h + 1) * hd], kg_ref[0]) * (1.0 / math.sqrt(hd))
        k_ref[0, 0, :, h * hd:(h + 1) * hd] = _bf(kh)
    v_ref[0, 0] = _bf(kv[:, d:])


def _mem_kv(mem, mem_norm, wkv, k_gain):
    b, m, d = mem.shape
    nl = wkv.shape[0]
    out = jax.ShapeDtypeStruct((nl, b, m, d), BF16)
    return pl.pallas_call(
        _mem_kv_kernel,
        grid=(nl, b),
        in_specs=[pl.BlockSpec((1, m, d), lambda l, i: (i, 0, 0)),
                  pl.BlockSpec((1, d), lambda l, i: (0, 0)),
                  pl.BlockSpec((1, d, 2 * d), lambda l, i: (l, 0, 0)),
                  pl.BlockSpec((1, 1, d // CROSS_HEADS), lambda l, i: (l, 0, 0))],
        out_specs=[pl.BlockSpec((1, 1, m, d), lambda l, i: (l, i, 0, 0)),
                   pl.BlockSpec((1, 1, m, d), lambda l, i: (l, i, 0, 0))],
        out_shape=[out, out],
        compiler_params=_params(2),
        name="mem_kv",
    )(mem, mem_norm.reshape(1, d), _bf(wkv), k_gain.reshape(nl, 1, -1))


def _mix_rwkv_sgu(x, yf_ref, yb_ref, bv_ref, g_ref, b_ref, og_ref, bd_ref, wo_ref):
    y = yf_ref[0] + yb_ref[0]
    ms = _dot(y * y, bd_ref[...]) * (1.0 / HEAD_DIM)
    yn = y * lax.rsqrt(ms + NORM_EPS) * og_ref[...]
    a_out = (yn + bv_ref[0].astype(F32)) * g_ref[0].astype(F32)
    cat = jnp.concatenate([_bf(a_out), b_ref[0]], axis=1)
    return x + _dot(cat, wo_ref[...])


def _mix_proj(x, at_ref, w_ref):
    a = jnp.concatenate([at_ref[0, h].astype(F32).T for h in range(at_ref.shape[1])], axis=1)
    return x + _dot(a, w_ref[...])


def _cross_kernel(*refs, n_mix, mix_fn):
    x_ref, mix_refs = refs[0], refs[1:1 + n_mix]
    g_ref, wq_ref, qg_ref, k_ref, v_ref, wo_ref, o_ref = refs[1 + n_mix:]
    x = mix_fn(x_ref[0], *mix_refs)
    d = x.shape[-1]
    hd = d // CROSS_HEADS
    hn = _bf(_rms(x, g_ref[...]))
    q = _dot(hn, wq_ref[...])
    outs = []
    for h in range(CROSS_HEADS):
        sl = slice(h * hd, (h + 1) * hd)
        qh = _rms(q[:, sl], qg_ref[...])
        logits = _dot_nt(qh, k_ref[0, 0, :, sl])
        p = jnp.exp(logits - jnp.max(logits, axis=-1, keepdims=True))
        s = jnp.sum(p, axis=-1, keepdims=True)
        outs.append(_dot(p, v_ref[0, 0, :, sl]) / s)
    o = jnp.concatenate(outs, axis=1)
    o_ref[0] = x + _dot(o, wo_ref[...])


def _cross_attention(x, mix_fn, mix_rows, mix_consts, gain, wq, q_gain, k_all, v_all, wo, layer, tq):
    b, t, d = x.shape
    m = k_all.shape[2]
    hd = d // CROSS_HEADS

    def row_spec(a):
        if a.ndim == 3:
            return pl.BlockSpec((1, tq, a.shape[-1]), lambda i, j: (i, j, 0))
        return pl.BlockSpec((1, a.shape[1], a.shape[2], tq), lambda i, j: (i, 0, 0, j))

    mix_specs = [row_spec(a) for a in mix_rows] + [_const_spec(a.shape) for a in mix_consts]
    return pl.pallas_call(
        functools.partial(_cross_kernel, n_mix=len(mix_specs), mix_fn=mix_fn),
        grid=(b, t // tq),
        in_specs=[pl.BlockSpec((1, tq, d), lambda i, j: (i, j, 0)),
                  *mix_specs,
                  _const_spec((1, d)),
                  _const_spec((d, d)),
                  _const_spec((1, hd)),
                  pl.BlockSpec((1, 1, m, d), lambda i, j: (layer, i, 0, 0)),
                  pl.BlockSpec((1, 1, m, d), lambda i, j: (layer, i, 0, 0)),
                  _const_spec((d, d))],
        out_specs=pl.BlockSpec((1, tq, d), lambda i, j: (i, j, 0)),
        out_shape=jax.ShapeDtypeStruct((b, t, d), F32),
        compiler_params=_params(2),
        name=f"cross_attn_{layer}",
    )(x, *mix_rows, *mix_consts, gain.reshape(1, d), _bf(wq), q_gain.reshape(1, hd), k_all, v_all, _bf(wo))


def _ffn_kernel(x_ref, xp_ref, xn_ref, g_ref, wup_ref, cw_ref, cb_ref, wdn_ref, o_ref,
                hn_ref, acc_ref):
    i = pl.program_id(1)
    nt = pl.num_programs(1)
    n_chunks = wdn_ref.shape[0] // FF_CHUNK
    x = x_ref[0]
    tm = x.shape[0]
    gain = g_ref[...]
    hn_ref[:tm] = _bf(_rms(x, gain))
    hn_ref[tm:] = _halo_rows(xp_ref, xn_ref, gain, i, nt)
    acc_ref[...] = jnp.zeros_like(acc_ref)

    def cols(idx):
        return slice(idx * FF_CHUNK, (idx + 1) * FF_CHUNK)

    def up_proj(idx):
        u = _dot_row_pieces(hn_ref[...], wup_ref[:, cols(idx)])
        return u[:tm], u[tm:]

    def conv(idx, u, uh):
        cw = cw_ref[:, cols(idx)]
        up = _shift_prev(u, uh[HALO - 1:HALO])
        un = _shift_next(u, uh[HALO:HALO + 1])
        return cb_ref[:, cols(idx)] + up * cw[0:1] + u * cw[1:2] + un * cw[2:3]

    ups = [up_proj(0), up_proj(n_chunks)]
    for c in range(n_chunks):
        nxt = [up_proj(c + 1), up_proj(c + 1 + n_chunks)] if c + 1 < n_chunks else None
        gate = conv(c, *ups[0])
        val = conv(c + n_chunks, *ups[1])
        acc_ref[...] += _dot_row_pieces(gate * _sigmoid(gate) * val, wdn_ref[cols(c), :])
        ups = nxt
    o_ref[0] = x + acc_ref[...]


def _conv_ffn(x, gain, w_up, conv_w, conv_b, w_down, tm):
    b, t, d = x.shape
    ff = w_down.shape[0]
    assert ff % FF_CHUNK == 0
    main, prev, nxt = _row_specs(tm, d, t)
    return pl.pallas_call(
        _ffn_kernel,
        grid=(b, t // tm),
        in_specs=[main, prev, nxt,
                  _const_spec((1, d)),
                  _const_spec((d, 2 * ff)),
                  _const_spec((CONV_WIDTH, 2 * ff)),
                  _const_spec((1, 2 * ff)),
                  _const_spec((ff, d))],
        out_specs=pl.BlockSpec((1, tm, d), lambda i, j: (i, j, 0)),
        out_shape=jax.ShapeDtypeStruct((b, t, d), F32),
        scratch_shapes=[pltpu.VMEM((tm + 2 * HALO, d), BF16),
                        pltpu.VMEM((tm, d), F32)],
        compiler_params=_params(2),
        name="conv_ffn",
    )(x, x, x, gain.reshape(1, d), _bf(w_up), conv_w, conv_b.reshape(1, 2 * ff), _bf(w_down))


def _ab_in_kernel(x_ref, xp_ref, xn_ref, g_ref, win_ref, mup_ref, mun_ref,
                  w0_ref, w2_ref, a0_ref, a2_ref, g2_ref, kk_ref, ka_ref, rk_ref, bd_ref,
                  vg_ref, ws_ref, sb_ref,
                  r_o, v_o, kk_o, lw0_o, lw1_o, be0_o, be1_o, k0_o, k1_o, g_o, bv_o, b_o):
    i = pl.program_id(1)
    nt = pl.num_programs(1)
    w = RWKV_WIDTH
    x = x_ref[0]
    tm = x.shape[0]
    gain = g_ref[...]
    hn = _bf(_rms(x, gain))
    hx = jnp.concatenate([hn, _halo_rows(xp_ref, xn_ref, gain, i, nt)], axis=0)
    zr = _dot(hx, win_ref[:, :RWKV_IN])
    za, zh = zr[:tm], zr[tm:]
    zp = _shift_prev(za, zh[HALO - 1:HALO])
    zn = _shift_next(za, zh[HALO:HALO + 1])
    mup, mun = mup_ref[...], mun_ref[...]
    zs = za * (1.0 - mup - mun) + mup * zp + mun * zn
    r = zs[:, 0:w]
    k = zs[:, w:2 * w]
    v = zs[:, 2 * w:3 * w]
    wl = jnp.tanh(zs[:, 3 * w:3 * w + LORA_W])
    al = zs[:, 3 * w + LORA_W:3 * w + 2 * LORA_W]
    gl = zs[:, 3 * w + 2 * LORA_W:]

    wpre = w0_ref[...] + _dot(wl, w2_ref[...])
    lw = -math.exp(-0.5) * _sigmoid(wpre)
    a = _sigmoid(a0_ref[...] + _dot(al, a2_ref[...]))
    g = _dot(_sigmoid(gl), g2_ref[...])

    bd = bd_ref[...]
    kk0 = k * kk_ref[...]
    kk = kk0 * lax.rsqrt(_dot(kk0 * kk0, bd) + KK_EPS)
    ka = ka_ref[...]
    k0 = k * (1.0 + (a[:, :w] - 1.0) * ka)
    k1 = k * (1.0 + (a[:, w:] - 1.0) * ka)
    bonus = _dot(r * rk_ref[...] * (k0 + k1), bd)

    r_o[0] = _bf(r)
    v_o[0] = _bf(v)
    kk_o[0] = _bf(kk)
    lw0_o[0] = lw[:, :w]
    lw1_o[0] = lw[:, w:]
    be0_o[0] = _bf(kk * a[:, :w])
    be1_o[0] = _bf(kk * a[:, w:])
    k0_o[0] = _bf(k0)
    k1_o[0] = _bf(k1)
    g_o[0] = _bf(g)
    bv_o[0] = _bf(bonus * v)

    zg = _dot(hn, win_ref[:, RWKV_IN:])
    zg = zg * (0.5 * (1.0 + jnp.tanh(math.sqrt(2.0 / math.pi) * (zg + 0.044715 * (zg * zg * zg)))))
    u = zg[:, :SGU_WIDTH]
    vn = _bf(_rms(zg[:, SGU_WIDTH:], vg_ref[...]))
    gd = SGU_WIDTH // SGU_GROUPS
    for n in range(tm // SGU_CHUNK):
        rows = slice(n * SGU_CHUNK, (n + 1) * SGU_CHUNK)
        for gi in range(SGU_GROUPS):
            cols = slice(gi * gd, (gi + 1) * gd)
            s = jnp.dot(ws_ref[gi], vn[rows, cols], preferred_element_type=F32) + sb_ref[gi]
            b_o[0, rows, cols] = _bf(u[rows, cols] * s)


def _ab_in(x, gain, w_in, mu_prev, mu_next, w0, w2, a0, a2, g2, k_k, k_a, r_k, v_gain, w_s, s_b, tm):
    b, t, d = x.shape
    w = RWKV_WIDTH
    n_in = w_in.shape[1]
    half = LORA_W // 2

    def lora_cat(m):
        z = jnp.zeros((half, w), m.dtype)
        return _bf(jnp.concatenate([jnp.concatenate([m[0], z], axis=1),
                                    jnp.concatenate([z, m[1]], axis=1)], axis=0))

    head = jnp.arange(w) // HEAD_DIM
    bd = _bf(head[:, None] == head[None, :])
    sb = jnp.broadcast_to(s_b[:, :, None], (SGU_GROUPS, SGU_CHUNK, SGU_WIDTH // SGU_GROUPS))
    main, prev, nxt = _row_specs(tm, d, t)
    ospec = pl.BlockSpec((1, tm, w), lambda i, j: (i, j, 0))
    return pl.pallas_call(
        _ab_in_kernel,
        grid=(b, t // tm),
        in_specs=[main, prev, nxt,
                  _const_spec((1, d)),
                  _const_spec((d, n_in)),
                  _const_spec((1, RWKV_IN)), _const_spec((1, RWKV_IN)),
                  _const_spec((1, 2 * w)), _const_spec((LORA_W, 2 * w)),
                  _const_spec((1, 2 * w)), _const_spec((LORA_W, 2 * w)),
                  _const_spec((LORA_W, w)),
                  _const_spec((1, w)), _const_spec((1, w)), _const_spec((1, w)),
                  _const_spec((w, w)),
                  _const_spec((1, SGU_WIDTH)),
                  _const_spec((SGU_GROUPS, SGU_CHUNK, SGU_CHUNK)),
                  _const_spec((SGU_GROUPS, SGU_CHUNK, SGU_WIDTH // SGU_GROUPS))],
        out_specs=[ospec] * 12,
        out_shape=[jax.ShapeDtypeStruct((b, t, w), dt) for dt in AB_IN_OUT_DTYPES],
        compiler_params=_params(2),
        name="ab_in",
    )(x, x, x, gain.reshape(1, d), _bf(w_in), mu_prev.reshape(1, -1), mu_next.reshape(1, -1),
      w0.reshape(1, 2 * w), lora_cat(w2), a0.reshape(1, 2 * w), lora_cat(a2), _bf(g2),
      k_k.reshape(1, w), k_a.reshape(1, w), r_k.reshape(1, w), bd,
      v_gain.reshape(1, -1), _bf(w_s), sb)


def _stack_heads(x):
    even = (_iota(x.shape, 1) & (LANES - 1)) < HEAD_DIM
    return jnp.concatenate([jnp.where(even, x, 0.0), jnp.where(even, 0.0, x)], axis=0)


def _rwkv_prep(r, v, kk, lw, be, kd, reverse):
    c = r.shape[0]
    rr = _iota((c, LANES), 0)
    cc = _iota((c, LANES), 1) & (HEAD_DIM - 1)
    if reverse:
        strict, incl = cc > rr, cc >= rr
    else:
        strict, incl = cc < rr, cc <= rr
    tri = _bf(incl[:, :c])

    l1 = _bf(lw)
    l2 = _bf(lw - l1.astype(F32))
    cum = jnp.dot(tri, l1, preferred_element_type=F32) + jnp.dot(tri, l2, preferred_element_type=F32)
    tot = cum[0:1] if reverse else cum[c - 1:c]
    e_neg = jnp.exp(-cum)
    e_end = jnp.exp(tot - cum)
    return dict(ag=-kk * jnp.exp(cum - lw), rg=r * jnp.exp(cum), bi=be * e_neg, ki=kd * e_neg,
                bee=be * e_end, kee=kd * e_end, gam=jnp.exp(tot), v=v, strict=strict, incl=incl)


def _rwkv_local(chunks):
    c = chunks[0]["v"].shape[0]
    n_tiles = chunks[0]["v"].shape[1] // LANES
    r128 = _iota((LANES, LANES), 0)
    c128 = _iota((LANES, LANES), 1)
    same_head = (r128 < HEAD_DIM) == (c128 < HEAD_DIM)
    diag = r128 == c128
    ctx = [(ch, slice(p * LANES, (p + 1) * LANES)) for ch in chunks for p in range(n_tiles)]

    aa = [_dot_nt(jnp.concatenate([ch["ag"][:, sl], ch["rg"][:, sl]], axis=0),
                  jnp.concatenate([_stack_heads(ch["bi"][:, sl]), _stack_heads(ch["ki"][:, sl])], axis=0))
          for ch, sl in ctx]
    a_ab = [jnp.where(ch["strict"], m[:c, :LANES], 0.0) for (ch, _), m in zip(ctx, aa)]
    a_rb = [jnp.where(ch["incl"], m[c:, :LANES], 0.0) for (ch, _), m in zip(ctx, aa)]
    a_k = [jnp.concatenate([jnp.where(ch["strict"], m[:c, LANES:], 0.0),
                            jnp.where(ch["incl"], m[c:, LANES:], 0.0)], axis=0) for (ch, _), m in zip(ctx, aa)]
    xv = [_dot(m, _stack_heads(ch["v"][:, sl])) for (ch, sl), m in zip(ctx, a_k)]
    x = [jnp.concatenate([ch["ag"][:, sl], m[:c]], axis=1) for (ch, sl), m in zip(ctx, xv)]
    def block_diag(pi):
        return jnp.where(same_head, jnp.concatenate([pi, pi], axis=0), 0.0)

    n_round = int(math.log2(c))
    rr = a_ab
    pw = [_dot(pi, block_diag(pi)) for pi in a_ab]
    for k in range(1, n_round):
        if k < n_round - 1:
            res = [_dot(jnp.concatenate([ri, pi], axis=0), block_diag(pi)) for ri, pi in zip(rr, pw)]
            rr = [ri + pi + m[:c] for ri, pi, m in zip(rr, pw, res)]
            pw = [m[c:] for m in res]
        else:
            rr = [ri + pi + _dot(ri, block_diag(pi)) for ri, pi in zip(rr, pw)]
    x = [xi + _dot(ri, _stack_heads(xi)) for xi, ri in zip(x, rr)]
    ry = [jnp.concatenate([ch["rg"][:, sl], m[c:]], axis=1) + _dot(ai, _stack_heads(xi))
          for (ch, sl), m, ai, xi in zip(ctx, xv, a_rb, x)]
    mg = [_dot(jnp.concatenate([ch["bee"][:, sl], ch["kee"][:, sl]], axis=0).T,
               jnp.concatenate([xi, jnp.concatenate([jnp.zeros_like(ch["v"][:, sl]), ch["v"][:, sl]], axis=1)],
                               axis=0))
          for (ch, sl), xi in zip(ctx, x)]
    out = []
    for (ch, sl), ryi, mgi in zip(ctx, ry, mg):
        gam_col = jnp.sum(jnp.where(diag, ch["gam"][:, sl], 0.0), axis=1, keepdims=True)
        out.append((ryi[:, :LANES], ryi[:, LANES:], jnp.where(same_head, mgi[:, :LANES], 0.0),
                    jnp.where(same_head, mgi[:, LANES:], 0.0), gam_col))
    return [out[i * n_tiles:(i + 1) * n_tiles] for i in range(len(chunks))]


def _rwkv_kernel(rf, vf, kkf, lwf, bef, kf, rb, vb, kkb, lwb, beb, kb, yf_o, yb_o, zf_ref, zb_ref):
    @pl.when(pl.program_id(1) == 0)
    def _():
        zf_ref[...] = jnp.zeros_like(zf_ref)
        zb_ref[...] = jnp.zeros_like(zb_ref)

    c = RWKV_CHUNK
    n = rf.shape[1] // c
    n_tiles = rf.shape[2] // LANES
    chunks = []
    for refs, reverse in (((rf, vf, kkf, lwf, bef, kf), False), ((rb, vb, kkb, lwb, beb, kb), True)):
        for ci in range(n):
            chunks.append(_rwkv_prep(*[ref[0, ci * c:(ci + 1) * c, :].astype(F32) for ref in refs], reverse))
    local = _rwkv_local(chunks)

    for d, (z_ref, y_o) in enumerate(((zf_ref, yf_o), (zb_ref, yb_o))):
        order = range(n) if d == 0 else range(n - 1, -1, -1)
        zs = [z_ref[p] for p in range(n_tiles)]
        for ci in order:
            ys = []
            for p in range(n_tiles):
                rt, yl, mt, gt, gam_col = local[d * n + ci][p]
                yz = _dot(jnp.concatenate([rt, mt], axis=0), zs[p])
                ys.append(yz[:c] + yl)
                zs[p] = gam_col * zs[p] + yz[c:] + gt
            y_o[0, ci * c:(ci + 1) * c, :] = jnp.concatenate(ys, axis=1)
        for p in range(n_tiles):
            z_ref[p] = zs[p]


def _rwkv_scan(r, v, kk, lw0, lw1, be0, be1, k0, k1):
    b, t, w = r.shape
    c = RWKV_CHUNK * RWKV_STEP_CHUNKS
    nc = t // c
    fwd = pl.BlockSpec((1, c, w), lambda i, j: (i, j, 0))
    bwd = pl.BlockSpec((1, c, w), lambda i, j: (i, nc - 1 - j, 0))
    out = jax.ShapeDtypeStruct((b, t, w), F32)
    zshape = pltpu.VMEM((w // LANES, LANES, LANES), F32)
    return pl.pallas_call(
        _rwkv_kernel,
        grid=(b, nc),
        in_specs=[fwd] * 6 + [bwd] * 6,
        out_specs=[fwd, bwd],
        out_shape=[out, out],
        scratch_shapes=[zshape, zshape],
        compiler_params=_params(2),
        name="rwkv_scan",
    )(r, v, kk, lw0, be0, k0, r, v, kk, lw1, be1, k1)


def _diff_in_kernel(x_ref, g_ref, w_ref, qgt_ref, kg_ref, bd_ref, qt_o, k_o, vt_o):
    x = x_ref[0]
    d = x.shape[-1]
    hn = _bf(_rms(x, g_ref[...]))
    z = _dot(hn, w_ref[...])
    bd = bd_ref[...]
    half = bd.shape[0]

    def qk_norm(y, gain, scale):
        parts = []
        for j in range(d // half):
            yj = y[:, j * half:(j + 1) * half]
            ms = _dot(yj * yj, bd) * (1.0 / HEAD_DIM)
            parts.append(yj * lax.rsqrt(ms + NORM_EPS) * gain * scale)
        return jnp.concatenate(parts, axis=1)

    k_o[0] = _bf(qk_norm(z[:, d:2 * d], kg_ref[...], 1.0))
    tm = x.shape[0]
    q_scale = qgt_ref[...] * (LOG2E / math.sqrt(HEAD_DIM))
    for h in range(d // DIFF_V_DIM):
        cols = slice(h * DIFF_V_DIM, (h + 1) * DIFF_V_DIM)
        qt = z[:, cols].T.reshape(2, HEAD_DIM, tm)
        inv = lax.rsqrt(jnp.mean(qt * qt, axis=1, keepdims=True) + NORM_EPS)
        qt_o[0, h] = _bf((qt * inv).reshape(DIFF_V_DIM, tm) * q_scale)
        vt_o[0, h, :DIFF_V_DIM, :] = _bf(z[:, 2 * d:][:, cols].T)
        vt_o[0, h, DIFF_V_DIM:, :] = jnp.ones((ONES_ROWS, tm), BF16)


def _diff_in(x, gain, w_in, q_gain, k_gain, tm):
    b, t, d = x.shape
    half = 512
    nh = d // DIFF_V_DIM
    head = jnp.arange(half) // HEAD_DIM
    bd = _bf(head[:, None] == head[None, :])
    xs = pl.BlockSpec((1, tm, d), lambda i, j: (i, j, 0))
    out = jax.ShapeDtypeStruct((b, t, d), BF16)
    vt_rows = DIFF_V_DIM + ONES_ROWS
    return pl.pallas_call(
        _diff_in_kernel,
        grid=(b, t // tm),
        in_specs=[xs, _const_spec((1, d)), _const_spec((d, 3 * d)),
                  _const_spec((DIFF_V_DIM, 1)), _const_spec((1, half)), _const_spec((half, half))],
        out_specs=[pl.BlockSpec((1, nh, DIFF_V_DIM, tm), lambda i, j: (i, 0, 0, j)), xs,
                   pl.BlockSpec((1, nh, vt_rows, tm), lambda i, j: (i, 0, 0, j))],
        out_shape=[jax.ShapeDtypeStruct((b, nh, DIFF_V_DIM, t), BF16), out,
                   jax.ShapeDtypeStruct((b, nh, vt_rows, t), BF16)],
        compiler_params=_params(2),
        name="diff_in",
    )(x, gain.reshape(1, d), _bf(w_in), jnp.tile(q_gain, 2).reshape(DIFF_V_DIM, 1),
      jnp.tile(k_gain, half // HEAD_DIM).reshape(1, half), bd)


def _diff_attn_kernel(qt_ref, k_ref, vt_ref, hc_ref, lam_ref, og_ref, o_ref, bias_ref, p_ref, *s_scratch,
                      lambda_init, bounded):
    t, tq = bias_ref.shape

    @pl.when(pl.program_id(2) == 0)
    def _():
        dist = jnp.abs((_iota((t, tq), 0) - pl.program_id(1) * tq - _iota((t, tq), 1)).astype(F32))
        bias = dist * (-LOG2E * hc_ref[0, 0:1, 0:1])
        bias_ref[...] = bias - hc_ref[0, 1:2, 0:1] if bounded else bias

    q = qt_ref[0, 0]
    first = _iota(q.shape, 0) < HEAD_DIM
    zero = jnp.zeros_like(q)
    kb = min(KEY_BLOCK, t)
    qb = min(QUERY_BLOCK, tq)
    sub = HALO

    qcs = (jnp.where(first, q, zero), jnp.where(first, zero, q))
    secs = [(c, slice(i * qb, (i + 1) * qb)) for i in range(tq // qb) for c in range(2)]
    n_kb = t // kb
    outs = []
    lv = lam_ref[...]
    lam = (jnp.exp(jnp.sum(lv[0:1] * lv[1:2], axis=-1, keepdims=True))
           - jnp.exp(jnp.sum(lv[2:3] * lv[3:4], axis=-1, keepdims=True)) + lambda_init)

    def finish(qs):
        acc1, acc0 = outs.pop(), outs.pop()
        o0, s0 = acc0[:DIFF_V_DIM], acc0[DIFF_V_DIM:DIFF_V_DIM + 1]
        o1, s1 = acc1[:DIFF_V_DIM], acc1[DIFF_V_DIM:DIFF_V_DIM + 1]
        o = o0 * (1.0 / s0) - o1 * (lam / s1)
        inv = lax.rsqrt(jnp.mean(o * o, axis=0, keepdims=True) + NORM_EPS)
        o_ref[0, 0, :, qs] = _bf(o * inv * (og_ref[...] * (1.0 - lambda_init)))

    def logits(sec, j):
        c, qs = sec
        ks = slice(j * kb, (j + 1) * kb)
        return jnp.dot(k_ref[0, ks, :], qcs[c][:, qs], preferred_element_type=F32) + bias_ref[ks, qs]

    if bounded:
        for sec in secs:
            c, qs = sec
            for j in range(n_kb):
                p_ref[c, j * kb:(j + 1) * kb, qs] = _bf(jnp.exp2(logits(sec, j)))
            outs.append(jnp.dot(vt_ref[0, 0], p_ref[c, :, qs], preferred_element_type=F32))
            if c == 1:
                finish(qs)
    else:
        s_ref, = s_scratch

        def score_block(sec, j, m):
            c, qs = sec
            s = logits(sec, j)
            s_ref[c, j * kb:(j + 1) * kb, qs] = s
            return jnp.maximum(m, jnp.max(s.reshape(kb // sub, sub, qb), axis=0))

        def prob_block(sec, j, m):
            c, qs = sec
            ks = slice(j * kb, (j + 1) * kb)
            p_ref[c, ks, qs] = _bf(jnp.exp2(s_ref[c, ks, qs] - m))

        m_init = jnp.full((sub, qb), -jnp.inf, F32)
        m = m_init
        for j in range(n_kb):
            m = score_block(secs[0], j, m)
        m_prev = jnp.max(m, axis=0, keepdims=True)
        for n in range(1, len(secs) + 1):
            m = m_init
            for j in range(n_kb):
                if n < len(secs):
                    m = score_block(secs[n], j, m)
                prob_block(secs[n - 1], j, m_prev)
            m_prev = jnp.max(m, axis=0, keepdims=True)
            c, qs = secs[n - 1]
            outs.append(jnp.dot(vt_ref[0, 0], p_ref[c, :, qs], preferred_element_type=F32))
            if c == 1:
                finish(qs)


def _diff_attention(qt, k, vt, q_gain, k_gain, lam_vecs, out_gain, lambda_init, tq):
    b, t, d = k.shape
    nh = d // DIFF_V_DIM
    slopes = 2.0 ** (-8.0 * jnp.arange(1, nh + 1, dtype=F32) / nh)
    bound = 1.02 * math.sqrt(HEAD_DIM) * LOG2E * jnp.max(jnp.abs(q_gain)) * jnp.max(jnp.abs(k_gain))
    head_consts = jnp.stack([jnp.broadcast_to(slopes[:, None], (nh, LANES)),
                             jnp.broadcast_to(bound, (nh, LANES))], axis=1)

    def run(bounded):
        scratch = [pltpu.VMEM((t, tq), F32), pltpu.VMEM((2, t, tq), BF16)]
        if not bounded:
            scratch.append(pltpu.VMEM((2, t, tq), F32))
        return pl.pallas_call(
            functools.partial(_diff_attn_kernel, lambda_init=lambda_init, bounded=bounded),
            grid=(nh, t // tq, b),
            in_specs=[pl.BlockSpec((1, 1, DIFF_V_DIM, tq), lambda h, j, i: (i, h, 0, j)),
                      pl.BlockSpec((1, t, DIFF_V_DIM), lambda h, j, i: (i, 0, h)),
                      pl.BlockSpec((1, 1, vt.shape[2], t), lambda h, j, i: (i, h, 0, 0)),
                      pl.BlockSpec((1, 2, LANES), lambda h, j, i: (h, 0, 0)),
                      _const_spec((4, HEAD_DIM)),
                      _const_spec((DIFF_V_DIM, 1))],
            out_specs=pl.BlockSpec((1, 1, DIFF_V_DIM, tq), lambda h, j, i: (i, h, 0, j)),
            out_shape=jax.ShapeDtypeStruct((b, nh, DIFF_V_DIM, t), BF16),
            scratch_shapes=scratch,
            compiler_params=_params(3),
            name="diff_attn_bounded" if bounded else "diff_attn",
        )(qt, k, vt, head_consts, lam_vecs, out_gain.reshape(DIFF_V_DIM, 1))

    return lax.cond(2.0 * bound <= SAFE_EXP2_SPAN, lambda: run(True), lambda: run(False))


def kernel(x, mem, mem_norm, mix_norm, cross_norm, ffn_norm, ab_w_in, ab_shift_prev, ab_shift_next, rwkv_w0, rwkv_w2, rwkv_a0, rwkv_a2, rwkv_g2, rwkv_k_k, rwkv_k_a, rwkv_r_k, rwkv_out_gain, sgu_v_gain, sgu_w_s, sgu_b, ab_w_out, diff_w_in, diff_q_gain, diff_k_gain, diff_lambda_q1, diff_lambda_k1, diff_lambda_q2, diff_lambda_k2, diff_out_gain, diff_w_out, cross_wq, cross_wkv, cross_q_gain, cross_k_gain, cross_wo, ffn_w_up, ffn_conv_w, ffn_conv_b, ffn_w_down):
    b, t, d = x.shape
    depth = mix_norm.shape[0]
    tm = min(512, t)
    tq = min(512, t)
    assert t % tm == 0 and t % (RWKV_CHUNK * RWKV_STEP_CHUNKS) == 0 and tm % SGU_CHUNK == 0

    k_all, v_all = _mem_kv(mem, mem_norm, cross_wkv, cross_k_gain)
    for l in range(depth):
        if l % 2 == 0:
            e = l // 2
            (r, v, kk, lw0, lw1, be0, be1, k0, k1, g, bv, b_out) = _ab_in(
                x, mix_norm[l], ab_w_in[e], ab_shift_prev[e], ab_shift_next[e],
                rwkv_w0[e], rwkv_w2[e], rwkv_a0[e], rwkv_a2[e], rwkv_g2[e],
                rwkv_k_k[e], rwkv_k_a[e], rwkv_r_k[e], sgu_v_gain[e], sgu_w_s[e], sgu_b[e], tm)
            yf, yb = _rwkv_scan(r, v, kk, lw0, lw1, be0, be1, k0, k1)
            head = jnp.arange(RWKV_WIDTH) // HEAD_DIM
            mix = (_mix_rwkv_sgu, (yf, yb, bv, g, b_out),
                   (rwkv_out_gain[e].reshape(1, RWKV_WIDTH), _bf(head[:, None] == head[None, :]), _bf(ab_w_out[e])))
        else:
            o = l // 2
            lambda_init = 0.8 - 0.6 * math.exp(-0.3 * l)
            q, k, vt = _diff_in(x, mix_norm[l], diff_w_in[o], diff_q_gain[o], diff_k_gain[o], tm)
            lam_vecs = jnp.stack([diff_lambda_q1[o], diff_lambda_k1[o], diff_lambda_q2[o], diff_lambda_k2[o]])
            c_out = _diff_attention(q, k, vt, diff_q_gain[o], diff_k_gain[o], lam_vecs, diff_out_gain[o],
                                    lambda_init, tq)
            mix = (_mix_proj, (c_out,), (_bf(diff_w_out[o]),))
        x = _cross_attention(x, *mix, cross_norm[l], cross_wq[l], cross_q_gain[l], k_all, v_all, cross_wo[l], l, tm)
        x = _conv_ffn(x, ffn_norm[l], ffn_w_up[l], ffn_conv_w[l], ffn_conv_b[l], ffn_w_down[l], tm)
    return x
```

```python
import functools
import math

import jax
import jax.numpy as jnp
from jax import lax
from jax.experimental import pallas as pl
from jax.experimental.pallas import tpu as pltpu

NORM_EPS = 1e-6
KK_EPS = 1e-12
RWKV_HEADS = 8
HEAD_DIM = 64
RWKV_WIDTH = RWKV_HEADS * HEAD_DIM
LORA_W = 128
RWKV_IN = 3 * RWKV_WIDTH + 3 * LORA_W
SGU_WIDTH = 512
SGU_GROUPS = 4
SGU_CHUNK = 128
DIFF_HEADS = 8
DIFF_V_DIM = 2 * HEAD_DIM
CROSS_HEADS = 4
CONV_WIDTH = 3
RWKV_CHUNK = 64
RWKV_STEP_CHUNKS = 2
LANES = 128
HALO = 8
FF_CHUNK = 256
MXU_ROWS = 256
KEY_BLOCK = 128
QUERY_BLOCK = 256
ONES_ROWS = 16
SAFE_EXP2_SPAN = 100.0
VMEM_LIMIT = 56 * 1024 * 1024
LOG2E = math.log2(math.e)

F32 = jnp.float32
BF16 = jnp.bfloat16
AB_IN_OUT_DTYPES = (BF16, BF16, BF16, F32, F32, BF16, BF16, BF16, BF16, BF16, BF16, BF16)


def _bf(x):
    return x.astype(BF16)


def _dot(a, b):
    return jnp.dot(_bf(a), _bf(b), preferred_element_type=F32)


def _dot_nt(a, b):
    return lax.dot_general(_bf(a), _bf(b), (((1,), (1,)), ((), ())), preferred_element_type=F32)


def _dot_row_pieces(a, b):
    m = a.shape[0]
    starts = list(range(0, m - m % MXU_ROWS, MXU_ROWS)) or [0]
    ends = starts[1:] + [m]
    return jnp.concatenate([_dot(a[s:e], b) for s, e in zip(starts, ends)], axis=0)


def _rms(x, gain):
    return x * lax.rsqrt(jnp.mean(x * x, axis=-1, keepdims=True) + NORM_EPS) * gain


def _sigmoid(x):
    return 1.0 / (1.0 + jnp.exp(-x))


def _iota(shape, dim):
    return lax.broadcasted_iota(jnp.int32, shape, dim)


def _shift_prev(u, halo_row):
    rolled = pltpu.roll(u, 1, 0)
    head = rolled[:HALO]
    head = jnp.where(_iota(head.shape, 0) == 0, halo_row, head)
    return jnp.concatenate([head, rolled[HALO:]], axis=0)


def _shift_next(u, halo_row):
    n = u.shape[0]
    rolled = pltpu.roll(u, n - 1, 0)
    tail = rolled[n - HALO:]
    tail = jnp.where(_iota(tail.shape, 0) == HALO - 1, halo_row, tail)
    return jnp.concatenate([rolled[:n - HALO], tail], axis=0)


def _halo_rows(xp_ref, xn_ref, gain, i, nt):
    xh = jnp.concatenate([xp_ref[0], xn_ref[0]], axis=0)
    hh = _rms(xh, gain)
    row = _iota(hh.shape, 0)
    has_prev = jnp.where(i > 0, 1.0, 0.0)
    has_next = jnp.where(i < nt - 1, 1.0, 0.0)
    return _bf(hh * jnp.where(row < HALO, has_prev, has_next))


def _const_spec(shape):
    zeros = (0,) * len(shape)
    return pl.BlockSpec(shape, lambda *_: zeros, pipeline_mode=pl.Buffered(1))


def _params(n_axes):
    return pltpu.CompilerParams(dimension_semantics=("arbitrary",) * n_axes,
                                vmem_limit_bytes=VMEM_LIMIT)


def _row_specs(tm, d, t):
    nb = tm // HALO
    last = t // HALO - 1
    main = pl.BlockSpec((1, tm, d), lambda b, i: (b, i, 0))
    prev = pl.BlockSpec((1, HALO, d), lambda b, i: (b, jnp.maximum(i * nb - 1, 0), 0))
    nxt = pl.BlockSpec((1, HALO, d), lambda b, i: (b, jnp.minimum((i + 1) * nb, last), 0))
    return main, prev, nxt


def _mem_kv_kernel(mem_ref, mnorm_ref, wkv_ref, kg_ref, k_ref, v_ref):
    d = mem_ref.shape[-1]
    hd = d // CROSS_HEADS
    mn = _rms(mem_ref[0], mnorm_ref[...])
    kv = jnp.dot(_bf(mn), wkv_ref[0], preferred_element_type=F32)
    for h in range(CROSS_HEADS):
        kh = _rms(kv[:, h * hd:(h + 1) * hd], kg_ref[0]) * (1.0 / math.sqrt(hd))
        k_ref[0, 0, :, h * hd:(h + 1) * hd] = _bf(kh)
    v_ref[0, 0] = _bf(kv[:, d:])


def _mem_kv(mem, mem_norm, wkv, k_gain):
    b, m, d = mem.shape
    nl = wkv.shape[0]
    out = jax.ShapeDtypeStruct((nl, b, m, d), BF16)
    return pl.pallas_call(
        _mem_kv_kernel,
        grid=(nl, b),
        in_specs=[pl.BlockSpec((1, m, d), lambda l, i: (i, 0, 0)),
                  pl.BlockSpec((1, d), lambda l, i: (0, 0)),
                  pl.BlockSpec((1, d, 2 * d), lambda l, i: (l, 0, 0)),
                  pl.BlockSpec((1, 1, d // CROSS_HEADS), lambda l, i: (l, 0, 0))],
        out_specs=[pl.BlockSpec((1, 1, m, d), lambda l, i: (l, i, 0, 0)),
                   pl.BlockSpec((1, 1, m, d), lambda l, i: (l, i, 0, 0))],
        out_shape=[out, out],
        compiler_params=_params(2),
        name="mem_kv",
    )(mem, mem_norm.reshape(1, d), _bf(wkv), k_gain.reshape(nl, 1, -1))


def _mix_rwkv_sgu(x, yf_ref, yb_ref, bv_ref, g_ref, b_ref, og_ref, bd_ref, wo_ref):
    y = yf_ref[0] + yb_ref[0]
    ms = _dot(y * y, bd_ref[...]) * (1.0 / HEAD_DIM)
    yn = y * lax.rsqrt(ms + NORM_EPS) * og_ref[...]
    a_out = (yn + bv_ref[0].astype(F32)) * g_ref[0].astype(F32)
    cat = jnp.concatenate([_bf(a_out), b_ref[0]], axis=1)
    return x + _dot(cat, wo_ref[...])


def _mix_proj(x, at_ref, w_ref):
    a = jnp.concatenate([at_ref[0, h].astype(F32).T for h in range(at_ref.shape[1])], axis=1)
    return x + _dot(a, w_ref[...])


def _cross_kernel(*refs, n_mix, mix_fn):
    x_ref, mix_refs = refs[0], refs[1:1 + n_mix]
    g_ref, wq_ref, qg_ref, k_ref, v_ref, wo_ref, o_ref = refs[1 + n_mix:]
    x = mix_fn(x_ref[0], *mix_refs)
    d = x.shape[-1]
    hd = d // CROSS_HEADS
    hn = _bf(_rms(x, g_ref[...]))
    q = _dot(hn, wq_ref[...])
    outs = []
    for h in range(CROSS_HEADS):
        sl = slice(h * hd, (h + 1) * hd)
        qh = _rms(q[:, sl], qg_ref[...])
        logits = _dot_nt(qh, k_ref[0, 0, :, sl])
        p = jnp.exp(logits - jnp.max(logits, axis=-1, keepdims=True))
        s = jnp.sum(p, axis=-1, keepdims=True)
        outs.append(_dot(p, v_ref[0, 0, :, sl]) / s)
    o = jnp.concatenate(outs, axis=1)
    o_ref[0] = x + _dot(o, wo_ref[...])


def _cross_attention(x, mix_fn, mix_rows, mix_consts, gain, wq, q_gain, k_all, v_all, wo, layer, tq):
    b, t, d = x.shape
    m = k_all.shape[2]
    hd = d // CROSS_HEADS

    def row_spec(a):
        if a.ndim == 3:
            return pl.BlockSpec((1, tq, a.shape[-1]), lambda i, j: (i, j, 0))
        return pl.BlockSpec((1, a.shape[1], a.shape[2], tq), lambda i, j: (i, 0, 0, j))

    mix_specs = [row_spec(a) for a in mix_rows] + [_const_spec(a.shape) for a in mix_consts]
    return pl.pallas_call(
        functools.partial(_cross_kernel, n_mix=len(mix_specs), mix_fn=mix_fn),
        grid=(b, t // tq),
        in_specs=[pl.BlockSpec((1, tq, d), lambda i, j: (i, j, 0)),
                  *mix_specs,
                  _const_spec((1, d)),
                  _const_spec((d, d)),
                  _const_spec((1, hd)),
                  pl.BlockSpec((1, 1, m, d), lambda i, j: (layer, i, 0, 0)),
                  pl.BlockSpec((1, 1, m, d), lambda i, j: (layer, i, 0, 0)),
                  _const_spec((d, d))],
        out_specs=pl.BlockSpec((1, tq, d), lambda i, j: (i, j, 0)),
        out_shape=jax.ShapeDtypeStruct((b, t, d), F32),
        compiler_params=_params(2),
        name=f"cross_attn_{layer}",
    )(x, *mix_rows, *mix_consts, gain.reshape(1, d), _bf(wq), q_gain.reshape(1, hd), k_all, v_all, _bf(wo))


def _ffn_kernel(x_ref, xp_ref, xn_ref, g_ref, wup_ref, cw_ref, cb_ref, wdn_ref, o_ref,
                hn_ref, acc_ref):
    i = pl.program_id(1)
    nt = pl.num_programs(1)
    n_chunks = wdn_ref.shape[0] // FF_CHUNK
    x = x_ref[0]
    tm = x.shape[0]
    gain = g_ref[...]
    hn_ref[:tm] = _bf(_rms(x, gain))
    hn_ref[tm:] = _halo_rows(xp_ref, xn_ref, gain, i, nt)
    acc_ref[...] = jnp.zeros_like(acc_ref)

    def cols(idx):
        return slice(idx * FF_CHUNK, (idx + 1) * FF_CHUNK)

    def up_proj(idx):
        u = _dot_row_pieces(hn_ref[...], wup_ref[:, cols(idx)])
        return u[:tm], u[tm:]

    def conv(idx, u, uh):
        cw = cw_ref[:, cols(idx)]
        up = _shift_prev(u, uh[HALO - 1:HALO])
        un = _shift_next(u, uh[HALO:HALO + 1])
        return cb_ref[:, cols(idx)] + up * cw[0:1] + u * cw[1:2] + un * cw[2:3]

    ups = [up_proj(0), up_proj(n_chunks)]
    for c in range(n_chunks):
        nxt = [up_proj(c + 1), up_proj(c + 1 + n_chunks)] if c + 1 < n_chunks else None
        gate = conv(c, *ups[0])
        val = conv(c + n_chunks, *ups[1])
        acc_ref[...] += _dot_row_pieces(gate * _sigmoid(gate) * val, wdn_ref[cols(c), :])
        ups = nxt
    o_ref[0] = x + acc_ref[...]


def _conv_ffn(x, gain, w_up, conv_w, conv_b, w_down, tm):
    b, t, d = x.shape
    ff = w_down.shape[0]
    assert ff % FF_CHUNK == 0
    main, prev, nxt = _row_specs(tm, d, t)
    return pl.pallas_call(
        _ffn_kernel,
        grid=(b, t // tm),
        in_specs=[main, prev, nxt,
                  _const_spec((1, d)),
                  _const_spec((d, 2 * ff)),
                  _const_spec((CONV_WIDTH, 2 * ff)),
                  _const_spec((1, 2 * ff)),
                  _const_spec((ff, d))],
        out_specs=pl.BlockSpec((1, tm, d), lambda i, j: (i, j, 0)),
        out_shape=jax.ShapeDtypeStruct((b, t, d), F32),
        scratch_shapes=[pltpu.VMEM((tm + 2 * HALO, d), BF16),
                        pltpu.VMEM((tm, d), F32)],
        compiler_params=_params(2),
        name="conv_ffn",
    )(x, x, x, gain.reshape(1, d), _bf(w_up), conv_w, conv_b.reshape(1, 2 * ff), _bf(w_down))


def _ab_in_kernel(x_ref, xp_ref, xn_ref, g_ref, win_ref, mup_ref, mun_ref,
                  w0_ref, w2_ref, a0_ref, a2_ref, g2_ref, kk_ref, ka_ref, rk_ref, bd_ref,
                  vg_ref, ws_ref, sb_ref,
                  r_o, v_o, kk_o, lw0_o, lw1_o, be0_o, be1_o, k0_o, k1_o, g_o, bv_o, b_o):
    i = pl.program_id(1)
    nt = pl.num_programs(1)
    w = RWKV_WIDTH
    x = x_ref[0]
    tm = x.shape[0]
    gain = g_ref[...]
    hn = _bf(_rms(x, gain))
    hx = jnp.concatenate([hn, _halo_rows(xp_ref, xn_ref, gain, i, nt)], axis=0)
    zr = _dot(hx, win_ref[:, :RWKV_IN])
    za, zh = zr[:tm], zr[tm:]
    zp = _shift_prev(za, zh[HALO - 1:HALO])
    zn = _shift_next(za, zh[HALO:HALO + 1])
    mup, mun = mup_ref[...], mun_ref[...]
    zs = za * (1.0 - mup - mun) + mup * zp + mun * zn
    r = zs[:, 0:w]
    k = zs[:, w:2 * w]
    v = zs[:, 2 * w:3 * w]
    wl = jnp.tanh(zs[:, 3 * w:3 * w + LORA_W])
    al = zs[:, 3 * w + LORA_W:3 * w + 2 * LORA_W]
    gl = zs[:, 3 * w + 2 * LORA_W:]

    wpre = w0_ref[...] + _dot(wl, w2_ref[...])
    lw = -math.exp(-0.5) * _sigmoid(wpre)
    a = _sigmoid(a0_ref[...] + _dot(al, a2_ref[...]))
    g = _dot(_sigmoid(gl), g2_ref[...])

    bd = bd_ref[...]
    kk0 = k * kk_ref[...]
    kk = kk0 * lax.rsqrt(_dot(kk0 * kk0, bd) + KK_EPS)
    ka = ka_ref[...]
    k0 = k * (1.0 + (a[:, :w] - 1.0) * ka)
    k1 = k * (1.0 + (a[:, w:] - 1.0) * ka)
    bonus = _dot(r * rk_ref[...] * (k0 + k1), bd)

    r_o[0] = _bf(r)
    v_o[0] = _bf(v)
    kk_o[0] = _bf(kk)
    lw0_o[0] = lw[:, :w]
    lw1_o[0] = lw[:, w:]
    be0_o[0] = _bf(kk * a[:, :w])
    be1_o[0] = _bf(kk * a[:, w:])
    k0_o[0] = _bf(k0)
    k1_o[0] = _bf(k1)
    g_o[0] = _bf(g)
    bv_o[0] = _bf(bonus * v)

    zg = _dot(hn, win_ref[:, RWKV_IN:])
    zg = zg * (0.5 * (1.0 + jnp.tanh(math.sqrt(2.0 / math.pi) * (zg + 0.044715 * (zg * zg * zg)))))
    u = zg[:, :SGU_WIDTH]
    vn = _bf(_rms(zg[:, SGU_WIDTH:], vg_ref[...]))
    gd = SGU_WIDTH // SGU_GROUPS
    for n in range(tm // SGU_CHUNK):
        rows = slice(n * SGU_CHUNK, (n + 1) * SGU_CHUNK)
        for gi in range(SGU_GROUPS):
            cols = slice(gi * gd, (gi + 1) * gd)
            s = jnp.dot(ws_ref[gi], vn[rows, cols], preferred_element_type=F32) + sb_ref[gi]
            b_o[0, rows, cols] = _bf(u[rows, cols] * s)


def _ab_in(x, gain, w_in, mu_prev, mu_next, w0, w2, a0, a2, g2, k_k, k_a, r_k, v_gain, w_s, s_b, tm):
    b, t, d = x.shape
    w = RWKV_WIDTH
    n_in = w_in.shape[1]
    half = LORA_W // 2

    def lora_cat(m):
        z = jnp.zeros((half, w), m.dtype)
        return _bf(jnp.concatenate([jnp.concatenate([m[0], z], axis=1),
                                    jnp.concatenate([z, m[1]], axis=1)], axis=0))

    head = jnp.arange(w) // HEAD_DIM
    bd = _bf(head[:, None] == head[None, :])
    sb = jnp.broadcast_to(s_b[:, :, None], (SGU_GROUPS, SGU_CHUNK, SGU_WIDTH // SGU_GROUPS))
    main, prev, nxt = _row_specs(tm, d, t)
    ospec = pl.BlockSpec((1, tm, w), lambda i, j: (i, j, 0))
    return pl.pallas_call(
        _ab_in_kernel,
        grid=(b, t // tm),
        in_specs=[main, prev, nxt,
                  _const_spec((1, d)),
                  _const_spec((d, n_in)),
                  _const_spec((1, RWKV_IN)), _const_spec((1, RWKV_IN)),
                  _const_spec((1, 2 * w)), _const_spec((LORA_W, 2 * w)),
                  _const_spec((1, 2 * w)), _const_spec((LORA_W, 2 * w)),
                  _const_spec((LORA_W, w)),
                  _const_spec((1, w)), _const_spec((1, w)), _const_spec((1, w)),
                  _const_spec((w, w)),
                  _const_spec((1, SGU_WIDTH)),
                  _const_spec((SGU_GROUPS, SGU_CHUNK, SGU_CHUNK)),
                  _const_spec((SGU_GROUPS, SGU_CHUNK, SGU_WIDTH // SGU_GROUPS))],
        out_specs=[ospec] * 12,
        out_shape=[jax.ShapeDtypeStruct((b, t, w), dt) for dt in AB_IN_OUT_DTYPES],
        compiler_params=_params(2),
        name="ab_in",
    )(x, x, x, gain.reshape(1, d), _bf(w_in), mu_prev.reshape(1, -1), mu_next.reshape(1, -1),
      w0.reshape(1, 2 * w), lora_cat(w2), a0.reshape(1, 2 * w), lora_cat(a2), _bf(g2),
      k_k.reshape(1, w), k_a.reshape(1, w), r_k.reshape(1, w), bd,
      v_gain.reshape(1, -1), _bf(w_s), sb)


def _stack_heads(x):
    even = (_iota(x.shape, 1) & (LANES - 1)) < HEAD_DIM
    return jnp.concatenate([jnp.where(even, x, 0.0), jnp.where(even, 0.0, x)], axis=0)


def _rwkv_prep(r, v, kk, lw, be, kd, reverse):
    c = r.shape[0]
    rr = _iota((c, LANES), 0)
    cc = _iota((c, LANES), 1) & (HEAD_DIM - 1)
    if reverse:
        strict, incl = cc > rr, cc >= rr
    else:
        strict, incl = cc < rr, cc <= rr
    tri = _bf(incl[:, :c])

    l1 = _bf(lw)
    l2 = _bf(lw - l1.astype(F32))
    cum = jnp.dot(tri, l1, preferred_element_type=F32) + jnp.dot(tri, l2, preferred_element_type=F32)
    tot = cum[0:1] if reverse else cum[c - 1:c]
    e_neg = jnp.exp(-cum)
    e_end = jnp.exp(tot - cum)
    return dict(ag=-kk * jnp.exp(cum - lw), rg=r * jnp.exp(cum), bi=be * e_neg, ki=kd * e_neg,
                bee=be * e_end, kee=kd * e_end, gam=jnp.exp(tot), v=v, strict=strict, incl=incl)


def _rwkv_local(chunks):
    c = chunks[0]["v"].shape[0]
    n_tiles = chunks[0]["v"].shape[1] // LANES
    r128 = _iota((LANES, LANES), 0)
    c128 = _iota((LANES, LANES), 1)
    same_head = (r128 < HEAD_DIM) == (c128 < HEAD_DIM)
    diag = r128 == c128
    ctx = [(ch, slice(p * LANES, (p + 1) * LANES)) for ch in chunks for p in range(n_tiles)]

    aa = [_dot_nt(jnp.concatenate([ch["ag"][:, sl], ch["rg"][:, sl]], axis=0),
                  jnp.concatenate([_stack_heads(ch["bi"][:, sl]), _stack_heads(ch["ki"][:, sl])], axis=0))
          for ch, sl in ctx]
    a_ab = [jnp.where(ch["strict"], m[:c, :LANES], 0.0) for (ch, _), m in zip(ctx, aa)]
    a_rb = [jnp.where(ch["incl"], m[c:, :LANES], 0.0) for (ch, _), m in zip(ctx, aa)]
    a_k = [jnp.concatenate([jnp.where(ch["strict"], m[:c, LANES:], 0.0),
                            jnp.where(ch["incl"], m[c:, LANES:], 0.0)], axis=0) for (ch, _), m in zip(ctx, aa)]
    xv = [_dot(m, _stack_heads(ch["v"][:, sl])) for (ch, sl), m in zip(ctx, a_k)]
    x = [jnp.concatenate([ch["ag"][:, sl], m[:c]], axis=1) for (ch, sl), m in zip(ctx, xv)]
    def block_diag(pi):
        return jnp.where(same_head, jnp.concatenate([pi, pi], axis=0), 0.0)

    n_round = int(math.log2(c))
    rr = a_ab
    pw = [_dot(pi, block_diag(pi)) for pi in a_ab]
    for k in range(1, n_round):
        if k < n_round - 1:
            res = [_dot(jnp.concatenate([ri, pi], axis=0), block_diag(pi)) for ri, pi in zip(rr, pw)]
            rr = [ri + pi + m[:c] for ri, pi, m in zip(rr, pw, res)]
            pw = [m[c:] for m in res]
        else:
            rr = [ri + pi + _dot(ri, block_diag(pi)) for ri, pi in zip(rr, pw)]
    x = [xi + _dot(ri, _stack_heads(xi)) for xi, ri in zip(x, rr)]
    ry = [jnp.concatenate([ch["rg"][:, sl], m[c:]], axis=1) + _dot(ai, _stack_heads(xi))
          for (ch, sl), m, ai, xi in zip(ctx, xv, a_rb, x)]
    mg = [_dot(jnp.concatenate([ch["bee"][:, sl], ch["kee"][:, sl]], axis=0).T,
               jnp.concatenate([xi, jnp.concatenate([jnp.zeros_like(ch["v"][:, sl]), ch["v"][:, sl]], axis=1)],
                               axis=0))
          for (ch, sl), xi in zip(ctx, x)]
    out = []
    for (ch, sl), ryi, mgi in zip(ctx, ry, mg):
        gam_col = jnp.sum(jnp.where(diag, ch["gam"][:, sl], 0.0), axis=1, keepdims=True)
        out.append((ryi[:, :LANES], ryi[:, LANES:], jnp.where(same_head, mgi[:, :LANES], 0.0),
                    jnp.where(same_head, mgi[:, LANES:], 0.0), gam_col))
    return [out[i * n_tiles:(i + 1) * n_tiles] for i in range(len(chunks))]


def _rwkv_kernel(rf, vf, kkf, lwf, bef, kf, rb, vb, kkb, lwb, beb, kb, yf_o, yb_o, zf_ref, zb_ref):
    @pl.when(pl.program_id(1) == 0)
    def _():
        zf_ref[...] = jnp.zeros_like(zf_ref)
        zb_ref[...] = jnp.zeros_like(zb_ref)

    c = RWKV_CHUNK
    n = rf.shape[1] // c
    n_tiles = rf.shape[2] // LANES
    chunks = []
    for refs, reverse in (((rf, vf, kkf, lwf, bef, kf), False), ((rb, vb, kkb, lwb, beb, kb), True)):
        for ci in range(n):
            chunks.append(_rwkv_prep(*[ref[0, ci * c:(ci + 1) * c, :].astype(F32) for ref in refs], reverse))
    local = _rwkv_local(chunks)

    for d, (z_ref, y_o) in enumerate(((zf_ref, yf_o), (zb_ref, yb_o))):
        order = range(n) if d == 0 else range(n - 1, -1, -1)
        zs = [z_ref[p] for p in range(n_tiles)]
        for ci in order:
            ys = []
            for p in range(n_tiles):
                rt, yl, mt, gt, gam_col = local[d * n + ci][p]
                yz = _dot(jnp.concatenate([rt, mt], axis=0), zs[p])
                ys.append(yz[:c] + yl)
                zs[p] = gam_col * zs[p] + yz[c:] + gt
            y_o[0, ci * c:(ci + 1) * c, :] = jnp.concatenate(ys, axis=1)
        for p in range(n_tiles):
            z_ref[p] = zs[p]


def _rwkv_scan(r, v, kk, lw0, lw1, be0, be1, k0, k1):
    b, t, w = r.shape
    c = RWKV_CHUNK * RWKV_STEP_CHUNKS
    nc = t // c
    fwd = pl.BlockSpec((1, c, w), lambda i, j: (i, j, 0))
    bwd = pl.BlockSpec((1, c, w), lambda i, j: (i, nc - 1 - j, 0))
    out = jax.ShapeDtypeStruct((b, t, w), F32)
    zshape = pltpu.VMEM((w // LANES, LANES, LANES), F32)
    return pl.pallas_call(
        _rwkv_kernel,
        grid=(b, nc),
        in_specs=[fwd] * 6 + [bwd] * 6,
        out_specs=[fwd, bwd],
        out_shape=[out, out],
        scratch_shapes=[zshape, zshape],
        compiler_params=_params(2),
        name="rwkv_scan",
    )(r, v, kk, lw0, be0, k0, r, v, kk, lw1, be1, k1)


def _diff_in_kernel(x_ref, g_ref, w_ref, qgt_ref, kg_ref, bd_ref, qt_o, k_o, vt_o):
    x = x_ref[0]
    d = x.shape[-1]
    hn = _bf(_rms(x, g_ref[...]))
    z = _dot(hn, w_ref[...])
    bd = bd_ref[...]
    half = bd.shape[0]

    def qk_norm(y, gain, scale):
        parts = []
        for j in range(d // half):
            yj = y[:, j * half:(j + 1) * half]
            ms = _dot(yj * yj, bd) * (1.0 / HEAD_DIM)
            parts.append(yj * lax.rsqrt(ms + NORM_EPS) * gain * scale)
        return jnp.concatenate(parts, axis=1)

    k_o[0] = _bf(qk_norm(z[:, d:2 * d], kg_ref[...], 1.0))
    tm = x.shape[0]
    q_scale = qgt_ref[...] * (LOG2E / math.sqrt(HEAD_DIM))
    for h in range(d // DIFF_V_DIM):
        cols = slice(h * DIFF_V_DIM, (h + 1) * DIFF_V_DIM)
        qt = z[:, cols].T.reshape(2, HEAD_DIM, tm)
        inv = lax.rsqrt(jnp.mean(qt * qt, axis=1, keepdims=True) + NORM_EPS)
        qt_o[0, h] = _bf((qt * inv).reshape(DIFF_V_DIM, tm) * q_scale)
        vt_o[0, h, :DIFF_V_DIM, :] = _bf(z[:, 2 * d:][:, cols].T)
        vt_o[0, h, DIFF_V_DIM:, :] = jnp.ones((ONES_ROWS, tm), BF16)


def _diff_in(x, gain, w_in, q_gain, k_gain, tm):
    b, t, d = x.shape
    half = 512
    nh = d // DIFF_V_DIM
    head = jnp.arange(half) // HEAD_DIM
    bd = _bf(head[:, None] == head[None, :])
    xs = pl.BlockSpec((1, tm, d), lambda i, j: (i, j, 0))
    out = jax.ShapeDtypeStruct((b, t, d), BF16)
    vt_rows = DIFF_V_DIM + ONES_ROWS
    return pl.pallas_call(
        _diff_in_kernel,
        grid=(b, t // tm),
        in_specs=[xs, _const_spec((1, d)), _const_spec((d, 3 * d)),
                  _const_spec((DIFF_V_DIM, 1)), _const_spec((1, half)), _const_spec((half, half))],
        out_specs=[pl.BlockSpec((1, nh, DIFF_V_DIM, tm), lambda i, j: (i, 0, 0, j)), xs,
                   pl.BlockSpec((1, nh, vt_rows, tm), lambda i, j: (i, 0, 0, j))],
        out_shape=[jax.ShapeDtypeStruct((b, nh, DIFF_V_DIM, t), BF16), out,
                   jax.ShapeDtypeStruct((b, nh, vt_rows, t), BF16)],
        compiler_params=_params(2),
        name="diff_in",
    )(x, gain.reshape(1, d), _bf(w_in), jnp.tile(q_gain, 2).reshape(DIFF_V_DIM, 1),
      jnp.tile(k_gain, half // HEAD_DIM).reshape(1, half), bd)


def _diff_attn_kernel(qt_ref, k_ref, vt_ref, hc_ref, lam_ref, og_ref, o_ref, bias_ref, p_ref, *s_scratch,
                      lambda_init, bounded):
    t, tq = bias_ref.shape

    @pl.when(pl.program_id(2) == 0)
    def _():
        dist = jnp.abs((_iota((t, tq), 0) - pl.program_id(1) * tq - _iota((t, tq), 1)).astype(F32))
        bias = dist * (-LOG2E * hc_ref[0, 0:1, 0:1])
        bias_ref[...] = bias - hc_ref[0, 1:2, 0:1] if bounded else bias

    q = qt_ref[0, 0]
    first = _iota(q.shape, 0) < HEAD_DIM
    zero = jnp.zeros_like(q)
    kb = min(KEY_BLOCK, t)
    qb = min(QUERY_BLOCK, tq)
    sub = HALO

    qcs = (jnp.where(first, q, zero), jnp.where(first, zero, q))
    secs = [(c, slice(i * qb, (i + 1) * qb)) for i in range(tq // qb) for c in range(2)]
    n_kb = t // kb
    outs = []
    lv = lam_ref[...]
    lam = (jnp.exp(jnp.sum(lv[0:1] * lv[1:2], axis=-1, keepdims=True))
           - jnp.exp(jnp.sum(lv[2:3] * lv[3:4], axis=-1, keepdims=True)) + lambda_init)

    def finish(qs):
        acc1, acc0 = outs.pop(), outs.pop()
        o0, s0 = acc0[:DIFF_V_DIM], acc0[DIFF_V_DIM:DIFF_V_DIM + 1]
        o1, s1 = acc1[:DIFF_V_DIM], acc1[DIFF_V_DIM:DIFF_V_DIM + 1]
        o = o0 * (1.0 / s0) - o1 * (lam / s1)
        inv = lax.rsqrt(jnp.mean(o * o, axis=0, keepdims=True) + NORM_EPS)
        o_ref[0, 0, :, qs] = _bf(o * inv * (og_ref[...] * (1.0 - lambda_init)))

    def logits(sec, j):
        c, qs = sec
        ks = slice(j * kb, (j + 1) * kb)
        return jnp.dot(k_ref[0, ks, :], qcs[c][:, qs], preferred_element_type=F32) + bias_ref[ks, qs]

    if bounded:
        for sec in secs:
            c, qs = sec
            for j in range(n_kb):
                p_ref[c, j * kb:(j + 1) * kb, qs] = _bf(jnp.exp2(logits(sec, j)))
            outs.append(jnp.dot(vt_ref[0, 0], p_ref[c, :, qs], preferred_element_type=F32))
            if c == 1:
                finish(qs)
    else:
        s_ref, = s_scratch

        def score_block(sec, j, m):
            c, qs = sec
            s = logits(sec, j)
            s_ref[c, j * kb:(j + 1) * kb, qs] = s
            return jnp.maximum(m, jnp.max(s.reshape(kb // sub, sub, qb), axis=0))

        def prob_block(sec, j, m):
            c, qs = sec
            ks = slice(j * kb, (j + 1) * kb)
            p_ref[c, ks, qs] = _bf(jnp.exp2(s_ref[c, ks, qs] - m))

        m_init = jnp.full((sub, qb), -jnp.inf, F32)
        m = m_init
        for j in range(n_kb):
            m = score_block(secs[0], j, m)
        m_prev = jnp.max(m, axis=0, keepdims=True)
        for n in range(1, len(secs) + 1):
            m = m_init
            for j in range(n_kb):
                if n < len(secs):
                    m = score_block(secs[n], j, m)
                prob_block(secs[n - 1], j, m_prev)
            m_prev = jnp.max(m, axis=0, keepdims=True)
            c, qs = secs[n - 1]
            outs.append(jnp.dot(vt_ref[0, 0], p_ref[c, :, qs], preferred_element_type=F32))
            if c == 1:
                finish(qs)


def _diff_attention(qt, k, vt, q_gain, k_gain, lam_vecs, out_gain, lambda_init, tq):
    b, t, d = k.shape
    nh = d // DIFF_V_DIM
    slopes = 2.0 ** (-8.0 * jnp.arange(1, nh + 1, dtype=F32) / nh)
    bound = 1.02 * math.sqrt(HEAD_DIM) * LOG2E * jnp.max(jnp.abs(q_gain)) * jnp.max(jnp.abs(k_gain))
    head_consts = jnp.stack([jnp.broadcast_to(slopes[:, None], (nh, LANES)),
                             jnp.broadcast_to(bound, (nh, LANES))], axis=1)

    def run(bounded):
        scratch = [pltpu.VMEM((t, tq), F32), pltpu.VMEM((2, t, tq), BF16)]
        if not bounded:
            scratch.append(pltpu.VMEM((2, t, tq), F32))
        return pl.pallas_call(
            functools.partial(_diff_attn_kernel, lambda_init=lambda_init, bounded=bounded),
            grid=(nh, t // tq, b),
            in_specs=[pl.BlockSpec((1, 1, DIFF_V_DIM, tq), lambda h, j, i: (i, h, 0, j)),
                      pl.BlockSpec((1, t, DIFF_V_DIM), lambda h, j, i: (i, 0, h)),
                      pl.BlockSpec((1, 1, vt.shape[2], t), lambda h, j, i: (i, h, 0, 0)),
                      pl.BlockSpec((1, 2, LANES), lambda h, j, i: (h, 0, 0)),
                      _const_spec((4, HEAD_DIM)),
                      _const_spec((DIFF_V_DIM, 1))],
            out_specs=pl.BlockSpec((1, 1, DIFF_V_DIM, tq), lambda h, j, i: (i, h, 0, j)),
            out_shape=jax.ShapeDtypeStruct((b, nh, DIFF_V_DIM, t), BF16),
            scratch_shapes=scratch,
            compiler_params=_params(3),
            name="diff_attn_bounded" if bounded else "diff_attn",
        )(qt, k, vt, head_consts, lam_vecs, out_gain.reshape(DIFF_V_DIM, 1))

    return lax.cond(2.0 * bound <= SAFE_EXP2_SPAN, lambda: run(True), lambda: run(False))


def kernel(x, mem, mem_norm, mix_norm, cross_norm, ffn_norm, ab_w_in, ab_shift_prev, ab_shift_next, rwkv_w0, rwkv_w2, rwkv_a0, rwkv_a2, rwkv_g2, rwkv_k_k, rwkv_k_a, rwkv_r_k, rwkv_out_gain, sgu_v_gain, sgu_w_s, sgu_b, ab_w_out, diff_w_in, diff_q_gain, diff_k_gain, diff_lambda_q1, diff_lambda_k1, diff_lambda_q2, diff_lambda_k2, diff_out_gain, diff_w_out, cross_wq, cross_wkv, cross_q_gain, cross_k_gain, cross_wo, ffn_w_up, ffn_conv_w, ffn_conv_b, ffn_w_down):
    b, t, d = x.shape
    depth = mix_norm.shape[0]
    tm = min(512, t)
    tq = min(1024, t)
    assert t % tm == 0 and t % (RWKV_CHUNK * RWKV_STEP_CHUNKS) == 0 and tm % SGU_CHUNK == 0

    k_all, v_all = _mem_kv(mem, mem_norm, cross_wkv, cross_k_gain)
    for l in range(depth):
        if l % 2 == 0:
            e = l // 2
            (r, v, kk, lw0, lw1, be0, be1, k0, k1, g, bv, b_out) = _ab_in(
                x, mix_norm[l], ab_w_in[e], ab_shift_prev[e], ab_shift_next[e],
                rwkv_w0[e], rwkv_w2[e], rwkv_a0[e], rwkv_a2[e], rwkv_g2[e],
                rwkv_k_k[e], rwkv_k_a[e], rwkv_r_k[e], sgu_v_gain[e], sgu_w_s[e], sgu_b[e], tm)
            yf, yb = _rwkv_scan(r, v, kk, lw0, lw1, be0, be1, k0, k1)
            head = jnp.arange(RWKV_WIDTH) // HEAD_DIM
            mix = (_mix_rwkv_sgu, (yf, yb, bv, g, b_out),
                   (rwkv_out_gain[e].reshape(1, RWKV_WIDTH), _bf(head[:, None] == head[None, :]), _bf(ab_w_out[e])))
        else:
            o = l // 2
            lambda_init = 0.8 - 0.6 * math.exp(-0.3 * l)
            q, k, vt = _diff_in(x, mix_norm[l], diff_w_in[o], diff_q_gain[o], diff_k_gain[o], tm)
            lam_vecs = jnp.stack([diff_lambda_q1[o], diff_lambda_k1[o], diff_lambda_q2[o], diff_lambda_k2[o]])
            c_out = _diff_attention(q, k, vt, diff_q_gain[o], diff_k_gain[o], lam_vecs, diff_out_gain[o],
                                    lambda_init, tq)
            mix = (_mix_proj, (c_out,), (_bf(diff_w_out[o]),))
        x = _cross_attention(x, *mix, cross_norm[l], cross_wq[l], cross_q_gain[l], k_all, v_all, cross_wo[l], l, tm)
        x = _conv_ffn(x, ffn_norm[l], ffn_w_up[l], ffn_conv_w[l], ffn_conv_b[l], ffn_w_down[l], tm)
    return x
```

```python
import functools
import math

import jax
import jax.numpy as jnp
from jax import lax
from jax.experimental import pallas as pl
from jax.experimental.pallas import tpu as pltpu

NORM_EPS = 1e-6
KK_EPS = 1e-12
RWKV_HEADS = 8
HEAD_DIM = 64
RWKV_WIDTH = RWKV_HEADS * HEAD_DIM
LORA_W = 128
RWKV_IN = 3 * RWKV_WIDTH + 3 * LORA_W
SGU_WIDTH = 512
SGU_GROUPS = 4
SGU_CHUNK = 128
DIFF_HEADS = 8
DIFF_V_DIM = 2 * HEAD_DIM
CROSS_HEADS = 4
CONV_WIDTH = 3
RWKV_CHUNK = 64
RWKV_STEP_CHUNKS = 2
LANES = 128
HALO = 8
FF_CHUNK = 256
MXU_ROWS = 256
KEY_BLOCK = 128
QUERY_BLOCK = 256
ONES_ROWS = 16
SAFE_EXP2_SPAN = 100.0
VMEM_LIMIT = 56 * 1024 * 1024
LOG2E = math.log2(math.e)

F32 = jnp.float32
BF16 = jnp.bfloat16
AB_IN_OUT_DTYPES = (BF16, BF16, BF16, F32, F32, BF16, BF16, BF16, BF16, BF16, BF16, BF16)


def _bf(x):
    return x.astype(BF16)


def _dot(a, b):
    return jnp.dot(_bf(a), _bf(b), preferred_element_type=F32)


def _dot_nt(a, b):
    return lax.dot_general(_bf(a), _bf(b), (((1,), (1,)), ((), ())), preferred_element_type=F32)


def _dot_row_pieces(a, b):
    m = a.shape[0]
    starts = list(range(0, m - m % MXU_ROWS, MXU_ROWS)) or [0]
    ends = starts[1:] + [m]
    return jnp.concatenate([_dot(a[s:e], b) for s, e in zip(starts, ends)], axis=0)


def _rms(x, gain):
    return x * lax.rsqrt(jnp.mean(x * x, axis=-1, keepdims=True) + NORM_EPS) * gain


def _sigmoid(x):
    return 1.0 / (1.0 + jnp.exp(-x))


def _iota(shape, dim):
    return lax.broadcasted_iota(jnp.int32, shape, dim)


def _shift_prev(u, halo_row):
    rolled = pltpu.roll(u, 1, 0)
    head = rolled[:HALO]
    head = jnp.where(_iota(head.shape, 0) == 0, halo_row, head)
    return jnp.concatenate([head, rolled[HALO:]], axis=0)


def _shift_next(u, halo_row):
    n = u.shape[0]
    rolled = pltpu.roll(u, n - 1, 0)
    tail = rolled[n - HALO:]
    tail = jnp.where(_iota(tail.shape, 0) == HALO - 1, halo_row, tail)
    return jnp.concatenate([rolled[:n - HALO], tail], axis=0)


def _halo_rows(xp_ref, xn_ref, gain, i, nt):
    xh = jnp.concatenate([xp_ref[0], xn_ref[0]], axis=0)
    hh = _rms(xh, gain)
    row = _iota(hh.shape, 0)
    has_prev = jnp.where(i > 0, 1.0, 0.0)
    has_next = jnp.where(i < nt - 1, 1.0, 0.0)
    return _bf(hh * jnp.where(row < HALO, has_prev, has_next))


def _const_spec(shape):
    zeros = (0,) * len(shape)
    return pl.BlockSpec(shape, lambda *_: zeros, pipeline_mode=pl.Buffered(1))


def _params(n_axes):
    return pltpu.CompilerParams(dimension_semantics=("arbitrary",) * n_axes,
                                vmem_limit_bytes=VMEM_LIMIT)


def _row_specs(tm, d, t):
    nb = tm // HALO
    last = t // HALO - 1
    main = pl.BlockSpec((1, tm, d), lambda b, i: (b, i, 0))
    prev = pl.BlockSpec((1, HALO, d), lambda b, i: (b, jnp.maximum(i * nb - 1, 0), 0))
    nxt = pl.BlockSpec((1, HALO, d), lambda b, i: (b, jnp.minimum((i + 1) * nb, last), 0))
    return main, prev, nxt


def _mem_kv_kernel(mem_ref, mnorm_ref, wkv_ref, kg_ref, k_ref, v_ref):
    d = mem_ref.shape[-1]
    hd = d // CROSS_HEADS
    mn = _rms(mem_ref[0], mnorm_ref[...])
    kv = jnp.dot(_bf(mn), wkv_ref[0], preferred_element_type=F32)
    for h in range(CROSS_HEADS):
        kh = _rms(kv[:, h * hd:(h + 1) * hd], kg_ref[0]) * (1.0 / math.sqrt(hd))
        k_ref[0, 0, :, h * hd:(h + 1) * hd] = _bf(kh)
    v_ref[0, 0] = _bf(kv[:, d:])


def _mem_kv(mem, mem_norm, wkv, k_gain):
    b, m, d = mem.shape
    nl = wkv.shape[0]
    out = jax.ShapeDtypeStruct((nl, b, m, d), BF16)
    return pl.pallas_call(
        _mem_kv_kernel,
        grid=(nl, b),
        in_specs=[pl.BlockSpec((1, m, d), lambda l, i: (i, 0, 0)),
                  pl.BlockSpec((1, d), lambda l, i: (0, 0)),
                  pl.BlockSpec((1, d, 2 * d), lambda l, i: (l, 0, 0)),
                  pl.BlockSpec((1, 1, d // CROSS_HEADS), lambda l, i: (l, 0, 0))],
        out_specs=[pl.BlockSpec((1, 1, m, d), lambda l, i: (l, i, 0, 0)),
                   pl.BlockSpec((1, 1, m, d), lambda l, i: (l, i, 0, 0))],
        out_shape=[out, out],
        compiler_params=_params(2),
        name="mem_kv",
    )(mem, mem_norm.reshape(1, d), _bf(wkv), k_gain.reshape(nl, 1, -1))


def _mix_rwkv_sgu(x, yf_ref, yb_ref, bv_ref, g_ref, b_ref, og_ref, bd_ref, wo_ref):
    y = yf_ref[0] + yb_ref[0]
    ms = _dot(y * y, bd_ref[...]) * (1.0 / HEAD_DIM)
    yn = y * lax.rsqrt(ms + NORM_EPS) * og_ref[...]
    a_out = (yn + bv_ref[0].astype(F32)) * g_ref[0].astype(F32)
    cat = jnp.concatenate([_bf(a_out), b_ref[0]], axis=1)
    return x + _dot(cat, wo_ref[...])


def _mix_proj(x, at_ref, w_ref):
    a = jnp.concatenate([at_ref[0, h].astype(F32).T for h in range(at_ref.shape[1])], axis=1)
    return x + _dot(a, w_ref[...])


def _cross_kernel(*refs, n_mix, mix_fn):
    x_ref, mix_refs = refs[0], refs[1:1 + n_mix]
    g_ref, wq_ref, qg_ref, k_ref, v_ref, wo_ref, o_ref = refs[1 + n_mix:]
    x = mix_fn(x_ref[0], *mix_refs)
    d = x.shape[-1]
    hd = d // CROSS_HEADS
    hn = _bf(_rms(x, g_ref[...]))
    q = _dot(hn, wq_ref[...])
    outs = []
    for h in range(CROSS_HEADS):
        sl = slice(h * hd, (h + 1) * hd)
        qh = _rms(q[:, sl], qg_ref[...])
        logits = _dot_nt(qh, k_ref[0, 0, :, sl])
        p = jnp.exp(logits - jnp.max(logits, axis=-1, keepdims=True))
        s = jnp.sum(p, axis=-1, keepdims=True)
        outs.append(_dot(p, v_ref[0, 0, :, sl]) / s)
    o = jnp.concatenate(outs, axis=1)
    o_ref[0] = x + _dot(o, wo_ref[...])


def _cross_attention(x, mix_fn, mix_rows, mix_consts, gain, wq, q_gain, k_all, v_all, wo, layer, tq):
    b, t, d = x.shape
    m = k_all.shape[2]
    hd = d // CROSS_HEADS

    def row_spec(a):
        if a.ndim == 3:
            return pl.BlockSpec((1, tq, a.shape[-1]), lambda i, j: (i, j, 0))
        return pl.BlockSpec((1, a.shape[1], a.shape[2], tq), lambda i, j: (i, 0, 0, j))

    mix_specs = [row_spec(a) for a in mix_rows] + [_const_spec(a.shape) for a in mix_consts]
    return pl.pallas_call(
        functools.partial(_cross_kernel, n_mix=len(mix_specs), mix_fn=mix_fn),
        grid=(b, t // tq),
        in_specs=[pl.BlockSpec((1, tq, d), lambda i, j: (i, j, 0)),
                  *mix_specs,
                  _const_spec((1, d)),
                  _const_spec((d, d)),
                  _const_spec((1, hd)),
                  pl.BlockSpec((1, 1, m, d), lambda i, j: (layer, i, 0, 0)),
                  pl.BlockSpec((1, 1, m, d), lambda i, j: (layer, i, 0, 0)),
                  _const_spec((d, d))],
        out_specs=pl.BlockSpec((1, tq, d), lambda i, j: (i, j, 0)),
        out_shape=jax.ShapeDtypeStruct((b, t, d), F32),
        compiler_params=_params(2),
        name=f"cross_attn_{layer}",
    )(x, *mix_rows, *mix_consts, gain.reshape(1, d), _bf(wq), q_gain.reshape(1, hd), k_all, v_all, _bf(wo))


def _ffn_kernel(x_ref, xp_ref, xn_ref, g_ref, wup_ref, cw_ref, cb_ref, wdn_ref, o_ref,
                hn_ref, acc_ref):
    i = pl.program_id(1)
    nt = pl.num_programs(1)
    n_chunks = wdn_ref.shape[0] // FF_CHUNK
    x = x_ref[0]
    tm = x.shape[0]
    gain = g_ref[...]
    hn_ref[:tm] = _bf(_rms(x, gain))
    hn_ref[tm:] = _halo_rows(xp_ref, xn_ref, gain, i, nt)
    acc_ref[...] = jnp.zeros_like(acc_ref)

    def cols(idx):
        return slice(idx * FF_CHUNK, (idx + 1) * FF_CHUNK)

    def up_proj(idx):
        u = _dot_row_pieces(hn_ref[...], wup_ref[:, cols(idx)])
        return u[:tm], u[tm:]

    def conv(idx, u, uh):
        cw = cw_ref[:, cols(idx)]
        up = _shift_prev(u, uh[HALO - 1:HALO])
        un = _shift_next(u, uh[HALO:HALO + 1])
        return cb_ref[:, cols(idx)] + up * cw[0:1] + u * cw[1:2] + un * cw[2:3]

    ups = [up_proj(0), up_proj(n_chunks)]
    for c in range(n_chunks):
        nxt = [up_proj(c + 1), up_proj(c + 1 + n_chunks)] if c + 1 < n_chunks else None
        gate = conv(c, *ups[0])
        val = conv(c + n_chunks, *ups[1])
        acc_ref[...] += _dot_row_pieces(gate * _sigmoid(gate) * val, wdn_ref[cols(c), :])
        ups = nxt
    o_ref[0] = x + acc_ref[...]


def _conv_ffn(x, gain, w_up, conv_w, conv_b, w_down, tm):
    b, t, d = x.shape
    ff = w_down.shape[0]
    assert ff % FF_CHUNK == 0
    main, prev, nxt = _row_specs(tm, d, t)
    return pl.pallas_call(
        _ffn_kernel,
        grid=(b, t // tm),
        in_specs=[main, prev, nxt,
                  _const_spec((1, d)),
                  _const_spec((d, 2 * ff)),
                  _const_spec((CONV_WIDTH, 2 * ff)),
                  _const_spec((1, 2 * ff)),
                  _const_spec((ff, d))],
        out_specs=pl.BlockSpec((1, tm, d), lambda i, j: (i, j, 0)),
        out_shape=jax.ShapeDtypeStruct((b, t, d), F32),
        scratch_shapes=[pltpu.VMEM((tm + 2 * HALO, d), BF16),
                        pltpu.VMEM((tm, d), F32)],
        compiler_params=_params(2),
        name="conv_ffn",
    )(x, x, x, gain.reshape(1, d), _bf(w_up), conv_w, conv_b.reshape(1, 2 * ff), _bf(w_down))


def _ab_in_kernel(x_ref, xp_ref, xn_ref, g_ref, win_ref, mup_ref, mun_ref,
                  w0_ref, w2_ref, a0_ref, a2_ref, g2_ref, kk_ref, ka_ref, rk_ref, bd_ref,
                  vg_ref, ws_ref, sb_ref,
                  r_o, v_o, kk_o, lw0_o, lw1_o, be0_o, be1_o, k0_o, k1_o, g_o, bv_o, b_o):
    i = pl.program_id(1)
    nt = pl.num_programs(1)
    w = RWKV_WIDTH
    x = x_ref[0]
    tm = x.shape[0]
    gain = g_ref[...]
    hn = _bf(_rms(x, gain))
    hx = jnp.concatenate([hn, _halo_rows(xp_ref, xn_ref, gain, i, nt)], axis=0)
    zr = _dot(hx, win_ref[:, :RWKV_IN])
    za, zh = zr[:tm], zr[tm:]
    zp = _shift_prev(za, zh[HALO - 1:HALO])
    zn = _shift_next(za, zh[HALO:HALO + 1])
    mup, mun = mup_ref[...], mun_ref[...]
    zs = za * (1.0 - mup - mun) + mup * zp + mun * zn
    r = zs[:, 0:w]
    k = zs[:, w:2 * w]
    v = zs[:, 2 * w:3 * w]
    wl = jnp.tanh(zs[:, 3 * w:3 * w + LORA_W])
    al = zs[:, 3 * w + LORA_W:3 * w + 2 * LORA_W]
    gl = zs[:, 3 * w + 2 * LORA_W:]

    wpre = w0_ref[...] + _dot(wl, w2_ref[...])
    lw = -math.exp(-0.5) * _sigmoid(wpre)
    a = _sigmoid(a0_ref[...] + _dot(al, a2_ref[...]))
    g = _dot(_sigmoid(gl), g2_ref[...])

    bd = bd_ref[...]
    kk0 = k * kk_ref[...]
    kk = kk0 * lax.rsqrt(_dot(kk0 * kk0, bd) + KK_EPS)
    ka = ka_ref[...]
    k0 = k * (1.0 + (a[:, :w] - 1.0) * ka)
    k1 = k * (1.0 + (a[:, w:] - 1.0) * ka)
    bonus = _dot(r * rk_ref[...] * (k0 + k1), bd)

    r_o[0] = _bf(r)
    v_o[0] = _bf(v)
    kk_o[0] = _bf(kk)
    lw0_o[0] = lw[:, :w]
    lw1_o[0] = lw[:, w:]
    be0_o[0] = _bf(kk * a[:, :w])
    be1_o[0] = _bf(kk * a[:, w:])
    k0_o[0] = _bf(k0)
    k1_o[0] = _bf(k1)
    g_o[0] = _bf(g)
    bv_o[0] = _bf(bonus * v)

    zg = _dot(hn, win_ref[:, RWKV_IN:])
    zg = zg * (0.5 * (1.0 + jnp.tanh(math.sqrt(2.0 / math.pi) * (zg + 0.044715 * (zg * zg * zg)))))
    u = zg[:, :SGU_WIDTH]
    vn = _bf(_rms(zg[:, SGU_WIDTH:], vg_ref[...]))
    gd = SGU_WIDTH // SGU_GROUPS
    for n in range(tm // SGU_CHUNK):
        rows = slice(n * SGU_CHUNK, (n + 1) * SGU_CHUNK)
        for gi in range(SGU_GROUPS):
            cols = slice(gi * gd, (gi + 1) * gd)
            s = jnp.dot(ws_ref[gi], vn[rows, cols], preferred_element_type=F32) + sb_ref[gi]
            b_o[0, rows, cols] = _bf(u[rows, cols] * s)


def _ab_in(x, gain, w_in, mu_prev, mu_next, w0, w2, a0, a2, g2, k_k, k_a, r_k, v_gain, w_s, s_b, tm):
    b, t, d = x.shape
    w = RWKV_WIDTH
    n_in = w_in.shape[1]
    half = LORA_W // 2

    def lora_cat(m):
        z = jnp.zeros((half, w), m.dtype)
        return _bf(jnp.concatenate([jnp.concatenate([m[0], z], axis=1),
                                    jnp.concatenate([z, m[1]], axis=1)], axis=0))

    head = jnp.arange(w) // HEAD_DIM
    bd = _bf(head[:, None] == head[None, :])
    sb = jnp.broadcast_to(s_b[:, :, None], (SGU_GROUPS, SGU_CHUNK, SGU_WIDTH // SGU_GROUPS))
    main, prev, nxt = _row_specs(tm, d, t)
    ospec = pl.BlockSpec((1, tm, w), lambda i, j: (i, j, 0))
    return pl.pallas_call(
        _ab_in_kernel,
        grid=(b, t // tm),
        in_specs=[main, prev, nxt,
                  _const_spec((1, d)),
                  _const_spec((d, n_in)),
                  _const_spec((1, RWKV_IN)), _const_spec((1, RWKV_IN)),
                  _const_spec((1, 2 * w)), _const_spec((LORA_W, 2 * w)),
                  _const_spec((1, 2 * w)), _const_spec((LORA_W, 2 * w)),
                  _const_spec((LORA_W, w)),
                  _const_spec((1, w)), _const_spec((1, w)), _const_spec((1, w)),
                  _const_spec((w, w)),
                  _const_spec((1, SGU_WIDTH)),
                  _const_spec((SGU_GROUPS, SGU_CHUNK, SGU_CHUNK)),
                  _const_spec((SGU_GROUPS, SGU_CHUNK, SGU_WIDTH // SGU_GROUPS))],
        out_specs=[ospec] * 12,
        out_shape=[jax.ShapeDtypeStruct((b, t, w), dt) for dt in AB_IN_OUT_DTYPES],
        compiler_params=_params(2),
        name="ab_in",
    )(x, x, x, gain.reshape(1, d), _bf(w_in), mu_prev.reshape(1, -1), mu_next.reshape(1, -1),
      w0.reshape(1, 2 * w), lora_cat(w2), a0.reshape(1, 2 * w), lora_cat(a2), _bf(g2),
      k_k.reshape(1, w), k_a.reshape(1, w), r_k.reshape(1, w), bd,
      v_gain.reshape(1, -1), _bf(w_s), sb)


def _stack_heads(x):
    even = (_iota(x.shape, 1) & (LANES - 1)) < HEAD_DIM
    return jnp.concatenate([jnp.where(even, x, 0.0), jnp.where(even, 0.0, x)], axis=0)


def _rwkv_prep(r, v, kk, lw, be, kd, reverse):
    c = r.shape[0]
    rr = _iota((c, LANES), 0)
    cc = _iota((c, LANES), 1) & (HEAD_DIM - 1)
    if reverse:
        strict, incl = cc > rr, cc >= rr
    else:
        strict, incl = cc < rr, cc <= rr
    tri = _bf(incl[:, :c])

    l1 = _bf(lw)
    l2 = _bf(lw - l1.astype(F32))
    cum = jnp.dot(tri, l1, preferred_element_type=F32) + jnp.dot(tri, l2, preferred_element_type=F32)
    tot = cum[0:1] if reverse else cum[c - 1:c]
    e_neg = jnp.exp(-cum)
    e_end = jnp.exp(tot - cum)
    return dict(ag=-kk * jnp.exp(cum - lw), rg=r * jnp.exp(cum), bi=be * e_neg, ki=kd * e_neg,
                bee=be * e_end, kee=kd * e_end, gam=jnp.exp(tot), v=v, strict=strict, incl=incl)


def _rwkv_local(chunks):
    c = chunks[0]["v"].shape[0]
    n_tiles = chunks[0]["v"].shape[1] // LANES
    r128 = _iota((LANES, LANES), 0)
    c128 = _iota((LANES, LANES), 1)
    same_head = (r128 < HEAD_DIM) == (c128 < HEAD_DIM)
    diag = r128 == c128
    ctx = [(ch, slice(p * LANES, (p + 1) * LANES)) for ch in chunks for p in range(n_tiles)]

    aa = [_dot_nt(jnp.concatenate([ch["ag"][:, sl], ch["rg"][:, sl]], axis=0),
                  jnp.concatenate([_stack_heads(ch["bi"][:, sl]), _stack_heads(ch["ki"][:, sl])], axis=0))
          for ch, sl in ctx]
    a_ab = [jnp.where(ch["strict"], m[:c, :LANES], 0.0) for (ch, _), m in zip(ctx, aa)]
    a_rb = [jnp.where(ch["incl"], m[c:, :LANES], 0.0) for (ch, _), m in zip(ctx, aa)]
    a_k = [jnp.concatenate([jnp.where(ch["strict"], m[:c, LANES:], 0.0),
                            jnp.where(ch["incl"], m[c:, LANES:], 0.0)], axis=0) for (ch, _), m in zip(ctx, aa)]
    xv = [_dot(m, _stack_heads(ch["v"][:, sl])) for (ch, sl), m in zip(ctx, a_k)]
    x = [jnp.concatenate([ch["ag"][:, sl], m[:c]], axis=1) for (ch, sl), m in zip(ctx, xv)]
    def block_diag(pi):
        return jnp.where(same_head, jnp.concatenate([pi, pi], axis=0), 0.0)

    n_round = int(math.log2(c))
    rr = a_ab
    pw = [_dot(pi, block_diag(pi)) for pi in a_ab]
    for k in range(1, n_round):
        if k < n_round - 1:
            res = [_dot(jnp.concatenate([ri, pi], axis=0), block_diag(pi)) for ri, pi in zip(rr, pw)]
            rr = [ri + pi + m[:c] for ri, pi, m in zip(rr, pw, res)]
            pw = [m[c:] for m in res]
        else:
            rr = [ri + pi + _dot(ri, block_diag(pi)) for ri, pi in zip(rr, pw)]
    x = [xi + _dot(ri, _stack_heads(xi)) for xi, ri in zip(x, rr)]
    ry = [jnp.concatenate([ch["rg"][:, sl], m[c:]], axis=1) + _dot(ai, _stack_heads(xi))
          for (ch, sl), m, ai, xi in zip(ctx, xv, a_rb, x)]
    mg = [_dot(jnp.concatenate([ch["bee"][:, sl], ch["kee"][:, sl]], axis=0).T,
               jnp.concatenate([xi, jnp.concatenate([jnp.zeros_like(ch["v"][:, sl]), ch["v"][:, sl]], axis=1)],
                               axis=0))
          for (ch, sl), xi in zip(ctx, x)]
    out = []
    for (ch, sl), ryi, mgi in zip(ctx, ry, mg):
        gam_col = jnp.sum(jnp.where(diag, ch["gam"][:, sl], 0.0), axis=1, keepdims=True)
        out.append((ryi[:, :LANES], ryi[:, LANES:], jnp.where(same_head, mgi[:, :LANES], 0.0),
                    jnp.where(same_head, mgi[:, LANES:], 0.0), gam_col))
    return [out[i * n_tiles:(i + 1) * n_tiles] for i in range(len(chunks))]


def _rwkv_kernel(rf, vf, kkf, lwf, bef, kf, rb, vb, kkb, lwb, beb, kb, yf_o, yb_o, zf_ref, zb_ref):
    @pl.when(pl.program_id(1) == 0)
    def _():
        zf_ref[...] = jnp.zeros_like(zf_ref)
        zb_ref[...] = jnp.zeros_like(zb_ref)

    c = RWKV_CHUNK
    n = rf.shape[1] // c
    n_tiles = rf.shape[2] // LANES
    chunks = []
    for refs, reverse in (((rf, vf, kkf, lwf, bef, kf), False), ((rb, vb, kkb, lwb, beb, kb), True)):
        for ci in range(n):
            chunks.append(_rwkv_prep(*[ref[0, ci * c:(ci + 1) * c, :].astype(F32) for ref in refs], reverse))
    local = _rwkv_local(chunks)

    for d, (z_ref, y_o) in enumerate(((zf_ref, yf_o), (zb_ref, yb_o))):
        order = range(n) if d == 0 else range(n - 1, -1, -1)
        zs = [z_ref[p] for p in range(n_tiles)]
        for ci in order:
            ys = []
            for p in range(n_tiles):
                rt, yl, mt, gt, gam_col = local[d * n + ci][p]
                yz = _dot(jnp.concatenate([rt, mt], axis=0), zs[p])
                ys.append(yz[:c] + yl)
                zs[p] = gam_col * zs[p] + yz[c:] + gt
            y_o[0, ci * c:(ci + 1) * c, :] = jnp.concatenate(ys, axis=1)
        for p in range(n_tiles):
            z_ref[p] = zs[p]


def _rwkv_scan(r, v, kk, lw0, lw1, be0, be1, k0, k1):
    b, t, w = r.shape
    c = RWKV_CHUNK * RWKV_STEP_CHUNKS
    nc = t // c
    fwd = pl.BlockSpec((1, c, w), lambda i, j: (i, j, 0))
    bwd = pl.BlockSpec((1, c, w), lambda i, j: (i, nc - 1 - j, 0))
    out = jax.ShapeDtypeStruct((b, t, w), F32)
    zshape = pltpu.VMEM((w // LANES, LANES, LANES), F32)
    return pl.pallas_call(
        _rwkv_kernel,
        grid=(b, nc),
        in_specs=[fwd] * 6 + [bwd] * 6,
        out_specs=[fwd, bwd],
        out_shape=[out, out],
        scratch_shapes=[zshape, zshape],
        compiler_params=_params(2),
        name="rwkv_scan",
    )(r, v, kk, lw0, be0, k0, r, v, kk, lw1, be1, k1)


def _diff_in_kernel(x_ref, g_ref, w_ref, qgt_ref, kg_ref, bd_ref, qt_o, k_o, vt_o):
    x = x_ref[0]
    d = x.shape[-1]
    hn = _bf(_rms(x, g_ref[...]))
    z = _dot(hn, w_ref[...])
    bd = bd_ref[...]
    half = bd.shape[0]

    def qk_norm(y, gain, scale):
        parts = []
        for j in range(d // half):
            yj = y[:, j * half:(j + 1) * half]
            ms = _dot(yj * yj, bd) * (1.0 / HEAD_DIM)
            parts.append(yj * lax.rsqrt(ms + NORM_EPS) * gain * scale)
        return jnp.concatenate(parts, axis=1)

    k_o[0] = _bf(qk_norm(z[:, d:2 * d], kg_ref[...], 1.0))
    tm = x.shape[0]
    q_scale = qgt_ref[...] * (LOG2E / math.sqrt(HEAD_DIM))
    for h in range(d // DIFF_V_DIM):
        cols = slice(h * DIFF_V_DIM, (h + 1) * DIFF_V_DIM)
        qt = z[:, cols].T.reshape(2, HEAD_DIM, tm)
        inv = lax.rsqrt(jnp.mean(qt * qt, axis=1, keepdims=True) + NORM_EPS)
        qt_o[0, h] = _bf((qt * inv).reshape(DIFF_V_DIM, tm) * q_scale)
        vt_o[0, h, :DIFF_V_DIM, :] = _bf(z[:, 2 * d:][:, cols].T)
        vt_o[0, h, DIFF_V_DIM:, :] = jnp.ones((ONES_ROWS, tm), BF16)


def _diff_in(x, gain, w_in, q_gain, k_gain, tm):
    b, t, d = x.shape
    half = 512
    nh = d // DIFF_V_DIM
    head = jnp.arange(half) // HEAD_DIM
    bd = _bf(head[:, None] == head[None, :])
    xs = pl.BlockSpec((1, tm, d), lambda i, j: (i, j, 0))
    out = jax.ShapeDtypeStruct((b, t, d), BF16)
    vt_rows = DIFF_V_DIM + ONES_ROWS
    return pl.pallas_call(
        _diff_in_kernel,
        grid=(b, t // tm),
        in_specs=[xs, _const_spec((1, d)), _const_spec((d, 3 * d)),
                  _const_spec((DIFF_V_DIM, 1)), _const_spec((1, half)), _const_spec((half, half))],
        out_specs=[pl.BlockSpec((1, nh, DIFF_V_DIM, tm), lambda i, j: (i, 0, 0, j)), xs,
                   pl.BlockSpec((1, nh, vt_rows, tm), lambda i, j: (i, 0, 0, j))],
        out_shape=[jax.ShapeDtypeStruct((b, nh, DIFF_V_DIM, t), BF16), out,
                   jax.ShapeDtypeStruct((b, nh, vt_rows, t), BF16)],
        compiler_params=_params(2),
        name="diff_in",
    )(x, gain.reshape(1, d), _bf(w_in), jnp.tile(q_gain, 2).reshape(DIFF_V_DIM, 1),
      jnp.tile(k_gain, half // HEAD_DIM).reshape(1, half), bd)


def _diff_attn_kernel(qt_ref, k_ref, vt_ref, hc_ref, lam_ref, og_ref, o_ref, bias_ref, p_ref, *s_scratch,
                      lambda_init, bounded):
    t, tq = bias_ref.shape

    @pl.when(pl.program_id(2) == 0)
    def _():
        dist = jnp.abs((_iota((t, tq), 0) - pl.program_id(1) * tq - _iota((t, tq), 1)).astype(F32))
        bias = dist * (-LOG2E * hc_ref[0, 0:1, 0:1])
        bias_ref[...] = bias - hc_ref[0, 1:2, 0:1] if bounded else bias

    q = qt_ref[0, 0]
    first = _iota(q.shape, 0) < HEAD_DIM
    zero = jnp.zeros_like(q)
    kb = min(KEY_BLOCK, t)
    qb = min(QUERY_BLOCK, tq)
    sub = HALO

    qcs = (jnp.where(first, q, zero), jnp.where(first, zero, q))
    secs = [(c, slice(i * qb, (i + 1) * qb)) for i in range(tq // qb) for c in range(2)]
    n_kb = t // kb
    outs = []
    lv = lam_ref[...]
    lam = (jnp.exp(jnp.sum(lv[0:1] * lv[1:2], axis=-1, keepdims=True))
           - jnp.exp(jnp.sum(lv[2:3] * lv[3:4], axis=-1, keepdims=True)) + lambda_init)

    def finish(qs):
        acc1, acc0 = outs.pop(), outs.pop()
        o0, s0 = acc0[:DIFF_V_DIM], acc0[DIFF_V_DIM:DIFF_V_DIM + 1]
        o1, s1 = acc1[:DIFF_V_DIM], acc1[DIFF_V_DIM:DIFF_V_DIM + 1]
        o = o0 * (1.0 / s0) - o1 * (lam / s1)
        inv = lax.rsqrt(jnp.mean(o * o, axis=0, keepdims=True) + NORM_EPS)
        o_ref[0, 0, :, qs] = _bf(o * inv * (og_ref[...] * (1.0 - lambda_init)))

    def logits(sec, j):
        c, qs = sec
        ks = slice(j * kb, (j + 1) * kb)
        return jnp.dot(k_ref[0, ks, :], qcs[c][:, qs], preferred_element_type=F32) + bias_ref[ks, qs]

    if bounded:
        for sec in secs:
            c, qs = sec
            for j in range(n_kb):
                p_ref[c, j * kb:(j + 1) * kb, qs] = _bf(jnp.exp2(logits(sec, j)))
            outs.append(jnp.dot(vt_ref[0, 0], p_ref[c, :, qs], preferred_element_type=F32))
            if c == 1:
                finish(qs)
    else:
        s_ref, = s_scratch

        def score_block(sec, j, m):
            c, qs = sec
            s = logits(sec, j)
            s_ref[c, j * kb:(j + 1) * kb, qs] = s
            return jnp.maximum(m, jnp.max(s.reshape(kb // sub, sub, qb), axis=0))

        def prob_block(sec, j, m):
            c, qs = sec
            ks = slice(j * kb, (j + 1) * kb)
            p_ref[c, ks, qs] = _bf(jnp.exp2(s_ref[c, ks, qs] - m))

        m_init = jnp.full((sub, qb), -jnp.inf, F32)
        m = m_init
        for j in range(n_kb):
            m = score_block(secs[0], j, m)
        m_prev = jnp.max(m, axis=0, keepdims=True)
        for n in range(1, len(secs) + 1):
            m = m_init
            for j in range(n_kb):
                if n < len(secs):
                    m = score_block(secs[n], j, m)
                prob_block(secs[n - 1], j, m_prev)
            m_prev = jnp.max(m, axis=0, keepdims=True)
            c, qs = secs[n - 1]
            outs.append(jnp.dot(vt_ref[0, 0], p_ref[c, :, qs], preferred_element_type=F32))
            if c == 1:
                finish(qs)


def _diff_attention(qt, k, vt, q_gain, k_gain, lam_vecs, out_gain, lambda_init, tq):
    b, t, d = k.shape
    nh = d // DIFF_V_DIM
    slopes = 2.0 ** (-8.0 * jnp.arange(1, nh + 1, dtype=F32) / nh)
    bound = 1.02 * math.sqrt(HEAD_DIM) * LOG2E * jnp.max(jnp.abs(q_gain)) * jnp.max(jnp.abs(k_gain))
    head_consts = jnp.stack([jnp.broadcast_to(slopes[:, None], (nh, LANES)),
                             jnp.broadcast_to(bound, (nh, LANES))], axis=1)

    def run(bounded):
        scratch = [pltpu.VMEM((t, tq), F32), pltpu.VMEM((2, t, tq), BF16)]
        if not bounded:
            scratch.append(pltpu.VMEM((2, t, tq), F32))
        return pl.pallas_call(
            functools.partial(_diff_attn_kernel, lambda_init=lambda_init, bounded=bounded),
            grid=(nh, t // tq, b),
            in_specs=[pl.BlockSpec((1, 1, DIFF_V_DIM, tq), lambda h, j, i: (i, h, 0, j)),
                      pl.BlockSpec((1, t, DIFF_V_DIM), lambda h, j, i: (i, 0, h)),
                      pl.BlockSpec((1, 1, vt.shape[2], t), lambda h, j, i: (i, h, 0, 0)),
                      pl.BlockSpec((1, 2, LANES), lambda h, j, i: (h, 0, 0)),
                      _const_spec((4, HEAD_DIM)),
                      _const_spec((DIFF_V_DIM, 1))],
            out_specs=pl.BlockSpec((1, 1, DIFF_V_DIM, tq), lambda h, j, i: (i, h, 0, j)),
            out_shape=jax.ShapeDtypeStruct((b, nh, DIFF_V_DIM, t), BF16),
            scratch_shapes=scratch,
            compiler_params=_params(3),
            name="diff_attn_bounded" if bounded else "diff_attn",
        )(qt, k, vt, head_consts, lam_vecs, out_gain.reshape(DIFF_V_DIM, 1))

    return lax.cond(2.0 * bound <= SAFE_EXP2_SPAN, lambda: run(True), lambda: run(False))


def kernel(x, mem, mem_norm, mix_norm, cross_norm, ffn_norm, ab_w_in, ab_shift_prev, ab_shift_next, rwkv_w0, rwkv_w2, rwkv_a0, rwkv_a2, rwkv_g2, rwkv_k_k, rwkv_k_a, rwkv_r_k, rwkv_out_gain, sgu_v_gain, sgu_w_s, sgu_b, ab_w_out, diff_w_in, diff_q_gain, diff_k_gain, diff_lambda_q1, diff_lambda_k1, diff_lambda_q2, diff_lambda_k2, diff_out_gain, diff_w_out, cross_wq, cross_wkv, cross_q_gain, cross_k_gain, cross_wo, ffn_w_up, ffn_conv_w, ffn_conv_b, ffn_w_down):
    b, t, d = x.shape
    depth = mix_norm.shape[0]
    tm = min(512, t)
    tq = min(1024, t)
    assert t % tm == 0 and t % (RWKV_CHUNK * RWKV_STEP_CHUNKS) == 0 and tm % SGU_CHUNK == 0

    k_all, v_all = _mem_kv(mem, mem_norm, cross_wkv, cross_k_gain)
    for l in range(depth):
        if l % 2 == 0:
            e = l // 2
            (r, v, kk, lw0, lw1, be0, be1, k0, k1, g, bv, b_out) = _ab_in(
                x, mix_norm[l], ab_w_in[e], ab_shift_prev[e], ab_shift_next[e],
                rwkv_w0[e], rwkv_w2[e], rwkv_a0[e], rwkv_a2[e], rwkv_g2[e],
                rwkv_k_k[e], rwkv_k_a[e], rwkv_r_k[e], sgu_v_gain[e], sgu_w_s[e], sgu_b[e], tm)
            yf, yb = _rwkv_scan(r, v, kk, lw0, lw1, be0, be1, k0, k1)
            head = jnp.arange(RWKV_WIDTH) // HEAD_DIM
            mix = (_mix_rwkv_sgu, (yf, yb, bv, g, b_out),
                   (rwkv_out_gain[e].reshape(1, RWKV_WIDTH), _bf(head[:, None] == head[None, :]), _bf(ab_w_out[e])))
        else:
            o = l // 2
            lambda_init = 0.8 - 0.6 * math.exp(-0.3 * l)
            q, k, vt = _diff_in(x, mix_norm[l], diff_w_in[o], diff_q_gain[o], diff_k_gain[o], tm)
            lam_vecs = jnp.stack([diff_lambda_q1[o], diff_lambda_k1[o], diff_lambda_q2[o], diff_lambda_k2[o]])
            c_out = _diff_attention(q, k, vt, diff_q_gain[o], diff_k_gain[o], lam_vecs, diff_out_gain[o],
                                    lambda_init, tq)
            mix = (_mix_proj, (c_out,), (_bf(diff_w_out[o]),))
        x = _cross_attention(x, *mix, cross_norm[l], cross_wq[l], cross_q_gain[l], k_all, v_all, cross_wo[l], l, tm)
        x = _conv_ffn(x, ffn_norm[l], ffn_w_up[l], ffn_conv_w[l], ffn_conv_b[l], ffn_w_down[l], min(1024, t))
    return x
```

```python
import functools
import math

import jax
import jax.numpy as jnp
from jax import lax
from jax.experimental import pallas as pl
from jax.experimental.pallas import tpu as pltpu

NORM_EPS = 1e-6
KK_EPS = 1e-12
RWKV_HEADS = 8
HEAD_DIM = 64
RWKV_WIDTH = RWKV_HEADS * HEAD_DIM
LORA_W = 128
RWKV_IN = 3 * RWKV_WIDTH + 3 * LORA_W
SGU_WIDTH = 512
SGU_GROUPS = 4
SGU_CHUNK = 128
DIFF_HEADS = 8
DIFF_V_DIM = 2 * HEAD_DIM
CROSS_HEADS = 4
CONV_WIDTH = 3
RWKV_CHUNK = 64
RWKV_STEP_CHUNKS = 2
LANES = 128
HALO = 8
FF_CHUNK = 256
CROSS_ROWS = 512
MXU_ROWS = 256
KEY_BLOCK = 128
QUERY_BLOCK = 256
ONES_ROWS = 16
SAFE_EXP2_SPAN = 100.0
VMEM_LIMIT = 56 * 1024 * 1024
LOG2E = math.log2(math.e)

F32 = jnp.float32
BF16 = jnp.bfloat16
AB_IN_OUT_DTYPES = (BF16, BF16, BF16, F32, F32, BF16, BF16, BF16, BF16, BF16, BF16, BF16)


def _bf(x):
    return x.astype(BF16)


def _dot(a, b):
    return jnp.dot(_bf(a), _bf(b), preferred_element_type=F32)


def _dot_nt(a, b):
    return lax.dot_general(_bf(a), _bf(b), (((1,), (1,)), ((), ())), preferred_element_type=F32)


def _dot_row_pieces(a, b):
    m = a.shape[0]
    starts = list(range(0, m - m % MXU_ROWS, MXU_ROWS)) or [0]
    ends = starts[1:] + [m]
    return jnp.concatenate([_dot(a[s:e], b) for s, e in zip(starts, ends)], axis=0)


def _rms(x, gain):
    return x * lax.rsqrt(jnp.mean(x * x, axis=-1, keepdims=True) + NORM_EPS) * gain


def _sigmoid(x):
    return 1.0 / (1.0 + jnp.exp(-x))


def _iota(shape, dim):
    return lax.broadcasted_iota(jnp.int32, shape, dim)


def _shift_prev(u, halo_row):
    rolled = pltpu.roll(u, 1, 0)
    head = rolled[:HALO]
    head = jnp.where(_iota(head.shape, 0) == 0, halo_row, head)
    return jnp.concatenate([head, rolled[HALO:]], axis=0)


def _shift_next(u, halo_row):
    n = u.shape[0]
    rolled = pltpu.roll(u, n - 1, 0)
    tail = rolled[n - HALO:]
    tail = jnp.where(_iota(tail.shape, 0) == HALO - 1, halo_row, tail)
    return jnp.concatenate([rolled[:n - HALO], tail], axis=0)


def _halo_rows(xp_ref, xn_ref, gain, i, nt):
    xh = jnp.concatenate([xp_ref[0], xn_ref[0]], axis=0)
    hh = _rms(xh, gain)
    row = _iota(hh.shape, 0)
    has_prev = jnp.where(i > 0, 1.0, 0.0)
    has_next = jnp.where(i < nt - 1, 1.0, 0.0)
    return _bf(hh * jnp.where(row < HALO, has_prev, has_next))


def _const_spec(shape):
    zeros = (0,) * len(shape)
    return pl.BlockSpec(shape, lambda *_: zeros, pipeline_mode=pl.Buffered(1))


def _params(n_axes):
    return pltpu.CompilerParams(dimension_semantics=("arbitrary",) * n_axes,
                                vmem_limit_bytes=VMEM_LIMIT)


def _row_specs(tm, d, t):
    nb = tm // HALO
    last = t // HALO - 1
    main = pl.BlockSpec((1, tm, d), lambda b, i: (b, i, 0))
    prev = pl.BlockSpec((1, HALO, d), lambda b, i: (b, jnp.maximum(i * nb - 1, 0), 0))
    nxt = pl.BlockSpec((1, HALO, d), lambda b, i: (b, jnp.minimum((i + 1) * nb, last), 0))
    return main, prev, nxt


def _mem_kv_kernel(mem_ref, mnorm_ref, wkv_ref, kg_ref, k_ref, v_ref):
    d = mem_ref.shape[-1]
    hd = d // CROSS_HEADS
    mn = _rms(mem_ref[0], mnorm_ref[...])
    kv = jnp.dot(_bf(mn), wkv_ref[0], preferred_element_type=F32)
    for h in range(CROSS_HEADS):
        kh = _rms(kv[:, h * hd:(h + 1) * hd], kg_ref[0]) * (1.0 / math.sqrt(hd))
        k_ref[0, 0, :, h * hd:(h + 1) * hd] = _bf(kh)
    v_ref[0, 0] = _bf(kv[:, d:])


def _mem_kv(mem, mem_norm, wkv, k_gain):
    b, m, d = mem.shape
    nl = wkv.shape[0]
    out = jax.ShapeDtypeStruct((nl, b, m, d), BF16)
    return pl.pallas_call(
        _mem_kv_kernel,
        grid=(nl, b),
        in_specs=[pl.BlockSpec((1, m, d), lambda l, i: (i, 0, 0)),
                  pl.BlockSpec((1, d), lambda l, i: (0, 0)),
                  pl.BlockSpec((1, d, 2 * d), lambda l, i: (l, 0, 0)),
                  pl.BlockSpec((1, 1, d // CROSS_HEADS), lambda l, i: (l, 0, 0))],
        out_specs=[pl.BlockSpec((1, 1, m, d), lambda l, i: (l, i, 0, 0)),
                   pl.BlockSpec((1, 1, m, d), lambda l, i: (l, i, 0, 0))],
        out_shape=[out, out],
        compiler_params=_params(2),
        name="mem_kv",
    )(mem, mem_norm.reshape(1, d), _bf(wkv), k_gain.reshape(nl, 1, -1))


def _mix_rwkv_sgu(x, rows, yf_ref, yb_ref, bv_ref, g_ref, b_ref, og_ref, bd_ref, wo_ref):
    y = yf_ref[0, rows] + yb_ref[0, rows]
    ms = _dot(y * y, bd_ref[...]) * (1.0 / HEAD_DIM)
    yn = y * lax.rsqrt(ms + NORM_EPS) * og_ref[...]
    a_out = (yn + bv_ref[0, rows].astype(F32)) * g_ref[0, rows].astype(F32)
    cat = jnp.concatenate([_bf(a_out), b_ref[0, rows]], axis=1)
    return x + _dot(cat, wo_ref[...])


def _mix_proj(x, rows, at_ref, w_ref):
    a = jnp.concatenate([at_ref[0, h, :, rows].astype(F32).T for h in range(at_ref.shape[1])], axis=1)
    return x + _dot(a, w_ref[...])


def _cross_kernel(*refs, n_mix, mix_fn):
    x_ref, mix_refs = refs[0], refs[1:1 + n_mix]
    g_ref, wq_ref, qg_ref, k_ref, v_ref, wo_ref, o_ref = refs[1 + n_mix:]
    tq, d = x_ref.shape[1:]
    hd = d // CROSS_HEADS
    step = min(CROSS_ROWS, tq)
    for start in range(0, tq, step):
        rows = slice(start, start + step)
        x = mix_fn(x_ref[0, rows], rows, *mix_refs)
        hn = _bf(_rms(x, g_ref[...]))
        q = _dot(hn, wq_ref[...])
        outs = []
        for h in range(CROSS_HEADS):
            sl = slice(h * hd, (h + 1) * hd)
            qh = _rms(q[:, sl], qg_ref[...])
            logits = _dot_nt(qh, k_ref[0, 0, :, sl])
            p = jnp.exp(logits - jnp.max(logits, axis=-1, keepdims=True))
            s = jnp.sum(p, axis=-1, keepdims=True)
            outs.append(_dot(p, v_ref[0, 0, :, sl]) / s)
        o = jnp.concatenate(outs, axis=1)
        o_ref[0, rows] = x + _dot(o, wo_ref[...])


def _cross_attention(x, mix_fn, mix_rows, mix_consts, gain, wq, q_gain, k_all, v_all, wo, layer, tq):
    b, t, d = x.shape
    m = k_all.shape[2]
    hd = d // CROSS_HEADS

    def row_spec(a):
        if a.ndim == 3:
            return pl.BlockSpec((1, tq, a.shape[-1]), lambda i, j: (i, j, 0))
        return pl.BlockSpec((1, a.shape[1], a.shape[2], tq), lambda i, j: (i, 0, 0, j))

    mix_specs = [row_spec(a) for a in mix_rows] + [_const_spec(a.shape) for a in mix_consts]
    return pl.pallas_call(
        functools.partial(_cross_kernel, n_mix=len(mix_specs), mix_fn=mix_fn),
        grid=(b, t // tq),
        in_specs=[pl.BlockSpec((1, tq, d), lambda i, j: (i, j, 0)),
                  *mix_specs,
                  _const_spec((1, d)),
                  _const_spec((d, d)),
                  _const_spec((1, hd)),
                  pl.BlockSpec((1, 1, m, d), lambda i, j: (layer, i, 0, 0)),
                  pl.BlockSpec((1, 1, m, d), lambda i, j: (layer, i, 0, 0)),
                  _const_spec((d, d))],
        out_specs=pl.BlockSpec((1, tq, d), lambda i, j: (i, j, 0)),
        out_shape=jax.ShapeDtypeStruct((b, t, d), F32),
        compiler_params=_params(2),
        name=f"cross_attn_{layer}",
    )(x, *mix_rows, *mix_consts, gain.reshape(1, d), _bf(wq), q_gain.reshape(1, hd), k_all, v_all, _bf(wo))


def _ffn_kernel(x_ref, xp_ref, xn_ref, g_ref, wup_ref, cw_ref, cb_ref, wdn_ref, o_ref,
                hn_ref, acc_ref):
    i = pl.program_id(1)
    nt = pl.num_programs(1)
    n_chunks = wdn_ref.shape[0] // FF_CHUNK
    x = x_ref[0]
    tm = x.shape[0]
    gain = g_ref[...]
    hn_ref[:tm] = _bf(_rms(x, gain))
    hn_ref[tm:] = _halo_rows(xp_ref, xn_ref, gain, i, nt)
    acc_ref[...] = jnp.zeros_like(acc_ref)

    def cols(idx):
        return slice(idx * FF_CHUNK, (idx + 1) * FF_CHUNK)

    def up_proj(idx):
        u = _dot_row_pieces(hn_ref[...], wup_ref[:, cols(idx)])
        return u[:tm], u[tm:]

    def conv(idx, u, uh):
        cw = cw_ref[:, cols(idx)]
        up = _shift_prev(u, uh[HALO - 1:HALO])
        un = _shift_next(u, uh[HALO:HALO + 1])
        return cb_ref[:, cols(idx)] + up * cw[0:1] + u * cw[1:2] + un * cw[2:3]

    ups = [up_proj(0), up_proj(n_chunks)]
    for c in range(n_chunks):
        nxt = [up_proj(c + 1), up_proj(c + 1 + n_chunks)] if c + 1 < n_chunks else None
        gate = conv(c, *ups[0])
        val = conv(c + n_chunks, *ups[1])
        acc_ref[...] += _dot_row_pieces(gate * _sigmoid(gate) * val, wdn_ref[cols(c), :])
        ups = nxt
    o_ref[0] = x + acc_ref[...]


def _conv_ffn(x, gain, w_up, conv_w, conv_b, w_down, tm):
    b, t, d = x.shape
    ff = w_down.shape[0]
    assert ff % FF_CHUNK == 0
    main, prev, nxt = _row_specs(tm, d, t)
    return pl.pallas_call(
        _ffn_kernel,
        grid=(b, t // tm),
        in_specs=[main, prev, nxt,
                  _const_spec((1, d)),
                  _const_spec((d, 2 * ff)),
                  _const_spec((CONV_WIDTH, 2 * ff)),
                  _const_spec((1, 2 * ff)),
                  _const_spec((ff, d))],
        out_specs=pl.BlockSpec((1, tm, d), lambda i, j: (i, j, 0)),
        out_shape=jax.ShapeDtypeStruct((b, t, d), F32),
        scratch_shapes=[pltpu.VMEM((tm + 2 * HALO, d), BF16),
                        pltpu.VMEM((tm, d), F32)],
        compiler_params=_params(2),
        name="conv_ffn",
    )(x, x, x, gain.reshape(1, d), _bf(w_up), conv_w, conv_b.reshape(1, 2 * ff), _bf(w_down))


def _ab_in_kernel(x_ref, xp_ref, xn_ref, g_ref, win_ref, mup_ref, mun_ref,
                  w0_ref, w2_ref, a0_ref, a2_ref, g2_ref, kk_ref, ka_ref, rk_ref, bd_ref,
                  vg_ref, ws_ref, sb_ref,
                  r_o, v_o, kk_o, lw0_o, lw1_o, be0_o, be1_o, k0_o, k1_o, g_o, bv_o, b_o):
    i = pl.program_id(1)
    nt = pl.num_programs(1)
    w = RWKV_WIDTH
    x = x_ref[0]
    tm = x.shape[0]
    gain = g_ref[...]
    hn = _bf(_rms(x, gain))
    hx = jnp.concatenate([hn, _halo_rows(xp_ref, xn_ref, gain, i, nt)], axis=0)
    zr = _dot(hx, win_ref[:, :RWKV_IN])
    za, zh = zr[:tm], zr[tm:]
    zp = _shift_prev(za, zh[HALO - 1:HALO])
    zn = _shift_next(za, zh[HALO:HALO + 1])
    mup, mun = mup_ref[...], mun_ref[...]
    zs = za * (1.0 - mup - mun) + mup * zp + mun * zn
    r = zs[:, 0:w]
    k = zs[:, w:2 * w]
    v = zs[:, 2 * w:3 * w]
    wl = jnp.tanh(zs[:, 3 * w:3 * w + LORA_W])
    al = zs[:, 3 * w + LORA_W:3 * w + 2 * LORA_W]
    gl = zs[:, 3 * w + 2 * LORA_W:]

    wpre = w0_ref[...] + _dot(wl, w2_ref[...])
    lw = -math.exp(-0.5) * _sigmoid(wpre)
    a = _sigmoid(a0_ref[...] + _dot(al, a2_ref[...]))
    g = _dot(_sigmoid(gl), g2_ref[...])

    bd = bd_ref[...]
    kk0 = k * kk_ref[...]
    kk = kk0 * lax.rsqrt(_dot(kk0 * kk0, bd) + KK_EPS)
    ka = ka_ref[...]
    k0 = k * (1.0 + (a[:, :w] - 1.0) * ka)
    k1 = k * (1.0 + (a[:, w:] - 1.0) * ka)
    bonus = _dot(r * rk_ref[...] * (k0 + k1), bd)

    r_o[0] = _bf(r)
    v_o[0] = _bf(v)
    kk_o[0] = _bf(kk)
    lw0_o[0] = lw[:, :w]
    lw1_o[0] = lw[:, w:]
    be0_o[0] = _bf(kk * a[:, :w])
    be1_o[0] = _bf(kk * a[:, w:])
    k0_o[0] = _bf(k0)
    k1_o[0] = _bf(k1)
    g_o[0] = _bf(g)
    bv_o[0] = _bf(bonus * v)

    zg = _dot(hn, win_ref[:, RWKV_IN:])
    zg = zg * (0.5 * (1.0 + jnp.tanh(math.sqrt(2.0 / math.pi) * (zg + 0.044715 * (zg * zg * zg)))))
    u = zg[:, :SGU_WIDTH]
    vn = _bf(_rms(zg[:, SGU_WIDTH:], vg_ref[...]))
    gd = SGU_WIDTH // SGU_GROUPS
    for n in range(tm // SGU_CHUNK):
        rows = slice(n * SGU_CHUNK, (n + 1) * SGU_CHUNK)
        for gi in range(SGU_GROUPS):
            cols = slice(gi * gd, (gi + 1) * gd)
            s = jnp.dot(ws_ref[gi], vn[rows, cols], preferred_element_type=F32) + sb_ref[gi]
            b_o[0, rows, cols] = _bf(u[rows, cols] * s)


def _ab_in(x, gain, w_in, mu_prev, mu_next, w0, w2, a0, a2, g2, k_k, k_a, r_k, v_gain, w_s, s_b, tm):
    b, t, d = x.shape
    w = RWKV_WIDTH
    n_in = w_in.shape[1]
    half = LORA_W // 2

    def lora_cat(m):
        z = jnp.zeros((half, w), m.dtype)
        return _bf(jnp.concatenate([jnp.concatenate([m[0], z], axis=1),
                                    jnp.concatenate([z, m[1]], axis=1)], axis=0))

    head = jnp.arange(w) // HEAD_DIM
    bd = _bf(head[:, None] == head[None, :])
    sb = jnp.broadcast_to(s_b[:, :, None], (SGU_GROUPS, SGU_CHUNK, SGU_WIDTH // SGU_GROUPS))
    main, prev, nxt = _row_specs(tm, d, t)
    ospec = pl.BlockSpec((1, tm, w), lambda i, j: (i, j, 0))
    return pl.pallas_call(
        _ab_in_kernel,
        grid=(b, t // tm),
        in_specs=[main, prev, nxt,
                  _const_spec((1, d)),
                  _const_spec((d, n_in)),
                  _const_spec((1, RWKV_IN)), _const_spec((1, RWKV_IN)),
                  _const_spec((1, 2 * w)), _const_spec((LORA_W, 2 * w)),
                  _const_spec((1, 2 * w)), _const_spec((LORA_W, 2 * w)),
                  _const_spec((LORA_W, w)),
                  _const_spec((1, w)), _const_spec((1, w)), _const_spec((1, w)),
                  _const_spec((w, w)),
                  _const_spec((1, SGU_WIDTH)),
                  _const_spec((SGU_GROUPS, SGU_CHUNK, SGU_CHUNK)),
                  _const_spec((SGU_GROUPS, SGU_CHUNK, SGU_WIDTH // SGU_GROUPS))],
        out_specs=[ospec] * 12,
        out_shape=[jax.ShapeDtypeStruct((b, t, w), dt) for dt in AB_IN_OUT_DTYPES],
        compiler_params=_params(2),
        name="ab_in",
    )(x, x, x, gain.reshape(1, d), _bf(w_in), mu_prev.reshape(1, -1), mu_next.reshape(1, -1),
      w0.reshape(1, 2 * w), lora_cat(w2), a0.reshape(1, 2 * w), lora_cat(a2), _bf(g2),
      k_k.reshape(1, w), k_a.reshape(1, w), r_k.reshape(1, w), bd,
      v_gain.reshape(1, -1), _bf(w_s), sb)


def _stack_heads(x):
    even = (_iota(x.shape, 1) & (LANES - 1)) < HEAD_DIM
    return jnp.concatenate([jnp.where(even, x, 0.0), jnp.where(even, 0.0, x)], axis=0)


def _rwkv_prep(r, v, kk, lw, be, kd, reverse):
    c = r.shape[0]
    rr = _iota((c, LANES), 0)
    cc = _iota((c, LANES), 1) & (HEAD_DIM - 1)
    if reverse:
        strict, incl = cc > rr, cc >= rr
    else:
        strict, incl = cc < rr, cc <= rr
    tri = _bf(incl[:, :c])

    l1 = _bf(lw)
    l2 = _bf(lw - l1.astype(F32))
    cum = jnp.dot(tri, l1, preferred_element_type=F32) + jnp.dot(tri, l2, preferred_element_type=F32)
    tot = cum[0:1] if reverse else cum[c - 1:c]
    e_neg = jnp.exp(-cum)
    e_end = jnp.exp(tot - cum)
    return dict(ag=-kk * jnp.exp(cum - lw), rg=r * jnp.exp(cum), bi=be * e_neg, ki=kd * e_neg,
                bee=be * e_end, kee=kd * e_end, gam=jnp.exp(tot), v=v, strict=strict, incl=incl)


def _rwkv_local(chunks):
    c = chunks[0]["v"].shape[0]
    n_tiles = chunks[0]["v"].shape[1] // LANES
    r128 = _iota((LANES, LANES), 0)
    c128 = _iota((LANES, LANES), 1)
    same_head = (r128 < HEAD_DIM) == (c128 < HEAD_DIM)
    diag = r128 == c128
    ctx = [(ch, slice(p * LANES, (p + 1) * LANES)) for ch in chunks for p in range(n_tiles)]

    aa = [_dot_nt(jnp.concatenate([ch["ag"][:, sl], ch["rg"][:, sl]], axis=0),
                  jnp.concatenate([_stack_heads(ch["bi"][:, sl]), _stack_heads(ch["ki"][:, sl])], axis=0))
          for ch, sl in ctx]
    a_ab = [jnp.where(ch["strict"], m[:c, :LANES], 0.0) for (ch, _), m in zip(ctx, aa)]
    a_rb = [jnp.where(ch["incl"], m[c:, :LANES], 0.0) for (ch, _), m in zip(ctx, aa)]
    a_k = [jnp.concatenate([jnp.where(ch["strict"], m[:c, LANES:], 0.0),
                            jnp.where(ch["incl"], m[c:, LANES:], 0.0)], axis=0) for (ch, _), m in zip(ctx, aa)]
    xv = [_dot(m, _stack_heads(ch["v"][:, sl])) for (ch, sl), m in zip(ctx, a_k)]
    x = [jnp.concatenate([ch["ag"][:, sl], m[:c]], axis=1) for (ch, sl), m in zip(ctx, xv)]
    def block_diag(pi):
        return jnp.where(same_head, jnp.concatenate([pi, pi], axis=0), 0.0)

    n_round = int(math.log2(c))
    rr = a_ab
    pw = [_dot(pi, block_diag(pi)) for pi in a_ab]
    for k in range(1, n_round):
        if k < n_round - 1:
            res = [_dot(jnp.concatenate([ri, pi], axis=0), block_diag(pi)) for ri, pi in zip(rr, pw)]
            rr = [ri + pi + m[:c] for ri, pi, m in zip(rr, pw, res)]
            pw = [m[c:] for m in res]
        else:
            rr = [ri + pi + _dot(ri, block_diag(pi)) for ri, pi in zip(rr, pw)]
    x = [xi + _dot(ri, _stack_heads(xi)) for xi, ri in zip(x, rr)]
    ry = [jnp.concatenate([ch["rg"][:, sl], m[c:]], axis=1) + _dot(ai, _stack_heads(xi))
          for (ch, sl), m, ai, xi in zip(ctx, xv, a_rb, x)]
    mg = [_dot(jnp.concatenate([ch["bee"][:, sl], ch["kee"][:, sl]], axis=0).T,
               jnp.concatenate([xi, jnp.concatenate([jnp.zeros_like(ch["v"][:, sl]), ch["v"][:, sl]], axis=1)],
                               axis=0))
          for (ch, sl), xi in zip(ctx, x)]
    out = []
    for (ch, sl), ryi, mgi in zip(ctx, ry, mg):
        gam_col = jnp.sum(jnp.where(diag, ch["gam"][:, sl], 0.0), axis=1, keepdims=True)
        out.append((ryi[:, :LANES], ryi[:, LANES:], jnp.where(same_head, mgi[:, :LANES], 0.0),
                    jnp.where(same_head, mgi[:, LANES:], 0.0), gam_col))
    return [out[i * n_tiles:(i + 1) * n_tiles] for i in range(len(chunks))]


def _rwkv_kernel(rf, vf, kkf, lwf, bef, kf, rb, vb, kkb, lwb, beb, kb, yf_o, yb_o, zf_ref, zb_ref):
    @pl.when(pl.program_id(1) == 0)
    def _():
        zf_ref[...] = jnp.zeros_like(zf_ref)
        zb_ref[...] = jnp.zeros_like(zb_ref)

    c = RWKV_CHUNK
    n = rf.shape[1] // c
    n_tiles = rf.shape[2] // LANES
    chunks = []
    for refs, reverse in (((rf, vf, kkf, lwf, bef, kf), False), ((rb, vb, kkb, lwb, beb, kb), True)):
        for ci in range(n):
            chunks.append(_rwkv_prep(*[ref[0, ci * c:(ci + 1) * c, :].astype(F32) for ref in refs], reverse))
    local = _rwkv_local(chunks)

    for d, (z_ref, y_o) in enumerate(((zf_ref, yf_o), (zb_ref, yb_o))):
        order = range(n) if d == 0 else range(n - 1, -1, -1)
        zs = [z_ref[p] for p in range(n_tiles)]
        for ci in order:
            ys = []
            for p in range(n_tiles):
                rt, yl, mt, gt, gam_col = local[d * n + ci][p]
                yz = _dot(jnp.concatenate([rt, mt], axis=0), zs[p])
                ys.append(yz[:c] + yl)
                zs[p] = gam_col * zs[p] + yz[c:] + gt
            y_o[0, ci * c:(ci + 1) * c, :] = jnp.concatenate(ys, axis=1)
        for p in range(n_tiles):
            z_ref[p] = zs[p]


def _rwkv_scan(r, v, kk, lw0, lw1, be0, be1, k0, k1):
    b, t, w = r.shape
    c = RWKV_CHUNK * RWKV_STEP_CHUNKS
    nc = t // c
    fwd = pl.BlockSpec((1, c, w), lambda i, j: (i, j, 0))
    bwd = pl.BlockSpec((1, c, w), lambda i, j: (i, nc - 1 - j, 0))
    out = jax.ShapeDtypeStruct((b, t, w), F32)
    zshape = pltpu.VMEM((w // LANES, LANES, LANES), F32)
    return pl.pallas_call(
        _rwkv_kernel,
        grid=(b, nc),
        in_specs=[fwd] * 6 + [bwd] * 6,
        out_specs=[fwd, bwd],
        out_shape=[out, out],
        scratch_shapes=[zshape, zshape],
        compiler_params=_params(2),
        name="rwkv_scan",
    )(r, v, kk, lw0, be0, k0, r, v, kk, lw1, be1, k1)


def _diff_in_kernel(x_ref, g_ref, w_ref, qgt_ref, kg_ref, bd_ref, qt_o, k_o, vt_o):
    x = x_ref[0]
    d = x.shape[-1]
    hn = _bf(_rms(x, g_ref[...]))
    z = _dot(hn, w_ref[...])
    bd = bd_ref[...]
    half = bd.shape[0]

    def qk_norm(y, gain, scale):
        parts = []
        for j in range(d // half):
            yj = y[:, j * half:(j + 1) * half]
            ms = _dot(yj * yj, bd) * (1.0 / HEAD_DIM)
            parts.append(yj * lax.rsqrt(ms + NORM_EPS) * gain * scale)
        return jnp.concatenate(parts, axis=1)

    k_o[0] = _bf(qk_norm(z[:, d:2 * d], kg_ref[...], 1.0))
    tm = x.shape[0]
    q_scale = qgt_ref[...] * (LOG2E / math.sqrt(HEAD_DIM))
    for h in range(d // DIFF_V_DIM):
        cols = slice(h * DIFF_V_DIM, (h + 1) * DIFF_V_DIM)
        qt = z[:, cols].T.reshape(2, HEAD_DIM, tm)
        inv = lax.rsqrt(jnp.mean(qt * qt, axis=1, keepdims=True) + NORM_EPS)
        qt_o[0, h] = _bf((qt * inv).reshape(DIFF_V_DIM, tm) * q_scale)
        vt_o[0, h, :DIFF_V_DIM, :] = _bf(z[:, 2 * d:][:, cols].T)
        vt_o[0, h, DIFF_V_DIM:, :] = jnp.ones((ONES_ROWS, tm), BF16)


def _diff_in(x, gain, w_in, q_gain, k_gain, tm):
    b, t, d = x.shape
    half = 512
    nh = d // DIFF_V_DIM
    head = jnp.arange(half) // HEAD_DIM
    bd = _bf(head[:, None] == head[None, :])
    xs = pl.BlockSpec((1, tm, d), lambda i, j: (i, j, 0))
    out = jax.ShapeDtypeStruct((b, t, d), BF16)
    vt_rows = DIFF_V_DIM + ONES_ROWS
    return pl.pallas_call(
        _diff_in_kernel,
        grid=(b, t // tm),
        in_specs=[xs, _const_spec((1, d)), _const_spec((d, 3 * d)),
                  _const_spec((DIFF_V_DIM, 1)), _const_spec((1, half)), _const_spec((half, half))],
        out_specs=[pl.BlockSpec((1, nh, DIFF_V_DIM, tm), lambda i, j: (i, 0, 0, j)), xs,
                   pl.BlockSpec((1, nh, vt_rows, tm), lambda i, j: (i, 0, 0, j))],
        out_shape=[jax.ShapeDtypeStruct((b, nh, DIFF_V_DIM, t), BF16), out,
                   jax.ShapeDtypeStruct((b, nh, vt_rows, t), BF16)],
        compiler_params=_params(2),
        name="diff_in",
    )(x, gain.reshape(1, d), _bf(w_in), jnp.tile(q_gain, 2).reshape(DIFF_V_DIM, 1),
      jnp.tile(k_gain, half // HEAD_DIM).reshape(1, half), bd)


def _diff_attn_kernel(qt_ref, k_ref, vt_ref, hc_ref, lam_ref, og_ref, o_ref, bias_ref, p_ref, *s_scratch,
                      lambda_init, bounded):
    t, tq = bias_ref.shape

    @pl.when(pl.program_id(2) == 0)
    def _():
        dist = jnp.abs((_iota((t, tq), 0) - pl.program_id(1) * tq - _iota((t, tq), 1)).astype(F32))
        bias = dist * (-LOG2E * hc_ref[0, 0:1, 0:1])
        bias_ref[...] = bias - hc_ref[0, 1:2, 0:1] if bounded else bias

    q = qt_ref[0, 0]
    first = _iota(q.shape, 0) < HEAD_DIM
    zero = jnp.zeros_like(q)
    kb = min(KEY_BLOCK, t)
    qb = min(QUERY_BLOCK, tq)
    sub = HALO

    qcs = (jnp.where(first, q, zero), jnp.where(first, zero, q))
    secs = [(c, slice(i * qb, (i + 1) * qb)) for i in range(tq // qb) for c in range(2)]
    n_kb = t // kb
    outs = []
    lv = lam_ref[...]
    lam = (jnp.exp(jnp.sum(lv[0:1] * lv[1:2], axis=-1, keepdims=True))
           - jnp.exp(jnp.sum(lv[2:3] * lv[3:4], axis=-1, keepdims=True)) + lambda_init)

    def finish(qs):
        acc1, acc0 = outs.pop(), outs.pop()
        o0, s0 = acc0[:DIFF_V_DIM], acc0[DIFF_V_DIM:DIFF_V_DIM + 1]
        o1, s1 = acc1[:DIFF_V_DIM], acc1[DIFF_V_DIM:DIFF_V_DIM + 1]
        o = o0 * (1.0 / s0) - o1 * (lam / s1)
        inv = lax.rsqrt(jnp.mean(o * o, axis=0, keepdims=True) + NORM_EPS)
        o_ref[0, 0, :, qs] = _bf(o * inv * (og_ref[...] * (1.0 - lambda_init)))

    def logits(sec, j):
        c, qs = sec
        ks = slice(j * kb, (j + 1) * kb)
        return jnp.dot(k_ref[0, ks, :], qcs[c][:, qs], preferred_element_type=F32) + bias_ref[ks, qs]

    if bounded:
        for sec in secs:
            c, qs = sec
            for j in range(n_kb):
                p_ref[c, j * kb:(j + 1) * kb, qs] = _bf(jnp.exp2(logits(sec, j)))
            outs.append(jnp.dot(vt_ref[0, 0], p_ref[c, :, qs], preferred_element_type=F32))
            if c == 1:
                finish(qs)
    else:
        s_ref, = s_scratch

        def score_block(sec, j, m):
            c, qs = sec
            s = logits(sec, j)
            s_ref[c, j * kb:(j + 1) * kb, qs] = s
            return jnp.maximum(m, jnp.max(s.reshape(kb // sub, sub, qb), axis=0))

        def prob_block(sec, j, m):
            c, qs = sec
            ks = slice(j * kb, (j + 1) * kb)
            p_ref[c, ks, qs] = _bf(jnp.exp2(s_ref[c, ks, qs] - m))

        m_init = jnp.full((sub, qb), -jnp.inf, F32)
        m = m_init
        for j in range(n_kb):
            m = score_block(secs[0], j, m)
        m_prev = jnp.max(m, axis=0, keepdims=True)
        for n in range(1, len(secs) + 1):
            m = m_init
            for j in range(n_kb):
                if n < len(secs):
                    m = score_block(secs[n], j, m)
                prob_block(secs[n - 1], j, m_prev)
            m_prev = jnp.max(m, axis=0, keepdims=True)
            c, qs = secs[n - 1]
            outs.append(jnp.dot(vt_ref[0, 0], p_ref[c, :, qs], preferred_element_type=F32))
            if c == 1:
                finish(qs)


def _diff_attention(qt, k, vt, q_gain, k_gain, lam_vecs, out_gain, lambda_init, tq):
    b, t, d = k.shape
    nh = d // DIFF_V_DIM
    slopes = 2.0 ** (-8.0 * jnp.arange(1, nh + 1, dtype=F32) / nh)
    bound = 1.02 * math.sqrt(HEAD_DIM) * LOG2E * jnp.max(jnp.abs(q_gain)) * jnp.max(jnp.abs(k_gain))
    head_consts = jnp.stack([jnp.broadcast_to(slopes[:, None], (nh, LANES)),
                             jnp.broadcast_to(bound, (nh, LANES))], axis=1)

    def run(bounded):
        scratch = [pltpu.VMEM((t, tq), F32), pltpu.VMEM((2, t, tq), BF16)]
        if not bounded:
            scratch.append(pltpu.VMEM((2, t, tq), F32))
        return pl.pallas_call(
            functools.partial(_diff_attn_kernel, lambda_init=lambda_init, bounded=bounded),
            grid=(nh, t // tq, b),
            in_specs=[pl.BlockSpec((1, 1, DIFF_V_DIM, tq), lambda h, j, i: (i, h, 0, j)),
                      pl.BlockSpec((1, t, DIFF_V_DIM), lambda h, j, i: (i, 0, h)),
                      pl.BlockSpec((1, 1, vt.shape[2], t), lambda h, j, i: (i, h, 0, 0)),
                      pl.BlockSpec((1, 2, LANES), lambda h, j, i: (h, 0, 0)),
                      _const_spec((4, HEAD_DIM)),
                      _const_spec((DIFF_V_DIM, 1))],
            out_specs=pl.BlockSpec((1, 1, DIFF_V_DIM, tq), lambda h, j, i: (i, h, 0, j)),
            out_shape=jax.ShapeDtypeStruct((b, nh, DIFF_V_DIM, t), BF16),
            scratch_shapes=scratch,
            compiler_params=_params(3),
            name="diff_attn_bounded" if bounded else "diff_attn",
        )(qt, k, vt, head_consts, lam_vecs, out_gain.reshape(DIFF_V_DIM, 1))

    return lax.cond(2.0 * bound <= SAFE_EXP2_SPAN, lambda: run(True), lambda: run(False))


def kernel(x, mem, mem_norm, mix_norm, cross_norm, ffn_norm, ab_w_in, ab_shift_prev, ab_shift_next, rwkv_w0, rwkv_w2, rwkv_a0, rwkv_a2, rwkv_g2, rwkv_k_k, rwkv_k_a, rwkv_r_k, rwkv_out_gain, sgu_v_gain, sgu_w_s, sgu_b, ab_w_out, diff_w_in, diff_q_gain, diff_k_gain, diff_lambda_q1, diff_lambda_k1, diff_lambda_q2, diff_lambda_k2, diff_out_gain, diff_w_out, cross_wq, cross_wkv, cross_q_gain, cross_k_gain, cross_wo, ffn_w_up, ffn_conv_w, ffn_conv_b, ffn_w_down):
    b, t, d = x.shape
    depth = mix_norm.shape[0]
    tm = min(512, t)
    tq = min(1024, t)
    assert t % tm == 0 and t % (RWKV_CHUNK * RWKV_STEP_CHUNKS) == 0 and tm % SGU_CHUNK == 0

    k_all, v_all = _mem_kv(mem, mem_norm, cross_wkv, cross_k_gain)
    for l in range(depth):
        if l % 2 == 0:
            e = l // 2
            (r, v, kk, lw0, lw1, be0, be1, k0, k1, g, bv, b_out) = _ab_in(
                x, mix_norm[l], ab_w_in[e], ab_shift_prev[e], ab_shift_next[e],
                rwkv_w0[e], rwkv_w2[e], rwkv_a0[e], rwkv_a2[e], rwkv_g2[e],
                rwkv_k_k[e], rwkv_k_a[e], rwkv_r_k[e], sgu_v_gain[e], sgu_w_s[e], sgu_b[e], tm)
            yf, yb = _rwkv_scan(r, v, kk, lw0, lw1, be0, be1, k0, k1)
            head = jnp.arange(RWKV_WIDTH) // HEAD_DIM
            mix = (_mix_rwkv_sgu, (yf, yb, bv, g, b_out),
                   (rwkv_out_gain[e].reshape(1, RWKV_WIDTH), _bf(head[:, None] == head[None, :]), _bf(ab_w_out[e])))
        else:
            o = l // 2
            lambda_init = 0.8 - 0.6 * math.exp(-0.3 * l)
            q, k, vt = _diff_in(x, mix_norm[l], diff_w_in[o], diff_q_gain[o], diff_k_gain[o], tm)
            lam_vecs = jnp.stack([diff_lambda_q1[o], diff_lambda_k1[o], diff_lambda_q2[o], diff_lambda_k2[o]])
            c_out = _diff_attention(q, k, vt, diff_q_gain[o], diff_k_gain[o], lam_vecs, diff_out_gain[o],
                                    lambda_init, tq)
            mix = (_mix_proj, (c_out,), (_bf(diff_w_out[o]),))
        x = _cross_attention(x, *mix, cross_norm[l], cross_wq[l], cross_q_gain[l], k_all, v_all, cross_wo[l], l,
                             min(1024, t))
        x = _conv_ffn(x, ffn_norm[l], ffn_w_up[l], ffn_conv_w[l], ffn_conv_b[l], ffn_w_down[l], min(1024, t))
    return x
```

```python
import functools
import math

import jax
import jax.numpy as jnp
from jax import lax
from jax.experimental import pallas as pl
from jax.experimental.pallas import tpu as pltpu

NORM_EPS = 1e-6
KK_EPS = 1e-12
RWKV_HEADS = 8
HEAD_DIM = 64
RWKV_WIDTH = RWKV_HEADS * HEAD_DIM
LORA_W = 128
RWKV_IN = 3 * RWKV_WIDTH + 3 * LORA_W
SGU_WIDTH = 512
SGU_GROUPS = 4
SGU_CHUNK = 128
DIFF_HEADS = 8
DIFF_V_DIM = 2 * HEAD_DIM
CROSS_HEADS = 4
CONV_WIDTH = 3
RWKV_CHUNK = 64
RWKV_STEP_CHUNKS = 2
LANES = 128
HALO = 8
FF_CHUNK = 256
CROSS_ROWS = 512
MXU_ROWS = 256
KEY_BLOCK = 128
QUERY_BLOCK = 256
ONES_ROWS = 16
SAFE_EXP2_SPAN = 100.0
VMEM_LIMIT = 56 * 1024 * 1024
LOG2E = math.log2(math.e)

F32 = jnp.float32
BF16 = jnp.bfloat16
AB_IN_OUT_DTYPES = (BF16, BF16, BF16, F32, F32, BF16, BF16, BF16, BF16, BF16, BF16, BF16)


def _bf(x):
    return x.astype(BF16)


def _dot(a, b):
    return jnp.dot(_bf(a), _bf(b), preferred_element_type=F32)


def _dot_nt(a, b):
    return lax.dot_general(_bf(a), _bf(b), (((1,), (1,)), ((), ())), preferred_element_type=F32)


def _dot_row_pieces(a, b):
    m = a.shape[0]
    starts = list(range(0, m - m % MXU_ROWS, MXU_ROWS)) or [0]
    ends = starts[1:] + [m]
    return jnp.concatenate([_dot(a[s:e], b) for s, e in zip(starts, ends)], axis=0)


def _rms(x, gain):
    return x * lax.rsqrt(jnp.mean(x * x, axis=-1, keepdims=True) + NORM_EPS) * gain


def _sigmoid(x):
    return 1.0 / (1.0 + jnp.exp(-x))


def _iota(shape, dim):
    return lax.broadcasted_iota(jnp.int32, shape, dim)


def _shift_prev(u, halo_row):
    rolled = pltpu.roll(u, 1, 0)
    head = rolled[:HALO]
    head = jnp.where(_iota(head.shape, 0) == 0, halo_row, head)
    return jnp.concatenate([head, rolled[HALO:]], axis=0)


def _shift_next(u, halo_row):
    n = u.shape[0]
    rolled = pltpu.roll(u, n - 1, 0)
    tail = rolled[n - HALO:]
    tail = jnp.where(_iota(tail.shape, 0) == HALO - 1, halo_row, tail)
    return jnp.concatenate([rolled[:n - HALO], tail], axis=0)


def _halo_rows(xp_ref, xn_ref, gain, i, nt):
    xh = jnp.concatenate([xp_ref[0], xn_ref[0]], axis=0)
    hh = _rms(xh, gain)
    row = _iota(hh.shape, 0)
    has_prev = jnp.where(i > 0, 1.0, 0.0)
    has_next = jnp.where(i < nt - 1, 1.0, 0.0)
    return _bf(hh * jnp.where(row < HALO, has_prev, has_next))


def _const_spec(shape):
    zeros = (0,) * len(shape)
    return pl.BlockSpec(shape, lambda *_: zeros, pipeline_mode=pl.Buffered(1))


def _params(n_axes):
    return pltpu.CompilerParams(dimension_semantics=("arbitrary",) * n_axes,
                                vmem_limit_bytes=VMEM_LIMIT)


def _row_specs(tm, d, t):
    nb = tm // HALO
    last = t // HALO - 1
    main = pl.BlockSpec((1, tm, d), lambda b, i: (b, i, 0))
    prev = pl.BlockSpec((1, HALO, d), lambda b, i: (b, jnp.maximum(i * nb - 1, 0), 0))
    nxt = pl.BlockSpec((1, HALO, d), lambda b, i: (b, jnp.minimum((i + 1) * nb, last), 0))
    return main, prev, nxt


def _mem_kv_kernel(mem_ref, mnorm_ref, wkv_ref, kg_ref, k_ref, v_ref):
    d = mem_ref.shape[-1]
    hd = d // CROSS_HEADS
    mn = _rms(mem_ref[0], mnorm_ref[...])
    kv = jnp.dot(_bf(mn), wkv_ref[0], preferred_element_type=F32)
    for h in range(CROSS_HEADS):
        kh = _rms(kv[:, h * hd:(h + 1) * hd], kg_ref[0]) * (1.0 / math.sqrt(hd))
        k_ref[0, 0, :, h * hd:(h + 1) * hd] = _bf(kh)
    v_ref[0, 0] = _bf(kv[:, d:])


def _mem_kv(mem, mem_norm, wkv, k_gain):
    b, m, d = mem.shape
    nl = wkv.shape[0]
    out = jax.ShapeDtypeStruct((nl, b, m, d), BF16)
    return pl.pallas_call(
        _mem_kv_kernel,
        grid=(nl, b),
        in_specs=[pl.BlockSpec((1, m, d), lambda l, i: (i, 0, 0)),
                  pl.BlockSpec((1, d), lambda l, i: (0, 0)),
                  pl.BlockSpec((1, d, 2 * d), lambda l, i: (l, 0, 0)),
                  pl.BlockSpec((1, 1, d // CROSS_HEADS), lambda l, i: (l, 0, 0))],
        out_specs=[pl.BlockSpec((1, 1, m, d), lambda l, i: (l, i, 0, 0)),
                   pl.BlockSpec((1, 1, m, d), lambda l, i: (l, i, 0, 0))],
        out_shape=[out, out],
        compiler_params=_params(2),
        name="mem_kv",
    )(mem, mem_norm.reshape(1, d), _bf(wkv), k_gain.reshape(nl, 1, -1))


def _mix_rwkv_sgu(x, rows, yf_ref, yb_ref, bv_ref, g_ref, b_ref, og_ref, bd_ref, wo_ref):
    y = yf_ref[0, rows] + yb_ref[0, rows]
    ms = _dot(y * y, bd_ref[...]) * (1.0 / HEAD_DIM)
    yn = y * lax.rsqrt(ms + NORM_EPS) * og_ref[...]
    a_out = (yn + bv_ref[0, rows].astype(F32)) * g_ref[0, rows].astype(F32)
    cat = jnp.concatenate([_bf(a_out), b_ref[0, rows]], axis=1)
    return x + _dot(cat, wo_ref[...])


def _mix_proj(x, rows, at_ref, w_ref):
    a = jnp.concatenate([at_ref[0, h, :, rows].astype(F32).T for h in range(at_ref.shape[1])], axis=1)
    return x + _dot(a, w_ref[...])


def _cross_kernel(*refs, n_mix, mix_fn):
    x_ref, mix_refs = refs[0], refs[1:1 + n_mix]
    g_ref, wq_ref, qg_ref, k_ref, v_ref, wo_ref, o_ref = refs[1 + n_mix:]
    tq, d = x_ref.shape[1:]
    hd = d // CROSS_HEADS
    step = min(CROSS_ROWS, tq)
    for start in range(0, tq, step):
        rows = slice(start, start + step)
        x = mix_fn(x_ref[0, rows], rows, *mix_refs)
        hn = _bf(_rms(x, g_ref[...]))
        q = _dot(hn, wq_ref[...])
        outs = []
        for h in range(CROSS_HEADS):
            sl = slice(h * hd, (h + 1) * hd)
            qh = _rms(q[:, sl], qg_ref[...])
            logits = _dot_nt(qh, k_ref[0, 0, :, sl])
            p = jnp.exp(logits - jnp.max(logits, axis=-1, keepdims=True))
            s = jnp.sum(p, axis=-1, keepdims=True)
            outs.append(_dot(p, v_ref[0, 0, :, sl]) / s)
        o = jnp.concatenate(outs, axis=1)
        o_ref[0, rows] = x + _dot(o, wo_ref[...])


def _cross_attention(x, mix_fn, mix_rows, mix_consts, gain, wq, q_gain, k_all, v_all, wo, layer, tq):
    b, t, d = x.shape
    m = k_all.shape[2]
    hd = d // CROSS_HEADS

    def row_spec(a):
        if a.ndim == 3:
            return pl.BlockSpec((1, tq, a.shape[-1]), lambda i, j: (i, j, 0))
        return pl.BlockSpec((1, a.shape[1], a.shape[2], tq), lambda i, j: (i, 0, 0, j))

    mix_specs = [row_spec(a) for a in mix_rows] + [_const_spec(a.shape) for a in mix_consts]
    return pl.pallas_call(
        functools.partial(_cross_kernel, n_mix=len(mix_specs), mix_fn=mix_fn),
        grid=(b, t // tq),
        in_specs=[pl.BlockSpec((1, tq, d), lambda i, j: (i, j, 0)),
                  *mix_specs,
                  _const_spec((1, d)),
                  _const_spec((d, d)),
                  _const_spec((1, hd)),
                  pl.BlockSpec((1, 1, m, d), lambda i, j: (layer, i, 0, 0)),
                  pl.BlockSpec((1, 1, m, d), lambda i, j: (layer, i, 0, 0)),
                  _const_spec((d, d))],
        out_specs=pl.BlockSpec((1, tq, d), lambda i, j: (i, j, 0)),
        out_shape=jax.ShapeDtypeStruct((b, t, d), F32),
        compiler_params=_params(2),
        name=f"cross_attn_{layer}",
    )(x, *mix_rows, *mix_consts, gain.reshape(1, d), _bf(wq), q_gain.reshape(1, hd), k_all, v_all, _bf(wo))


def _ffn_kernel(x_ref, xp_ref, xn_ref, g_ref, wup_ref, cw_ref, cb_ref, wdn_ref, o_ref,
                hn_ref, acc_ref):
    i = pl.program_id(1)
    nt = pl.num_programs(1)
    n_chunks = wdn_ref.shape[0] // FF_CHUNK
    x = x_ref[0]
    tm = x.shape[0]
    gain = g_ref[...]
    hn_ref[:tm] = _bf(_rms(x, gain))
    hn_ref[tm:] = _halo_rows(xp_ref, xn_ref, gain, i, nt)
    acc_ref[...] = jnp.zeros_like(acc_ref)

    def cols(idx):
        return slice(idx * FF_CHUNK, (idx + 1) * FF_CHUNK)

    def up_proj(idx):
        u = _dot_row_pieces(hn_ref[...], wup_ref[:, cols(idx)])
        return u[:tm], u[tm:]

    def conv(idx, u, uh):
        cw = cw_ref[:, cols(idx)]
        up = _shift_prev(u, uh[HALO - 1:HALO])
        un = _shift_next(u, uh[HALO:HALO + 1])
        return cb_ref[:, cols(idx)] + up * cw[0:1] + u * cw[1:2] + un * cw[2:3]

    ups = [up_proj(0), up_proj(n_chunks)]
    for c in range(n_chunks):
        nxt = [up_proj(c + 1), up_proj(c + 1 + n_chunks)] if c + 1 < n_chunks else None
        gate = conv(c, *ups[0])
        val = conv(c + n_chunks, *ups[1])
        acc_ref[...] += _dot_row_pieces(gate * _sigmoid(gate) * val, wdn_ref[cols(c), :])
        ups = nxt
    o_ref[0] = x + acc_ref[...]


def _conv_ffn(x, gain, w_up, conv_w, conv_b, w_down, tm):
    b, t, d = x.shape
    ff = w_down.shape[0]
    assert ff % FF_CHUNK == 0
    main, prev, nxt = _row_specs(tm, d, t)
    return pl.pallas_call(
        _ffn_kernel,
        grid=(b, t // tm),
        in_specs=[main, prev, nxt,
                  _const_spec((1, d)),
                  _const_spec((d, 2 * ff)),
                  _const_spec((CONV_WIDTH, 2 * ff)),
                  _const_spec((1, 2 * ff)),
                  _const_spec((ff, d))],
        out_specs=pl.BlockSpec((1, tm, d), lambda i, j: (i, j, 0)),
        out_shape=jax.ShapeDtypeStruct((b, t, d), F32),
        scratch_shapes=[pltpu.VMEM((tm + 2 * HALO, d), BF16),
                        pltpu.VMEM((tm, d), F32)],
        compiler_params=_params(2),
        name="conv_ffn",
    )(x, x, x, gain.reshape(1, d), _bf(w_up), conv_w, conv_b.reshape(1, 2 * ff), _bf(w_down))


def _ab_in_kernel(x_ref, xp_ref, xn_ref, g_ref, win_ref, mup_ref, mun_ref,
                  w0_ref, w2_ref, a0_ref, a2_ref, g2_ref, kk_ref, ka_ref, rk_ref, bd_ref,
                  vg_ref, ws_ref, sb_ref,
                  r_o, v_o, kk_o, lw0_o, lw1_o, be0_o, be1_o, k0_o, k1_o, g_o, bv_o, b_o):
    i = pl.program_id(1)
    nt = pl.num_programs(1)
    w = RWKV_WIDTH
    x = x_ref[0]
    tm = x.shape[0]
    gain = g_ref[...]
    hn = _bf(_rms(x, gain))
    hx = jnp.concatenate([hn, _halo_rows(xp_ref, xn_ref, gain, i, nt)], axis=0)
    zr = _dot(hx, win_ref[:, :RWKV_IN])
    za, zh = zr[:tm], zr[tm:]
    zp = _shift_prev(za, zh[HALO - 1:HALO])
    zn = _shift_next(za, zh[HALO:HALO + 1])
    mup, mun = mup_ref[...], mun_ref[...]
    zs = za * (1.0 - mup - mun) + mup * zp + mun * zn
    r = zs[:, 0:w]
    k = zs[:, w:2 * w]
    v = zs[:, 2 * w:3 * w]
    wl = jnp.tanh(zs[:, 3 * w:3 * w + LORA_W])
    al = zs[:, 3 * w + LORA_W:3 * w + 2 * LORA_W]
    gl = zs[:, 3 * w + 2 * LORA_W:]

    wpre = w0_ref[...] + _dot(wl, w2_ref[...])
    lw = -math.exp(-0.5) * _sigmoid(wpre)
    a = _sigmoid(a0_ref[...] + _dot(al, a2_ref[...]))
    g = _dot(_sigmoid(gl), g2_ref[...])

    bd = bd_ref[...]
    kk0 = k * kk_ref[...]
    kk = kk0 * lax.rsqrt(_dot(kk0 * kk0, bd) + KK_EPS)
    ka = ka_ref[...]
    k0 = k * (1.0 + (a[:, :w] - 1.0) * ka)
    k1 = k * (1.0 + (a[:, w:] - 1.0) * ka)
    bonus = _dot(r * rk_ref[...] * (k0 + k1), bd)

    r_o[0] = _bf(r)
    v_o[0] = _bf(v)
    kk_o[0] = _bf(kk)
    lw0_o[0] = lw[:, :w]
    lw1_o[0] = lw[:, w:]
    be0_o[0] = _bf(kk * a[:, :w])
    be1_o[0] = _bf(kk * a[:, w:])
    k0_o[0] = _bf(k0)
    k1_o[0] = _bf(k1)
    g_o[0] = _bf(g)
    bv_o[0] = _bf(bonus * v)

    zg = _dot(hn, win_ref[:, RWKV_IN:])
    zg = zg * (0.5 * (1.0 + jnp.tanh(math.sqrt(2.0 / math.pi) * (zg + 0.044715 * (zg * zg * zg)))))
    u = zg[:, :SGU_WIDTH]
    vn = _bf(_rms(zg[:, SGU_WIDTH:], vg_ref[...]))
    gd = SGU_WIDTH // SGU_GROUPS
    for n in range(tm // SGU_CHUNK):
        rows = slice(n * SGU_CHUNK, (n + 1) * SGU_CHUNK)
        for gi in range(SGU_GROUPS):
            cols = slice(gi * gd, (gi + 1) * gd)
            s = jnp.dot(ws_ref[gi], vn[rows, cols], preferred_element_type=F32) + sb_ref[gi]
            b_o[0, rows, cols] = _bf(u[rows, cols] * s)


def _ab_in(x, gain, w_in, mu_prev, mu_next, w0, w2, a0, a2, g2, k_k, k_a, r_k, v_gain, w_s, s_b, tm):
    b, t, d = x.shape
    w = RWKV_WIDTH
    n_in = w_in.shape[1]
    half = LORA_W // 2

    def lora_cat(m):
        z = jnp.zeros((half, w), m.dtype)
        return _bf(jnp.concatenate([jnp.concatenate([m[0], z], axis=1),
                                    jnp.concatenate([z, m[1]], axis=1)], axis=0))

    head = jnp.arange(w) // HEAD_DIM
    bd = _bf(head[:, None] == head[None, :])
    sb = jnp.broadcast_to(s_b[:, :, None], (SGU_GROUPS, SGU_CHUNK, SGU_WIDTH // SGU_GROUPS))
    main, prev, nxt = _row_specs(tm, d, t)
    ospec = pl.BlockSpec((1, tm, w), lambda i, j: (i, j, 0))
    return pl.pallas_call(
        _ab_in_kernel,
        grid=(b, t // tm),
        in_specs=[main, prev, nxt,
                  _const_spec((1, d)),
                  _const_spec((d, n_in)),
                  _const_spec((1, RWKV_IN)), _const_spec((1, RWKV_IN)),
                  _const_spec((1, 2 * w)), _const_spec((LORA_W, 2 * w)),
                  _const_spec((1, 2 * w)), _const_spec((LORA_W, 2 * w)),
                  _const_spec((LORA_W, w)),
                  _const_spec((1, w)), _const_spec((1, w)), _const_spec((1, w)),
                  _const_spec((w, w)),
                  _const_spec((1, SGU_WIDTH)),
                  _const_spec((SGU_GROUPS, SGU_CHUNK, SGU_CHUNK)),
                  _const_spec((SGU_GROUPS, SGU_CHUNK, SGU_WIDTH // SGU_GROUPS))],
        out_specs=[ospec] * 12,
        out_shape=[jax.ShapeDtypeStruct((b, t, w), dt) for dt in AB_IN_OUT_DTYPES],
        compiler_params=_params(2),
        name="ab_in",
    )(x, x, x, gain.reshape(1, d), _bf(w_in), mu_prev.reshape(1, -1), mu_next.reshape(1, -1),
      w0.reshape(1, 2 * w), lora_cat(w2), a0.reshape(1, 2 * w), lora_cat(a2), _bf(g2),
      k_k.reshape(1, w), k_a.reshape(1, w), r_k.reshape(1, w), bd,
      v_gain.reshape(1, -1), _bf(w_s), sb)


def _stack_heads(x):
    even = (_iota(x.shape, 1) & (LANES - 1)) < HEAD_DIM
    return jnp.concatenate([jnp.where(even, x, 0.0), jnp.where(even, 0.0, x)], axis=0)


def _rwkv_prep(r, v, kk, lw, be, kd, reverse):
    c = r.shape[0]
    rr = _iota((c, LANES), 0)
    cc = _iota((c, LANES), 1) & (HEAD_DIM - 1)
    if reverse:
        strict, incl = cc > rr, cc >= rr
    else:
        strict, incl = cc < rr, cc <= rr
    tri = _bf(incl[:, :c])

    l1 = _bf(lw)
    l2 = _bf(lw - l1.astype(F32))
    cum = jnp.dot(tri, l1, preferred_element_type=F32) + jnp.dot(tri, l2, preferred_element_type=F32)
    tot = cum[0:1] if reverse else cum[c - 1:c]
    e_neg = jnp.exp(-cum)
    e_end = jnp.exp(tot - cum)
    return dict(ag=-kk * jnp.exp(cum - lw), rg=r * jnp.exp(cum), bi=be * e_neg, ki=kd * e_neg,
                bee=be * e_end, kee=kd * e_end, gam=jnp.exp(tot), v=v, strict=strict, incl=incl)


def _rwkv_local(chunks):
    c = chunks[0]["v"].shape[0]
    n_tiles = chunks[0]["v"].shape[1] // LANES
    r128 = _iota((LANES, LANES), 0)
    c128 = _iota((LANES, LANES), 1)
    same_head = (r128 < HEAD_DIM) == (c128 < HEAD_DIM)
    diag = r128 == c128
    ctx = [(ch, slice(p * LANES, (p + 1) * LANES)) for ch in chunks for p in range(n_tiles)]

    aa = [_dot_nt(jnp.concatenate([ch["ag"][:, sl], ch["rg"][:, sl]], axis=0),
                  jnp.concatenate([_stack_heads(ch["bi"][:, sl]), _stack_heads(ch["ki"][:, sl])], axis=0))
          for ch, sl in ctx]
    a_ab = [jnp.where(ch["strict"], m[:c, :LANES], 0.0) for (ch, _), m in zip(ctx, aa)]
    a_rb = [jnp.where(ch["incl"], m[c:, :LANES], 0.0) for (ch, _), m in zip(ctx, aa)]
    a_k = [jnp.concatenate([jnp.where(ch["strict"], m[:c, LANES:], 0.0),
                            jnp.where(ch["incl"], m[c:, LANES:], 0.0)], axis=0) for (ch, _), m in zip(ctx, aa)]
    xv = [_dot(m, _stack_heads(ch["v"][:, sl])) for (ch, sl), m in zip(ctx, a_k)]
    x = [jnp.concatenate([ch["ag"][:, sl], m[:c]], axis=1) for (ch, sl), m in zip(ctx, xv)]
    def block_diag(pi):
        return jnp.where(same_head, jnp.concatenate([pi, pi], axis=0), 0.0)

    n_round = int(math.log2(c))
    rr = a_ab
    pw = [_dot(pi, block_diag(pi)) for pi in a_ab]
    for k in range(1, n_round):
        if k < n_round - 1:
            res = [_dot(jnp.concatenate([ri, pi], axis=0), block_diag(pi)) for ri, pi in zip(rr, pw)]
            rr = [ri + pi + m[:c] for ri, pi, m in zip(rr, pw, res)]
            pw = [m[c:] for m in res]
        else:
            rr = [ri + pi + _dot(ri, block_diag(pi)) for ri, pi in zip(rr, pw)]
    x = [xi + _dot(ri, _stack_heads(xi)) for xi, ri in zip(x, rr)]
    ry = [jnp.concatenate([ch["rg"][:, sl], m[c:]], axis=1) + _dot(ai, _stack_heads(xi))
          for (ch, sl), m, ai, xi in zip(ctx, xv, a_rb, x)]
    mg = [_dot(jnp.concatenate([ch["bee"][:, sl], ch["kee"][:, sl]], axis=0).T,
               jnp.concatenate([xi, jnp.concatenate([jnp.zeros_like(ch["v"][:, sl]), ch["v"][:, sl]], axis=1)],
                               axis=0))
          for (ch, sl), xi in zip(ctx, x)]
    out = []
    for (ch, sl), ryi, mgi in zip(ctx, ry, mg):
        gam_col = jnp.sum(jnp.where(diag, ch["gam"][:, sl], 0.0), axis=1, keepdims=True)
        out.append((ryi[:, :LANES], ryi[:, LANES:], jnp.where(same_head, mgi[:, :LANES], 0.0),
                    jnp.where(same_head, mgi[:, LANES:], 0.0), gam_col))
    return [out[i * n_tiles:(i + 1) * n_tiles] for i in range(len(chunks))]


def _rwkv_kernel(rf, vf, kkf, lwf, bef, kf, rb, vb, kkb, lwb, beb, kb, yf_o, yb_o, zf_ref, zb_ref):
    @pl.when(pl.program_id(1) == 0)
    def _():
        zf_ref[...] = jnp.zeros_like(zf_ref)
        zb_ref[...] = jnp.zeros_like(zb_ref)

    c = RWKV_CHUNK
    n = rf.shape[1] // c
    n_tiles = rf.shape[2] // LANES
    chunks = []
    for refs, reverse in (((rf, vf, kkf, lwf, bef, kf), False), ((rb, vb, kkb, lwb, beb, kb), True)):
        for ci in range(n):
            chunks.append(_rwkv_prep(*[ref[0, ci * c:(ci + 1) * c, :].astype(F32) for ref in refs], reverse))
    local = _rwkv_local(chunks)

    for d, (z_ref, y_o) in enumerate(((zf_ref, yf_o), (zb_ref, yb_o))):
        order = range(n) if d == 0 else range(n - 1, -1, -1)
        zs = [z_ref[p] for p in range(n_tiles)]
        for ci in order:
            ys = []
            for p in range(n_tiles):
                rt, yl, mt, gt, gam_col = local[d * n + ci][p]
                yz = _dot(jnp.concatenate([rt, mt], axis=0), zs[p])
                ys.append(yz[:c] + yl)
                zs[p] = gam_col * zs[p] + yz[c:] + gt
            y_o[0, ci * c:(ci + 1) * c, :] = jnp.concatenate(ys, axis=1)
        for p in range(n_tiles):
            z_ref[p] = zs[p]


def _rwkv_scan(r, v, kk, lw0, lw1, be0, be1, k0, k1):
    b, t, w = r.shape
    c = RWKV_CHUNK * RWKV_STEP_CHUNKS
    nc = t // c
    fwd = pl.BlockSpec((1, c, w), lambda i, j: (i, j, 0))
    bwd = pl.BlockSpec((1, c, w), lambda i, j: (i, nc - 1 - j, 0))
    out = jax.ShapeDtypeStruct((b, t, w), F32)
    zshape = pltpu.VMEM((w // LANES, LANES, LANES), F32)
    return pl.pallas_call(
        _rwkv_kernel,
        grid=(b, nc),
        in_specs=[fwd] * 6 + [bwd] * 6,
        out_specs=[fwd, bwd],
        out_shape=[out, out],
        scratch_shapes=[zshape, zshape],
        compiler_params=_params(2),
        name="rwkv_scan",
    )(r, v, kk, lw0, be0, k0, r, v, kk, lw1, be1, k1)


def _diff_in_kernel(x_ref, g_ref, w_ref, qgt_ref, kg_ref, bd_ref, qt_o, k_o, vt_o):
    x = x_ref[0]
    d = x.shape[-1]
    hn = _bf(_rms(x, g_ref[...]))
    z = _dot(hn, w_ref[...])
    bd = bd_ref[...]
    half = bd.shape[0]

    def qk_norm(y, gain, scale):
        parts = []
        for j in range(d // half):
            yj = y[:, j * half:(j + 1) * half]
            ms = _dot(yj * yj, bd) * (1.0 / HEAD_DIM)
            parts.append(yj * lax.rsqrt(ms + NORM_EPS) * gain * scale)
        return jnp.concatenate(parts, axis=1)

    k_o[0] = _bf(qk_norm(z[:, d:2 * d], kg_ref[...], 1.0))
    tm = x.shape[0]
    q_scale = qgt_ref[...] * (LOG2E / math.sqrt(HEAD_DIM))
    for h in range(d // DIFF_V_DIM):
        cols = slice(h * DIFF_V_DIM, (h + 1) * DIFF_V_DIM)
        qt = z[:, cols].T.reshape(2, HEAD_DIM, tm)
        inv = lax.rsqrt(jnp.mean(qt * qt, axis=1, keepdims=True) + NORM_EPS)
        qt_o[0, h] = _bf((qt * inv).reshape(DIFF_V_DIM, tm) * q_scale)
        vt_o[0, h, :DIFF_V_DIM, :] = _bf(z[:, 2 * d:][:, cols].T)
        vt_o[0, h, DIFF_V_DIM:, :] = jnp.ones((ONES_ROWS, tm), BF16)


def _diff_in(x, gain, w_in, q_gain, k_gain, tm):
    b, t, d = x.shape
    half = 512
    nh = d // DIFF_V_DIM
    head = jnp.arange(half) // HEAD_DIM
    bd = _bf(head[:, None] == head[None, :])
    xs = pl.BlockSpec((1, tm, d), lambda i, j: (i, j, 0))
    out = jax.ShapeDtypeStruct((b, t, d), BF16)
    vt_rows = DIFF_V_DIM + ONES_ROWS
    return pl.pallas_call(
        _diff_in_kernel,
        grid=(b, t // tm),
        in_specs=[xs, _const_spec((1, d)), _const_spec((d, 3 * d)),
                  _const_spec((DIFF_V_DIM, 1)), _const_spec((1, half)), _const_spec((half, half))],
        out_specs=[pl.BlockSpec((1, nh, DIFF_V_DIM, tm), lambda i, j: (i, 0, 0, j)), xs,
                   pl.BlockSpec((1, nh, vt_rows, tm), lambda i, j: (i, 0, 0, j))],
        out_shape=[jax.ShapeDtypeStruct((b, nh, DIFF_V_DIM, t), BF16), out,
                   jax.ShapeDtypeStruct((b, nh, vt_rows, t), BF16)],
        compiler_params=_params(2),
        name="diff_in",
    )(x, gain.reshape(1, d), _bf(w_in), jnp.tile(q_gain, 2).reshape(DIFF_V_DIM, 1),
      jnp.tile(k_gain, half // HEAD_DIM).reshape(1, half), bd)


def _diff_attn_kernel(qt_ref, k_ref, vt_ref, hc_ref, lam_ref, og_ref, o_ref, bias_ref, p_ref, *s_scratch,
                      lambda_init, bounded):
    t, tq = bias_ref.shape

    @pl.when(pl.program_id(2) == 0)
    def _():
        dist = jnp.abs((_iota((t, tq), 0) - pl.program_id(1) * tq - _iota((t, tq), 1)).astype(F32))
        bias = dist * (-LOG2E * hc_ref[0, 0:1, 0:1])
        bias_ref[...] = bias - hc_ref[0, 1:2, 0:1] if bounded else bias

    q = qt_ref[0, 0]
    first = _iota(q.shape, 0) < HEAD_DIM
    zero = jnp.zeros_like(q)
    kb = min(KEY_BLOCK, t)
    qb = min(QUERY_BLOCK, tq)
    sub = HALO

    qcs = (jnp.where(first, q, zero), jnp.where(first, zero, q))
    secs = [(c, slice(i * qb, (i + 1) * qb)) for i in range(tq // qb) for c in range(2)]
    n_kb = t // kb
    outs = []
    lv = lam_ref[...]
    lam = (jnp.exp(jnp.sum(lv[0:1] * lv[1:2], axis=-1, keepdims=True))
           - jnp.exp(jnp.sum(lv[2:3] * lv[3:4], axis=-1, keepdims=True)) + lambda_init)

    def finish(qs):
        acc1, acc0 = outs.pop(), outs.pop()
        o0, s0 = acc0[:DIFF_V_DIM], acc0[DIFF_V_DIM:DIFF_V_DIM + 1]
        o1, s1 = acc1[:DIFF_V_DIM], acc1[DIFF_V_DIM:DIFF_V_DIM + 1]
        o = o0 * (1.0 / s0) - o1 * (lam / s1)
        inv = lax.rsqrt(jnp.mean(o * o, axis=0, keepdims=True) + NORM_EPS)
        o_ref[0, 0, :, qs] = _bf(o * inv * (og_ref[...] * (1.0 - lambda_init)))

    def logits(sec, j):
        c, qs = sec
        ks = slice(j * kb, (j + 1) * kb)
        return jnp.dot(k_ref[0, ks, :], qcs[c][:, qs], preferred_element_type=F32) + bias_ref[ks, qs]

    if bounded:
        for sec in secs:
            c, qs = sec
            for j in range(n_kb):
                p_ref[c, j * kb:(j + 1) * kb, qs] = _bf(jnp.exp2(logits(sec, j)))
            outs.append(jnp.dot(vt_ref[0, 0], p_ref[c, :, qs], preferred_element_type=F32))
            if c == 1:
                finish(qs)
    else:
        s_ref, = s_scratch

        def score_block(sec, j, m):
            c, qs = sec
            s = logits(sec, j)
            s_ref[c, j * kb:(j + 1) * kb, qs] = s
            return jnp.maximum(m, jnp.max(s.reshape(kb // sub, sub, qb), axis=0))

        def prob_block(sec, j, m):
            c, qs = sec
            ks = slice(j * kb, (j + 1) * kb)
            p_ref[c, ks, qs] = _bf(jnp.exp2(s_ref[c, ks, qs] - m))

        m_init = jnp.full((sub, qb), -jnp.inf, F32)
        m = m_init
        for j in range(n_kb):
            m = score_block(secs[0], j, m)
        m_prev = jnp.max(m, axis=0, keepdims=True)
        for n in range(1, len(secs) + 1):
            m = m_init
            for j in range(n_kb):
                if n < len(secs):
                    m = score_block(secs[n], j, m)
                prob_block(secs[n - 1], j, m_prev)
            m_prev = jnp.max(m, axis=0, keepdims=True)
            c, qs = secs[n - 1]
            outs.append(jnp.dot(vt_ref[0, 0], p_ref[c, :, qs], preferred_element_type=F32))
            if c == 1:
                finish(qs)


def _diff_attention(qt, k, vt, q_gain, k_gain, lam_vecs, out_gain, lambda_init, tq_two_pass):
    b, t, d = k.shape
    nh = d // DIFF_V_DIM
    slopes = 2.0 ** (-8.0 * jnp.arange(1, nh + 1, dtype=F32) / nh)
    bound = 1.02 * math.sqrt(HEAD_DIM) * LOG2E * jnp.max(jnp.abs(q_gain)) * jnp.max(jnp.abs(k_gain))
    head_consts = jnp.stack([jnp.broadcast_to(slopes[:, None], (nh, LANES)),
                             jnp.broadcast_to(bound, (nh, LANES))], axis=1)

    def run(bounded):
        tq = min(2 * tq_two_pass, t) if bounded else tq_two_pass
        scratch = [pltpu.VMEM((t, tq), F32), pltpu.VMEM((2, t, tq), BF16)]
        if not bounded:
            scratch.append(pltpu.VMEM((2, t, tq), F32))
        return pl.pallas_call(
            functools.partial(_diff_attn_kernel, lambda_init=lambda_init, bounded=bounded),
            grid=(nh, t // tq, b),
            in_specs=[pl.BlockSpec((1, 1, DIFF_V_DIM, tq), lambda h, j, i: (i, h, 0, j)),
                      pl.BlockSpec((1, t, DIFF_V_DIM), lambda h, j, i: (i, 0, h)),
                      pl.BlockSpec((1, 1, vt.shape[2], t), lambda h, j, i: (i, h, 0, 0)),
                      pl.BlockSpec((1, 2, LANES), lambda h, j, i: (h, 0, 0)),
                      _const_spec((4, HEAD_DIM)),
                      _const_spec((DIFF_V_DIM, 1))],
            out_specs=pl.BlockSpec((1, 1, DIFF_V_DIM, tq), lambda h, j, i: (i, h, 0, j)),
            out_shape=jax.ShapeDtypeStruct((b, nh, DIFF_V_DIM, t), BF16),
            scratch_shapes=scratch,
            compiler_params=_params(3),
            name="diff_attn_bounded" if bounded else "diff_attn",
        )(qt, k, vt, head_consts, lam_vecs, out_gain.reshape(DIFF_V_DIM, 1))

    return lax.cond(2.0 * bound <= SAFE_EXP2_SPAN, lambda: run(True), lambda: run(False))


def kernel(x, mem, mem_norm, mix_norm, cross_norm, ffn_norm, ab_w_in, ab_shift_prev, ab_shift_next, rwkv_w0, rwkv_w2, rwkv_a0, rwkv_a2, rwkv_g2, rwkv_k_k, rwkv_k_a, rwkv_r_k, rwkv_out_gain, sgu_v_gain, sgu_w_s, sgu_b, ab_w_out, diff_w_in, diff_q_gain, diff_k_gain, diff_lambda_q1, diff_lambda_k1, diff_lambda_q2, diff_lambda_k2, diff_out_gain, diff_w_out, cross_wq, cross_wkv, cross_q_gain, cross_k_gain, cross_wo, ffn_w_up, ffn_conv_w, ffn_conv_b, ffn_w_down):
    b, t, d = x.shape
    depth = mix_norm.shape[0]
    tm = min(512, t)
    tq = min(1024, t)
    assert t % tm == 0 and t % (RWKV_CHUNK * RWKV_STEP_CHUNKS) == 0 and tm % SGU_CHUNK == 0

    k_all, v_all = _mem_kv(mem, mem_norm, cross_wkv, cross_k_gain)
    for l in range(depth):
        if l % 2 == 0:
            e = l // 2
            (r, v, kk, lw0, lw1, be0, be1, k0, k1, g, bv, b_out) = _ab_in(
                x, mix_norm[l], ab_w_in[e], ab_shift_prev[e], ab_shift_next[e],
                rwkv_w0[e], rwkv_w2[e], rwkv_a0[e], rwkv_a2[e], rwkv_g2[e],
                rwkv_k_k[e], rwkv_k_a[e], rwkv_r_k[e], sgu_v_gain[e], sgu_w_s[e], sgu_b[e], tm)
            yf, yb = _rwkv_scan(r, v, kk, lw0, lw1, be0, be1, k0, k1)
            head = jnp.arange(RWKV_WIDTH) // HEAD_DIM
            mix = (_mix_rwkv_sgu, (yf, yb, bv, g, b_out),
                   (rwkv_out_gain[e].reshape(1, RWKV_WIDTH), _bf(head[:, None] == head[None, :]), _bf(ab_w_out[e])))
        else:
            o = l // 2
            lambda_init = 0.8 - 0.6 * math.exp(-0.3 * l)
            q, k, vt = _diff_in(x, mix_norm[l], diff_w_in[o], diff_q_gain[o], diff_k_gain[o], tm)
            lam_vecs = jnp.stack([diff_lambda_q1[o], diff_lambda_k1[o], diff_lambda_q2[o], diff_lambda_k2[o]])
            c_out = _diff_attention(q, k, vt, diff_q_gain[o], diff_k_gain[o], lam_vecs, diff_out_gain[o],
                                    lambda_init, tq)
            mix = (_mix_proj, (c_out,), (_bf(diff_w_out[o]),))
        x = _cross_attention(x, *mix, cross_norm[l], cross_wq[l], cross_q_gain[l], k_all, v_all, cross_wo[l], l,
                             min(1024, t))
        x = _conv_ffn(x, ffn_norm[l], ffn_w_up[l], ffn_conv_w[l], ffn_conv_b[l], ffn_w_down[l], min(1024, t))
    return x
```

```python
import functools
import math

import jax
import jax.numpy as jnp
from jax import lax
from jax.experimental import pallas as pl
from jax.experimental.pallas import tpu as pltpu

NORM_EPS = 1e-6
KK_EPS = 1e-12
RWKV_HEADS = 8
HEAD_DIM = 64
RWKV_WIDTH = RWKV_HEADS * HEAD_DIM
LORA_W = 128
RWKV_IN = 3 * RWKV_WIDTH + 3 * LORA_W
SGU_WIDTH = 512
SGU_GROUPS = 4
SGU_CHUNK = 128
DIFF_V_DIM = 2 * HEAD_DIM
CROSS_HEADS = 4
CONV_WIDTH = 3
RWKV_CHUNK = 64
RWKV_STEP_CHUNKS = 2
LANES = 128
HALO = 8
FF_CHUNK = 256
CROSS_ROWS = 512
MXU_ROWS = 256
KEY_BLOCK = 128
QUERY_BLOCK = 256
ONES_ROWS = 16
SAFE_EXP2_SPAN = 100.0
VMEM_LIMIT = 56 * 1024 * 1024
LOG2E = math.log2(math.e)

F32 = jnp.float32
BF16 = jnp.bfloat16
AB_IN_OUT_DTYPES = (BF16, BF16, BF16, F32, F32, BF16, BF16, BF16, BF16, BF16, BF16, BF16)


def _bf(x):
    return x.astype(BF16)


def _dot(a, b):
    return jnp.dot(_bf(a), _bf(b), preferred_element_type=F32)


def _dot_nt(a, b):
    return lax.dot_general(_bf(a), _bf(b), (((1,), (1,)), ((), ())), preferred_element_type=F32)


def _dot_row_pieces(a, b):
    m = a.shape[0]
    starts = list(range(0, m - m % MXU_ROWS, MXU_ROWS)) or [0]
    ends = starts[1:] + [m]
    return jnp.concatenate([_dot(a[s:e], b) for s, e in zip(starts, ends)], axis=0)


def _rms(x, gain):
    return x * lax.rsqrt(jnp.mean(x * x, axis=-1, keepdims=True) + NORM_EPS) * gain


def _sigmoid(x):
    return 1.0 / (1.0 + jnp.exp(-x))


def _iota(shape, dim):
    return lax.broadcasted_iota(jnp.int32, shape, dim)


def _shift_prev(u, halo_row):
    rolled = pltpu.roll(u, 1, 0)
    head = rolled[:HALO]
    head = jnp.where(_iota(head.shape, 0) == 0, halo_row, head)
    return jnp.concatenate([head, rolled[HALO:]], axis=0)


def _shift_next(u, halo_row):
    n = u.shape[0]
    rolled = pltpu.roll(u, n - 1, 0)
    tail = rolled[n - HALO:]
    tail = jnp.where(_iota(tail.shape, 0) == HALO - 1, halo_row, tail)
    return jnp.concatenate([rolled[:n - HALO], tail], axis=0)


def _halo_rows(xp_ref, xn_ref, gain, i, nt):
    xh = jnp.concatenate([xp_ref[0], xn_ref[0]], axis=0)
    hh = _rms(xh, gain)
    row = _iota(hh.shape, 0)
    has_prev = jnp.where(i > 0, 1.0, 0.0)
    has_next = jnp.where(i < nt - 1, 1.0, 0.0)
    return _bf(hh * jnp.where(row < HALO, has_prev, has_next))


def _const_spec(shape):
    zeros = (0,) * len(shape)
    return pl.BlockSpec(shape, lambda *_: zeros, pipeline_mode=pl.Buffered(1))


def _params(n_axes):
    return pltpu.CompilerParams(dimension_semantics=("arbitrary",) * n_axes,
                                vmem_limit_bytes=VMEM_LIMIT)


def _row_specs(tm, d, t):
    nb = tm // HALO
    last = t // HALO - 1
    main = pl.BlockSpec((1, tm, d), lambda b, i: (b, i, 0))
    prev = pl.BlockSpec((1, HALO, d), lambda b, i: (b, jnp.maximum(i * nb - 1, 0), 0))
    nxt = pl.BlockSpec((1, HALO, d), lambda b, i: (b, jnp.minimum((i + 1) * nb, last), 0))
    return main, prev, nxt


def _mem_kv_kernel(mem_ref, mnorm_ref, wkv_ref, kg_ref, k_ref, v_ref):
    d = mem_ref.shape[-1]
    hd = d // CROSS_HEADS
    mn = _rms(mem_ref[0], mnorm_ref[...])
    kv = jnp.dot(_bf(mn), wkv_ref[0], preferred_element_type=F32)
    for h in range(CROSS_HEADS):
        kh = _rms(kv[:, h * hd:(h + 1) * hd], kg_ref[0]) * (1.0 / math.sqrt(hd))
        k_ref[0, 0, :, h * hd:(h + 1) * hd] = _bf(kh)
    v_ref[0, 0] = _bf(kv[:, d:])


def _mem_kv(mem, mem_norm, wkv, k_gain):
    b, m, d = mem.shape
    nl = wkv.shape[0]
    out = jax.ShapeDtypeStruct((nl, b, m, d), BF16)
    return pl.pallas_call(
        _mem_kv_kernel,
        grid=(nl, b),
        in_specs=[pl.BlockSpec((1, m, d), lambda l, i: (i, 0, 0)),
                  pl.BlockSpec((1, d), lambda l, i: (0, 0)),
                  pl.BlockSpec((1, d, 2 * d), lambda l, i: (l, 0, 0)),
                  pl.BlockSpec((1, 1, d // CROSS_HEADS), lambda l, i: (l, 0, 0))],
        out_specs=[pl.BlockSpec((1, 1, m, d), lambda l, i: (l, i, 0, 0)),
                   pl.BlockSpec((1, 1, m, d), lambda l, i: (l, i, 0, 0))],
        out_shape=[out, out],
        compiler_params=_params(2),
        name="mem_kv",
    )(mem, mem_norm.reshape(1, d), _bf(wkv), k_gain.reshape(nl, 1, -1))


def _mix_rwkv_sgu(x, rows, yf_ref, yb_ref, bv_ref, g_ref, b_ref, og_ref, bd_ref, wo_ref):
    y = yf_ref[0, rows] + yb_ref[0, rows]
    ms = _dot(y * y, bd_ref[...]) * (1.0 / HEAD_DIM)
    yn = y * lax.rsqrt(ms + NORM_EPS) * og_ref[...]
    a_out = (yn + bv_ref[0, rows].astype(F32)) * g_ref[0, rows].astype(F32)
    cat = jnp.concatenate([_bf(a_out), b_ref[0, rows]], axis=1)
    return x + _dot(cat, wo_ref[...])


def _mix_proj(x, rows, at_ref, w_ref):
    a = jnp.concatenate([at_ref[0, h, :, rows].astype(F32).T for h in range(at_ref.shape[1])], axis=1)
    return x + _dot(a, w_ref[...])


def _cross_kernel(*refs, n_mix, mix_fn):
    x_ref, mix_refs = refs[0], refs[1:1 + n_mix]
    g_ref, wq_ref, qg_ref, k_ref, v_ref, wo_ref, o_ref = refs[1 + n_mix:]
    tq, d = x_ref.shape[1:]
    hd = d // CROSS_HEADS
    step = min(CROSS_ROWS, tq)
    for start in range(0, tq, step):
        rows = slice(start, start + step)
        x = mix_fn(x_ref[0, rows], rows, *mix_refs)
        hn = _bf(_rms(x, g_ref[...]))
        q = _dot(hn, wq_ref[...])
        outs = []
        for h in range(CROSS_HEADS):
            sl = slice(h * hd, (h + 1) * hd)
            qh = _rms(q[:, sl], qg_ref[...])
            logits = _dot_nt(qh, k_ref[0, 0, :, sl])
            p = jnp.exp(logits - jnp.max(logits, axis=-1, keepdims=True))
            s = jnp.sum(p, axis=-1, keepdims=True)
            outs.append(_dot(p, v_ref[0, 0, :, sl]) / s)
        o = jnp.concatenate(outs, axis=1)
        o_ref[0, rows] = x + _dot(o, wo_ref[...])


def _cross_attention(x, mix_fn, mix_rows, mix_consts, gain, wq, q_gain, k_all, v_all, wo, layer, tq):
    b, t, d = x.shape
    m = k_all.shape[2]
    hd = d // CROSS_HEADS

    def row_spec(a):
        if a.ndim == 3:
            return pl.BlockSpec((1, tq, a.shape[-1]), lambda i, j: (i, j, 0))
        return pl.BlockSpec((1, a.shape[1], a.shape[2], tq), lambda i, j: (i, 0, 0, j))

    mix_specs = [row_spec(a) for a in mix_rows] + [_const_spec(a.shape) for a in mix_consts]
    return pl.pallas_call(
        functools.partial(_cross_kernel, n_mix=len(mix_specs), mix_fn=mix_fn),
        grid=(b, t // tq),
        in_specs=[pl.BlockSpec((1, tq, d), lambda i, j: (i, j, 0)),
                  *mix_specs,
                  _const_spec((1, d)),
                  _const_spec((d, d)),
                  _const_spec((1, hd)),
                  pl.BlockSpec((1, 1, m, d), lambda i, j: (layer, i, 0, 0)),
                  pl.BlockSpec((1, 1, m, d), lambda i, j: (layer, i, 0, 0)),
                  _const_spec((d, d))],
        out_specs=pl.BlockSpec((1, tq, d), lambda i, j: (i, j, 0)),
        out_shape=jax.ShapeDtypeStruct((b, t, d), F32),
        compiler_params=_params(2),
        name=f"cross_attn_{layer}",
    )(x, *mix_rows, *mix_consts, gain.reshape(1, d), _bf(wq), q_gain.reshape(1, hd), k_all, v_all, _bf(wo))


def _ffn_kernel(x_ref, xp_ref, xn_ref, g_ref, wup_ref, cw_ref, cb_ref, wdn_ref, o_ref,
                hn_ref, acc_ref):
    i = pl.program_id(1)
    nt = pl.num_programs(1)
    n_chunks = wdn_ref.shape[0] // FF_CHUNK
    x = x_ref[0]
    tm = x.shape[0]
    gain = g_ref[...]
    hn_ref[:tm] = _bf(_rms(x, gain))
    hn_ref[tm:] = _halo_rows(xp_ref, xn_ref, gain, i, nt)
    acc_ref[...] = jnp.zeros_like(acc_ref)

    def cols(idx):
        return slice(idx * FF_CHUNK, (idx + 1) * FF_CHUNK)

    def up_proj(idx):
        u = _dot_row_pieces(hn_ref[...], wup_ref[:, cols(idx)])
        return u[:tm], u[tm:]

    def conv(idx, u, uh):
        cw = cw_ref[:, cols(idx)]
        up = _shift_prev(u, uh[HALO - 1:HALO])
        un = _shift_next(u, uh[HALO:HALO + 1])
        return cb_ref[:, cols(idx)] + up * cw[0:1] + u * cw[1:2] + un * cw[2:3]

    ups = [up_proj(0), up_proj(n_chunks)]
    for c in range(n_chunks):
        nxt = [up_proj(c + 1), up_proj(c + 1 + n_chunks)] if c + 1 < n_chunks else None
        gate = conv(c, *ups[0])
        val = conv(c + n_chunks, *ups[1])
        acc_ref[...] += _dot_row_pieces(gate * _sigmoid(gate) * val, wdn_ref[cols(c), :])
        ups = nxt
    o_ref[0] = x + acc_ref[...]


def _conv_ffn(x, gain, w_up, conv_w, conv_b, w_down, tm):
    b, t, d = x.shape
    ff = w_down.shape[0]
    assert ff % FF_CHUNK == 0
    main, prev, nxt = _row_specs(tm, d, t)
    return pl.pallas_call(
        _ffn_kernel,
        grid=(b, t // tm),
        in_specs=[main, prev, nxt,
                  _const_spec((1, d)),
                  _const_spec((d, 2 * ff)),
                  _const_spec((CONV_WIDTH, 2 * ff)),
                  _const_spec((1, 2 * ff)),
                  _const_spec((ff, d))],
        out_specs=pl.BlockSpec((1, tm, d), lambda i, j: (i, j, 0)),
        out_shape=jax.ShapeDtypeStruct((b, t, d), F32),
        scratch_shapes=[pltpu.VMEM((tm + 2 * HALO, d), BF16),
                        pltpu.VMEM((tm, d), F32)],
        compiler_params=_params(2),
        name="conv_ffn",
    )(x, x, x, gain.reshape(1, d), _bf(w_up), conv_w, conv_b.reshape(1, 2 * ff), _bf(w_down))


def _ab_in_kernel(x_ref, xp_ref, xn_ref, g_ref, win_ref, mup_ref, mun_ref,
                  w0_ref, w2_ref, a0_ref, a2_ref, g2_ref, kk_ref, ka_ref, rk_ref, bd_ref,
                  vg_ref, ws_ref, sb_ref,
                  r_o, v_o, kk_o, lw0_o, lw1_o, be0_o, be1_o, k0_o, k1_o, g_o, bv_o, b_o):
    i = pl.program_id(1)
    nt = pl.num_programs(1)
    w = RWKV_WIDTH
    x = x_ref[0]
    tm = x.shape[0]
    gain = g_ref[...]
    hn = _bf(_rms(x, gain))
    hx = jnp.concatenate([hn, _halo_rows(xp_ref, xn_ref, gain, i, nt)], axis=0)
    zr = _dot(hx, win_ref[:, :RWKV_IN])
    za, zh = zr[:tm], zr[tm:]
    zp = _shift_prev(za, zh[HALO - 1:HALO])
    zn = _shift_next(za, zh[HALO:HALO + 1])
    mup, mun = mup_ref[...], mun_ref[...]
    zs = za * (1.0 - mup - mun) + mup * zp + mun * zn
    r = zs[:, 0:w]
    k = zs[:, w:2 * w]
    v = zs[:, 2 * w:3 * w]
    wl = jnp.tanh(zs[:, 3 * w:3 * w + LORA_W])
    al = zs[:, 3 * w + LORA_W:3 * w + 2 * LORA_W]
    gl = zs[:, 3 * w + 2 * LORA_W:]

    wpre = w0_ref[...] + _dot(wl, w2_ref[...])
    lw = -math.exp(-0.5) * _sigmoid(wpre)
    a = _sigmoid(a0_ref[...] + _dot(al, a2_ref[...]))
    g = _dot(_sigmoid(gl), g2_ref[...])

    bd = bd_ref[...]
    kk0 = k * kk_ref[...]
    kk = kk0 * lax.rsqrt(_dot(kk0 * kk0, bd) + KK_EPS)
    ka = ka_ref[...]
    k0 = k * (1.0 + (a[:, :w] - 1.0) * ka)
    k1 = k * (1.0 + (a[:, w:] - 1.0) * ka)
    bonus = _dot(r * rk_ref[...] * (k0 + k1), bd)

    r_o[0] = _bf(r)
    v_o[0] = _bf(v)
    kk_o[0] = _bf(kk)
    lw0_o[0] = lw[:, :w]
    lw1_o[0] = lw[:, w:]
    be0_o[0] = _bf(kk * a[:, :w])
    be1_o[0] = _bf(kk * a[:, w:])
    k0_o[0] = _bf(k0)
    k1_o[0] = _bf(k1)
    g_o[0] = _bf(g)
    bv_o[0] = _bf(bonus * v)

    zg = _dot(hn, win_ref[:, RWKV_IN:])
    zg = zg * (0.5 * (1.0 + jnp.tanh(math.sqrt(2.0 / math.pi) * (zg + 0.044715 * (zg * zg * zg)))))
    u = zg[:, :SGU_WIDTH]
    vn = _bf(_rms(zg[:, SGU_WIDTH:], vg_ref[...]))
    gd = SGU_WIDTH // SGU_GROUPS
    for n in range(tm // SGU_CHUNK):
        rows = slice(n * SGU_CHUNK, (n + 1) * SGU_CHUNK)
        for gi in range(SGU_GROUPS):
            cols = slice(gi * gd, (gi + 1) * gd)
            s = jnp.dot(ws_ref[gi], vn[rows, cols], preferred_element_type=F32) + sb_ref[gi]
            b_o[0, rows, cols] = _bf(u[rows, cols] * s)


def _ab_in(x, gain, w_in, mu_prev, mu_next, w0, w2, a0, a2, g2, k_k, k_a, r_k, v_gain, w_s, s_b, tm):
    b, t, d = x.shape
    w = RWKV_WIDTH
    n_in = w_in.shape[1]
    half = LORA_W // 2

    def lora_cat(m):
        z = jnp.zeros((half, w), m.dtype)
        return _bf(jnp.concatenate([jnp.concatenate([m[0], z], axis=1),
                                    jnp.concatenate([z, m[1]], axis=1)], axis=0))

    head = jnp.arange(w) // HEAD_DIM
    bd = _bf(head[:, None] == head[None, :])
    sb = jnp.broadcast_to(s_b[:, :, None], (SGU_GROUPS, SGU_CHUNK, SGU_WIDTH // SGU_GROUPS))
    main, prev, nxt = _row_specs(tm, d, t)
    ospec = pl.BlockSpec((1, tm, w), lambda i, j: (i, j, 0))
    return pl.pallas_call(
        _ab_in_kernel,
        grid=(b, t // tm),
        in_specs=[main, prev, nxt,
                  _const_spec((1, d)),
                  _const_spec((d, n_in)),
                  _const_spec((1, RWKV_IN)), _const_spec((1, RWKV_IN)),
                  _const_spec((1, 2 * w)), _const_spec((LORA_W, 2 * w)),
                  _const_spec((1, 2 * w)), _const_spec((LORA_W, 2 * w)),
                  _const_spec((LORA_W, w)),
                  _const_spec((1, w)), _const_spec((1, w)), _const_spec((1, w)),
                  _const_spec((w, w)),
                  _const_spec((1, SGU_WIDTH)),
                  _const_spec((SGU_GROUPS, SGU_CHUNK, SGU_CHUNK)),
                  _const_spec((SGU_GROUPS, SGU_CHUNK, SGU_WIDTH // SGU_GROUPS))],
        out_specs=[ospec] * 12,
        out_shape=[jax.ShapeDtypeStruct((b, t, w), dt) for dt in AB_IN_OUT_DTYPES],
        compiler_params=_params(2),
        name="ab_in",
    )(x, x, x, gain.reshape(1, d), _bf(w_in), mu_prev.reshape(1, -1), mu_next.reshape(1, -1),
      w0.reshape(1, 2 * w), lora_cat(w2), a0.reshape(1, 2 * w), lora_cat(a2), _bf(g2),
      k_k.reshape(1, w), k_a.reshape(1, w), r_k.reshape(1, w), bd,
      v_gain.reshape(1, -1), _bf(w_s), sb)


def _stack_heads(x):
    even = (_iota(x.shape, 1) & (LANES - 1)) < HEAD_DIM
    return jnp.concatenate([jnp.where(even, x, 0.0), jnp.where(even, 0.0, x)], axis=0)


def _rwkv_prep(r, v, kk, lw, be, kd, reverse):
    c = r.shape[0]
    rr = _iota((c, LANES), 0)
    cc = _iota((c, LANES), 1) & (HEAD_DIM - 1)
    if reverse:
        strict, incl = cc > rr, cc >= rr
    else:
        strict, incl = cc < rr, cc <= rr
    tri = _bf(incl[:, :c])

    l1 = _bf(lw)
    l2 = _bf(lw - l1.astype(F32))
    cum = jnp.dot(tri, l1, preferred_element_type=F32) + jnp.dot(tri, l2, preferred_element_type=F32)
    tot = cum[0:1] if reverse else cum[c - 1:c]
    e_neg = jnp.exp(-cum)
    e_end = jnp.exp(tot - cum)
    return dict(ag=-kk * jnp.exp(cum - lw), rg=r * jnp.exp(cum), bi=be * e_neg, ki=kd * e_neg,
                bee=be * e_end, kee=kd * e_end, gam=jnp.exp(tot), v=v, strict=strict, incl=incl)


def _rwkv_local(chunks):
    c = chunks[0]["v"].shape[0]
    n_tiles = chunks[0]["v"].shape[1] // LANES
    r128 = _iota((LANES, LANES), 0)
    c128 = _iota((LANES, LANES), 1)
    same_head = (r128 < HEAD_DIM) == (c128 < HEAD_DIM)
    diag = r128 == c128
    ctx = [(ch, slice(p * LANES, (p + 1) * LANES)) for ch in chunks for p in range(n_tiles)]

    aa = [_dot_nt(jnp.concatenate([ch["ag"][:, sl], ch["rg"][:, sl]], axis=0),
                  jnp.concatenate([_stack_heads(ch["bi"][:, sl]), _stack_heads(ch["ki"][:, sl])], axis=0))
          for ch, sl in ctx]
    a_ab = [jnp.where(ch["strict"], m[:c, :LANES], 0.0) for (ch, _), m in zip(ctx, aa)]
    a_rb = [jnp.where(ch["incl"], m[c:, :LANES], 0.0) for (ch, _), m in zip(ctx, aa)]
    a_k = [jnp.concatenate([jnp.where(ch["strict"], m[:c, LANES:], 0.0),
                            jnp.where(ch["incl"], m[c:, LANES:], 0.0)], axis=0) for (ch, _), m in zip(ctx, aa)]
    xv = [_dot(m, _stack_heads(ch["v"][:, sl])) for (ch, sl), m in zip(ctx, a_k)]
    x = [jnp.concatenate([ch["ag"][:, sl], m[:c]], axis=1) for (ch, sl), m in zip(ctx, xv)]
    def block_diag(pi):
        return jnp.where(same_head, jnp.concatenate([pi, pi], axis=0), 0.0)

    n_round = int(math.log2(c))
    rr = a_ab
    pw = [_dot(pi, block_diag(pi)) for pi in a_ab]
    for k in range(1, n_round):
        if k < n_round - 1:
            res = [_dot(jnp.concatenate([ri, pi], axis=0), block_diag(pi)) for ri, pi in zip(rr, pw)]
            rr = [ri + pi + m[:c] for ri, pi, m in zip(rr, pw, res)]
            pw = [m[c:] for m in res]
        else:
            rr = [ri + pi + _dot(ri, block_diag(pi)) for ri, pi in zip(rr, pw)]
    x = [xi + _dot(ri, _stack_heads(xi)) for xi, ri in zip(x, rr)]
    ry = [jnp.concatenate([ch["rg"][:, sl], m[c:]], axis=1) + _dot(ai, _stack_heads(xi))
          for (ch, sl), m, ai, xi in zip(ctx, xv, a_rb, x)]
    mg = [_dot(jnp.concatenate([ch["bee"][:, sl], ch["kee"][:, sl]], axis=0).T,
               jnp.concatenate([xi, jnp.concatenate([jnp.zeros_like(ch["v"][:, sl]), ch["v"][:, sl]], axis=1)],
                               axis=0))
          for (ch, sl), xi in zip(ctx, x)]
    out = []
    for (ch, sl), ryi, mgi in zip(ctx, ry, mg):
        gam_col = jnp.sum(jnp.where(diag, ch["gam"][:, sl], 0.0), axis=1, keepdims=True)
        out.append((ryi[:, :LANES], ryi[:, LANES:], jnp.where(same_head, mgi[:, :LANES], 0.0),
                    jnp.where(same_head, mgi[:, LANES:], 0.0), gam_col))
    return [out[i * n_tiles:(i + 1) * n_tiles] for i in range(len(chunks))]


def _rwkv_kernel(rf, vf, kkf, lwf, bef, kf, rb, vb, kkb, lwb, beb, kb, yf_o, yb_o, zf_ref, zb_ref):
    @pl.when(pl.program_id(1) == 0)
    def _():
        zf_ref[...] = jnp.zeros_like(zf_ref)
        zb_ref[...] = jnp.zeros_like(zb_ref)

    c = RWKV_CHUNK
    n = rf.shape[1] // c
    n_tiles = rf.shape[2] // LANES
    chunks = []
    for refs, reverse in (((rf, vf, kkf, lwf, bef, kf), False), ((rb, vb, kkb, lwb, beb, kb), True)):
        for ci in range(n):
            chunks.append(_rwkv_prep(*[ref[0, ci * c:(ci + 1) * c, :].astype(F32) for ref in refs], reverse))
    local = _rwkv_local(chunks)

    for d, (z_ref, y_o) in enumerate(((zf_ref, yf_o), (zb_ref, yb_o))):
        order = range(n) if d == 0 else range(n - 1, -1, -1)
        zs = [z_ref[p] for p in range(n_tiles)]
        for ci in order:
            ys = []
            for p in range(n_tiles):
                rt, yl, mt, gt, gam_col = local[d * n + ci][p]
                yz = _dot(jnp.concatenate([rt, mt], axis=0), zs[p])
                ys.append(yz[:c] + yl)
                zs[p] = gam_col * zs[p] + yz[c:] + gt
            y_o[0, ci * c:(ci + 1) * c, :] = jnp.concatenate(ys, axis=1)
        for p in range(n_tiles):
            z_ref[p] = zs[p]


def _rwkv_scan(r, v, kk, lw0, lw1, be0, be1, k0, k1):
    b, t, w = r.shape
    c = RWKV_CHUNK * RWKV_STEP_CHUNKS
    nc = t // c
    fwd = pl.BlockSpec((1, c, w), lambda i, j: (i, j, 0))
    bwd = pl.BlockSpec((1, c, w), lambda i, j: (i, nc - 1 - j, 0))
    out = jax.ShapeDtypeStruct((b, t, w), F32)
    zshape = pltpu.VMEM((w // LANES, LANES, LANES), F32)
    return pl.pallas_call(
        _rwkv_kernel,
        grid=(b, nc),
        in_specs=[fwd] * 6 + [bwd] * 6,
        out_specs=[fwd, bwd],
        out_shape=[out, out],
        scratch_shapes=[zshape, zshape],
        compiler_params=_params(2),
        name="rwkv_scan",
    )(r, v, kk, lw0, be0, k0, r, v, kk, lw1, be1, k1)


def _diff_in_kernel(x_ref, g_ref, w_ref, qgt_ref, kg_ref, bd_ref, qt_o, k_o, vt_o):
    x = x_ref[0]
    d = x.shape[-1]
    hn = _bf(_rms(x, g_ref[...]))
    z = _dot(hn, w_ref[...])
    bd = bd_ref[...]
    half = bd.shape[0]

    def qk_norm(y, gain, scale):
        parts = []
        for j in range(d // half):
            yj = y[:, j * half:(j + 1) * half]
            ms = _dot(yj * yj, bd) * (1.0 / HEAD_DIM)
            parts.append(yj * lax.rsqrt(ms + NORM_EPS) * gain * scale)
        return jnp.concatenate(parts, axis=1)

    k_o[0] = _bf(qk_norm(z[:, d:2 * d], kg_ref[...], 1.0))
    tm = x.shape[0]
    q_scale = qgt_ref[...] * (LOG2E / math.sqrt(HEAD_DIM))
    for h in range(d // DIFF_V_DIM):
        cols = slice(h * DIFF_V_DIM, (h + 1) * DIFF_V_DIM)
        qt = z[:, cols].T.reshape(2, HEAD_DIM, tm)
        inv = lax.rsqrt(jnp.mean(qt * qt, axis=1, keepdims=True) + NORM_EPS)
        qt_o[0, h] = _bf((qt * inv).reshape(DIFF_V_DIM, tm) * q_scale)
        vt_o[0, h, :DIFF_V_DIM, :] = _bf(z[:, 2 * d:][:, cols].T)
        vt_o[0, h, DIFF_V_DIM:, :] = jnp.ones((ONES_ROWS, tm), BF16)


def _diff_in(x, gain, w_in, q_gain, k_gain, tm):
    b, t, d = x.shape
    half = 512
    nh = d // DIFF_V_DIM
    head = jnp.arange(half) // HEAD_DIM
    bd = _bf(head[:, None] == head[None, :])
    xs = pl.BlockSpec((1, tm, d), lambda i, j: (i, j, 0))
    out = jax.ShapeDtypeStruct((b, t, d), BF16)
    vt_rows = DIFF_V_DIM + ONES_ROWS
    return pl.pallas_call(
        _diff_in_kernel,
        grid=(b, t // tm),
        in_specs=[xs, _const_spec((1, d)), _const_spec((d, 3 * d)),
                  _const_spec((DIFF_V_DIM, 1)), _const_spec((1, half)), _const_spec((half, half))],
        out_specs=[pl.BlockSpec((1, nh, DIFF_V_DIM, tm), lambda i, j: (i, 0, 0, j)), xs,
                   pl.BlockSpec((1, nh, vt_rows, tm), lambda i, j: (i, 0, 0, j))],
        out_shape=[jax.ShapeDtypeStruct((b, nh, DIFF_V_DIM, t), BF16), out,
                   jax.ShapeDtypeStruct((b, nh, vt_rows, t), BF16)],
        compiler_params=_params(2),
        name="diff_in",
    )(x, gain.reshape(1, d), _bf(w_in), jnp.tile(q_gain, 2).reshape(DIFF_V_DIM, 1),
      jnp.tile(k_gain, half // HEAD_DIM).reshape(1, half), bd)


def _diff_attn_kernel(qt_ref, k_ref, vt_ref, hc_ref, lam_ref, og_ref, o_ref, bias_ref, p_ref, *s_scratch,
                      lambda_init, bounded):
    t, tq = bias_ref.shape

    @pl.when(pl.program_id(2) == 0)
    def _():
        dist = jnp.abs((_iota((t, tq), 0) - pl.program_id(1) * tq - _iota((t, tq), 1)).astype(F32))
        bias = dist * (-LOG2E * hc_ref[0, 0:1, 0:1])
        bias_ref[...] = bias - hc_ref[0, 1:2, 0:1] if bounded else bias

    q = qt_ref[0, 0]
    first = _iota(q.shape, 0) < HEAD_DIM
    zero = jnp.zeros_like(q)
    kb = min(KEY_BLOCK, t)
    qb = min(QUERY_BLOCK, tq)
    sub = HALO

    qcs = (jnp.where(first, q, zero), jnp.where(first, zero, q))
    secs = [(c, slice(i * qb, (i + 1) * qb)) for i in range(tq // qb) for c in range(2)]
    n_kb = t // kb
    outs = []
    lv = lam_ref[...]
    lam = (jnp.exp(jnp.sum(lv[0:1] * lv[1:2], axis=-1, keepdims=True))
           - jnp.exp(jnp.sum(lv[2:3] * lv[3:4], axis=-1, keepdims=True)) + lambda_init)

    def finish(qs):
        acc1, acc0 = outs.pop(), outs.pop()
        o0, s0 = acc0[:DIFF_V_DIM], acc0[DIFF_V_DIM:DIFF_V_DIM + 1]
        o1, s1 = acc1[:DIFF_V_DIM], acc1[DIFF_V_DIM:DIFF_V_DIM + 1]
        o = o0 * (1.0 / s0) - o1 * (lam / s1)
        inv = lax.rsqrt(jnp.mean(o * o, axis=0, keepdims=True) + NORM_EPS)
        o_ref[0, 0, :, qs] = _bf(o * inv * (og_ref[...] * (1.0 - lambda_init)))

    def logits(sec, j):
        c, qs = sec
        ks = slice(j * kb, (j + 1) * kb)
        return jnp.dot(k_ref[0, ks, :], qcs[c][:, qs], preferred_element_type=F32) + bias_ref[ks, qs]

    if bounded:
        for sec in secs:
            c, qs = sec
            for j in range(n_kb):
                p_ref[c, j * kb:(j + 1) * kb, qs] = _bf(jnp.exp2(logits(sec, j)))
            outs.append(jnp.dot(vt_ref[0, 0], p_ref[c, :, qs], preferred_element_type=F32))
            if c == 1:
                finish(qs)
    else:
        s_ref, = s_scratch

        def score_block(sec, j, m):
            c, qs = sec
            s = logits(sec, j)
            s_ref[c, j * kb:(j + 1) * kb, qs] = s
            return jnp.maximum(m, jnp.max(s.reshape(kb // sub, sub, qb), axis=0))

        def prob_block(sec, j, m):
            c, qs = sec
            ks = slice(j * kb, (j + 1) * kb)
            p_ref[c, ks, qs] = _bf(jnp.exp2(s_ref[c, ks, qs] - m))

        m_init = jnp.full((sub, qb), -jnp.inf, F32)
        m = m_init
        for j in range(n_kb):
            m = score_block(secs[0], j, m)
        m_prev = jnp.max(m, axis=0, keepdims=True)
        for n in range(1, len(secs) + 1):
            m = m_init
            for j in range(n_kb):
                if n < len(secs):
                    m = score_block(secs[n], j, m)
                prob_block(secs[n - 1], j, m_prev)
            m_prev = jnp.max(m, axis=0, keepdims=True)
            c, qs = secs[n - 1]
            outs.append(jnp.dot(vt_ref[0, 0], p_ref[c, :, qs], preferred_element_type=F32))
            if c == 1:
                finish(qs)


def _diff_attention(qt, k, vt, q_gain, k_gain, lam_vecs, out_gain, lambda_init, tq_two_pass):
    b, t, d = k.shape
    nh = d // DIFF_V_DIM
    slopes = 2.0 ** (-8.0 * jnp.arange(1, nh + 1, dtype=F32) / nh)
    bound = 1.02 * math.sqrt(HEAD_DIM) * LOG2E * jnp.max(jnp.abs(q_gain)) * jnp.max(jnp.abs(k_gain))
    head_consts = jnp.stack([jnp.broadcast_to(slopes[:, None], (nh, LANES)),
                             jnp.broadcast_to(bound, (nh, LANES))], axis=1)

    def run(bounded):
        tq = min(2 * tq_two_pass, t) if bounded else tq_two_pass
        scratch = [pltpu.VMEM((t, tq), F32), pltpu.VMEM((2, t, tq), BF16)]
        if not bounded:
            scratch.append(pltpu.VMEM((2, t, tq), F32))
        return pl.pallas_call(
            functools.partial(_diff_attn_kernel, lambda_init=lambda_init, bounded=bounded),
            grid=(nh, t // tq, b),
            in_specs=[pl.BlockSpec((1, 1, DIFF_V_DIM, tq), lambda h, j, i: (i, h, 0, j)),
                      pl.BlockSpec((1, t, DIFF_V_DIM), lambda h, j, i: (i, 0, h)),
                      pl.BlockSpec((1, 1, vt.shape[2], t), lambda h, j, i: (i, h, 0, 0)),
                      pl.BlockSpec((1, 2, LANES), lambda h, j, i: (h, 0, 0)),
                      _const_spec((4, HEAD_DIM)),
                      _const_spec((DIFF_V_DIM, 1))],
            out_specs=pl.BlockSpec((1, 1, DIFF_V_DIM, tq), lambda h, j, i: (i, h, 0, j)),
            out_shape=jax.ShapeDtypeStruct((b, nh, DIFF_V_DIM, t), BF16),
            scratch_shapes=scratch,
            compiler_params=_params(3),
            name="diff_attn_bounded" if bounded else "diff_attn",
        )(qt, k, vt, head_consts, lam_vecs, out_gain.reshape(DIFF_V_DIM, 1))

    return lax.cond(2.0 * bound <= SAFE_EXP2_SPAN, lambda: run(True), lambda: run(False))


def kernel(x, mem, mem_norm, mix_norm, cross_norm, ffn_norm, ab_w_in, ab_shift_prev, ab_shift_next, rwkv_w0, rwkv_w2, rwkv_a0, rwkv_a2, rwkv_g2, rwkv_k_k, rwkv_k_a, rwkv_r_k, rwkv_out_gain, sgu_v_gain, sgu_w_s, sgu_b, ab_w_out, diff_w_in, diff_q_gain, diff_k_gain, diff_lambda_q1, diff_lambda_k1, diff_lambda_q2, diff_lambda_k2, diff_out_gain, diff_w_out, cross_wq, cross_wkv, cross_q_gain, cross_k_gain, cross_wo, ffn_w_up, ffn_conv_w, ffn_conv_b, ffn_w_down):
    b, t, d = x.shape
    depth = mix_norm.shape[0]
    tm = min(512, t)
    tm_wide = min(1024, t)
    assert t % tm_wide == 0 and t % (RWKV_CHUNK * RWKV_STEP_CHUNKS) == 0 and tm % SGU_CHUNK == 0

    k_all, v_all = _mem_kv(mem, mem_norm, cross_wkv, cross_k_gain)
    for l in range(depth):
        if l % 2 == 0:
            e = l // 2
            (r, v, kk, lw0, lw1, be0, be1, k0, k1, g, bv, b_out) = _ab_in(
                x, mix_norm[l], ab_w_in[e], ab_shift_prev[e], ab_shift_next[e],
                rwkv_w0[e], rwkv_w2[e], rwkv_a0[e], rwkv_a2[e], rwkv_g2[e],
                rwkv_k_k[e], rwkv_k_a[e], rwkv_r_k[e], sgu_v_gain[e], sgu_w_s[e], sgu_b[e], tm)
            yf, yb = _rwkv_scan(r, v, kk, lw0, lw1, be0, be1, k0, k1)
            head = jnp.arange(RWKV_WIDTH) // HEAD_DIM
            mix = (_mix_rwkv_sgu, (yf, yb, bv, g, b_out),
                   (rwkv_out_gain[e].reshape(1, RWKV_WIDTH), _bf(head[:, None] == head[None, :]), _bf(ab_w_out[e])))
        else:
            o = l // 2
            lambda_init = 0.8 - 0.6 * math.exp(-0.3 * l)
            q, k, vt = _diff_in(x, mix_norm[l], diff_w_in[o], diff_q_gain[o], diff_k_gain[o], tm)
            lam_vecs = jnp.stack([diff_lambda_q1[o], diff_lambda_k1[o], diff_lambda_q2[o], diff_lambda_k2[o]])
            c_out = _diff_attention(q, k, vt, diff_q_gain[o], diff_k_gain[o], lam_vecs, diff_out_gain[o],
                                    lambda_init, tm_wide)
            mix = (_mix_proj, (c_out,), (_bf(diff_w_out[o]),))
        x = _cross_attention(x, *mix, cross_norm[l], cross_wq[l], cross_q_gain[l], k_all, v_all, cross_wo[l], l,
                             tm_wide)
        x = _conv_ffn(x, ffn_norm[l], ffn_w_up[l], ffn_conv_w[l], ffn_conv_b[l], ffn_w_down[l], tm_wide)
    return x
```

```python
import functools
import math

import jax
import jax.numpy as jnp
from jax import lax
from jax.experimental import pallas as pl
from jax.experimental.pallas import tpu as pltpu

NORM_EPS = 1e-6
KK_EPS = 1e-12
RWKV_HEADS = 8
HEAD_DIM = 64
RWKV_WIDTH = RWKV_HEADS * HEAD_DIM
LORA_W = 128
RWKV_IN = 3 * RWKV_WIDTH + 3 * LORA_W
SGU_WIDTH = 512
SGU_GROUPS = 4
SGU_CHUNK = 128
DIFF_V_DIM = 2 * HEAD_DIM
CROSS_HEADS = 4
CONV_WIDTH = 3
RWKV_CHUNK = 64
RWKV_STEP_CHUNKS = 2
LANES = 128
HALO = 8
FF_CHUNK = 256
CROSS_ROWS = 512
MXU_ROWS = 128
KEY_BLOCK = 128
QUERY_BLOCK = 256
ONES_ROWS = 16
SAFE_EXP2_SPAN = 100.0
VMEM_LIMIT = 56 * 1024 * 1024
LOG2E = math.log2(math.e)

F32 = jnp.float32
BF16 = jnp.bfloat16
AB_IN_OUT_DTYPES = (BF16, BF16, BF16, F32, F32, BF16, BF16, BF16, BF16, BF16, BF16, BF16)


def _bf(x):
    return x.astype(BF16)


def _dot(a, b):
    return jnp.dot(_bf(a), _bf(b), preferred_element_type=F32)


def _dot_nt(a, b):
    return lax.dot_general(_bf(a), _bf(b), (((1,), (1,)), ((), ())), preferred_element_type=F32)


def _dot_row_pieces(a, b):
    m = a.shape[0]
    starts = list(range(0, m - m % MXU_ROWS, MXU_ROWS)) or [0]
    ends = starts[1:] + [m]
    return jnp.concatenate([_dot(a[s:e], b) for s, e in zip(starts, ends)], axis=0)


def _rms(x, gain):
    return x * lax.rsqrt(jnp.mean(x * x, axis=-1, keepdims=True) + NORM_EPS) * gain


def _sigmoid(x):
    return 1.0 / (1.0 + jnp.exp(-x))


def _iota(shape, dim):
    return lax.broadcasted_iota(jnp.int32, shape, dim)


def _shift_prev(u, halo_row):
    rolled = pltpu.roll(u, 1, 0)
    head = rolled[:HALO]
    head = jnp.where(_iota(head.shape, 0) == 0, halo_row, head)
    return jnp.concatenate([head, rolled[HALO:]], axis=0)


def _shift_next(u, halo_row):
    n = u.shape[0]
    rolled = pltpu.roll(u, n - 1, 0)
    tail = rolled[n - HALO:]
    tail = jnp.where(_iota(tail.shape, 0) == HALO - 1, halo_row, tail)
    return jnp.concatenate([rolled[:n - HALO], tail], axis=0)


def _halo_rows(xp_ref, xn_ref, gain, i, nt):
    xh = jnp.concatenate([xp_ref[0], xn_ref[0]], axis=0)
    hh = _rms(xh, gain)
    row = _iota(hh.shape, 0)
    has_prev = jnp.where(i > 0, 1.0, 0.0)
    has_next = jnp.where(i < nt - 1, 1.0, 0.0)
    return _bf(hh * jnp.where(row < HALO, has_prev, has_next))


def _const_spec(shape):
    zeros = (0,) * len(shape)
    return pl.BlockSpec(shape, lambda *_: zeros, pipeline_mode=pl.Buffered(1))


def _params(n_axes):
    return pltpu.CompilerParams(dimension_semantics=("arbitrary",) * n_axes,
                                vmem_limit_bytes=VMEM_LIMIT)


def _row_specs(tm, d, t):
    nb = tm // HALO
    last = t // HALO - 1
    main = pl.BlockSpec((1, tm, d), lambda b, i: (b, i, 0))
    prev = pl.BlockSpec((1, HALO, d), lambda b, i: (b, jnp.maximum(i * nb - 1, 0), 0))
    nxt = pl.BlockSpec((1, HALO, d), lambda b, i: (b, jnp.minimum((i + 1) * nb, last), 0))
    return main, prev, nxt


def _mem_kv_kernel(mem_ref, mnorm_ref, wkv_ref, kg_ref, k_ref, v_ref):
    d = mem_ref.shape[-1]
    hd = d // CROSS_HEADS
    mn = _rms(mem_ref[0], mnorm_ref[...])
    kv = jnp.dot(_bf(mn), wkv_ref[0], preferred_element_type=F32)
    for h in range(CROSS_HEADS):
        kh = _rms(kv[:, h * hd:(h + 1) * hd], kg_ref[0]) * (1.0 / math.sqrt(hd))
        k_ref[0, 0, :, h * hd:(h + 1) * hd] = _bf(kh)
    v_ref[0, 0] = _bf(kv[:, d:])


def _mem_kv(mem, mem_norm, wkv, k_gain):
    b, m, d = mem.shape
    nl = wkv.shape[0]
    out = jax.ShapeDtypeStruct((nl, b, m, d), BF16)
    return pl.pallas_call(
        _mem_kv_kernel,
        grid=(nl, b),
        in_specs=[pl.BlockSpec((1, m, d), lambda l, i: (i, 0, 0)),
                  pl.BlockSpec((1, d), lambda l, i: (0, 0)),
                  pl.BlockSpec((1, d, 2 * d), lambda l, i: (l, 0, 0)),
                  pl.BlockSpec((1, 1, d // CROSS_HEADS), lambda l, i: (l, 0, 0))],
        out_specs=[pl.BlockSpec((1, 1, m, d), lambda l, i: (l, i, 0, 0)),
                   pl.BlockSpec((1, 1, m, d), lambda l, i: (l, i, 0, 0))],
        out_shape=[out, out],
        compiler_params=_params(2),
        name="mem_kv",
    )(mem, mem_norm.reshape(1, d), _bf(wkv), k_gain.reshape(nl, 1, -1))


def _mix_rwkv_sgu(x, rows, yf_ref, yb_ref, bv_ref, g_ref, b_ref, og_ref, bd_ref, wo_ref):
    y = yf_ref[0, rows] + yb_ref[0, rows]
    ms = _dot(y * y, bd_ref[...]) * (1.0 / HEAD_DIM)
    yn = y * lax.rsqrt(ms + NORM_EPS) * og_ref[...]
    a_out = (yn + bv_ref[0, rows].astype(F32)) * g_ref[0, rows].astype(F32)
    cat = jnp.concatenate([_bf(a_out), b_ref[0, rows]], axis=1)
    return x + _dot(cat, wo_ref[...])


def _mix_proj(x, rows, at_ref, w_ref):
    a = jnp.concatenate([at_ref[0, h, :, rows].astype(F32).T for h in range(at_ref.shape[1])], axis=1)
    return x + _dot(a, w_ref[...])


def _cross_kernel(*refs, n_mix, mix_fn):
    x_ref, mix_refs = refs[0], refs[1:1 + n_mix]
    g_ref, wq_ref, qg_ref, k_ref, v_ref, wo_ref, o_ref = refs[1 + n_mix:]
    tq, d = x_ref.shape[1:]
    hd = d // CROSS_HEADS
    step = min(CROSS_ROWS, tq)
    for start in range(0, tq, step):
        rows = slice(start, start + step)
        x = mix_fn(x_ref[0, rows], rows, *mix_refs)
        hn = _bf(_rms(x, g_ref[...]))
        q = _dot(hn, wq_ref[...])
        outs = []
        for h in range(CROSS_HEADS):
            sl = slice(h * hd, (h + 1) * hd)
            qh = _rms(q[:, sl], qg_ref[...])
            logits = _dot_nt(qh, k_ref[0, 0, :, sl])
            p = jnp.exp(logits - jnp.max(logits, axis=-1, keepdims=True))
            s = jnp.sum(p, axis=-1, keepdims=True)
            outs.append(_dot(p, v_ref[0, 0, :, sl]) / s)
        o = jnp.concatenate(outs, axis=1)
        o_ref[0, rows] = x + _dot(o, wo_ref[...])


def _cross_attention(x, mix_fn, mix_rows, mix_consts, gain, wq, q_gain, k_all, v_all, wo, layer, tq):
    b, t, d = x.shape
    m = k_all.shape[2]
    hd = d // CROSS_HEADS

    def row_spec(a):
        if a.ndim == 3:
            return pl.BlockSpec((1, tq, a.shape[-1]), lambda i, j: (i, j, 0))
        return pl.BlockSpec((1, a.shape[1], a.shape[2], tq), lambda i, j: (i, 0, 0, j))

    mix_specs = [row_spec(a) for a in mix_rows] + [_const_spec(a.shape) for a in mix_consts]
    return pl.pallas_call(
        functools.partial(_cross_kernel, n_mix=len(mix_specs), mix_fn=mix_fn),
        grid=(b, t // tq),
        in_specs=[pl.BlockSpec((1, tq, d), lambda i, j: (i, j, 0)),
                  *mix_specs,
                  _const_spec((1, d)),
                  _const_spec((d, d)),
                  _const_spec((1, hd)),
                  pl.BlockSpec((1, 1, m, d), lambda i, j: (layer, i, 0, 0)),
                  pl.BlockSpec((1, 1, m, d), lambda i, j: (layer, i, 0, 0)),
                  _const_spec((d, d))],
        out_specs=pl.BlockSpec((1, tq, d), lambda i, j: (i, j, 0)),
        out_shape=jax.ShapeDtypeStruct((b, t, d), F32),
        compiler_params=_params(2),
        name=f"cross_attn_{layer}",
    )(x, *mix_rows, *mix_consts, gain.reshape(1, d), _bf(wq), q_gain.reshape(1, hd), k_all, v_all, _bf(wo))


def _ffn_kernel(x_ref, xp_ref, xn_ref, g_ref, wup_ref, cw_ref, cb_ref, wdn_ref, o_ref,
                hn_ref, acc_ref):
    i = pl.program_id(1)
    nt = pl.num_programs(1)
    n_chunks = wdn_ref.shape[0] // FF_CHUNK
    x = x_ref[0]
    tm = x.shape[0]
    gain = g_ref[...]
    hn_ref[:tm] = _bf(_rms(x, gain))
    hn_ref[tm:] = _halo_rows(xp_ref, xn_ref, gain, i, nt)
    acc_ref[...] = jnp.zeros_like(acc_ref)

    def cols(idx):
        return slice(idx * FF_CHUNK, (idx + 1) * FF_CHUNK)

    def up_proj(idx):
        u = _dot_row_pieces(hn_ref[...], wup_ref[:, cols(idx)])
        return u[:tm], u[tm:]

    def conv(idx, u, uh):
        cw = cw_ref[:, cols(idx)]
        up = _shift_prev(u, uh[HALO - 1:HALO])
        un = _shift_next(u, uh[HALO:HALO + 1])
        return cb_ref[:, cols(idx)] + up * cw[0:1] + u * cw[1:2] + un * cw[2:3]

    ups = [up_proj(0), up_proj(n_chunks)]
    for c in range(n_chunks):
        nxt = [up_proj(c + 1), up_proj(c + 1 + n_chunks)] if c + 1 < n_chunks else None
        gate = conv(c, *ups[0])
        val = conv(c + n_chunks, *ups[1])
        acc_ref[...] += _dot_row_pieces(gate * _sigmoid(gate) * val, wdn_ref[cols(c), :])
        ups = nxt
    o_ref[0] = x + acc_ref[...]


def _conv_ffn(x, gain, w_up, conv_w, conv_b, w_down, tm):
    b, t, d = x.shape
    ff = w_down.shape[0]
    assert ff % FF_CHUNK == 0
    main, prev, nxt = _row_specs(tm, d, t)
    return pl.pallas_call(
        _ffn_kernel,
        grid=(b, t // tm),
        in_specs=[main, prev, nxt,
                  _const_spec((1, d)),
                  _const_spec((d, 2 * ff)),
                  _const_spec((CONV_WIDTH, 2 * ff)),
                  _const_spec((1, 2 * ff)),
                  _const_spec((ff, d))],
        out_specs=pl.BlockSpec((1, tm, d), lambda i, j: (i, j, 0)),
        out_shape=jax.ShapeDtypeStruct((b, t, d), F32),
        scratch_shapes=[pltpu.VMEM((tm + 2 * HALO, d), BF16),
                        pltpu.VMEM((tm, d), F32)],
        compiler_params=_params(2),
        name="conv_ffn",
    )(x, x, x, gain.reshape(1, d), _bf(w_up), conv_w, conv_b.reshape(1, 2 * ff), _bf(w_down))


def _ab_in_kernel(x_ref, xp_ref, xn_ref, g_ref, win_ref, mup_ref, mun_ref,
                  w0_ref, w2_ref, a0_ref, a2_ref, g2_ref, kk_ref, ka_ref, rk_ref, bd_ref,
                  vg_ref, ws_ref, sb_ref,
                  r_o, v_o, kk_o, lw0_o, lw1_o, be0_o, be1_o, k0_o, k1_o, g_o, bv_o, b_o):
    i = pl.program_id(1)
    nt = pl.num_programs(1)
    w = RWKV_WIDTH
    x = x_ref[0]
    tm = x.shape[0]
    gain = g_ref[...]
    hn = _bf(_rms(x, gain))
    hx = jnp.concatenate([hn, _halo_rows(xp_ref, xn_ref, gain, i, nt)], axis=0)
    zr = _dot(hx, win_ref[:, :RWKV_IN])
    za, zh = zr[:tm], zr[tm:]
    zp = _shift_prev(za, zh[HALO - 1:HALO])
    zn = _shift_next(za, zh[HALO:HALO + 1])
    mup, mun = mup_ref[...], mun_ref[...]
    zs = za * (1.0 - mup - mun) + mup * zp + mun * zn
    r = zs[:, 0:w]
    k = zs[:, w:2 * w]
    v = zs[:, 2 * w:3 * w]
    wl = jnp.tanh(zs[:, 3 * w:3 * w + LORA_W])
    al = zs[:, 3 * w + LORA_W:3 * w + 2 * LORA_W]
    gl = zs[:, 3 * w + 2 * LORA_W:]

    wpre = w0_ref[...] + _dot(wl, w2_ref[...])
    lw = -math.exp(-0.5) * _sigmoid(wpre)
    a = _sigmoid(a0_ref[...] + _dot(al, a2_ref[...]))
    g = _dot(_sigmoid(gl), g2_ref[...])

    bd = bd_ref[...]
    kk0 = k * kk_ref[...]
    kk = kk0 * lax.rsqrt(_dot(kk0 * kk0, bd) + KK_EPS)
    ka = ka_ref[...]
    k0 = k * (1.0 + (a[:, :w] - 1.0) * ka)
    k1 = k * (1.0 + (a[:, w:] - 1.0) * ka)
    bonus = _dot(r * rk_ref[...] * (k0 + k1), bd)

    r_o[0] = _bf(r)
    v_o[0] = _bf(v)
    kk_o[0] = _bf(kk)
    lw0_o[0] = lw[:, :w]
    lw1_o[0] = lw[:, w:]
    be0_o[0] = _bf(kk * a[:, :w])
    be1_o[0] = _bf(kk * a[:, w:])
    k0_o[0] = _bf(k0)
    k1_o[0] = _bf(k1)
    g_o[0] = _bf(g)
    bv_o[0] = _bf(bonus * v)

    zg = _dot(hn, win_ref[:, RWKV_IN:])
    zg = zg * (0.5 * (1.0 + jnp.tanh(math.sqrt(2.0 / math.pi) * (zg + 0.044715 * (zg * zg * zg)))))
    u = zg[:, :SGU_WIDTH]
    vn = _bf(_rms(zg[:, SGU_WIDTH:], vg_ref[...]))
    gd = SGU_WIDTH // SGU_GROUPS
    for n in range(tm // SGU_CHUNK):
        rows = slice(n * SGU_CHUNK, (n + 1) * SGU_CHUNK)
        for gi in range(SGU_GROUPS):
            cols = slice(gi * gd, (gi + 1) * gd)
            s = jnp.dot(ws_ref[gi], vn[rows, cols], preferred_element_type=F32) + sb_ref[gi]
            b_o[0, rows, cols] = _bf(u[rows, cols] * s)


def _ab_in(x, gain, w_in, mu_prev, mu_next, w0, w2, a0, a2, g2, k_k, k_a, r_k, v_gain, w_s, s_b, tm):
    b, t, d = x.shape
    w = RWKV_WIDTH
    n_in = w_in.shape[1]
    half = LORA_W // 2

    def lora_cat(m):
        z = jnp.zeros((half, w), m.dtype)
        return _bf(jnp.concatenate([jnp.concatenate([m[0], z], axis=1),
                                    jnp.concatenate([z, m[1]], axis=1)], axis=0))

    head = jnp.arange(w) // HEAD_DIM
    bd = _bf(head[:, None] == head[None, :])
    sb = jnp.broadcast_to(s_b[:, :, None], (SGU_GROUPS, SGU_CHUNK, SGU_WIDTH // SGU_GROUPS))
    main, prev, nxt = _row_specs(tm, d, t)
    ospec = pl.BlockSpec((1, tm, w), lambda i, j: (i, j, 0))
    return pl.pallas_call(
        _ab_in_kernel,
        grid=(b, t // tm),
        in_specs=[main, prev, nxt,
                  _const_spec((1, d)),
                  _const_spec((d, n_in)),
                  _const_spec((1, RWKV_IN)), _const_spec((1, RWKV_IN)),
                  _const_spec((1, 2 * w)), _const_spec((LORA_W, 2 * w)),
                  _const_spec((1, 2 * w)), _const_spec((LORA_W, 2 * w)),
                  _const_spec((LORA_W, w)),
                  _const_spec((1, w)), _const_spec((1, w)), _const_spec((1, w)),
                  _const_spec((w, w)),
                  _const_spec((1, SGU_WIDTH)),
                  _const_spec((SGU_GROUPS, SGU_CHUNK, SGU_CHUNK)),
                  _const_spec((SGU_GROUPS, SGU_CHUNK, SGU_WIDTH // SGU_GROUPS))],
        out_specs=[ospec] * 12,
        out_shape=[jax.ShapeDtypeStruct((b, t, w), dt) for dt in AB_IN_OUT_DTYPES],
        compiler_params=_params(2),
        name="ab_in",
    )(x, x, x, gain.reshape(1, d), _bf(w_in), mu_prev.reshape(1, -1), mu_next.reshape(1, -1),
      w0.reshape(1, 2 * w), lora_cat(w2), a0.reshape(1, 2 * w), lora_cat(a2), _bf(g2),
      k_k.reshape(1, w), k_a.reshape(1, w), r_k.reshape(1, w), bd,
      v_gain.reshape(1, -1), _bf(w_s), sb)


def _stack_heads(x):
    even = (_iota(x.shape, 1) & (LANES - 1)) < HEAD_DIM
    return jnp.concatenate([jnp.where(even, x, 0.0), jnp.where(even, 0.0, x)], axis=0)


def _rwkv_prep(r, v, kk, lw, be, kd, reverse):
    c = r.shape[0]
    rr = _iota((c, LANES), 0)
    cc = _iota((c, LANES), 1) & (HEAD_DIM - 1)
    if reverse:
        strict, incl = cc > rr, cc >= rr
    else:
        strict, incl = cc < rr, cc <= rr
    tri = _bf(incl[:, :c])

    l1 = _bf(lw)
    l2 = _bf(lw - l1.astype(F32))
    cum = jnp.dot(tri, l1, preferred_element_type=F32) + jnp.dot(tri, l2, preferred_element_type=F32)
    tot = cum[0:1] if reverse else cum[c - 1:c]
    e_neg = jnp.exp(-cum)
    e_end = jnp.exp(tot - cum)
    return dict(ag=-kk * jnp.exp(cum - lw), rg=r * jnp.exp(cum), bi=be * e_neg, ki=kd * e_neg,
                bee=be * e_end, kee=kd * e_end, gam=jnp.exp(tot), v=v, strict=strict, incl=incl)


def _rwkv_local(chunks):
    c = chunks[0]["v"].shape[0]
    n_tiles = chunks[0]["v"].shape[1] // LANES
    r128 = _iota((LANES, LANES), 0)
    c128 = _iota((LANES, LANES), 1)
    same_head = (r128 < HEAD_DIM) == (c128 < HEAD_DIM)
    diag = r128 == c128
    ctx = [(ch, slice(p * LANES, (p + 1) * LANES)) for ch in chunks for p in range(n_tiles)]

    aa = [_dot_nt(jnp.concatenate([ch["ag"][:, sl], ch["rg"][:, sl]], axis=0),
                  jnp.concatenate([_stack_heads(ch["bi"][:, sl]), _stack_heads(ch["ki"][:, sl])], axis=0))
          for ch, sl in ctx]
    a_ab = [jnp.where(ch["strict"], m[:c, :LANES], 0.0) for (ch, _), m in zip(ctx, aa)]
    a_rb = [jnp.where(ch["incl"], m[c:, :LANES], 0.0) for (ch, _), m in zip(ctx, aa)]
    a_k = [jnp.concatenate([jnp.where(ch["strict"], m[:c, LANES:], 0.0),
                            jnp.where(ch["incl"], m[c:, LANES:], 0.0)], axis=0) for (ch, _), m in zip(ctx, aa)]
    xv = [_dot(m, _stack_heads(ch["v"][:, sl])) for (ch, sl), m in zip(ctx, a_k)]
    x = [jnp.concatenate([ch["ag"][:, sl], m[:c]], axis=1) for (ch, sl), m in zip(ctx, xv)]
    def block_diag(pi):
        return jnp.where(same_head, jnp.concatenate([pi, pi], axis=0), 0.0)

    n_round = int(math.log2(c))
    rr = a_ab
    pw = [_dot(pi, block_diag(pi)) for pi in a_ab]
    for k in range(1, n_round):
        if k < n_round - 1:
            res = [_dot(jnp.concatenate([ri, pi], axis=0), block_diag(pi)) for ri, pi in zip(rr, pw)]
            rr = [ri + pi + m[:c] for ri, pi, m in zip(rr, pw, res)]
            pw = [m[c:] for m in res]
        else:
            rr = [ri + pi + _dot(ri, block_diag(pi)) for ri, pi in zip(rr, pw)]
    x = [xi + _dot(ri, _stack_heads(xi)) for xi, ri in zip(x, rr)]
    ry = [jnp.concatenate([ch["rg"][:, sl], m[c:]], axis=1) + _dot(ai, _stack_heads(xi))
          for (ch, sl), m, ai, xi in zip(ctx, xv, a_rb, x)]
    mg = [_dot(jnp.concatenate([ch["bee"][:, sl], ch["kee"][:, sl]], axis=0).T,
               jnp.concatenate([xi, jnp.concatenate([jnp.zeros_like(ch["v"][:, sl]), ch["v"][:, sl]], axis=1)],
                               axis=0))
          for (ch, sl), xi in zip(ctx, x)]
    out = []
    for (ch, sl), ryi, mgi in zip(ctx, ry, mg):
        gam_col = jnp.sum(jnp.where(diag, ch["gam"][:, sl], 0.0), axis=1, keepdims=True)
        out.append((ryi[:, :LANES], ryi[:, LANES:], jnp.where(same_head, mgi[:, :LANES], 0.0),
                    jnp.where(same_head, mgi[:, LANES:], 0.0), gam_col))
    return [out[i * n_tiles:(i + 1) * n_tiles] for i in range(len(chunks))]


def _rwkv_kernel(rf, vf, kkf, lwf, bef, kf, rb, vb, kkb, lwb, beb, kb, yf_o, yb_o, zf_ref, zb_ref):
    @pl.when(pl.program_id(1) == 0)
    def _():
        zf_ref[...] = jnp.zeros_like(zf_ref)
        zb_ref[...] = jnp.zeros_like(zb_ref)

    c = RWKV_CHUNK
    n = rf.shape[1] // c
    n_tiles = rf.shape[2] // LANES
    chunks = []
    for refs, reverse in (((rf, vf, kkf, lwf, bef, kf), False), ((rb, vb, kkb, lwb, beb, kb), True)):
        for ci in range(n):
            chunks.append(_rwkv_prep(*[ref[0, ci * c:(ci + 1) * c, :].astype(F32) for ref in refs], reverse))
    local = _rwkv_local(chunks)

    for d, (z_ref, y_o) in enumerate(((zf_ref, yf_o), (zb_ref, yb_o))):
        order = range(n) if d == 0 else range(n - 1, -1, -1)
        zs = [z_ref[p] for p in range(n_tiles)]
        for ci in order:
            ys = []
            for p in range(n_tiles):
                rt, yl, mt, gt, gam_col = local[d * n + ci][p]
                yz = _dot(jnp.concatenate([rt, mt], axis=0), zs[p])
                ys.append(yz[:c] + yl)
                zs[p] = gam_col * zs[p] + yz[c:] + gt
            y_o[0, ci * c:(ci + 1) * c, :] = jnp.concatenate(ys, axis=1)
        for p in range(n_tiles):
            z_ref[p] = zs[p]


def _rwkv_scan(r, v, kk, lw0, lw1, be0, be1, k0, k1):
    b, t, w = r.shape
    c = RWKV_CHUNK * RWKV_STEP_CHUNKS
    nc = t // c
    fwd = pl.BlockSpec((1, c, w), lambda i, j: (i, j, 0))
    bwd = pl.BlockSpec((1, c, w), lambda i, j: (i, nc - 1 - j, 0))
    out = jax.ShapeDtypeStruct((b, t, w), F32)
    zshape = pltpu.VMEM((w // LANES, LANES, LANES), F32)
    return pl.pallas_call(
        _rwkv_kernel,
        grid=(b, nc),
        in_specs=[fwd] * 6 + [bwd] * 6,
        out_specs=[fwd, bwd],
        out_shape=[out, out],
        scratch_shapes=[zshape, zshape],
        compiler_params=_params(2),
        name="rwkv_scan",
    )(r, v, kk, lw0, be0, k0, r, v, kk, lw1, be1, k1)


def _diff_in_kernel(x_ref, g_ref, w_ref, qgt_ref, kg_ref, bd_ref, qt_o, k_o, vt_o):
    x = x_ref[0]
    d = x.shape[-1]
    hn = _bf(_rms(x, g_ref[...]))
    z = _dot(hn, w_ref[...])
    bd = bd_ref[...]
    half = bd.shape[0]

    def qk_norm(y, gain, scale):
        parts = []
        for j in range(d // half):
            yj = y[:, j * half:(j + 1) * half]
            ms = _dot(yj * yj, bd) * (1.0 / HEAD_DIM)
            parts.append(yj * lax.rsqrt(ms + NORM_EPS) * gain * scale)
        return jnp.concatenate(parts, axis=1)

    k_o[0] = _bf(qk_norm(z[:, d:2 * d], kg_ref[...], 1.0))
    tm = x.shape[0]
    q_scale = qgt_ref[...] * (LOG2E / math.sqrt(HEAD_DIM))
    for h in range(d // DIFF_V_DIM):
        cols = slice(h * DIFF_V_DIM, (h + 1) * DIFF_V_DIM)
        qt = z[:, cols].T.reshape(2, HEAD_DIM, tm)
        inv = lax.rsqrt(jnp.mean(qt * qt, axis=1, keepdims=True) + NORM_EPS)
        qt_o[0, h] = _bf((qt * inv).reshape(DIFF_V_DIM, tm) * q_scale)
        vt_o[0, h, :DIFF_V_DIM, :] = _bf(z[:, 2 * d:][:, cols].T)
        vt_o[0, h, DIFF_V_DIM:, :] = jnp.ones((ONES_ROWS, tm), BF16)


def _diff_in(x, gain, w_in, q_gain, k_gain, tm):
    b, t, d = x.shape
    half = 512
    nh = d // DIFF_V_DIM
    head = jnp.arange(half) // HEAD_DIM
    bd = _bf(head[:, None] == head[None, :])
    xs = pl.BlockSpec((1, tm, d), lambda i, j: (i, j, 0))
    out = jax.ShapeDtypeStruct((b, t, d), BF16)
    vt_rows = DIFF_V_DIM + ONES_ROWS
    return pl.pallas_call(
        _diff_in_kernel,
        grid=(b, t // tm),
        in_specs=[xs, _const_spec((1, d)), _const_spec((d, 3 * d)),
                  _const_spec((DIFF_V_DIM, 1)), _const_spec((1, half)), _const_spec((half, half))],
        out_specs=[pl.BlockSpec((1, nh, DIFF_V_DIM, tm), lambda i, j: (i, 0, 0, j)), xs,
                   pl.BlockSpec((1, nh, vt_rows, tm), lambda i, j: (i, 0, 0, j))],
        out_shape=[jax.ShapeDtypeStruct((b, nh, DIFF_V_DIM, t), BF16), out,
                   jax.ShapeDtypeStruct((b, nh, vt_rows, t), BF16)],
        compiler_params=_params(2),
        name="diff_in",
    )(x, gain.reshape(1, d), _bf(w_in), jnp.tile(q_gain, 2).reshape(DIFF_V_DIM, 1),
      jnp.tile(k_gain, half // HEAD_DIM).reshape(1, half), bd)


def _diff_attn_kernel(qt_ref, k_ref, vt_ref, hc_ref, lam_ref, og_ref, o_ref, bias_ref, p_ref, *s_scratch,
                      lambda_init, bounded):
    t, tq = bias_ref.shape

    @pl.when(pl.program_id(2) == 0)
    def _():
        dist = jnp.abs((_iota((t, tq), 0) - pl.program_id(1) * tq - _iota((t, tq), 1)).astype(F32))
        bias = dist * (-LOG2E * hc_ref[0, 0:1, 0:1])
        bias_ref[...] = bias - hc_ref[0, 1:2, 0:1] if bounded else bias

    q = qt_ref[0, 0]
    first = _iota(q.shape, 0) < HEAD_DIM
    zero = jnp.zeros_like(q)
    kb = min(KEY_BLOCK, t)
    qb = min(QUERY_BLOCK, tq)
    sub = HALO

    qcs = (jnp.where(first, q, zero), jnp.where(first, zero, q))
    secs = [(c, slice(i * qb, (i + 1) * qb)) for i in range(tq // qb) for c in range(2)]
    n_kb = t // kb
    outs = []
    lv = lam_ref[...]
    lam = (jnp.exp(jnp.sum(lv[0:1] * lv[1:2], axis=-1, keepdims=True))
           - jnp.exp(jnp.sum(lv[2:3] * lv[3:4], axis=-1, keepdims=True)) + lambda_init)

    def finish(qs):
        acc1, acc0 = outs.pop(), outs.pop()
        o0, s0 = acc0[:DIFF_V_DIM], acc0[DIFF_V_DIM:DIFF_V_DIM + 1]
        o1, s1 = acc1[:DIFF_V_DIM], acc1[DIFF_V_DIM:DIFF_V_DIM + 1]
        o = o0 * (1.0 / s0) - o1 * (lam / s1)
        inv = lax.rsqrt(jnp.mean(o * o, axis=0, keepdims=True) + NORM_EPS)
        o_ref[0, 0, :, qs] = _bf(o * inv * (og_ref[...] * (1.0 - lambda_init)))

    def logits(sec, j):
        c, qs = sec
        ks = slice(j * kb, (j + 1) * kb)
        return jnp.dot(k_ref[0, ks, :], qcs[c][:, qs], preferred_element_type=F32) + bias_ref[ks, qs]

    if bounded:
        for sec in secs:
            c, qs = sec
            for j in range(n_kb):
                p_ref[c, j * kb:(j + 1) * kb, qs] = _bf(jnp.exp2(logits(sec, j)))
            outs.append(jnp.dot(vt_ref[0, 0], p_ref[c, :, qs], preferred_element_type=F32))
            if c == 1:
                finish(qs)
    else:
        s_ref, = s_scratch

        def score_block(sec, j, m):
            c, qs = sec
            s = logits(sec, j)
            s_ref[c, j * kb:(j + 1) * kb, qs] = s
            return jnp.maximum(m, jnp.max(s.reshape(kb // sub, sub, qb), axis=0))

        def prob_block(sec, j, m):
            c, qs = sec
            ks = slice(j * kb, (j + 1) * kb)
            p_ref[c, ks, qs] = _bf(jnp.exp2(s_ref[c, ks, qs] - m))

        m_init = jnp.full((sub, qb), -jnp.inf, F32)
        m = m_init
        for j in range(n_kb):
            m = score_block(secs[0], j, m)
        m_prev = jnp.max(m, axis=0, keepdims=True)
        for n in range(1, len(secs) + 1):
            m = m_init
            for j in range(n_kb):
                if n < len(secs):
                    m = score_block(secs[n], j, m)
                prob_block(secs[n - 1], j, m_prev)
            m_prev = jnp.max(m, axis=0, keepdims=True)
            c, qs = secs[n - 1]
            outs.append(jnp.dot(vt_ref[0, 0], p_ref[c, :, qs], preferred_element_type=F32))
            if c == 1:
                finish(qs)


def _diff_attention(qt, k, vt, q_gain, k_gain, lam_vecs, out_gain, lambda_init, tq_two_pass):
    b, t, d = k.shape
    nh = d // DIFF_V_DIM
    slopes = 2.0 ** (-8.0 * jnp.arange(1, nh + 1, dtype=F32) / nh)
    bound = 1.02 * math.sqrt(HEAD_DIM) * LOG2E * jnp.max(jnp.abs(q_gain)) * jnp.max(jnp.abs(k_gain))
    head_consts = jnp.stack([jnp.broadcast_to(slopes[:, None], (nh, LANES)),
                             jnp.broadcast_to(bound, (nh, LANES))], axis=1)

    def run(bounded):
        tq = min(2 * tq_two_pass, t) if bounded else tq_two_pass
        scratch = [pltpu.VMEM((t, tq), F32), pltpu.VMEM((2, t, tq), BF16)]
        if not bounded:
            scratch.append(pltpu.VMEM((2, t, tq), F32))
        return pl.pallas_call(
            functools.partial(_diff_attn_kernel, lambda_init=lambda_init, bounded=bounded),
            grid=(nh, t // tq, b),
            in_specs=[pl.BlockSpec((1, 1, DIFF_V_DIM, tq), lambda h, j, i: (i, h, 0, j)),
                      pl.BlockSpec((1, t, DIFF_V_DIM), lambda h, j, i: (i, 0, h)),
                      pl.BlockSpec((1, 1, vt.shape[2], t), lambda h, j, i: (i, h, 0, 0)),
                      pl.BlockSpec((1, 2, LANES), lambda h, j, i: (h, 0, 0)),
                      _const_spec((4, HEAD_DIM)),
                      _const_spec((DIFF_V_DIM, 1))],
            out_specs=pl.BlockSpec((1, 1, DIFF_V_DIM, tq), lambda h, j, i: (i, h, 0, j)),
            out_shape=jax.ShapeDtypeStruct((b, nh, DIFF_V_DIM, t), BF16),
            scratch_shapes=scratch,
            compiler_params=_params(3),
            name="diff_attn_bounded" if bounded else "diff_attn",
        )(qt, k, vt, head_consts, lam_vecs, out_gain.reshape(DIFF_V_DIM, 1))

    return lax.cond(2.0 * bound <= SAFE_EXP2_SPAN, lambda: run(True), lambda: run(False))


def kernel(x, mem, mem_norm, mix_norm, cross_norm, ffn_norm, ab_w_in, ab_shift_prev, ab_shift_next, rwkv_w0, rwkv_w2, rwkv_a0, rwkv_a2, rwkv_g2, rwkv_k_k, rwkv_k_a, rwkv_r_k, rwkv_out_gain, sgu_v_gain, sgu_w_s, sgu_b, ab_w_out, diff_w_in, diff_q_gain, diff_k_gain, diff_lambda_q1, diff_lambda_k1, diff_lambda_q2, diff_lambda_k2, diff_out_gain, diff_w_out, cross_wq, cross_wkv, cross_q_gain, cross_k_gain, cross_wo, ffn_w_up, ffn_conv_w, ffn_conv_b, ffn_w_down):
    b, t, d = x.shape
    depth = mix_norm.shape[0]
    tm = min(512, t)
    tm_wide = min(1024, t)
    assert t % tm_wide == 0 and t % (RWKV_CHUNK * RWKV_STEP_CHUNKS) == 0 and tm % SGU_CHUNK == 0

    k_all, v_all = _mem_kv(mem, mem_norm, cross_wkv, cross_k_gain)
    for l in range(depth):
        if l % 2 == 0:
            e = l // 2
            (r, v, kk, lw0, lw1, be0, be1, k0, k1, g, bv, b_out) = _ab_in(
                x, mix_norm[l], ab_w_in[e], ab_shift_prev[e], ab_shift_next[e],
                rwkv_w0[e], rwkv_w2[e], rwkv_a0[e], rwkv_a2[e], rwkv_g2[e],
                rwkv_k_k[e], rwkv_k_a[e], rwkv_r_k[e], sgu_v_gain[e], sgu_w_s[e], sgu_b[e], tm)
            yf, yb = _rwkv_scan(r, v, kk, lw0, lw1, be0, be1, k0, k1)
            head = jnp.arange(RWKV_WIDTH) // HEAD_DIM
            mix = (_mix_rwkv_sgu, (yf, yb, bv, g, b_out),
                   (rwkv_out_gain[e].reshape(1, RWKV_WIDTH), _bf(head[:, None] == head[None, :]), _bf(ab_w_out[e])))
        else:
            o = l // 2
            lambda_init = 0.8 - 0.6 * math.exp(-0.3 * l)
            q, k, vt = _diff_in(x, mix_norm[l], diff_w_in[o], diff_q_gain[o], diff_k_gain[o], tm)
            lam_vecs = jnp.stack([diff_lambda_q1[o], diff_lambda_k1[o], diff_lambda_q2[o], diff_lambda_k2[o]])
            c_out = _diff_attention(q, k, vt, diff_q_gain[o], diff_k_gain[o], lam_vecs, diff_out_gain[o],
                                    lambda_init, tm_wide)
            mix = (_mix_proj, (c_out,), (_bf(diff_w_out[o]),))
        x = _cross_attention(x, *mix, cross_norm[l], cross_wq[l], cross_q_gain[l], k_all, v_all, cross_wo[l], l,
                             tm_wide)
        x = _conv_ffn(x, ffn_norm[l], ffn_w_up[l], ffn_conv_w[l], ffn_conv_b[l], ffn_w_down[l], tm_wide)
    return x
```
